```python
import jax, jax.numpy as jnp
from jax import lax
import numpy as np

D_MODEL = 1024
BATCH = 4
SEQ = 4096
DEPTH = 2
DEC_BATCH = 32
DEC_SEQ = 8
PAST_LEN = 16384
PAGE_SIZE = 128

HEAD_DIM = 64
DSWA_GROUPS = ((128, 1), (512, 4), (2048, 16))
N_GROUPS = 3
HEADS_PER_GROUP = 8
A_HEADS = N_GROUPS * HEADS_PER_GROUP
A_WIDTH = A_HEADS * HEAD_DIM
A_OUT = HEADS_PER_GROUP * HEAD_DIM
Q_BLOCK = 128
GLA_HEADS = 4
GLA_KEY = D_MODEL // 2
GLA_VAL = D_MODEL
GLA_DK = GLA_KEY // GLA_HEADS
GLA_DV = GLA_VAL // GLA_HEADS
GATE_RANK = 16
GATE_TAU = 16.0
GLA_CHUNK = 64
D_FF = 2816
N_EXPERTS = 8
TOP_K = 2
D_FF_EXPERT = 3584
ROPE_THETA = 10000.0
EPS = 1e-6
N_DENSE = (DEPTH + 1) // 2
N_MOE = DEPTH // 2
IN_SPLITS = (A_WIDTH, A_WIDTH, A_WIDTH, GLA_KEY, GLA_KEY, GLA_VAL, GLA_VAL, GATE_RANK, D_MODEL, D_MODEL)
D_IN = 3 * A_WIDTH + 2 * GLA_KEY + 2 * GLA_VAL + GATE_RANK + 2 * D_MODEL

kernel_name = "hybrid_dilated_gla_decoder_step"

F32 = jnp.float32


def rms_norm(x, g):
    xf = x.astype(F32)
    y = xf * lax.rsqrt(jnp.mean(xf * xf, axis=-1, keepdims=True) + EPS)
    return y.astype(x.dtype) * g


def rope(x, pos):
    half = HEAD_DIM // 2
    inv = ROPE_THETA ** (-jnp.arange(half, dtype=F32) / half)
    ang = pos.astype(F32)[:, None] * inv[None, :]
    cos = jnp.cos(ang)[:, None, :]
    sin = jnp.sin(ang)[:, None, :]
    xf = x.astype(F32)
    x1, x2 = xf[..., :half], xf[..., half:]
    return jnp.concatenate([x1 * cos - x2 * sin, x1 * sin + x2 * cos], axis=-1).astype(x.dtype)


def dilated_window_prompt(q, k, v, window, dil):
    b, s, h, dh = q.shape
    span = window // dil
    unit = dil * Q_BLOCK
    s_pad = -(-s // unit) * unit
    ls = s_pad // dil
    nb = ls // Q_BLOCK

    def split(t):
        t = jnp.pad(t, ((0, 0), (0, s_pad - s), (0, 0), (0, 0))).reshape(b, ls, dil, h, dh)
        return jnp.moveaxis(t, 2, 1).reshape(b, dil, nb, Q_BLOCK, h, dh)

    def with_prev(t):
        prev = jnp.pad(t[:, :, :-1], ((0, 0), (0, 0), (1, 0), (0, 0), (0, 0), (0, 0)))
        return jnp.concatenate([prev, t], axis=3)

    qb = split(q)
    kk = with_prev(split(k))
    vv = with_prev(split(v))
    scores = jnp.einsum('brnqhd,brnkhd->brnhqk', qb, kk, preferred_element_type=F32) * (dh ** -0.5)
    qi = jnp.arange(Q_BLOCK)[:, None]
    kj = jnp.arange(2 * Q_BLOCK)[None, :]
    dist = Q_BLOCK + qi - kj
    band = (dist >= 0) & (dist <= span)
    not_first = jnp.arange(nb)[:, None, None] > 0
    mask = band[None] & (not_first | (kj >= Q_BLOCK)[None])
    scores = jnp.where(mask[None, None, :, None], scores, -jnp.inf)
    lse = jax.nn.logsumexp(scores, axis=-1)
    p = jnp.exp(scores - lse[..., None])
    o = jnp.einsum('brnhqk,brnkhd->brnqhd', p.astype(v.dtype), vv)
    o = jnp.moveaxis(o.reshape(b, dil, ls, h, dh), 1, 2).reshape(b, s_pad, h, dh)[:, :s]
    lse = jnp.moveaxis(lse, -1, -2).reshape(b, dil, ls, h)
    lse = jnp.moveaxis(lse, 1, 2).reshape(b, s_pad, h)[:, :s]
    return o, lse


def dilated_window_sample(q, kk, vv, window, dil):
    b, t, h, dh = q.shape
    lb = kk.shape[1] - t
    span = window // dil
    idx = lb + jnp.arange(t)[:, None] - dil * jnp.arange(span + 1)[None, :]
    valid = idx >= 0
    idx = jnp.maximum(idx, 0)
    kg = kk[:, idx]
    vg = vv[:, idx]
    scores = jnp.einsum('bthd,btjhd->bthj', q, kg, preferred_element_type=F32) * (dh ** -0.5)
    scores = jnp.where(valid[None, :, None, :], scores, -jnp.inf)
    lse = jax.nn.logsumexp(scores, axis=-1)
    p = jnp.exp(scores - lse[..., None])
    o = jnp.einsum('bthj,btjhd->bthd', p.astype(vv.dtype), vg)
    return o, lse


def gla_chunked(q, k, v, log_a, s0):
    b, L, h, _ = q.shape
    dv = v.shape[-1]
    c = GLA_CHUNK if L % GLA_CHUNK == 0 else L
    n = L // c

    def blocks(t):
        return jnp.moveaxis(t.astype(F32).reshape(b, n, c, h, -1), 3, 1)

    qc, kc, vc, ac = blocks(q), blocks(k), blocks(v), blocks(log_a)
    cum = jnp.cumsum(ac, axis=3)
    last = cum[:, :, :, -1:, :]
    q_in = qc * jnp.exp(cum)
    k_in = kc * jnp.exp(-cum)
    k_out = kc * jnp.exp(last - cum)
    causal = jnp.tril(jnp.ones((c, c), dtype=bool))
    att = jnp.where(causal, jnp.einsum('bhncd,bhnsd->bhncs', q_in, k_in), 0.0)
    o = jnp.einsum('bhncs,bhnse->bhnce', att, vc)
    kv = jnp.einsum('bhncd,bhnce->nbhde', k_out, vc)
    decay = jnp.exp(jnp.moveaxis(last[:, :, :, 0], 2, 0))

    def step(state, inp):
        d, u = inp
        return d[..., None] * state + u, state

    s_final, s_prev = lax.scan(step, s0.astype(F32), (decay, kv))
    o = o + jnp.einsum('bhncd,nbhde->bhnce', q_in, s_prev)
    o = jnp.moveaxis(o, 1, 3).reshape(b, L, h, dv)
    return o, s_final


def token_mix(h, pos, kv_bufs, s0, w_in, w_a2, b_a, g_gla, w_pa, w_pb, w_o):
    b, L, _ = h.shape
    offs = [int(i) for i in np.cumsum(IN_SPLITS)[:-1]]
    qa, ka, va, qb, kb, vb, rb, glr, ga, gb = jnp.split(h @ w_in, offs, axis=-1)
    qa = rope(qa.reshape(b, L, A_HEADS, HEAD_DIM), pos)
    ka = rope(ka.reshape(b, L, A_HEADS, HEAD_DIM), pos)
    va = va.reshape(b, L, A_HEADS, HEAD_DIM)
    outs, lses, new_bufs = [], [], []
    for g, (window, dil) in enumerate(DSWA_GROUPS):
        sl = slice(g * HEADS_PER_GROUP, (g + 1) * HEADS_PER_GROUP)
        qg, kg, vg = qa[:, :, sl], ka[:, :, sl], va[:, :, sl]
        if kv_bufs is None:
            o, lse = dilated_window_prompt(qg, kg, vg, window, dil)
            keep = min(window, L)
            new_bufs.append(jnp.stack([kg[:, L - keep:], vg[:, L - keep:]], axis=2))
        else:
            buf = kv_bufs[g]
            lb = buf.shape[1]
            kk = jnp.concatenate([buf[:, :, 0].astype(kg.dtype), kg], axis=1)
            vv = jnp.concatenate([buf[:, :, 1].astype(vg.dtype), vg], axis=1)
            o, lse = dilated_window_sample(qg, kk, vv, window, dil)
            new_bufs.append(jnp.stack([kk[:, -lb:], vv[:, -lb:]], axis=2))
        outs.append(o)
        lses.append(lse)
    alpha = jax.nn.softmax(jnp.stack(lses), axis=0)
    o_a = jnp.sum(alpha[..., None] * jnp.stack(outs).astype(F32), axis=0)
    o_a = o_a.astype(h.dtype).reshape(b, L, A_OUT)

    log_a = jax.nn.log_sigmoid((glr @ w_a2 + b_a).astype(F32)) / GATE_TAU
    o_b, s_new = gla_chunked(
        qb.reshape(b, L, GLA_HEADS, GLA_DK) * (GLA_DK ** -0.5),
        kb.reshape(b, L, GLA_HEADS, GLA_DK),
        vb.reshape(b, L, GLA_HEADS, GLA_DV),
        log_a.reshape(b, L, GLA_HEADS, GLA_DK), s0)
    o_b = rms_norm(o_b, g_gla.reshape(GLA_HEADS, GLA_DV).astype(F32)).reshape(b, L, GLA_VAL)
    o_b = jax.nn.silu(rb) * o_b.astype(h.dtype)

    merged = jax.nn.sigmoid(ga) * (o_a @ w_pa) + jax.nn.sigmoid(gb) * (o_b @ w_pb)
    return merged @ w_o, new_bufs, s_new


def swiglu(h, wg, wu, wd):
    return (jax.nn.silu(h @ wg) * (h @ wu)) @ wd


def moe_ffn(h, w_router, wg, wu, wd):
    logits = (h @ w_router).astype(F32)
    top_v, top_i = lax.top_k(logits, TOP_K)
    gates = jax.nn.softmax(top_v, axis=-1)
    combine = jnp.sum(jax.nn.one_hot(top_i, N_EXPERTS, dtype=F32) * gates[..., None], axis=-2).astype(h.dtype)
    out = jnp.zeros_like(h)
    for e in range(N_EXPERTS):
        out = out + combine[..., e:e + 1] * swiglu(h, wg[e], wu[e], wd[e])
    return out


def run_trunk(x, pos, kv_caches, gla_state, g_mix, w_in, w_gate_a2, b_gate_a, g_gla, w_branch_a,
              w_branch_b, w_out, g_ffn, w_ff_gate, w_ff_up, w_ff_down, w_router, w_exp_gate,
              w_exp_up, w_exp_down, g_final):
    b = x.shape[0]
    bufs_out = [[] for _ in range(N_GROUPS)]
    states_out = []
    for l in range(DEPTH):
        h = rms_norm(x, g_mix[l])
        bufs = None if kv_caches is None else [c[l] for c in kv_caches]
        s0 = jnp.zeros((b, GLA_HEADS, GLA_DK, GLA_DV), F32) if gla_state is None else gla_state[l]
        mix, nbufs, s_new = token_mix(h, pos, bufs, s0, w_in[l], w_gate_a2[l], b_gate_a[l], g_gla[l],
                                      w_branch_a[l], w_branch_b[l], w_out[l])
        x = x + mix
        h = rms_norm(x, g_ffn[l])
        if l % 2 == 0:
            i = l // 2
            x = x + swiglu(h, w_ff_gate[i], w_ff_up[i], w_ff_down[i])
        else:
            i = l // 2
            x = x + moe_ffn(h, w_router[i], w_exp_gate[i], w_exp_up[i], w_exp_down[i])
        for g in range(N_GROUPS):
            bufs_out[g].append(nbufs[g])
        states_out.append(s_new)
    y = rms_norm(x, g_final)
    return (y, jnp.stack(bufs_out[0]), jnp.stack(bufs_out[1]), jnp.stack(bufs_out[2]), jnp.stack(states_out))


def setup_inputs(seed: int = 0) -> dict:
    key = jax.random.key(seed)
    ks = jax.random.split(key, 24)

    def nrm(k, shape, scale):
        return jax.random.normal(k, shape, F32) * scale

    def gain(k, shape):
        return 1.0 + 0.02 * jax.random.normal(k, shape, F32)

    def buf_shape(window):
        return (DEPTH, DEC_BATCH, min(window, PAST_LEN), 2, HEADS_PER_GROUP, HEAD_DIM)

    return {
        "x_prompt": nrm(ks[0], (BATCH, SEQ, D_MODEL), 1.0),
        "x_sample": nrm(ks[1], (DEC_BATCH, DEC_SEQ, D_MODEL), 1.0),
        "cache_kv_w128": nrm(ks[2], buf_shape(128), 1.0),
        "cache_kv_w512": nrm(ks[3], buf_shape(512), 1.0),
        "cache_kv_w2048": nrm(ks[4], buf_shape(2048), 1.0),
        "state_gla": nrm(ks[5], (DEPTH, DEC_BATCH, GLA_HEADS, GLA_DK, GLA_DV), 1.0),
        "g_mix": gain(ks[6], (DEPTH, D_MODEL)),
        "w_in": nrm(ks[7], (DEPTH, D_MODEL, D_IN), D_MODEL ** -0.5),
        "w_gate_a2": nrm(ks[8], (DEPTH, GATE_RANK, GLA_KEY), GATE_RANK ** -0.5),
        "b_gate_a": nrm(ks[9], (DEPTH, GLA_KEY), 0.1),
        "g_gla": gain(ks[10], (DEPTH, GLA_VAL)),
        "w_branch_a": nrm(ks[11], (DEPTH, A_OUT, D_MODEL), A_OUT ** -0.5),
        "w_branch_b": nrm(ks[12], (DEPTH, GLA_VAL, D_MODEL), GLA_VAL ** -0.5),
        "w_out": nrm(ks[13], (DEPTH, D_MODEL, D_MODEL), D_MODEL ** -0.5),
        "g_ffn": gain(ks[14], (DEPTH, D_MODEL)),
        "w_ff_gate": nrm(ks[15], (N_DENSE, D_MODEL, D_FF), D_MODEL ** -0.5),
        "w_ff_up": nrm(ks[16], (N_DENSE, D_MODEL, D_FF), D_MODEL ** -0.5),
        "w_ff_down": nrm(ks[17], (N_DENSE, D_FF, D_MODEL), D_FF ** -0.5),
        "w_router": nrm(ks[18], (N_MOE, D_MODEL, N_EXPERTS), D_MODEL ** -0.5),
        "w_exp_gate": nrm(ks[19], (N_MOE, N_EXPERTS, D_MODEL, D_FF_EXPERT), D_MODEL ** -0.5),
        "w_exp_up": nrm(ks[20], (N_MOE, N_EXPERTS, D_MODEL, D_FF_EXPERT), D_MODEL ** -0.5),
        "w_exp_down": nrm(ks[21], (N_MOE, N_EXPERTS, D_FF_EXPERT, D_MODEL), D_FF_EXPERT ** -0.5),
        "g_final": gain(ks[22], (D_MODEL,)),
    }


def reference(x_prompt, x_sample, cache_kv_w128, cache_kv_w512, cache_kv_w2048, state_gla, g_mix, w_in,
              w_gate_a2, b_gate_a, g_gla, w_branch_a, w_branch_b, w_out, g_ffn, w_ff_gate, w_ff_up,
              w_ff_down, w_router, w_exp_gate, w_exp_up, w_exp_down, g_final):
    pos_prompt = jnp.arange(x_prompt.shape[1], dtype=jnp.int32)
    pos_sample = PAST_LEN + jnp.arange(x_sample.shape[1], dtype=jnp.int32)
    y_prompt, kv128_p, kv512_p, kv2048_p, gla_p = run_trunk(
        x_prompt, pos_prompt, None, None, g_mix, w_in, w_gate_a2, b_gate_a, g_gla, w_branch_a,
        w_branch_b, w_out, g_ffn, w_ff_gate, w_ff_up, w_ff_down, w_router, w_exp_gate, w_exp_up,
        w_exp_down, g_final)
    y_sample, kv128_s, kv512_s, kv2048_s, gla_s = run_trunk(
        x_sample, pos_sample, (cache_kv_w128, cache_kv_w512, cache_kv_w2048), state_gla, g_mix, w_in,
        w_gate_a2, b_gate_a, g_gla, w_branch_a, w_branch_b, w_out, g_ffn, w_ff_gate, w_ff_up, w_ff_down,
        w_router, w_exp_gate, w_exp_up, w_exp_down, g_final)
    return (y_prompt, y_sample, kv128_p, kv512_p, kv2048_p, gla_p, kv128_s, kv512_s, kv2048_s, gla_s)
```

```python
import functools

import jax
import jax.numpy as jnp
import numpy as np
from jax import lax
from jax.experimental import pallas as pl
from jax.experimental.pallas import tpu as pltpu

F32 = jnp.float32
BF16 = jnp.bfloat16

D_MODEL = 1024
DEPTH = 2
PAST_LEN = 16384
HEAD_DIM = 64
DSWA_GROUPS = ((128, 1), (512, 4), (2048, 16))
N_GROUPS = 3
HEADS_PER_GROUP = 8
A_WIDTH = N_GROUPS * HEADS_PER_GROUP * HEAD_DIM
A_OUT = HEADS_PER_GROUP * HEAD_DIM
Q_BLOCK = 128
GLA_HEADS = 4
GLA_KEY = 512
GLA_VAL = 1024
GLA_DK = 128
GLA_DV = 256
GATE_RANK = 16
GATE_TAU = 16.0
GLA_CHUNK = 64
D_FF = 2816
N_EXPERTS = 8
D_FF_EXPERT = 3584
ROPE_THETA = 10000.0
EPS = 1e-6

LANE = 128
COL_BLOCK = 512
P_QA, P_KA, P_VA = 0, 1536, 3072
P_QB, P_KB, P_GLR, P_VB, P_RB, P_GA, P_GB = 4608, 5120, 5632, 6144, 7168, 8192, 9216
P_WIDTH = 10240
N_ROPE_BLOCKS = (P_VA - P_QA) // COL_BLOCK
N_Q_BLOCKS = A_WIDTH // COL_BLOCK
VMEM_LIMIT = 56 * 1024 * 1024


def _cparams(sem):
    return pltpu.CompilerParams(dimension_semantics=sem, vmem_limit_bytes=VMEM_LIMIT)


def _rms(xf, g):
    return xf * lax.rsqrt(jnp.mean(xf * xf, axis=-1, keepdims=True) + EPS) * g


def _dot(a, b):
    return jnp.dot(a, b, preferred_element_type=F32)


def _dot_nt(a, b):
    return lax.dot_general(a, b, (((1,), (1,)), ((), ())), preferred_element_type=F32)


def _dot_tn(a, b):
    return lax.dot_general(a, b, (((0,), (0,)), ((), ())), preferred_element_type=F32)


def _div_pow2(x, d):
    assert d & (d - 1) == 0
    return lax.shift_right_logical(x, int(d).bit_length() - 1)


def _mod_pow2(x, d):
    assert d & (d - 1) == 0
    return x & (d - 1)


def _split_dot(a_f32, b_bf16, dot=_dot):
    hi = a_f32.astype(BF16)
    lo = (a_f32 - hi.astype(F32)).astype(BF16)
    return dot(hi, b_bf16) + dot(lo, b_bf16)


def _inproj_kernel(x_ref, g_ref, w_ref, cos_ref, sin_ref, o_ref, h_ref):
    j = pl.program_id(1)

    @pl.when(j == 0)
    def _():
        h_ref[...] = _rms(x_ref[...].astype(F32), g_ref[...]).astype(BF16)

    acc = _dot(h_ref[...], w_ref[...])

    @pl.when(j >= N_ROPE_BLOCKS)
    def _():
        o_ref[...] = acc.astype(o_ref.dtype)

    @pl.when(j < N_ROPE_BLOCKS)
    def _():
        cos = cos_ref[...]
        sin = sin_ref[...]
        first_half = _mod_pow2(lax.broadcasted_iota(jnp.int32, cos.shape, 1), HEAD_DIM) < HEAD_DIM // 2
        scale = jnp.where(j < N_Q_BLOCKS, HEAD_DIM ** -0.5, 1.0).astype(F32)
        for c in range(COL_BLOCK // LANE):
            xs = acc[:, c * LANE:(c + 1) * LANE]
            swapped = jnp.where(first_half, pltpu.roll(xs, LANE - HEAD_DIM // 2, 1),
                                pltpu.roll(xs, HEAD_DIM // 2, 1))
            o_ref[:, c * LANE:(c + 1) * LANE] = ((xs * cos + swapped * sin) * scale).astype(o_ref.dtype)


def _inproj(x, g, w, cos_t, sin_t, tm, out_dtype):
    n = x.shape[0]
    n_tab = cos_t.shape[0] // tm
    return pl.pallas_call(
        _inproj_kernel,
        grid=(n // tm, P_WIDTH // COL_BLOCK),
        in_specs=[
            pl.BlockSpec((tm, D_MODEL), lambda i, j: (i, 0)),
            pl.BlockSpec((1, D_MODEL), lambda i, j: (0, 0)),
            pl.BlockSpec((D_MODEL, COL_BLOCK), lambda i, j: (0, j)),
            pl.BlockSpec((tm, LANE), lambda i, j: (i % n_tab, 0)),
            pl.BlockSpec((tm, LANE), lambda i, j: (i % n_tab, 0)),
        ],
        out_specs=pl.BlockSpec((tm, COL_BLOCK), lambda i, j: (i, j)),
        out_shape=jax.ShapeDtypeStruct((n, P_WIDTH), out_dtype),
        scratch_shapes=[pltpu.VMEM((tm, D_MODEL), BF16)],
        compiler_params=_cparams(("parallel", "arbitrary")),
        name="inproj",
    )(x, g, w, cos_t, sin_t)


def _attn_prompt_kernel(q_ref, kp_ref, kc_ref, vp_ref, vc_ref, o_ref, lse_ref, *, span):
    n = pl.program_id(2)
    qi = lax.broadcasted_iota(jnp.int32, (Q_BLOCK, 2 * Q_BLOCK), 0)
    kj = lax.broadcasted_iota(jnp.int32, (Q_BLOCK, 2 * Q_BLOCK), 1)
    dist = Q_BLOCK + qi - kj
    mask = (dist >= 0) & (dist <= span) & ((n > 0) | (kj >= Q_BLOCK))
    lane = lax.broadcasted_iota(jnp.int32, (Q_BLOCK, LANE), 1)
    lse_all = jnp.zeros((Q_BLOCK, LANE), F32)
    for h in range(HEADS_PER_GROUP):
        sl = slice(h * HEAD_DIM, (h + 1) * HEAD_DIM)
        q = q_ref[:, sl]
        k = jnp.concatenate([kp_ref[:, sl], kc_ref[:, sl]], axis=0)
        v = jnp.concatenate([vp_ref[:, sl], vc_ref[:, sl]], axis=0)
        s = jnp.where(mask, _dot_nt(q, k), -jnp.inf)
        m = jnp.max(s, axis=-1, keepdims=True)
        p = jnp.exp(s - m)
        l = jnp.sum(p, axis=-1, keepdims=True)
        o = _dot(p.astype(BF16), v) / l
        o_ref[:, sl] = o.astype(o_ref.dtype)
        lse_all = jnp.where(lane == h, m + jnp.log(l), lse_all)
    lse_ref[...] = lse_all


def _attn_prompt(p_arr, batch, seq, group):
    window, dil = DSWA_GROUPS[group]
    span = window // dil
    n_rows = p_arr.shape[0]
    nb = seq // dil // Q_BLOCK
    cpb = P_WIDTH // COL_BLOCK
    pv = p_arr.reshape(n_rows // dil, dil * P_WIDTH)
    q_col, k_col, v_col = (off // COL_BLOCK + group for off in (P_QA, P_KA, P_VA))

    def spec(col, prev):
        if prev:
            return pl.BlockSpec((Q_BLOCK, COL_BLOCK), lambda b, r, n: (b * nb + jnp.maximum(n - 1, 0), r * cpb + col))
        return pl.BlockSpec((Q_BLOCK, COL_BLOCK), lambda b, r, n: (b * nb + n, r * cpb + col))

    o, lse = pl.pallas_call(
        functools.partial(_attn_prompt_kernel, span=span),
        grid=(batch, dil, nb),
        in_specs=[spec(q_col, False), spec(k_col, True), spec(k_col, False), spec(v_col, True), spec(v_col, False)],
        out_specs=[
            pl.BlockSpec((Q_BLOCK, A_OUT), lambda b, r, n: (b * nb + n, r)),
            pl.BlockSpec((Q_BLOCK, LANE), lambda b, r, n: (b * nb + n, r)),
        ],
        out_shape=[
            jax.ShapeDtypeStruct((n_rows // dil, dil * A_OUT), BF16),
            jax.ShapeDtypeStruct((n_rows // dil, dil * LANE), F32),
        ],
        compiler_params=_cparams(("parallel", "parallel", "arbitrary")),
        name=f"attn_prompt_g{group}",
    )(pv, pv, pv, pv, pv)
    return o.reshape(n_rows, A_OUT), lse.reshape(n_rows, LANE)


def _attn_sample_kernel(q_ref, k_ref, v_ref, c_ref, o_ref, lse_ref, cout_ref, *, dil, span):
    t_new = q_ref.shape[0]
    lb = c_ref.shape[0]
    rows = HEADS_PER_GROUP * t_new
    q = q_ref[...].astype(F32)
    k_new = k_ref[...].astype(F32)
    v_new = v_ref[...].astype(F32)

    cout_ref[0:lb - t_new, :] = c_ref[t_new:lb, :]
    cout_ref[lb - t_new:lb, 0:A_OUT] = k_new.astype(cout_ref.dtype)
    cout_ref[lb - t_new:lb, A_OUT:2 * A_OUT] = v_new.astype(cout_ref.dtype)

    q_rep = jnp.concatenate([q] * HEADS_PER_GROUP, axis=0)
    row_head = _div_pow2(lax.broadcasted_iota(jnp.int32, (rows, A_OUT), 0), t_new)
    col_head = _div_pow2(lax.broadcasted_iota(jnp.int32, (rows, A_OUT), 1), HEAD_DIM)
    diag = row_head == col_head
    q_bd = jnp.where(diag, q_rep, 0.0).astype(BF16)

    k_buf = c_ref[:, 0:A_OUT].astype(BF16)
    v_buf = c_ref[:, A_OUT:2 * A_OUT].astype(BF16)
    s_buf = _dot_nt(q_bd, k_buf)
    s_new = _dot_nt(q_bd, k_new.astype(BF16))

    tok_b = _mod_pow2(lax.broadcasted_iota(jnp.int32, (rows, lb), 0), t_new)
    pos_b = lax.broadcasted_iota(jnp.int32, (rows, lb), 1)
    dist_b = lb + tok_b - pos_b
    ok_b = (_mod_pow2(dist_b, dil) == 0) & (dist_b <= span * dil)
    tok_n = _mod_pow2(lax.broadcasted_iota(jnp.int32, (rows, t_new), 0), t_new)
    pos_n = lax.broadcasted_iota(jnp.int32, (rows, t_new), 1)
    dist_n = tok_n - pos_n
    ok_n = (dist_n >= 0) & (_mod_pow2(dist_n, dil) == 0) & (dist_n <= span * dil)
    s_buf = jnp.where(ok_b, s_buf, -jnp.inf)
    s_new = jnp.where(ok_n, s_new, -jnp.inf)

    m = jnp.maximum(jnp.max(s_buf, axis=-1, keepdims=True), jnp.max(s_new, axis=-1, keepdims=True))
    p_buf = jnp.exp(s_buf - m)
    p_new = jnp.exp(s_new - m)
    l = jnp.sum(p_buf, axis=-1, keepdims=True) + jnp.sum(p_new, axis=-1, keepdims=True)
    o_full = (_dot(p_buf.astype(BF16), v_buf) + _dot(p_new.astype(BF16), v_new.astype(BF16))) / l
    o_full = jnp.where(diag, o_full, 0.0)
    lse = m + jnp.log(l)

    o = o_full[0:t_new, :]
    lane = lax.broadcasted_iota(jnp.int32, (t_new, LANE), 1)
    lse_all = jnp.zeros((t_new, LANE), F32)
    for h in range(HEADS_PER_GROUP):
        if h > 0:
            o = o + o_full[h * t_new:(h + 1) * t_new, :]
        lse_all = jnp.where(lane == h, lse[h * t_new:(h + 1) * t_new, :], lse_all)
    o_ref[...] = o.astype(o_ref.dtype)
    lse_ref[...] = lse_all


def _attn_sample(p_arr, cache, layer, group, t_new):
    window, dil = DSWA_GROUPS[group]
    n_rows = p_arr.shape[0]
    batch = n_rows // t_new
    lb = cache.shape[2]
    q_col, k_col, v_col = (off // COL_BLOCK + group for off in (P_QA, P_KA, P_VA))

    def pspec(col):
        return pl.BlockSpec((t_new, COL_BLOCK), lambda b: (b, col))

    return pl.pallas_call(
        functools.partial(_attn_sample_kernel, dil=dil, span=window // dil),
        grid=(batch,),
        in_specs=[pspec(q_col), pspec(k_col), pspec(v_col),
                  pl.BlockSpec((None, None, lb, 2 * A_OUT), lambda b: (layer, b, 0, 0))],
        out_specs=[
            pl.BlockSpec((t_new, A_OUT), lambda b: (b, 0)),
            pl.BlockSpec((t_new, LANE), lambda b: (b, 0)),
            pl.BlockSpec((None, lb, 2 * A_OUT), lambda b: (b, 0, 0)),
        ],
        out_shape=[
            jax.ShapeDtypeStruct((n_rows, A_OUT), F32),
            jax.ShapeDtypeStruct((n_rows, LANE), F32),
            jax.ShapeDtypeStruct((batch, lb, 2 * A_OUT), F32),
        ],
        compiler_params=_cparams(("parallel",)),
        name=f"attn_sample_g{group}",
    )(p_arr, p_arr, p_arr, cache)


def _gla_kernel(*refs, chunk, n_chunks, has_s0):
    if has_s0:
        q_ref, k_ref, v_ref, r_ref, glr_ref, wa_ref, ba_ref, gg_ref, s0_ref, o_ref, sout_ref, s_ref = refs
    else:
        q_ref, k_ref, v_ref, r_ref, glr_ref, wa_ref, ba_ref, gg_ref, o_ref, sout_ref, s_ref = refs
    step = pl.program_id(2)

    @pl.when(step == 0)
    def _():
        if has_s0:
            s_ref[...] = s0_ref[...].astype(F32)
        else:
            s_ref[...] = jnp.zeros_like(s_ref)

    ri = lax.broadcasted_iota(jnp.int32, (chunk, chunk), 0)
    ci = lax.broadcasted_iota(jnp.int32, (chunk, chunk), 1)
    causal = ri >= ci
    tri = causal.astype(BF16)
    ones = jnp.ones((chunk, GLA_DV), BF16)
    wa = wa_ref[...]
    ba = ba_ref[...]
    gg = gg_ref[...]

    for c in range(n_chunks):
        rs = slice(c * chunk, (c + 1) * chunk)
        q = q_ref[rs, :].astype(F32) * (GLA_DK ** -0.5)
        k = k_ref[rs, :].astype(F32)
        v = v_ref[rs, :].astype(BF16)
        gate = _dot(glr_ref[rs, :].astype(BF16), wa) + ba
        log_a = jax.nn.log_sigmoid(gate) * (1.0 / GATE_TAU)
        cum = _cumsum(tri, log_a)
        last = cum[chunk - 1:chunk, :]
        last_t = _colsum(log_a, ones)
        q_in = (q * jnp.exp(cum)).astype(BF16)
        k_in = (k * jnp.exp(-cum)).astype(BF16)
        k_out = (k * jnp.exp(last - cum)).astype(BF16)
        att = jnp.where(causal, _dot_nt(q_in, k_in), 0.0).astype(BF16)
        state = s_ref[...]
        o = _dot(att, v) + _dot(q_in, state.astype(BF16))
        s_ref[...] = jnp.exp(last_t) * state + _dot_tn(k_out, v)
        y = _rms(o, gg) * jax.nn.silu(r_ref[rs, :].astype(F32))
        o_ref[rs, :] = y.astype(o_ref.dtype)

    @pl.when(step == pl.num_programs(2) - 1)
    def _():
        sout_ref[...] = s_ref[...]


def _cumsum(tri, x):
    hi = x.astype(BF16)
    lo = (x - hi.astype(F32)).astype(BF16)
    return _dot(tri, hi) + _dot(tri, lo)


def _colsum(x, ones):
    hi = x.astype(BF16)
    lo = (x - hi.astype(F32)).astype(BF16)
    return _dot_tn(hi, ones) + _dot_tn(lo, ones)


def _gla(p_arr, wa, ba, gg, s0, layer, batch, seq, chunk, n_chunks, out_dtype):
    n_rows = p_arr.shape[0]
    lblk = chunk * n_chunks
    steps = seq // lblk

    def pspec(width, off):
        return pl.BlockSpec((lblk, width), lambda b, h, n: (b * steps + n, off // width + h))

    in_specs = [
        pspec(GLA_DK, P_QB), pspec(GLA_DK, P_KB), pspec(GLA_DV, P_VB), pspec(GLA_DV, P_RB),
        pl.BlockSpec((lblk, LANE), lambda b, h, n: (b * steps + n, P_GLR // LANE)),
        pl.BlockSpec((LANE, GLA_DK), lambda b, h, n: (0, h)),
        pl.BlockSpec((1, GLA_DK), lambda b, h, n: (0, h)),
        pl.BlockSpec((1, GLA_DV), lambda b, h, n: (0, h)),
    ]
    args = [p_arr, p_arr, p_arr, p_arr, p_arr, wa, ba, gg]
    if s0 is not None:
        in_specs.append(pl.BlockSpec((None, None, None, GLA_DK, GLA_DV), lambda b, h, n: (layer, b, h, 0, 0)))
        args.append(s0)
    return pl.pallas_call(
        functools.partial(_gla_kernel, chunk=chunk, n_chunks=n_chunks, has_s0=s0 is not None),
        grid=(batch, GLA_HEADS, steps),
        in_specs=in_specs,
        out_specs=[
            pl.BlockSpec((lblk, GLA_DV), lambda b, h, n: (b * steps + n, h)),
            pl.BlockSpec((None, None, GLA_DK, GLA_DV), lambda b, h, n: (b, h, 0, 0)),
        ],
        out_shape=[
            jax.ShapeDtypeStruct((n_rows, GLA_VAL), out_dtype),
            jax.ShapeDtypeStruct((batch, GLA_HEADS, GLA_DK, GLA_DV), F32),
        ],
        scratch_shapes=[pltpu.VMEM((GLA_DK, GLA_DV), F32)],
        compiler_params=_cparams(("parallel", "parallel", "arbitrary")),
        name="gla",
    )(*args)


def _merge_kernel(x_ref, o1_ref, o2_ref, o3_ref, l1_ref, l2_ref, l3_ref, ob_ref, ga_ref, gb_ref,
                  wpa_ref, wpb_ref, wo_ref, e_ref, out_ref):
    lses = [l1_ref[...], l2_ref[...], l3_ref[...]]
    m = jnp.maximum(jnp.maximum(lses[0], lses[1]), lses[2])
    ws = [jnp.exp(l - m) for l in lses]
    inv = 1.0 / (ws[0] + ws[1] + ws[2])
    expand = e_ref[...]
    o_a = jnp.zeros(o1_ref.shape, F32)
    for w, o_ref in zip(ws, (o1_ref, o2_ref, o3_ref)):
        o_a = o_a + _split_dot(w * inv, expand) * o_ref[...].astype(F32)
    ya = _dot(o_a.astype(BF16), wpa_ref[...])
    yb = _dot(ob_ref[...].astype(BF16), wpb_ref[...])
    merged = jax.nn.sigmoid(ga_ref[...].astype(F32)) * ya + jax.nn.sigmoid(gb_ref[...].astype(F32)) * yb
    out_ref[...] = x_ref[...] + _dot(merged.astype(BF16), wo_ref[...])


def _merge(x, outs, lses, ob, p_arr, wpa, wpb, wo, expand, tm):
    n = x.shape[0]

    def rows(width, col=0):
        return pl.BlockSpec((tm, width), lambda i: (i, col))

    def full(a):
        return pl.BlockSpec(a.shape, lambda i: (0, 0))

    return pl.pallas_call(
        _merge_kernel,
        grid=(n // tm,),
        in_specs=[rows(D_MODEL)] + [rows(A_OUT)] * 3 + [rows(LANE)] * 3 + [rows(GLA_VAL)]
                 + [rows(D_MODEL, P_GA // D_MODEL), rows(D_MODEL, P_GB // D_MODEL)]
                 + [full(wpa), full(wpb), full(wo), full(expand)],
        out_specs=rows(D_MODEL),
        out_shape=jax.ShapeDtypeStruct((n, D_MODEL), F32),
        compiler_params=_cparams(("parallel",)),
        name="merge",
    )(x, *outs, *lses, ob, p_arr, p_arr, wpa, wpb, wo, expand)


def _ffn_kernel(x_ref, g_ref, wg_ref, wu_ref, wd_ref, o_ref, h_ref, acc_ref):
    f = pl.program_id(1)

    @pl.when(f == 0)
    def _():
        h_ref[...] = _rms(x_ref[...], g_ref[...]).astype(BF16)
        acc_ref[...] = x_ref[...]

    h = h_ref[...]
    mid = jax.nn.silu(_dot(h, wg_ref[...])) * _dot(h, wu_ref[...])
    acc_ref[...] += _dot(mid.astype(BF16), wd_ref[...])

    @pl.when(f == pl.num_programs(1) - 1)
    def _():
        o_ref[...] = acc_ref[...]


def _ffn(x, g, wg, wu, wd, tm, tf):
    n = x.shape[0]
    d_ff = wg.shape[1]
    return pl.pallas_call(
        _ffn_kernel,
        grid=(n // tm, d_ff // tf),
        in_specs=[
            pl.BlockSpec((tm, D_MODEL), lambda i, f: (i, 0)),
            pl.BlockSpec((1, D_MODEL), lambda i, f: (0, 0)),
            pl.BlockSpec((D_MODEL, tf), lambda i, f: (0, f)),
            pl.BlockSpec((D_MODEL, tf), lambda i, f: (0, f)),
            pl.BlockSpec((tf, D_MODEL), lambda i, f: (f, 0)),
        ],
        out_specs=pl.BlockSpec((tm, D_MODEL), lambda i, f: (i, 0)),
        out_shape=jax.ShapeDtypeStruct((n, D_MODEL), F32),
        scratch_shapes=[pltpu.VMEM((tm, D_MODEL), BF16), pltpu.VMEM((tm, D_MODEL), F32)],
        compiler_params=_cparams(("parallel", "arbitrary")),
        name="ffn",
    )(x, g, wg, wu, wd)


def _router_kernel(x_ref, g_ref, wr_ref, comb_ref):
    h = _rms(x_ref[...], g_ref[...])
    logits = jnp.dot(h, wr_ref[...], preferred_element_type=F32, precision=lax.Precision.HIGHEST)
    lane = lax.broadcasted_iota(jnp.int32, logits.shape, 1)
    logits = jnp.where(lane < N_EXPERTS, logits, -jnp.inf)
    v1 = jnp.max(logits, axis=-1, keepdims=True)
    i1 = jnp.min(jnp.where(logits == v1, lane, LANE), axis=-1, keepdims=True)
    rest = jnp.where(lane == i1, -jnp.inf, logits)
    v2 = jnp.max(rest, axis=-1, keepdims=True)
    i2 = jnp.min(jnp.where(rest == v2, lane, LANE), axis=-1, keepdims=True)
    e2 = jnp.exp(v2 - v1)
    g1 = 1.0 / (1.0 + e2)
    g2 = e2 / (1.0 + e2)
    comb_ref[...] = jnp.where(lane == i1, g1, 0.0) + jnp.where(lane == i2, g2, 0.0)


def _router(x, g, wr, tm):
    n = x.shape[0]
    return pl.pallas_call(
        _router_kernel,
        grid=(n // tm,),
        in_specs=[
            pl.BlockSpec((tm, D_MODEL), lambda i: (i, 0)),
            pl.BlockSpec((1, D_MODEL), lambda i: (0, 0)),
            pl.BlockSpec((D_MODEL, LANE), lambda i: (0, 0)),
        ],
        out_specs=pl.BlockSpec((tm, LANE), lambda i: (i, 0)),
        out_shape=jax.ShapeDtypeStruct((n, LANE), F32),
        compiler_params=_cparams(("parallel",)),
        name="router",
    )(x, g, wr)


def _moe_dense_kernel(x_ref, g_ref, comb_ref, wg_ref, wu_ref, wd_ref, o_ref, h_ref, acc_ref):
    e = pl.program_id(1)
    f = pl.program_id(2)

    @pl.when((e == 0) & (f == 0))
    def _():
        h_ref[...] = _rms(x_ref[...], g_ref[...]).astype(BF16)
        acc_ref[...] = x_ref[...]

    h = h_ref[...]
    mid = jax.nn.silu(_dot(h, wg_ref[...])) * _dot(h, wu_ref[...])
    lane = lax.broadcasted_iota(jnp.int32, comb_ref.shape, 1)
    weight = jnp.sum(jnp.where(lane == e, comb_ref[...], 0.0), axis=-1, keepdims=True)
    acc_ref[...] += weight * _dot(mid.astype(BF16), wd_ref[...])

    @pl.when((e == pl.num_programs(1) - 1) & (f == pl.num_programs(2) - 1))
    def _():
        o_ref[...] = acc_ref[...]


def _moe_dense(x, g, comb, wg, wu, wd, tm, tf):
    n = x.shape[0]
    n_exp, _, d_ff = wg.shape
    return pl.pallas_call(
        _moe_dense_kernel,
        grid=(n // tm, n_exp, d_ff // tf),
        in_specs=[
            pl.BlockSpec((tm, D_MODEL), lambda i, e, f: (i, 0)),
            pl.BlockSpec((1, D_MODEL), lambda i, e, f: (0, 0)),
            pl.BlockSpec((tm, LANE), lambda i, e, f: (i, 0)),
            pl.BlockSpec((None, D_MODEL, tf), lambda i, e, f: (e, 0, f)),
            pl.BlockSpec((None, D_MODEL, tf), lambda i, e, f: (e, 0, f)),
            pl.BlockSpec((None, tf, D_MODEL), lambda i, e, f: (e, f, 0)),
        ],
        out_specs=pl.BlockSpec((tm, D_MODEL), lambda i, e, f: (i, 0)),
        out_shape=jax.ShapeDtypeStruct((n, D_MODEL), F32),
        scratch_shapes=[pltpu.VMEM((tm, D_MODEL), BF16), pltpu.VMEM((tm, D_MODEL), F32)],
        compiler_params=_cparams(("parallel", "arbitrary", "arbitrary")),
        name="moe",
    )(x, g, comb, wg, wu, wd)


def _final_norm_kernel(x_ref, g_ref, o_ref):
    o_ref[...] = _rms(x_ref[...], g_ref[...])


def _final_norm(x, g, tm):
    n = x.shape[0]
    return pl.pallas_call(
        _final_norm_kernel,
        grid=(n // tm,),
        in_specs=[pl.BlockSpec((tm, D_MODEL), lambda i: (i, 0)), pl.BlockSpec((1, D_MODEL), lambda i: (0, 0))],
        out_specs=pl.BlockSpec((tm, D_MODEL), lambda i: (i, 0)),
        out_shape=jax.ShapeDtypeStruct((n, D_MODEL), F32),
        compiler_params=_cparams(("parallel",)),
        name="final_norm",
    )(x, g)


def _rope_tables(pos):
    half = HEAD_DIM // 2
    inv = ROPE_THETA ** (-jnp.arange(half, dtype=F32) / half)
    ang = pos.astype(F32)[:, None] * inv[None, :]
    cos = jnp.cos(ang)
    sin = jnp.sin(ang)
    cos_h = jnp.concatenate([cos, cos], axis=-1)
    sin_h = jnp.concatenate([-sin, sin], axis=-1)
    reps = LANE // HEAD_DIM
    return jnp.tile(cos_h, (1, reps)), jnp.tile(sin_h, (1, reps))


def _layer_weights(l, w_in, w_gate_a2, b_gate_a, g_gla, w_branch_a, w_branch_b, w_out):
    w = w_in[l]
    offs = np.cumsum((A_WIDTH, A_WIDTH, A_WIDTH, GLA_KEY, GLA_KEY, GLA_VAL, GLA_VAL, GATE_RANK, D_MODEL, D_MODEL))
    qa, ka, va, qb, kb, vb, rb, glr, ga, gb = jnp.split(w, [int(o) for o in offs[:-1]], axis=1)
    glr = jnp.pad(glr, ((0, 0), (0, P_VB - P_GLR - GATE_RANK)))
    w_perm = jnp.concatenate([qa, ka, va, qb, kb, glr, vb, rb, ga, gb], axis=1).astype(BF16)
    wa = jnp.pad(w_gate_a2[l], ((0, LANE - GATE_RANK), (0, 0))).astype(BF16)
    return dict(w_in=w_perm, wa=wa, ba=b_gate_a[l][None, :], gg=g_gla[l][None, :],
                wpa=w_branch_a[l].astype(BF16), wpb=w_branch_b[l].astype(BF16), wo=w_out[l].astype(BF16))


def _head_expand():
    e = np.zeros((LANE, A_OUT), np.float32)
    for h in range(HEADS_PER_GROUP):
        e[h, h * HEAD_DIM:(h + 1) * HEAD_DIM] = 1.0
    return jnp.asarray(e, BF16)


def _trunk(x, pos_table, caches, gla_state, lw, ffw, g_mix, g_ffn, g_final, batch, seq, tm, p_dtype):
    n = batch * seq
    cos_t, sin_t = pos_table
    expand = _head_expand()
    bufs_out = [[] for _ in range(N_GROUPS)]
    states_out = []
    for l in range(DEPTH):
        w = lw[l]
        p_arr = _inproj(x, g_mix[l][None, :], w["w_in"], cos_t, sin_t, tm, p_dtype)
        outs, lses = [], []
        for g, (window, _) in enumerate(DSWA_GROUPS):
            if caches is None:
                o, lse = _attn_prompt(p_arr, batch, seq, g)
                keep = min(window, seq)
                p3 = p_arr.reshape(batch, seq, P_WIDTH)
                kg = p3[:, seq - keep:, P_KA + g * A_OUT:P_KA + (g + 1) * A_OUT]
                vg = p3[:, seq - keep:, P_VA + g * A_OUT:P_VA + (g + 1) * A_OUT]
                new_buf = jnp.stack([kg, vg], axis=2).astype(F32).reshape(batch, keep, 2, HEADS_PER_GROUP, HEAD_DIM)
            else:
                o, lse, cout = _attn_sample(p_arr, caches[g], l, g, seq)
                new_buf = cout.reshape(batch, cout.shape[1], 2, HEADS_PER_GROUP, HEAD_DIM)
            outs.append(o)
            lses.append(lse)
            bufs_out[g].append(new_buf)
        if gla_state is None:
            ob, s_new = _gla(p_arr, w["wa"], w["ba"], w["gg"], None, l, batch, seq, GLA_CHUNK, 8, BF16)
        else:
            ob, s_new = _gla(p_arr, w["wa"], w["ba"], w["gg"], gla_state, l, batch, seq, seq, 1, F32)
        states_out.append(s_new)
        x = _merge(x, outs, lses, ob, p_arr, w["wpa"], w["wpb"], w["wo"], expand, min(tm, 512))
        if l % 2 == 0:
            i = l // 2
            x = _ffn(x, g_ffn[l][None, :], ffw["wg"][i], ffw["wu"][i], ffw["wd"][i], tm, D_FF // 2)
        else:
            i = l // 2
            comb = _router(x, g_ffn[l][None, :], ffw["wr"][i], tm)
            x = _moe_dense(x, g_ffn[l][None, :], comb, ffw["eg"][i], ffw["eu"][i], ffw["ed"][i], tm, D_FF_EXPERT // 2)
    y = _final_norm(x, g_final[None, :], tm)
    return (y, jnp.stack(bufs_out[0]), jnp.stack(bufs_out[1]), jnp.stack(bufs_out[2]), jnp.stack(states_out))


def kernel(x_prompt, x_sample, cache_kv_w128, cache_kv_w512, cache_kv_w2048, state_gla, g_mix, w_in, w_gate_a2, b_gate_a, g_gla, w_branch_a, w_branch_b, w_out, g_ffn, w_ff_gate, w_ff_up, w_ff_down, w_router, w_exp_gate, w_exp_up, w_exp_down, g_final):
    batch, seq, _ = x_prompt.shape
    dec_batch, dec_seq, _ = x_sample.shape
    assert seq % (DSWA_GROUPS[-1][1] * Q_BLOCK) == 0 and seq % (8 * GLA_CHUNK) == 0
    assert dec_seq % GLA_CHUNK != 0 and dec_seq % 8 == 0
    assert all(c.shape[2] == w for c, (w, _) in zip((cache_kv_w128, cache_kv_w512, cache_kv_w2048), DSWA_GROUPS))

    lw = [_layer_weights(l, w_in, w_gate_a2, b_gate_a, g_gla, w_branch_a, w_branch_b, w_out) for l in range(DEPTH)]
    ffw = dict(wg=w_ff_gate.astype(BF16), wu=w_ff_up.astype(BF16), wd=w_ff_down.astype(BF16),
               wr=jnp.pad(w_router, ((0, 0), (0, 0), (0, LANE - N_EXPERTS))),
               eg=w_exp_gate.astype(BF16), eu=w_exp_up.astype(BF16), ed=w_exp_down.astype(BF16))

    tm_p = 1024
    y_p, kv128_p, kv512_p, kv2048_p, gla_p = _trunk(
        x_prompt.reshape(batch * seq, D_MODEL), _rope_tables(jnp.arange(seq, dtype=jnp.int32)), None, None,
        lw, ffw, g_mix, g_ffn, g_final, batch, seq, tm_p, BF16)

    n_s = dec_batch * dec_seq
    pos_s = jnp.tile(PAST_LEN + jnp.arange(dec_seq, dtype=jnp.int32), dec_batch)
    caches = [c.reshape(DEPTH, dec_batch, c.shape[2], 2 * A_OUT) for c in (cache_kv_w128, cache_kv_w512, cache_kv_w2048)]
    y_s, kv128_s, kv512_s, kv2048_s, gla_s = _trunk(
        x_sample.reshape(n_s, D_MODEL), _rope_tables(pos_s), caches, state_gla,
        lw, ffw, g_mix, g_ffn, g_final, dec_batch, dec_seq, n_s, F32)

    return (y_p.reshape(batch, seq, D_MODEL), y_s.reshape(dec_batch, dec_seq, D_MODEL),
            kv128_p, kv512_p, kv2048_p, gla_p, kv128_s, kv512_s, kv2048_s, gla_s)
```

```python
import functools

import jax
import jax.numpy as jnp
import numpy as np
from jax import lax
from jax.experimental import pallas as pl
from jax.experimental.pallas import tpu as pltpu

F32 = jnp.float32
BF16 = jnp.bfloat16

D_MODEL = 1024
DEPTH = 2
PAST_LEN = 16384
HEAD_DIM = 64
DSWA_GROUPS = ((128, 1), (512, 4), (2048, 16))
N_GROUPS = 3
HEADS_PER_GROUP = 8
A_WIDTH = N_GROUPS * HEADS_PER_GROUP * HEAD_DIM
A_OUT = HEADS_PER_GROUP * HEAD_DIM
Q_BLOCK = 128
GLA_HEADS = 4
GLA_KEY = 512
GLA_VAL = 1024
GLA_DK = 128
GLA_DV = 256
GATE_RANK = 16
GATE_TAU = 16.0
GLA_CHUNK = 64
D_FF = 2816
N_EXPERTS = 8
D_FF_EXPERT = 3584
ROPE_THETA = 10000.0
EPS = 1e-6

LANE = 128
COL_BLOCK = 512
R_QB, R_KB, R_VB, R_RB, R_GA, R_GB, R_GLR, N_REST = 0, 1, 2, 4, 6, 8, 10, 11
VMEM_LIMIT = 56 * 1024 * 1024


def _cparams(sem):
    return pltpu.CompilerParams(dimension_semantics=sem, vmem_limit_bytes=VMEM_LIMIT)


def _rms(xf, g):
    return xf * lax.rsqrt(jnp.mean(xf * xf, axis=-1, keepdims=True) + EPS) * g


def _dot(a, b):
    return jnp.dot(a, b, preferred_element_type=F32)


def _dot_nt(a, b):
    return lax.dot_general(a, b, (((1,), (1,)), ((), ())), preferred_element_type=F32)


def _dot_tn(a, b):
    return lax.dot_general(a, b, (((0,), (0,)), ((), ())), preferred_element_type=F32)


def _div_pow2(x, d):
    assert d & (d - 1) == 0
    return lax.shift_right_logical(x, int(d).bit_length() - 1)


def _mod_pow2(x, d):
    assert d & (d - 1) == 0
    return x & (d - 1)


def _split_dot(a_f32, b_bf16, dot=_dot):
    hi = a_f32.astype(BF16)
    lo = (a_f32 - hi.astype(F32)).astype(BF16)
    return dot(hi, b_bf16) + dot(lo, b_bf16)


def _inproj_attn_kernel(x_ref, g_ref, w_ref, cos_ref, sin_ref, o_ref, h_ref, *scratch, dil):
    @pl.when(pl.program_id(1) == 0)
    def _():
        h_ref[...] = _rms(x_ref[...].astype(F32), g_ref[...]).astype(BF16)

    acc = _dot(h_ref[...], w_ref[...])
    cos = cos_ref[...]
    sin = sin_ref[...]
    first_half = _mod_pow2(lax.broadcasted_iota(jnp.int32, cos.shape, 1), HEAD_DIM) < HEAD_DIM // 2
    for c in range(COL_BLOCK // LANE):
        cs = slice(c * LANE, (c + 1) * LANE)
        xs = acc[:, cs]
        swapped = jnp.where(first_half, pltpu.roll(xs, LANE - HEAD_DIM // 2, 1), pltpu.roll(xs, HEAD_DIM // 2, 1))
        y = xs * cos + swapped * sin
        if dil == 1:
            o_ref[0, :, cs] = y.astype(o_ref.dtype)
        else:
            scratch[0][c] = y
    if dil > 1:
        rows = x_ref.shape[0] // dil
        for r in range(dil):
            for c in range(COL_BLOCK // LANE):
                o_ref[r, :, c * LANE:(c + 1) * LANE] = scratch[0][c, pl.ds(r, rows, stride=dil), :].astype(o_ref.dtype)


def _inproj_attn(x, g, w, cos_t, sin_t, group, batch, seq, dil, tm, out_dtype):
    tiles = seq // tm
    scratch = [pltpu.VMEM((tm, D_MODEL), BF16)]
    if dil > 1:
        scratch.append(pltpu.VMEM((COL_BLOCK // LANE, tm, LANE), F32))
    return pl.pallas_call(
        functools.partial(_inproj_attn_kernel, dil=dil),
        grid=(batch * tiles, 3),
        in_specs=[
            pl.BlockSpec((tm, D_MODEL), lambda i, j: (i, 0)),
            pl.BlockSpec((1, D_MODEL), lambda i, j: (0, 0)),
            pl.BlockSpec((None, None, D_MODEL, COL_BLOCK), lambda i, j: (group, j, 0, 0)),
            pl.BlockSpec((None, tm, LANE), lambda i, j: (j, i % tiles, 0)),
            pl.BlockSpec((None, tm, LANE), lambda i, j: (j, i % tiles, 0)),
        ],
        out_specs=pl.BlockSpec((None, None, dil, tm // dil, COL_BLOCK), lambda i, j: (j, i // tiles, 0, i % tiles, 0)),
        out_shape=jax.ShapeDtypeStruct((3, batch, dil, seq // dil, COL_BLOCK), out_dtype),
        scratch_shapes=scratch,
        compiler_params=_cparams(("parallel", "arbitrary")),
        name=f"inproj_attn_g{group}",
    )(x, g, w, cos_t, sin_t)


def _inproj_rest_kernel(x_ref, g_ref, w_ref, o_ref, h_ref):
    @pl.when(pl.program_id(1) == 0)
    def _():
        h_ref[...] = _rms(x_ref[...].astype(F32), g_ref[...]).astype(BF16)

    o_ref[...] = _dot(h_ref[...], w_ref[...]).astype(o_ref.dtype)


def _inproj_rest(x, g, w, tm, out_dtype):
    n = x.shape[0]
    return pl.pallas_call(
        _inproj_rest_kernel,
        grid=(n // tm, N_REST),
        in_specs=[
            pl.BlockSpec((tm, D_MODEL), lambda i, j: (i, 0)),
            pl.BlockSpec((1, D_MODEL), lambda i, j: (0, 0)),
            pl.BlockSpec((None, D_MODEL, COL_BLOCK), lambda i, j: (j, 0, 0)),
        ],
        out_specs=pl.BlockSpec((None, tm, COL_BLOCK), lambda i, j: (j, i, 0)),
        out_shape=jax.ShapeDtypeStruct((N_REST, n, COL_BLOCK), out_dtype),
        scratch_shapes=[pltpu.VMEM((tm, D_MODEL), BF16)],
        compiler_params=_cparams(("parallel", "arbitrary")),
        name="inproj_rest",
    )(x, g, w)


def _attn_prompt_kernel(q_ref, kp_ref, kc_ref, vp_ref, vc_ref, o_ref, lse_ref, *, span):
    n = pl.program_id(2)
    qi = lax.broadcasted_iota(jnp.int32, (Q_BLOCK, 2 * Q_BLOCK), 0)
    kj = lax.broadcasted_iota(jnp.int32, (Q_BLOCK, 2 * Q_BLOCK), 1)
    dist = Q_BLOCK + qi - kj
    mask = (dist >= 0) & (dist <= span) & ((n > 0) | (kj >= Q_BLOCK))
    lane = lax.broadcasted_iota(jnp.int32, (Q_BLOCK, LANE), 1)
    lse_all = jnp.zeros((Q_BLOCK, LANE), F32)
    for h in range(HEADS_PER_GROUP):
        sl = slice(h * HEAD_DIM, (h + 1) * HEAD_DIM)
        q = q_ref[:, sl]
        k = jnp.concatenate([kp_ref[:, sl], kc_ref[:, sl]], axis=0)
        v = jnp.concatenate([vp_ref[:, sl], vc_ref[:, sl]], axis=0)
        s = jnp.where(mask, _dot_nt(q, k), -jnp.inf)
        m = jnp.max(s, axis=-1, keepdims=True)
        p = jnp.exp(s - m)
        l = jnp.sum(p, axis=-1, keepdims=True)
        o = _dot(p.astype(BF16), v) / l
        o_ref[:, sl] = o.astype(o_ref.dtype)
        lse_all = jnp.where(lane == h, m + jnp.log(l), lse_all)
    lse_ref[...] = lse_all


def _attn_prompt(qkv, group):
    window, dil = DSWA_GROUPS[group]
    _, batch, _, sub_len, _ = qkv.shape
    nb = sub_len // Q_BLOCK

    def spec(part, prev):
        if prev:
            return pl.BlockSpec((None, None, None, Q_BLOCK, COL_BLOCK),
                                lambda b, r, n: (part, b, r, jnp.maximum(n - 1, 0), 0))
        return pl.BlockSpec((None, None, None, Q_BLOCK, COL_BLOCK), lambda b, r, n: (part, b, r, n, 0))

    return pl.pallas_call(
        functools.partial(_attn_prompt_kernel, span=window // dil),
        grid=(batch, dil, nb),
        in_specs=[spec(0, False), spec(1, True), spec(1, False), spec(2, True), spec(2, False)],
        out_specs=[
            pl.BlockSpec((None, None, Q_BLOCK, A_OUT), lambda b, r, n: (b, r, n, 0)),
            pl.BlockSpec((None, None, Q_BLOCK, LANE), lambda b, r, n: (b, r, n, 0)),
        ],
        out_shape=[
            jax.ShapeDtypeStruct((batch, dil, sub_len, A_OUT), BF16),
            jax.ShapeDtypeStruct((batch, dil, sub_len, LANE), F32),
        ],
        compiler_params=_cparams(("parallel", "parallel", "arbitrary")),
        name=f"attn_prompt_g{group}",
    )(qkv, qkv, qkv, qkv, qkv)


SHIFT_ROWS = 128


def _attn_sample_kernel(*refs, dil, span, aliased):
    if aliased:
        q_ref, k_ref, v_ref, c_ref, _, o_ref, lse_ref, cout_ref = refs
    else:
        q_ref, k_ref, v_ref, c_ref, o_ref, lse_ref, cout_ref = refs
    t_new = q_ref.shape[0]
    lb = c_ref.shape[2]
    rows = HEADS_PER_GROUP * t_new
    q = q_ref[...].astype(F32)
    k_new = k_ref[...].astype(F32)
    v_new = v_ref[...].astype(F32)

    lane = lax.broadcasted_iota(jnp.int32, (SHIFT_ROWS, LANE), 1)
    zeros = jnp.zeros((LANE - t_new, A_OUT), F32)
    for kv, new in enumerate((k_new, v_new)):
        new_t = pltpu.roll(jnp.concatenate([new, zeros], axis=0).T, LANE - t_new, 1)
        for rb in range(A_OUT // SHIFT_ROWS):
            rs = slice(rb * SHIFT_ROWS, (rb + 1) * SHIFT_ROWS)
            shifted = pltpu.roll(c_ref[kv, rs, :], lb - t_new, 1)
            if lb > LANE:
                cout_ref[kv, rs, 0:lb - LANE] = shifted[:, 0:lb - LANE]
            cout_ref[kv, rs, lb - LANE:lb] = jnp.where(lane >= LANE - t_new, new_t[rs, :], shifted[:, lb - LANE:lb])

    q_rep = jnp.concatenate([q] * HEADS_PER_GROUP, axis=0)
    row_head = _div_pow2(lax.broadcasted_iota(jnp.int32, (rows, A_OUT), 0), t_new)
    col_head = _div_pow2(lax.broadcasted_iota(jnp.int32, (rows, A_OUT), 1), HEAD_DIM)
    diag = row_head == col_head
    q_bd = jnp.where(diag, q_rep, 0.0).astype(BF16)

    s_buf = _dot(q_bd, c_ref[0].astype(BF16))
    s_new = _dot_nt(q_bd, k_new.astype(BF16))

    tok_b = _mod_pow2(lax.broadcasted_iota(jnp.int32, (rows, lb), 0), t_new)
    pos_b = lax.broadcasted_iota(jnp.int32, (rows, lb), 1)
    dist_b = lb + tok_b - pos_b
    ok_b = (_mod_pow2(dist_b, dil) == 0) & (dist_b <= span * dil)
    tok_n = _mod_pow2(lax.broadcasted_iota(jnp.int32, (rows, t_new), 0), t_new)
    pos_n = lax.broadcasted_iota(jnp.int32, (rows, t_new), 1)
    dist_n = tok_n - pos_n
    ok_n = (dist_n >= 0) & (_mod_pow2(dist_n, dil) == 0) & (dist_n <= span * dil)
    s_buf = jnp.where(ok_b, s_buf, -jnp.inf)
    s_new = jnp.where(ok_n, s_new, -jnp.inf)

    m = jnp.maximum(jnp.max(s_buf, axis=-1, keepdims=True), jnp.max(s_new, axis=-1, keepdims=True))
    p_buf = jnp.exp(s_buf - m)
    p_new = jnp.exp(s_new - m)
    l = jnp.sum(p_buf, axis=-1, keepdims=True) + jnp.sum(p_new, axis=-1, keepdims=True)
    o_full = (_dot_nt(p_buf.astype(BF16), c_ref[1].astype(BF16)) + _dot(p_new.astype(BF16), v_new.astype(BF16))) / l
    o_full = jnp.where(diag, o_full, 0.0)
    lse = m + jnp.log(l)

    o = o_full[0:t_new, :]
    lane_t = lax.broadcasted_iota(jnp.int32, (t_new, LANE), 1)
    lse_all = jnp.zeros((t_new, LANE), F32)
    for h in range(HEADS_PER_GROUP):
        if h > 0:
            o = o + o_full[h * t_new:(h + 1) * t_new, :]
        lse_all = jnp.where(lane_t == h, lse[h * t_new:(h + 1) * t_new, :], lse_all)
    o_ref[...] = o.astype(o_ref.dtype)
    lse_ref[...] = lse_all


def _attn_sample(qkv, cache, prev_out, layer, group, t_new):
    window, dil = DSWA_GROUPS[group]
    n_rows = qkv.shape[1]
    batch = n_rows // t_new
    lb = cache.shape[-1]

    def pspec(part):
        return pl.BlockSpec((None, t_new, COL_BLOCK), lambda b: (part, b, 0))

    in_specs = [pspec(0), pspec(1), pspec(2),
                pl.BlockSpec((None, None, 2, A_OUT, lb), lambda b: (layer, b, 0, 0, 0))]
    args = [qkv, qkv, qkv, cache]
    aliases = {}
    if prev_out is not None:
        in_specs.append(pl.BlockSpec(memory_space=pl.ANY))
        args.append(prev_out)
        aliases = {4: 2}
    return pl.pallas_call(
        functools.partial(_attn_sample_kernel, dil=dil, span=window // dil, aliased=prev_out is not None),
        grid=(batch,),
        in_specs=in_specs,
        out_specs=[
            pl.BlockSpec((t_new, A_OUT), lambda b: (b, 0)),
            pl.BlockSpec((t_new, LANE), lambda b: (b, 0)),
            pl.BlockSpec((None, None, 2, A_OUT, lb), lambda b: (layer, b, 0, 0, 0)),
        ],
        out_shape=[
            jax.ShapeDtypeStruct((n_rows, A_OUT), F32),
            jax.ShapeDtypeStruct((n_rows, LANE), F32),
            jax.ShapeDtypeStruct(cache.shape, F32),
        ],
        input_output_aliases=aliases,
        compiler_params=_cparams(("parallel",)),
        name=f"attn_sample_g{group}",
    )(*args)


def _gla_kernel(*refs, chunk, n_chunks, has_s0):
    if has_s0:
        q_ref, k_ref, v_ref, r_ref, glr_ref, wa_ref, ba_ref, gg_ref, s0_ref, o_ref, sout_ref, s_ref = refs
    else:
        q_ref, k_ref, v_ref, r_ref, glr_ref, wa_ref, ba_ref, gg_ref, o_ref, sout_ref, s_ref = refs
    step = pl.program_id(2)

    @pl.when(step == 0)
    def _():
        if has_s0:
            s_ref[...] = s0_ref[...].astype(F32)
        else:
            s_ref[...] = jnp.zeros_like(s_ref)

    ri = lax.broadcasted_iota(jnp.int32, (chunk, chunk), 0)
    ci = lax.broadcasted_iota(jnp.int32, (chunk, chunk), 1)
    causal = ri >= ci
    tri = causal.astype(BF16)
    ones = jnp.ones((chunk, GLA_DV), BF16)
    wa = wa_ref[...]
    ba = ba_ref[...]
    gg = gg_ref[...]

    for c in range(n_chunks):
        rs = slice(c * chunk, (c + 1) * chunk)
        q = q_ref[rs, :].astype(F32) * (GLA_DK ** -0.5)
        k = k_ref[rs, :].astype(F32)
        v = v_ref[rs, :].astype(BF16)
        gate = _dot(glr_ref[rs, :].astype(BF16), wa) + ba
        log_a = jax.nn.log_sigmoid(gate) * (1.0 / GATE_TAU)
        cum = _cumsum(tri, log_a)
        last = cum[chunk - 1:chunk, :]
        last_t = _colsum(log_a, ones)
        q_in = (q * jnp.exp(cum)).astype(BF16)
        k_in = (k * jnp.exp(-cum)).astype(BF16)
        k_out = (k * jnp.exp(last - cum)).astype(BF16)
        att = jnp.where(causal, _dot_nt(q_in, k_in), 0.0).astype(BF16)
        state = s_ref[...]
        o = _dot(att, v) + _dot(q_in, state.astype(BF16))
        s_ref[...] = jnp.exp(last_t) * state + _dot_tn(k_out, v)
        y = _rms(o, gg) * jax.nn.silu(r_ref[rs, :].astype(F32))
        o_ref[rs, :] = y.astype(o_ref.dtype)

    @pl.when(step == pl.num_programs(2) - 1)
    def _():
        sout_ref[...] = s_ref[...]


def _cumsum(tri, x):
    hi = x.astype(BF16)
    lo = (x - hi.astype(F32)).astype(BF16)
    return _dot(tri, hi) + _dot(tri, lo)


def _colsum(x, ones):
    hi = x.astype(BF16)
    lo = (x - hi.astype(F32)).astype(BF16)
    return _dot_tn(hi, ones) + _dot_tn(lo, ones)


def _gla(rest, wa, ba, gg, s0, layer, batch, seq, chunk, n_chunks, out_dtype):
    n_rows = rest.shape[1]
    lblk = chunk * n_chunks
    steps = seq // lblk

    def pspec(width, block):
        per = COL_BLOCK // width
        return pl.BlockSpec((None, lblk, width), lambda b, h, n: (block + h // per, b * steps + n, h % per))

    in_specs = [
        pspec(GLA_DK, R_QB), pspec(GLA_DK, R_KB), pspec(GLA_DV, R_VB), pspec(GLA_DV, R_RB),
        pl.BlockSpec((None, lblk, LANE), lambda b, h, n: (R_GLR, b * steps + n, 0)),
        pl.BlockSpec((LANE, GLA_DK), lambda b, h, n: (0, h)),
        pl.BlockSpec((1, GLA_DK), lambda b, h, n: (0, h)),
        pl.BlockSpec((1, GLA_DV), lambda b, h, n: (0, h)),
    ]
    args = [rest, rest, rest, rest, rest, wa, ba, gg]
    if s0 is not None:
        in_specs.append(pl.BlockSpec((None, None, None, GLA_DK, GLA_DV), lambda b, h, n: (layer, b, h, 0, 0)))
        args.append(s0)
    return pl.pallas_call(
        functools.partial(_gla_kernel, chunk=chunk, n_chunks=n_chunks, has_s0=s0 is not None),
        grid=(batch, GLA_HEADS, steps),
        in_specs=in_specs,
        out_specs=[
            pl.BlockSpec((lblk, GLA_DV), lambda b, h, n: (b * steps + n, h)),
            pl.BlockSpec((None, None, GLA_DK, GLA_DV), lambda b, h, n: (b, h, 0, 0)),
        ],
        out_shape=[
            jax.ShapeDtypeStruct((n_rows, GLA_VAL), out_dtype),
            jax.ShapeDtypeStruct((batch, GLA_HEADS, GLA_DK, GLA_DV), F32),
        ],
        scratch_shapes=[pltpu.VMEM((GLA_DK, GLA_DV), F32)],
        compiler_params=_cparams(("parallel", "parallel", "arbitrary")),
        name="gla",
    )(*args)


def _merge_kernel(x_ref, o1_ref, o2_ref, o3_ref, l1_ref, l2_ref, l3_ref, ob_ref, ga_ref, gb_ref,
                  wpa_ref, wpb_ref, wo_ref, e_ref, out_ref, o_scr, l_scr, *, dils):
    tm = x_ref.shape[0]

    def token_order(ref, scr, dil):
        if dil == 1:
            return ref[0].astype(F32)
        chunks = ref.shape[2] // LANE
        for r in range(dil):
            for c in range(chunks):
                scr[c, pl.ds(r, tm // dil, stride=dil), :] = ref[r, :, c * LANE:(c + 1) * LANE].astype(F32)
        return jnp.concatenate([scr[c] for c in range(chunks)], axis=1)

    lses = [token_order(l_ref, l_scr, d) for l_ref, d in zip((l1_ref, l2_ref, l3_ref), dils)]
    m = jnp.maximum(jnp.maximum(lses[0], lses[1]), lses[2])
    ws = [jnp.exp(l - m) for l in lses]
    inv = 1.0 / (ws[0] + ws[1] + ws[2])
    expand = e_ref[...]
    o_a = jnp.zeros((tm, A_OUT), F32)
    for w, o_ref, d in zip(ws, (o1_ref, o2_ref, o3_ref), dils):
        o_a = o_a + _split_dot(w * inv, expand) * token_order(o_ref, o_scr, d)
    ya = _dot(o_a.astype(BF16), wpa_ref[...])
    yb = _dot(ob_ref[...].astype(BF16), wpb_ref[...])
    gate_a = jnp.concatenate([ga_ref[0], ga_ref[1]], axis=1).astype(F32)
    gate_b = jnp.concatenate([gb_ref[0], gb_ref[1]], axis=1).astype(F32)
    merged = jax.nn.sigmoid(gate_a) * ya + jax.nn.sigmoid(gate_b) * yb
    out_ref[...] = x_ref[...] + _dot(merged.astype(BF16), wo_ref[...])


def _merge(x, outs, lses, ob, rest, wpa, wpb, wo, expand, seq, tm):
    n = x.shape[0]
    tiles = seq // tm
    dils = tuple(o.shape[1] for o in outs)

    def rows(width):
        return pl.BlockSpec((tm, width), lambda i: (i, 0))

    def dilated(a):
        dil, width = a.shape[1], a.shape[3]
        return pl.BlockSpec((None, dil, tm // dil, width), lambda i: (i // tiles, 0, i % tiles, 0))

    def gate(block):
        return pl.BlockSpec((2, tm, COL_BLOCK), lambda i: (block // 2, i, 0))

    def full(a):
        return pl.BlockSpec(a.shape, lambda i: (0, 0))

    return pl.pallas_call(
        functools.partial(_merge_kernel, dils=dils),
        grid=(n // tm,),
        in_specs=[rows(D_MODEL)] + [dilated(o) for o in outs] + [dilated(l) for l in lses] + [rows(GLA_VAL)]
                 + [gate(R_GA), gate(R_GB)] + [full(wpa), full(wpb), full(wo), full(expand)],
        out_specs=rows(D_MODEL),
        out_shape=jax.ShapeDtypeStruct((n, D_MODEL), F32),
        scratch_shapes=[pltpu.VMEM((A_OUT // LANE, tm, LANE), F32), pltpu.VMEM((1, tm, LANE), F32)],
        compiler_params=_cparams(("parallel",)),
        name="merge",
    )(x, *outs, *lses, ob, rest, rest, wpa, wpb, wo, expand)


def _ffn_kernel(x_ref, g_ref, wg_ref, wu_ref, wd_ref, o_ref, h_ref, acc_ref):
    f = pl.program_id(1)

    @pl.when(f == 0)
    def _():
        h_ref[...] = _rms(x_ref[...], g_ref[...]).astype(BF16)
        acc_ref[...] = x_ref[...]

    h = h_ref[...]
    mid = jax.nn.silu(_dot(h, wg_ref[...])) * _dot(h, wu_ref[...])
    acc_ref[...] += _dot(mid.astype(BF16), wd_ref[...])

    @pl.when(f == pl.num_programs(1) - 1)
    def _():
        o_ref[...] = acc_ref[...]


def _ffn(x, g, wg, wu, wd, tm, tf):
    n = x.shape[0]
    d_ff = wg.shape[1]
    return pl.pallas_call(
        _ffn_kernel,
        grid=(n // tm, d_ff // tf),
        in_specs=[
            pl.BlockSpec((tm, D_MODEL), lambda i, f: (i, 0)),
            pl.BlockSpec((1, D_MODEL), lambda i, f: (0, 0)),
            pl.BlockSpec((D_MODEL, tf), lambda i, f: (0, f)),
            pl.BlockSpec((D_MODEL, tf), lambda i, f: (0, f)),
            pl.BlockSpec((tf, D_MODEL), lambda i, f: (f, 0)),
        ],
        out_specs=pl.BlockSpec((tm, D_MODEL), lambda i, f: (i, 0)),
        out_shape=jax.ShapeDtypeStruct((n, D_MODEL), F32),
        scratch_shapes=[pltpu.VMEM((tm, D_MODEL), BF16), pltpu.VMEM((tm, D_MODEL), F32)],
        compiler_params=_cparams(("parallel", "arbitrary")),
        name="ffn",
    )(x, g, wg, wu, wd)


def _router_kernel(x_ref, g_ref, wr_ref, comb_ref):
    h = _rms(x_ref[...], g_ref[...])
    logits = jnp.dot(h, wr_ref[...], preferred_element_type=F32, precision=lax.Precision.HIGHEST)
    lane = lax.broadcasted_iota(jnp.int32, logits.shape, 1)
    logits = jnp.where(lane < N_EXPERTS, logits, -jnp.inf)
    v1 = jnp.max(logits, axis=-1, keepdims=True)
    i1 = jnp.min(jnp.where(logits == v1, lane, LANE), axis=-1, keepdims=True)
    rest = jnp.where(lane == i1, -jnp.inf, logits)
    v2 = jnp.max(rest, axis=-1, keepdims=True)
    i2 = jnp.min(jnp.where(rest == v2, lane, LANE), axis=-1, keepdims=True)
    e2 = jnp.exp(v2 - v1)
    g1 = 1.0 / (1.0 + e2)
    g2 = e2 / (1.0 + e2)
    comb_ref[...] = jnp.where(lane == i1, g1, 0.0) + jnp.where(lane == i2, g2, 0.0)


def _router(x, g, wr, tm):
    n = x.shape[0]
    return pl.pallas_call(
        _router_kernel,
        grid=(n // tm,),
        in_specs=[
            pl.BlockSpec((tm, D_MODEL), lambda i: (i, 0)),
            pl.BlockSpec((1, D_MODEL), lambda i: (0, 0)),
            pl.BlockSpec((D_MODEL, LANE), lambda i: (0, 0)),
        ],
        out_specs=pl.BlockSpec((tm, LANE), lambda i: (i, 0)),
        out_shape=jax.ShapeDtypeStruct((n, LANE), F32),
        compiler_params=_cparams(("parallel",)),
        name="router",
    )(x, g, wr)


def _route_tables(comb, tile):
    n = comb.shape[0]
    tiles = n // tile
    sel = (comb[:, :N_EXPERTS] > 0).reshape(tiles, tile, N_EXPERTS)
    seli = sel.astype(jnp.int32)
    slot = jnp.where(sel, jnp.cumsum(seli, axis=1) - seli, -1).astype(F32)
    counts = jnp.sum(seli, axis=1).reshape(tiles * N_EXPERTS)
    slot_col = jnp.pad(slot.reshape(n, N_EXPERTS), ((0, 0), (0, LANE - N_EXPERTS)), constant_values=-1.0)
    slot_row = jnp.swapaxes(slot, 1, 2).reshape(tiles * N_EXPERTS, 1, tile)
    return counts, slot_col, slot_row


def _moe_kernel(cnt_ref, x_ref, g_ref, comb_ref, scol_ref, srow_ref, wg_ref, wu_ref, wd_ref, o_ref,
                h_ref, xc_ref, y_ref, *, sub, n_sub):
    i, e, f = pl.program_id(0), pl.program_id(1), pl.program_id(2)
    last_f = pl.num_programs(2) - 1
    tile = x_ref.shape[0]
    sub_pad = y_ref.shape[1]
    count = cnt_ref[i * N_EXPERTS + e]

    @pl.when((e == 0) & (f == 0))
    def _():
        h_ref[...] = _rms(x_ref[...], g_ref[...]).astype(BF16)
        o_ref[...] = x_ref[...]

    for s in range(n_sub):
        @pl.when((f == 0) & (s * sub < count))
        def _():
            slots = (lax.broadcasted_iota(jnp.int32, (sub, tile), 0) + s * sub).astype(F32)
            pick = (srow_ref[...] == slots).astype(BF16)
            xc_ref[s] = _dot(pick, h_ref[...]).astype(BF16)
            y_ref[s] = jnp.zeros((sub_pad, D_MODEL), F32)

        @pl.when(s * sub < count)
        def _():
            xs = xc_ref[s]
            mid = jax.nn.silu(_dot(xs, wg_ref[...])) * _dot(xs, wu_ref[...])
            y_ref[s, 0:sub, :] += _dot(mid.astype(BF16), wd_ref[...])

        @pl.when((f == last_f) & (s * sub < count))
        def _():
            mine = lax.broadcasted_iota(jnp.int32, (tile, LANE), 1) == e
            slot = jnp.sum(jnp.where(mine, scol_ref[...], 0.0), axis=-1, keepdims=True)
            gate = jnp.sum(jnp.where(mine, comb_ref[...], 0.0), axis=-1, keepdims=True)
            slots = (lax.broadcasted_iota(jnp.int32, (tile, sub_pad), 1) + s * sub).astype(F32)
            place = (slot == slots).astype(BF16)
            o_ref[...] += gate * _dot(place, y_ref[s].astype(BF16))


def _moe(x, g, comb, wg, wu, wd, tile, sub, tf):
    n = x.shape[0]
    n_sub = -(-tile // sub)
    sub_pad = -(-sub // LANE) * LANE
    counts, slot_col, slot_row = _route_tables(comb, tile)
    grid_spec = pltpu.PrefetchScalarGridSpec(
        num_scalar_prefetch=1,
        grid=(n // tile, N_EXPERTS, D_FF_EXPERT // tf),
        in_specs=[
            pl.BlockSpec((tile, D_MODEL), lambda i, e, f, c: (i, 0)),
            pl.BlockSpec((1, D_MODEL), lambda i, e, f, c: (0, 0)),
            pl.BlockSpec((tile, LANE), lambda i, e, f, c: (i, 0)),
            pl.BlockSpec((tile, LANE), lambda i, e, f, c: (i, 0)),
            pl.BlockSpec((None, 1, tile), lambda i, e, f, c: (i * N_EXPERTS + e, 0, 0)),
            pl.BlockSpec((None, D_MODEL, tf), lambda i, e, f, c: (e, 0, f)),
            pl.BlockSpec((None, D_MODEL, tf), lambda i, e, f, c: (e, 0, f)),
            pl.BlockSpec((None, tf, D_MODEL), lambda i, e, f, c: (e, f, 0)),
        ],
        out_specs=pl.BlockSpec((tile, D_MODEL), lambda i, e, f, c: (i, 0)),
        scratch_shapes=[
            pltpu.VMEM((tile, D_MODEL), BF16),
            pltpu.VMEM((n_sub, sub, D_MODEL), BF16),
            pltpu.VMEM((n_sub, sub_pad, D_MODEL), F32),
        ],
    )
    return pl.pallas_call(
        functools.partial(_moe_kernel, sub=sub, n_sub=n_sub),
        grid_spec=grid_spec,
        out_shape=jax.ShapeDtypeStruct((n, D_MODEL), F32),
        compiler_params=_cparams(("parallel", "arbitrary", "arbitrary")),
        name="moe",
    )(counts, x, g, comb, slot_col, slot_row, wg, wu, wd)


def _final_norm_kernel(x_ref, g_ref, o_ref):
    o_ref[...] = _rms(x_ref[...], g_ref[...])


def _final_norm(x, g, tm):
    n = x.shape[0]
    return pl.pallas_call(
        _final_norm_kernel,
        grid=(n // tm,),
        in_specs=[pl.BlockSpec((tm, D_MODEL), lambda i: (i, 0)), pl.BlockSpec((1, D_MODEL), lambda i: (0, 0))],
        out_specs=pl.BlockSpec((tm, D_MODEL), lambda i: (i, 0)),
        out_shape=jax.ShapeDtypeStruct((n, D_MODEL), F32),
        compiler_params=_cparams(("parallel",)),
        name="final_norm",
    )(x, g)


def _rope_tables(pos):
    half = HEAD_DIM // 2
    inv = ROPE_THETA ** (-jnp.arange(half, dtype=F32) / half)
    ang = pos.astype(F32)[:, None] * inv[None, :]
    cos = jnp.cos(ang)
    sin = jnp.sin(ang)
    reps = LANE // HEAD_DIM
    cos_k = jnp.tile(jnp.concatenate([cos, cos], axis=-1), (1, reps))
    sin_k = jnp.tile(jnp.concatenate([-sin, sin], axis=-1), (1, reps))
    scale = HEAD_DIM ** -0.5
    return (jnp.stack([cos_k * scale, cos_k, jnp.ones_like(cos_k)]),
            jnp.stack([sin_k * scale, sin_k, jnp.zeros_like(sin_k)]))


def _layer_weights(l, w_in, w_gate_a2, b_gate_a, g_gla, w_branch_a, w_branch_b, w_out):
    w = w_in[l]
    offs = np.cumsum((A_WIDTH, A_WIDTH, A_WIDTH, GLA_KEY, GLA_KEY, GLA_VAL, GLA_VAL, GATE_RANK, D_MODEL, D_MODEL))
    qa, ka, va, qb, kb, vb, rb, glr, ga, gb = jnp.split(w, [int(o) for o in offs[:-1]], axis=1)
    w_attn = jnp.stack([jnp.stack([part[:, g * A_OUT:(g + 1) * A_OUT] for part in (qa, ka, va)])
                        for g in range(N_GROUPS)]).astype(BF16)
    glr = jnp.pad(glr, ((0, 0), (0, COL_BLOCK - GATE_RANK)))
    rest = jnp.concatenate([qb, kb, vb, rb, ga, gb, glr], axis=1)
    w_rest = jnp.swapaxes(rest.reshape(D_MODEL, N_REST, COL_BLOCK), 0, 1).astype(BF16)
    wa = jnp.pad(w_gate_a2[l], ((0, LANE - GATE_RANK), (0, 0))).astype(BF16)
    return dict(w_attn=w_attn, w_rest=w_rest, wa=wa, ba=b_gate_a[l][None, :], gg=g_gla[l][None, :],
                wpa=w_branch_a[l].astype(BF16), wpb=w_branch_b[l].astype(BF16), wo=w_out[l].astype(BF16))


def _head_expand():
    e = np.zeros((LANE, A_OUT), np.float32)
    for h in range(HEADS_PER_GROUP):
        e[h, h * HEAD_DIM:(h + 1) * HEAD_DIM] = 1.0
    return jnp.asarray(e, BF16)


def _token_order(a, keep):
    batch, dil, sub_len, width = a.shape
    tail = a[:, :, sub_len - keep // dil:, :]
    return jnp.swapaxes(tail, 1, 2).reshape(batch, keep, width)


def _trunk(x, rope, caches, gla_state, lw, ffw, g_mix, g_ffn, g_final, batch, seq, tm, moe_tile, moe_sub, p_dtype):
    cos_t, sin_t = rope
    expand = _head_expand()
    prompt = caches is None
    bufs_out = [[] for _ in range(N_GROUPS)]
    cache_out = [None] * N_GROUPS
    states_out = []
    for l in range(DEPTH):
        w = lw[l]
        g_row = g_mix[l][None, :]
        rest = _inproj_rest(x, g_row, w["w_rest"], tm, p_dtype)
        outs, lses = [], []
        for g, (window, dil) in enumerate(DSWA_GROUPS):
            if prompt:
                qkv = _inproj_attn(x, g_row, w["w_attn"], cos_t, sin_t, g, batch, seq, dil, tm, p_dtype)
                o, lse = _attn_prompt(qkv, g)
                keep = min(window, seq)
                kv = jnp.stack([_token_order(qkv[1], keep), _token_order(qkv[2], keep)], axis=2)
                bufs_out[g].append(kv.astype(F32).reshape(batch, keep, 2, HEADS_PER_GROUP, HEAD_DIM))
            else:
                qkv = _inproj_attn(x, g_row, w["w_attn"], cos_t, sin_t, g, 1, batch * seq, 1, tm, p_dtype)
                o, lse, cache_out[g] = _attn_sample(qkv.reshape(3, batch * seq, COL_BLOCK), caches[g], cache_out[g],
                                                    l, g, seq)
                o = o.reshape(1, 1, batch * seq, A_OUT)
                lse = lse.reshape(1, 1, batch * seq, LANE)
            outs.append(o)
            lses.append(lse)
        if prompt:
            ob, s_new = _gla(rest, w["wa"], w["ba"], w["gg"], None, l, batch, seq, GLA_CHUNK, 8, BF16)
            x = _merge(x, outs, lses, ob, rest, w["wpa"], w["wpb"], w["wo"], expand, seq, min(tm, 512))
        else:
            ob, s_new = _gla(rest, w["wa"], w["ba"], w["gg"], gla_state, l, batch, seq, seq, 1, F32)
            x = _merge(x, outs, lses, ob, rest, w["wpa"], w["wpb"], w["wo"], expand, batch * seq, tm)
        states_out.append(s_new)
        i = l // 2
        if l % 2 == 0:
            x = _ffn(x, g_ffn[l][None, :], ffw["wg"][i], ffw["wu"][i], ffw["wd"][i], tm, D_FF // 2)
        else:
            comb = _router(x, g_ffn[l][None, :], ffw["wr"][i], tm)
            x = _moe(x, g_ffn[l][None, :], comb, ffw["eg"][i], ffw["eu"][i], ffw["ed"][i],
                     moe_tile, moe_sub, D_FF_EXPERT // 4)
    y = _final_norm(x, g_final[None, :], tm)
    if prompt:
        bufs = [jnp.stack(b) for b in bufs_out]
    else:
        bufs = [jnp.transpose(c.reshape(DEPTH, batch, 2, HEADS_PER_GROUP, HEAD_DIM, c.shape[-1]), (0, 1, 5, 2, 3, 4))
                for c in cache_out]
    return (y, bufs[0], bufs[1], bufs[2], jnp.stack(states_out))


def kernel(x_prompt, x_sample, cache_kv_w128, cache_kv_w512, cache_kv_w2048, state_gla, g_mix, w_in, w_gate_a2, b_gate_a, g_gla, w_branch_a, w_branch_b, w_out, g_ffn, w_ff_gate, w_ff_up, w_ff_down, w_router, w_exp_gate, w_exp_up, w_exp_down, g_final):
    batch, seq, _ = x_prompt.shape
    dec_batch, dec_seq, _ = x_sample.shape
    assert seq % (DSWA_GROUPS[-1][1] * Q_BLOCK) == 0 and seq % (8 * GLA_CHUNK) == 0
    assert dec_seq % GLA_CHUNK != 0 and dec_seq % 8 == 0
    assert all(c.shape[2] == w for c, (w, _) in zip((cache_kv_w128, cache_kv_w512, cache_kv_w2048), DSWA_GROUPS))

    lw = [_layer_weights(l, w_in, w_gate_a2, b_gate_a, g_gla, w_branch_a, w_branch_b, w_out) for l in range(DEPTH)]
    ffw = dict(wg=w_ff_gate.astype(BF16), wu=w_ff_up.astype(BF16), wd=w_ff_down.astype(BF16),
               wr=jnp.pad(w_router, ((0, 0), (0, 0), (0, LANE - N_EXPERTS))),
               eg=w_exp_gate.astype(BF16), eu=w_exp_up.astype(BF16), ed=w_exp_down.astype(BF16))

    y_p, kv128_p, kv512_p, kv2048_p, gla_p = _trunk(
        x_prompt.reshape(batch * seq, D_MODEL), _rope_tables(jnp.arange(seq, dtype=jnp.int32)), None, None,
        lw, ffw, g_mix, g_ffn, g_final, batch, seq, 1024, 1024, 288, BF16)

    n_s = dec_batch * dec_seq
    pos_s = jnp.tile(PAST_LEN + jnp.arange(dec_seq, dtype=jnp.int32), dec_batch)
    caches = [jnp.transpose(c, (0, 1, 3, 4, 5, 2)).reshape(DEPTH, dec_batch, 2, A_OUT, c.shape[2])
              for c in (cache_kv_w128, cache_kv_w512, cache_kv_w2048)]
    y_s, kv128_s, kv512_s, kv2048_s, gla_s = _trunk(
        x_sample.reshape(n_s, D_MODEL), _rope_tables(pos_s), caches, state_gla,
        lw, ffw, g_mix, g_ffn, g_final, dec_batch, dec_seq, n_s, n_s, 96, F32)

    return (y_p.reshape(batch, seq, D_MODEL), y_s.reshape(dec_batch, dec_seq, D_MODEL),
            kv128_p, kv512_p, kv2048_p, gla_p, kv128_s, kv512_s, kv2048_s, gla_s)
```

```python
import functools

import jax
import jax.numpy as jnp
import numpy as np
from jax import lax
from jax.experimental import pallas as pl
from jax.experimental.pallas import tpu as pltpu

F32 = jnp.float32
BF16 = jnp.bfloat16

D_MODEL = 1024
DEPTH = 2
PAST_LEN = 16384
HEAD_DIM = 64
DSWA_GROUPS = ((128, 1), (512, 4), (2048, 16))
N_GROUPS = 3
HEADS_PER_GROUP = 8
A_WIDTH = N_GROUPS * HEADS_PER_GROUP * HEAD_DIM
A_OUT = HEADS_PER_GROUP * HEAD_DIM
Q_BLOCK = 128
GLA_HEADS = 4
GLA_KEY = 512
GLA_VAL = 1024
GLA_DK = 128
GLA_DV = 256
GATE_RANK = 16
GATE_TAU = 16.0
GLA_CHUNK = 64
D_FF = 2816
N_EXPERTS = 8
D_FF_EXPERT = 3584
ROPE_THETA = 10000.0
EPS = 1e-6
LOG2_E = 1.4426950408889634

LANE = 128
COL_BLOCK = 512
R_QB, R_KB, R_VB, R_RB, R_GA, R_GB, R_GLR, N_REST = 0, 1, 2, 4, 6, 8, 10, 11
ROW_CHUNK = 256
VMEM_LIMIT = 56 * 1024 * 1024


def _cparams(sem):
    return pltpu.CompilerParams(dimension_semantics=sem, vmem_limit_bytes=VMEM_LIMIT)


def _rms(xf, g):
    return xf * lax.rsqrt(jnp.mean(xf * xf, axis=-1, keepdims=True) + EPS) * g


def _dot(a, b):
    return jnp.dot(a, b, preferred_element_type=F32)


def _dot_nt(a, b):
    return lax.dot_general(a, b, (((1,), (1,)), ((), ())), preferred_element_type=F32)


def _dot_tn(a, b):
    return lax.dot_general(a, b, (((0,), (0,)), ((), ())), preferred_element_type=F32)


def _div_pow2(x, d):
    assert d & (d - 1) == 0
    return lax.shift_right_logical(x, int(d).bit_length() - 1)


def _mod_pow2(x, d):
    assert d & (d - 1) == 0
    return x & (d - 1)


def _split_dot(a_f32, b_bf16, dot=_dot):
    hi = a_f32.astype(BF16)
    lo = (a_f32 - hi.astype(F32)).astype(BF16)
    return dot(hi, b_bf16) + dot(lo, b_bf16)


def _inproj_attn_kernel(x_ref, g_ref, w_ref, cos_ref, sin_ref, o_ref, h_ref, *scratch, dil, split_half):
    @pl.when(pl.program_id(1) == 0)
    def _():
        h_ref[...] = _rms(x_ref[...].astype(F32), g_ref[...]).astype(BF16)

    tm = x_ref.shape[0]
    chunk = min(tm, ROW_CHUNK)
    n_lane_groups = COL_BLOCK // LANE
    for rc in range(tm // chunk):
        rows = slice(rc * chunk, (rc + 1) * chunk)
        acc = _dot(h_ref[rows, :], w_ref[...])
        cos = cos_ref[rows, :]
        sin = sin_ref[rows, :]
        ys = []
        if split_half:
            for c in range(0, n_lane_groups, 2):
                a = acc[:, c * LANE:(c + 1) * LANE]
                b = acc[:, (c + 1) * LANE:(c + 2) * LANE]
                ys += [a * cos - b * sin, a * sin + b * cos]
        else:
            first_half = _mod_pow2(lax.broadcasted_iota(jnp.int32, cos.shape, 1), HEAD_DIM) < HEAD_DIM // 2
            for c in range(n_lane_groups):
                xs = acc[:, c * LANE:(c + 1) * LANE]
                swapped = jnp.where(first_half, pltpu.roll(xs, LANE - HEAD_DIM // 2, 1), pltpu.roll(xs, HEAD_DIM // 2, 1))
                ys.append(xs * cos + swapped * sin)
        for c, y in enumerate(ys):
            if dil == 1:
                o_ref[0, rows, c * LANE:(c + 1) * LANE] = y.astype(o_ref.dtype)
            else:
                scratch[0][c, rows, :] = y
        if dil > 1:
            sub = chunk // dil
            for r in range(dil):
                for c in range(n_lane_groups):
                    o_ref[r, rc * sub:(rc + 1) * sub, c * LANE:(c + 1) * LANE] = (
                        scratch[0][c, pl.ds(rc * chunk + r, sub, stride=dil), :].astype(o_ref.dtype))


def _inproj_attn(x, g, w, cos_t, sin_t, group, batch, seq, dil, tm, out_dtype, split_half):
    tiles = seq // tm
    scratch = [pltpu.VMEM((tm, D_MODEL), BF16)]
    if dil > 1:
        scratch.append(pltpu.VMEM((COL_BLOCK // LANE, tm, LANE), F32))
    return pl.pallas_call(
        functools.partial(_inproj_attn_kernel, dil=dil, split_half=split_half),
        grid=(batch * tiles, 3),
        in_specs=[
            pl.BlockSpec((tm, D_MODEL), lambda i, j: (i, 0)),
            pl.BlockSpec((1, D_MODEL), lambda i, j: (0, 0)),
            pl.BlockSpec((None, None, D_MODEL, COL_BLOCK), lambda i, j: (group, j, 0, 0)),
            pl.BlockSpec((None, tm, LANE), lambda i, j: (j, i % tiles, 0)),
            pl.BlockSpec((None, tm, LANE), lambda i, j: (j, i % tiles, 0)),
        ],
        out_specs=pl.BlockSpec((None, None, dil, tm // dil, COL_BLOCK), lambda i, j: (j, i // tiles, 0, i % tiles, 0)),
        out_shape=jax.ShapeDtypeStruct((3, batch, dil, seq // dil, COL_BLOCK), out_dtype),
        scratch_shapes=scratch,
        compiler_params=_cparams(("parallel", "arbitrary")),
        name=f"inproj_attn_g{group}",
    )(x, g, w, cos_t, sin_t)


def _inproj_rest_kernel(x_ref, g_ref, w_ref, o_ref, h_ref):
    @pl.when(pl.program_id(1) == 0)
    def _():
        h_ref[...] = _rms(x_ref[...].astype(F32), g_ref[...]).astype(BF16)

    tm = x_ref.shape[0]
    chunk = min(tm, ROW_CHUNK)
    for rc in range(tm // chunk):
        rows = slice(rc * chunk, (rc + 1) * chunk)
        o_ref[rows, :] = _dot(h_ref[rows, :], w_ref[...]).astype(o_ref.dtype)


def _inproj_rest(x, g, w, tm, out_dtype):
    n = x.shape[0]
    return pl.pallas_call(
        _inproj_rest_kernel,
        grid=(n // tm, N_REST),
        in_specs=[
            pl.BlockSpec((tm, D_MODEL), lambda i, j: (i, 0)),
            pl.BlockSpec((1, D_MODEL), lambda i, j: (0, 0)),
            pl.BlockSpec((None, D_MODEL, COL_BLOCK), lambda i, j: (j, 0, 0)),
        ],
        out_specs=pl.BlockSpec((None, tm, COL_BLOCK), lambda i, j: (j, i, 0)),
        out_shape=jax.ShapeDtypeStruct((N_REST, n, COL_BLOCK), out_dtype),
        scratch_shapes=[pltpu.VMEM((tm, D_MODEL), BF16)],
        compiler_params=_cparams(("parallel", "arbitrary")),
        name="inproj_rest",
    )(x, g, w)


QUAD = 4
QUAD_WIDTH = QUAD * HEAD_DIM
MAX_Q_BLOCKS_PER_STEP = 4


def _attn_prompt_kernel(bias_ref, q_ref, kp_ref, kc_ref, vp_ref, vc_ref, o_ref, m_ref, l_ref):
    n = pl.program_id(2)
    rows = QUAD * Q_BLOCK
    row_head = _div_pow2(lax.broadcasted_iota(jnp.int32, (rows, QUAD_WIDTH), 0), Q_BLOCK)
    col = lax.broadcasted_iota(jnp.int32, (rows, QUAD_WIDTH), 1)
    q_lanes = _div_pow2(_mod_pow2(col, LANE), HEAD_DIM // 2) == row_head
    v_lanes = _div_pow2(col, HEAD_DIM) == row_head
    lane = lax.broadcasted_iota(jnp.int32, (Q_BLOCK, LANE), 1)
    for qb in range(q_ref.shape[0] // Q_BLOCK):
        cur = slice(qb * Q_BLOCK, (qb + 1) * Q_BLOCK)
        if qb == 0:
            bias = bias_ref[jnp.where(n == 0, 1, 0)]
        else:
            bias = bias_ref[0]
            prev = slice((qb - 1) * Q_BLOCK, qb * Q_BLOCK)
        m_all = jnp.zeros((Q_BLOCK, LANE), F32)
        l_all = jnp.ones((Q_BLOCK, LANE), F32)
        for c in range(A_OUT // QUAD_WIDTH):
            cs = slice(c * QUAD_WIDTH, (c + 1) * QUAD_WIDTH)
            q = q_ref[cur, cs].astype(F32)
            q4 = jnp.where(q_lanes, jnp.concatenate([q] * QUAD, axis=0), 0.0).astype(BF16)
            k_prev = kp_ref[:, cs] if qb == 0 else kc_ref[prev, cs]
            v_prev = vp_ref[:, cs] if qb == 0 else vc_ref[prev, cs]
            k = jnp.concatenate([k_prev, kc_ref[cur, cs]], axis=0)
            v = jnp.concatenate([v_prev, vc_ref[cur, cs]], axis=0)
            s = _dot_nt(q4, k) + bias
            m = jnp.max(s, axis=-1, keepdims=True)
            p = jnp.exp2(s - m)
            l = jnp.sum(p, axis=-1, keepdims=True)
            o4 = jnp.where(v_lanes, _dot(p.astype(BF16), v), 0.0)
            o = o4[0:Q_BLOCK]
            for j in range(QUAD):
                hs = slice(j * Q_BLOCK, (j + 1) * Q_BLOCK)
                if j > 0:
                    o = o + o4[hs]
                m_all = jnp.where(lane == c * QUAD + j, m[hs], m_all)
                l_all = jnp.where(lane == c * QUAD + j, l[hs], l_all)
            o_ref[cur, cs] = o.astype(o_ref.dtype)
        m_ref[cur, :] = m_all
        l_ref[cur, :] = l_all


def _band_bias(span):
    qi = np.arange(QUAD * Q_BLOCK)[:, None] % Q_BLOCK
    kj = np.arange(2 * Q_BLOCK)[None, :]
    dist = Q_BLOCK + qi - kj
    band = (dist >= 0) & (dist <= span)
    first = band & (kj >= Q_BLOCK)
    return jnp.asarray(np.where(np.stack([band, first]), 0.0, -np.inf), F32)


def _attn_prompt(qkv, group):
    window, dil = DSWA_GROUPS[group]
    _, batch, _, sub_len, _ = qkv.shape
    q_blocks = min(MAX_Q_BLOCKS_PER_STEP, sub_len // Q_BLOCK)
    step_rows = q_blocks * Q_BLOCK
    nb = sub_len // step_rows

    def spec(part, prev):
        if prev:
            return pl.BlockSpec((None, None, None, Q_BLOCK, COL_BLOCK),
                                lambda b, r, n: (part, b, r, jnp.maximum(n * q_blocks - 1, 0), 0))
        return pl.BlockSpec((None, None, None, step_rows, COL_BLOCK), lambda b, r, n: (part, b, r, n, 0))

    bias = _band_bias(window // dil)
    stat_spec = pl.BlockSpec((None, None, step_rows, LANE), lambda b, r, n: (b, r, n, 0))
    stat_shape = jax.ShapeDtypeStruct((batch, dil, sub_len, LANE), F32)
    return pl.pallas_call(
        _attn_prompt_kernel,
        grid=(batch, dil, nb),
        in_specs=[pl.BlockSpec(bias.shape, lambda b, r, n: (0, 0, 0)),
                  spec(0, False), spec(1, True), spec(1, False), spec(2, True), spec(2, False)],
        out_specs=[pl.BlockSpec((None, None, step_rows, A_OUT), lambda b, r, n: (b, r, n, 0)), stat_spec, stat_spec],
        out_shape=[jax.ShapeDtypeStruct((batch, dil, sub_len, A_OUT), BF16), stat_shape, stat_shape],
        compiler_params=_cparams(("parallel", "parallel", "arbitrary")),
        name=f"attn_prompt_g{group}",
    )(bias, qkv, qkv, qkv, qkv, qkv)


SHIFT_ROWS = 128


def _attn_sample_kernel(*refs, dil, span, aliased):
    if aliased:
        q_ref, k_ref, v_ref, c_ref, _, o_ref, lse_ref, cout_ref = refs
    else:
        q_ref, k_ref, v_ref, c_ref, o_ref, lse_ref, cout_ref = refs
    t_new = q_ref.shape[0]
    lb = c_ref.shape[2]
    rows = HEADS_PER_GROUP * t_new
    q = q_ref[...].astype(F32)
    k_new = k_ref[...].astype(F32)
    v_new = v_ref[...].astype(F32)

    lane = lax.broadcasted_iota(jnp.int32, (SHIFT_ROWS, LANE), 1)
    zeros = jnp.zeros((LANE - t_new, A_OUT), F32)
    for kv, new in enumerate((k_new, v_new)):
        new_t = pltpu.roll(jnp.concatenate([new, zeros], axis=0).T, LANE - t_new, 1)
        for rb in range(A_OUT // SHIFT_ROWS):
            rs = slice(rb * SHIFT_ROWS, (rb + 1) * SHIFT_ROWS)
            shifted = pltpu.roll(c_ref[kv, rs, :], lb - t_new, 1)
            if lb > LANE:
                cout_ref[kv, rs, 0:lb - LANE] = shifted[:, 0:lb - LANE]
            cout_ref[kv, rs, lb - LANE:lb] = jnp.where(lane >= LANE - t_new, new_t[rs, :], shifted[:, lb - LANE:lb])

    q_rep = jnp.concatenate([q] * HEADS_PER_GROUP, axis=0)
    row_head = _div_pow2(lax.broadcasted_iota(jnp.int32, (rows, A_OUT), 0), t_new)
    col_head = _div_pow2(lax.broadcasted_iota(jnp.int32, (rows, A_OUT), 1), HEAD_DIM)
    diag = row_head == col_head
    q_bd = jnp.where(diag, q_rep, 0.0).astype(BF16)

    s_buf = _dot(q_bd, c_ref[0].astype(BF16))
    s_new = _dot_nt(q_bd, k_new.astype(BF16))

    tok_b = _mod_pow2(lax.broadcasted_iota(jnp.int32, (rows, lb), 0), t_new)
    pos_b = lax.broadcasted_iota(jnp.int32, (rows, lb), 1)
    dist_b = lb + tok_b - pos_b
    ok_b = (_mod_pow2(dist_b, dil) == 0) & (dist_b <= span * dil)
    tok_n = _mod_pow2(lax.broadcasted_iota(jnp.int32, (rows, t_new), 0), t_new)
    pos_n = lax.broadcasted_iota(jnp.int32, (rows, t_new), 1)
    dist_n = tok_n - pos_n
    ok_n = (dist_n >= 0) & (_mod_pow2(dist_n, dil) == 0) & (dist_n <= span * dil)
    s_buf = jnp.where(ok_b, s_buf, -jnp.inf)
    s_new = jnp.where(ok_n, s_new, -jnp.inf)

    m = jnp.maximum(jnp.max(s_buf, axis=-1, keepdims=True), jnp.max(s_new, axis=-1, keepdims=True))
    p_buf = jnp.exp(s_buf - m)
    p_new = jnp.exp(s_new - m)
    l = jnp.sum(p_buf, axis=-1, keepdims=True) + jnp.sum(p_new, axis=-1, keepdims=True)
    o_full = (_dot_nt(p_buf.astype(BF16), c_ref[1].astype(BF16)) + _dot(p_new.astype(BF16), v_new.astype(BF16))) / l
    o_full = jnp.where(diag, o_full, 0.0)
    lse = (m + jnp.log(l)) * LOG2_E

    o = o_full[0:t_new, :]
    lane_t = lax.broadcasted_iota(jnp.int32, (t_new, LANE), 1)
    lse_all = jnp.zeros((t_new, LANE), F32)
    for h in range(HEADS_PER_GROUP):
        if h > 0:
            o = o + o_full[h * t_new:(h + 1) * t_new, :]
        lse_all = jnp.where(lane_t == h, lse[h * t_new:(h + 1) * t_new, :], lse_all)
    o_ref[...] = o.astype(o_ref.dtype)
    lse_ref[...] = lse_all


def _attn_sample(qkv, cache, prev_out, layer, group, t_new):
    window, dil = DSWA_GROUPS[group]
    n_rows = qkv.shape[1]
    batch = n_rows // t_new
    lb = cache.shape[-1]

    def pspec(part):
        return pl.BlockSpec((None, t_new, COL_BLOCK), lambda b: (part, b, 0))

    in_specs = [pspec(0), pspec(1), pspec(2),
                pl.BlockSpec((None, None, 2, A_OUT, lb), lambda b: (layer, b, 0, 0, 0))]
    args = [qkv, qkv, qkv, cache]
    aliases = {}
    if prev_out is not None:
        in_specs.append(pl.BlockSpec(memory_space=pl.ANY))
        args.append(prev_out)
        aliases = {4: 2}
    return pl.pallas_call(
        functools.partial(_attn_sample_kernel, dil=dil, span=window // dil, aliased=prev_out is not None),
        grid=(batch,),
        in_specs=in_specs,
        out_specs=[
            pl.BlockSpec((t_new, A_OUT), lambda b: (b, 0)),
            pl.BlockSpec((t_new, LANE), lambda b: (b, 0)),
            pl.BlockSpec((None, None, 2, A_OUT, lb), lambda b: (layer, b, 0, 0, 0)),
        ],
        out_shape=[
            jax.ShapeDtypeStruct((n_rows, A_OUT), F32),
            jax.ShapeDtypeStruct((n_rows, LANE), F32),
            jax.ShapeDtypeStruct(cache.shape, F32),
        ],
        input_output_aliases=aliases,
        compiler_params=_cparams(("parallel",)),
        name=f"attn_sample_g{group}",
    )(*args)


def _gla_kernel(*refs, chunk, n_chunks, has_s0):
    if has_s0:
        q_ref, k_ref, v_ref, r_ref, glr_ref, wa_ref, ba_ref, gg_ref, s0_ref, o_ref, sout_ref, s_ref = refs
    else:
        q_ref, k_ref, v_ref, r_ref, glr_ref, wa_ref, ba_ref, gg_ref, o_ref, sout_ref, s_ref = refs
    step = pl.program_id(2)

    @pl.when(step == 0)
    def _():
        if has_s0:
            s_ref[...] = s0_ref[...].astype(F32)
        else:
            s_ref[...] = jnp.zeros_like(s_ref)

    ri = lax.broadcasted_iota(jnp.int32, (chunk, chunk), 0)
    ci = lax.broadcasted_iota(jnp.int32, (chunk, chunk), 1)
    causal = ri >= ci
    tri = causal.astype(BF16)
    gg = gg_ref[...]

    gate = _dot(glr_ref[...].astype(BF16), wa_ref[...]) + ba_ref[...]
    log_a = jax.nn.log_sigmoid(gate) * (1.0 / GATE_TAU)
    log_hi = log_a.astype(BF16)
    log_lo = (log_a - log_hi.astype(F32)).astype(BF16)
    chunks = [slice(c * chunk, (c + 1) * chunk) for c in range(n_chunks)]
    cums = [_dot(tri, log_hi[rs, :]) + _dot(tri, log_lo[rs, :]) for rs in chunks]
    lasts = [cum[chunk - 1:chunk, :] for cum in cums]
    cum = jnp.concatenate(cums, axis=0)
    last = jnp.concatenate([jnp.broadcast_to(l, (chunk, GLA_DK)) for l in lasts], axis=0)
    grow = jnp.exp(cum)
    q_in = (q_ref[...].astype(F32) * (GLA_DK ** -0.5) * grow).astype(BF16)
    k = k_ref[...].astype(F32)
    k_in = (k * jnp.exp(-cum)).astype(BF16)
    k_out = (k * jnp.exp(last - cum)).astype(BF16)
    v = v_ref[...].astype(BF16)
    o_intra, kv, decay = [], [], []
    for rs, l in zip(chunks, lasts):
        att = jnp.where(causal, _dot_nt(q_in[rs, :], k_in[rs, :]), 0.0).astype(BF16)
        o_intra.append(_dot(att, v[rs, :]))
        kv.append(_dot_tn(k_out[rs, :], v[rs, :]))
        d = jnp.broadcast_to(jnp.exp(l), (GLA_DK, GLA_DK)).T
        decay.append(jnp.concatenate([d] * (GLA_DV // GLA_DK), axis=1))

    state = s_ref[...]
    for c, rs in enumerate(chunks):
        o = o_intra[c] + _dot(q_in[rs, :], state.astype(BF16))
        state = decay[c] * state + kv[c]
        y = _rms(o, gg) * jax.nn.silu(r_ref[rs, :].astype(F32))
        o_ref[rs, :] = y.astype(o_ref.dtype)
    s_ref[...] = state

    @pl.when(step == pl.num_programs(2) - 1)
    def _():
        sout_ref[...] = state


def _gla(rest, wa, ba, gg, s0, layer, batch, seq, chunk, n_chunks, out_dtype):
    n_rows = rest.shape[1]
    lblk = chunk * n_chunks
    steps = seq // lblk

    def pspec(width, block):
        per = COL_BLOCK // width
        return pl.BlockSpec((None, lblk, width), lambda b, h, n: (block + h // per, b * steps + n, h % per))

    in_specs = [
        pspec(GLA_DK, R_QB), pspec(GLA_DK, R_KB), pspec(GLA_DV, R_VB), pspec(GLA_DV, R_RB),
        pl.BlockSpec((None, lblk, LANE), lambda b, h, n: (R_GLR, b * steps + n, 0)),
        pl.BlockSpec((LANE, GLA_DK), lambda b, h, n: (0, h)),
        pl.BlockSpec((1, GLA_DK), lambda b, h, n: (0, h)),
        pl.BlockSpec((1, GLA_DV), lambda b, h, n: (0, h)),
    ]
    args = [rest, rest, rest, rest, rest, wa, ba, gg]
    if s0 is not None:
        in_specs.append(pl.BlockSpec((None, None, None, GLA_DK, GLA_DV), lambda b, h, n: (layer, b, h, 0, 0)))
        args.append(s0)
    return pl.pallas_call(
        functools.partial(_gla_kernel, chunk=chunk, n_chunks=n_chunks, has_s0=s0 is not None),
        grid=(batch, GLA_HEADS, steps),
        in_specs=in_specs,
        out_specs=[
            pl.BlockSpec((lblk, GLA_DV), lambda b, h, n: (b * steps + n, h)),
            pl.BlockSpec((None, None, GLA_DK, GLA_DV), lambda b, h, n: (b, h, 0, 0)),
        ],
        out_shape=[
            jax.ShapeDtypeStruct((n_rows, GLA_VAL), out_dtype),
            jax.ShapeDtypeStruct((batch, GLA_HEADS, GLA_DK, GLA_DV), F32),
        ],
        scratch_shapes=[pltpu.VMEM((GLA_DK, GLA_DV), F32)],
        compiler_params=_cparams(("parallel", "parallel", "arbitrary")),
        name="gla",
    )(*args)


def _merge_kernel(x_ref, o1_ref, o2_ref, o3_ref, m1_ref, m2_ref, m3_ref, l1_ref, l2_ref, l3_ref, ob_ref, ga_ref, gb_ref,
                  wpa_ref, wpb_ref, wo_ref, e_ref, out_ref, o_scr, l_scr, *, dils):
    tm = x_ref.shape[0]

    def token_order(ref, scr, dil):
        if dil == 1:
            return ref[0].astype(F32)
        chunks = ref.shape[2] // LANE
        for r in range(dil):
            for c in range(chunks):
                scr[c, pl.ds(r, tm // dil, stride=dil), :] = ref[r, :, c * LANE:(c + 1) * LANE].astype(F32)
        return jnp.concatenate([scr[c] for c in range(chunks)], axis=1)

    ms = [token_order(m_ref, l_scr, d) for m_ref, d in zip((m1_ref, m2_ref, m3_ref), dils)]
    ls = [token_order(l_ref, l_scr, d) for l_ref, d in zip((l1_ref, l2_ref, l3_ref), dils)]
    m = jnp.maximum(jnp.maximum(ms[0], ms[1]), ms[2])
    ws = [jnp.exp2(mg - m) for mg in ms]
    inv = 1.0 / (ws[0] * ls[0] + ws[1] * ls[1] + ws[2] * ls[2])
    expand = e_ref[...]
    o_a = jnp.zeros((tm, A_OUT), F32)
    for w, o_ref, d in zip(ws, (o1_ref, o2_ref, o3_ref), dils):
        o_a = o_a + _split_dot(w * inv, expand) * token_order(o_ref, o_scr, d)
    ya = _dot(o_a.astype(BF16), wpa_ref[...])
    yb = _dot(ob_ref[...].astype(BF16), wpb_ref[...])
    gate_a = jnp.concatenate([ga_ref[0], ga_ref[1]], axis=1).astype(F32)
    gate_b = jnp.concatenate([gb_ref[0], gb_ref[1]], axis=1).astype(F32)
    merged = jax.nn.sigmoid(gate_a) * ya + jax.nn.sigmoid(gate_b) * yb
    out_ref[...] = x_ref[...] + _dot(merged.astype(BF16), wo_ref[...])


def _merge(x, outs, maxes, dens, ob, rest, wpa, wpb, wo, expand, seq, tm):
    n = x.shape[0]
    tiles = seq // tm
    dils = tuple(o.shape[1] for o in outs)

    def rows(width):
        return pl.BlockSpec((tm, width), lambda i: (i, 0))

    def dilated(a):
        dil, width = a.shape[1], a.shape[3]
        return pl.BlockSpec((None, dil, tm // dil, width), lambda i: (i // tiles, 0, i % tiles, 0))

    def gate(block):
        return pl.BlockSpec((2, tm, COL_BLOCK), lambda i: (block // 2, i, 0))

    def full(a):
        return pl.BlockSpec(a.shape, lambda i: (0, 0))

    return pl.pallas_call(
        functools.partial(_merge_kernel, dils=dils),
        grid=(n // tm,),
        in_specs=[rows(D_MODEL)] + [dilated(a) for a in (*outs, *maxes, *dens)] + [rows(GLA_VAL)]
                 + [gate(R_GA), gate(R_GB)] + [full(wpa), full(wpb), full(wo), full(expand)],
        out_specs=rows(D_MODEL),
        out_shape=jax.ShapeDtypeStruct((n, D_MODEL), F32),
        scratch_shapes=[pltpu.VMEM((A_OUT // LANE, tm, LANE), F32), pltpu.VMEM((1, tm, LANE), F32)],
        compiler_params=_cparams(("parallel",)),
        name="merge",
    )(x, *outs, *maxes, *dens, ob, rest, rest, wpa, wpb, wo, expand)


def _ffn_kernel(x_ref, g_ref, wg_ref, wu_ref, wd_ref, o_ref, h_ref, acc_ref):
    f = pl.program_id(1)

    @pl.when(f == 0)
    def _():
        h_ref[...] = _rms(x_ref[...], g_ref[...]).astype(BF16)
        acc_ref[...] = x_ref[...]

    h = h_ref[...]
    mid = jax.nn.silu(_dot(h, wg_ref[...])) * _dot(h, wu_ref[...])
    acc_ref[...] += _dot(mid.astype(BF16), wd_ref[...])

    @pl.when(f == pl.num_programs(1) - 1)
    def _():
        o_ref[...] = acc_ref[...]


def _ffn(x, g, wg, wu, wd, tm, tf):
    n = x.shape[0]
    d_ff = wg.shape[1]
    return pl.pallas_call(
        _ffn_kernel,
        grid=(n // tm, d_ff // tf),
        in_specs=[
            pl.BlockSpec((tm, D_MODEL), lambda i, f: (i, 0)),
            pl.BlockSpec((1, D_MODEL), lambda i, f: (0, 0)),
            pl.BlockSpec((D_MODEL, tf), lambda i, f: (0, f)),
            pl.BlockSpec((D_MODEL, tf), lambda i, f: (0, f)),
            pl.BlockSpec((tf, D_MODEL), lambda i, f: (f, 0)),
        ],
        out_specs=pl.BlockSpec((tm, D_MODEL), lambda i, f: (i, 0)),
        out_shape=jax.ShapeDtypeStruct((n, D_MODEL), F32),
        scratch_shapes=[pltpu.VMEM((tm, D_MODEL), BF16), pltpu.VMEM((tm, D_MODEL), F32)],
        compiler_params=_cparams(("parallel", "arbitrary")),
        name="ffn",
    )(x, g, wg, wu, wd)


def _router_kernel(x_ref, g_ref, wr_ref, comb_ref):
    h = _rms(x_ref[...], g_ref[...])
    logits = jnp.dot(h, wr_ref[...], preferred_element_type=F32, precision=lax.Precision.HIGHEST)
    lane = lax.broadcasted_iota(jnp.int32, logits.shape, 1)
    logits = jnp.where(lane < N_EXPERTS, logits, -jnp.inf)
    v1 = jnp.max(logits, axis=-1, keepdims=True)
    i1 = jnp.min(jnp.where(logits == v1, lane, LANE), axis=-1, keepdims=True)
    rest = jnp.where(lane == i1, -jnp.inf, logits)
    v2 = jnp.max(rest, axis=-1, keepdims=True)
    i2 = jnp.min(jnp.where(rest == v2, lane, LANE), axis=-1, keepdims=True)
    e2 = jnp.exp(v2 - v1)
    g1 = 1.0 / (1.0 + e2)
    g2 = e2 / (1.0 + e2)
    comb_ref[...] = jnp.where(lane == i1, g1, 0.0) + jnp.where(lane == i2, g2, 0.0)


def _router(x, g, wr, tm):
    n = x.shape[0]
    return pl.pallas_call(
        _router_kernel,
        grid=(n // tm,),
        in_specs=[
            pl.BlockSpec((tm, D_MODEL), lambda i: (i, 0)),
            pl.BlockSpec((1, D_MODEL), lambda i: (0, 0)),
            pl.BlockSpec((D_MODEL, LANE), lambda i: (0, 0)),
        ],
        out_specs=pl.BlockSpec((tm, LANE), lambda i: (i, 0)),
        out_shape=jax.ShapeDtypeStruct((n, LANE), F32),
        compiler_params=_cparams(("parallel",)),
        name="router",
    )(x, g, wr)


def _route_tables(comb, tile):
    n = comb.shape[0]
    tiles = n // tile
    sel = (comb[:, :N_EXPERTS] > 0).reshape(tiles, tile, N_EXPERTS)
    seli = sel.astype(jnp.int32)
    slot = jnp.where(sel, jnp.cumsum(seli, axis=1) - seli, -1).astype(F32)
    counts = jnp.sum(seli, axis=1).reshape(tiles * N_EXPERTS)
    slot_col = jnp.pad(slot.reshape(n, N_EXPERTS), ((0, 0), (0, LANE - N_EXPERTS)), constant_values=-1.0)
    slot_row = jnp.swapaxes(slot, 1, 2).reshape(tiles * N_EXPERTS, 1, tile)
    return counts, slot_col, slot_row


def _moe_kernel(cnt_ref, x_ref, g_ref, comb_ref, scol_ref, srow_ref, wg_ref, wu_ref, wd_ref, o_ref,
                h_ref, xc_ref, y_ref, *, sub, n_sub):
    i, e, f = pl.program_id(0), pl.program_id(1), pl.program_id(2)
    last_f = pl.num_programs(2) - 1
    tile = x_ref.shape[0]
    sub_pad = y_ref.shape[1]
    count = cnt_ref[i * N_EXPERTS + e]

    @pl.when((e == 0) & (f == 0))
    def _():
        h_ref[...] = _rms(x_ref[...], g_ref[...]).astype(BF16)
        o_ref[...] = x_ref[...]

    for s in range(n_sub):
        @pl.when((f == 0) & (s * sub < count))
        def _():
            slots = (lax.broadcasted_iota(jnp.int32, (sub, tile), 0) + s * sub).astype(F32)
            pick = (srow_ref[...] == slots).astype(BF16)
            xc_ref[s] = _dot(pick, h_ref[...]).astype(BF16)
            y_ref[s] = jnp.zeros((sub_pad, D_MODEL), F32)

        @pl.when(s * sub < count)
        def _():
            xs = xc_ref[s]
            mid = jax.nn.silu(_dot(xs, wg_ref[...])) * _dot(xs, wu_ref[...])
            y_ref[s, 0:sub, :] += _dot(mid.astype(BF16), wd_ref[...])

        @pl.when((f == last_f) & (s * sub < count))
        def _():
            mine = lax.broadcasted_iota(jnp.int32, (tile, LANE), 1) == e
            slot = jnp.sum(jnp.where(mine, scol_ref[...], 0.0), axis=-1, keepdims=True)
            gate = jnp.sum(jnp.where(mine, comb_ref[...], 0.0), axis=-1, keepdims=True)
            slots = (lax.broadcasted_iota(jnp.int32, (tile, sub_pad), 1) + s * sub).astype(F32)
            place = (slot == slots).astype(BF16)
            o_ref[...] += gate * _dot(place, y_ref[s].astype(BF16))


def _moe(x, g, comb, wg, wu, wd, tile, sub, tf):
    n = x.shape[0]
    n_sub = -(-tile // sub)
    sub_pad = -(-sub // LANE) * LANE
    counts, slot_col, slot_row = _route_tables(comb, tile)
    grid_spec = pltpu.PrefetchScalarGridSpec(
        num_scalar_prefetch=1,
        grid=(n // tile, N_EXPERTS, D_FF_EXPERT // tf),
        in_specs=[
            pl.BlockSpec((tile, D_MODEL), lambda i, e, f, c: (i, 0)),
            pl.BlockSpec((1, D_MODEL), lambda i, e, f, c: (0, 0)),
            pl.BlockSpec((tile, LANE), lambda i, e, f, c: (i, 0)),
            pl.BlockSpec((tile, LANE), lambda i, e, f, c: (i, 0)),
            pl.BlockSpec((None, 1, tile), lambda i, e, f, c: (i * N_EXPERTS + e, 0, 0)),
            pl.BlockSpec((None, D_MODEL, tf), lambda i, e, f, c: (e, 0, f)),
            pl.BlockSpec((None, D_MODEL, tf), lambda i, e, f, c: (e, 0, f)),
            pl.BlockSpec((None, tf, D_MODEL), lambda i, e, f, c: (e, f, 0)),
        ],
        out_specs=pl.BlockSpec((tile, D_MODEL), lambda i, e, f, c: (i, 0)),
        scratch_shapes=[
            pltpu.VMEM((tile, D_MODEL), BF16),
            pltpu.VMEM((n_sub, sub, D_MODEL), BF16),
            pltpu.VMEM((n_sub, sub_pad, D_MODEL), F32),
        ],
    )
    return pl.pallas_call(
        functools.partial(_moe_kernel, sub=sub, n_sub=n_sub),
        grid_spec=grid_spec,
        out_shape=jax.ShapeDtypeStruct((n, D_MODEL), F32),
        compiler_params=_cparams(("parallel", "arbitrary", "arbitrary")),
        name="moe",
    )(counts, x, g, comb, slot_col, slot_row, wg, wu, wd)


def _final_norm_kernel(x_ref, g_ref, o_ref):
    o_ref[...] = _rms(x_ref[...], g_ref[...])


def _final_norm(x, g, tm):
    n = x.shape[0]
    return pl.pallas_call(
        _final_norm_kernel,
        grid=(n // tm,),
        in_specs=[pl.BlockSpec((tm, D_MODEL), lambda i: (i, 0)), pl.BlockSpec((1, D_MODEL), lambda i: (0, 0))],
        out_specs=pl.BlockSpec((tm, D_MODEL), lambda i: (i, 0)),
        out_shape=jax.ShapeDtypeStruct((n, D_MODEL), F32),
        compiler_params=_cparams(("parallel",)),
        name="final_norm",
    )(x, g)


def _rope_tables(pos, split_half):
    half = HEAD_DIM // 2
    inv = ROPE_THETA ** (-jnp.arange(half, dtype=F32) / half)
    ang = pos.astype(F32)[:, None] * inv[None, :]
    cos = jnp.cos(ang)
    sin = jnp.sin(ang)
    if split_half:
        cos_k = jnp.tile(cos, (1, LANE // half))
        sin_k = jnp.tile(sin, (1, LANE // half))
        scale = HEAD_DIM ** -0.5 * LOG2_E
    else:
        cos_k = jnp.tile(jnp.concatenate([cos, cos], axis=-1), (1, LANE // HEAD_DIM))
        sin_k = jnp.tile(jnp.concatenate([-sin, sin], axis=-1), (1, LANE // HEAD_DIM))
        scale = HEAD_DIM ** -0.5
    return (jnp.stack([cos_k * scale, cos_k, jnp.ones_like(cos_k)]),
            jnp.stack([sin_k * scale, sin_k, jnp.zeros_like(sin_k)]))


def _to_split_half(a):
    lead = a.shape[:-1]
    a = a.reshape(*lead, HEADS_PER_GROUP // QUAD, QUAD, 2, HEAD_DIM // 2)
    return jnp.swapaxes(a, -2, -3).reshape(*lead, A_OUT)


def _from_split_half(a):
    lead = a.shape[:-1]
    a = a.reshape(*lead, HEADS_PER_GROUP // QUAD, 2, QUAD, HEAD_DIM // 2)
    return jnp.swapaxes(a, -2, -3).reshape(*lead, A_OUT)


def _layer_weights(l, w_in, w_gate_a2, b_gate_a, g_gla, w_branch_a, w_branch_b, w_out):
    w = w_in[l]
    offs = np.cumsum((A_WIDTH, A_WIDTH, A_WIDTH, GLA_KEY, GLA_KEY, GLA_VAL, GLA_VAL, GATE_RANK, D_MODEL, D_MODEL))
    qa, ka, va, qb, kb, vb, rb, glr, ga, gb = jnp.split(w, [int(o) for o in offs[:-1]], axis=1)
    def attn_weights(split_half):
        relay = _to_split_half if split_half else (lambda a: a)
        return jnp.stack([jnp.stack([relay(qa[:, g * A_OUT:(g + 1) * A_OUT]), relay(ka[:, g * A_OUT:(g + 1) * A_OUT]),
                                     va[:, g * A_OUT:(g + 1) * A_OUT]]) for g in range(N_GROUPS)]).astype(BF16)

    w_attn = {False: attn_weights(False), True: attn_weights(True)}
    glr = jnp.pad(glr, ((0, 0), (0, COL_BLOCK - GATE_RANK)))
    rest = jnp.concatenate([qb, kb, vb, rb, ga, gb, glr], axis=1)
    w_rest = jnp.swapaxes(rest.reshape(D_MODEL, N_REST, COL_BLOCK), 0, 1).astype(BF16)
    wa = jnp.pad(w_gate_a2[l], ((0, LANE - GATE_RANK), (0, 0))).astype(BF16)
    return dict(w_attn=w_attn, w_rest=w_rest, wa=wa, ba=b_gate_a[l][None, :], gg=g_gla[l][None, :],
                wpa=w_branch_a[l].astype(BF16), wpb=w_branch_b[l].astype(BF16), wo=w_out[l].astype(BF16))


def _head_expand():
    e = np.zeros((LANE, A_OUT), np.float32)
    for h in range(HEADS_PER_GROUP):
        e[h, h * HEAD_DIM:(h + 1) * HEAD_DIM] = 1.0
    return jnp.asarray(e, BF16)


def _token_order(qkv, part, keep):
    _, batch, dil, sub_len, width = qkv.shape
    tail = qkv[part, :, :, sub_len - keep // dil:, :]
    return jnp.swapaxes(tail, 1, 2).reshape(batch, keep, width)


def _trunk(x, rope, caches, gla_state, lw, ffw, g_mix, g_ffn, g_final, batch, seq, tm, moe_tile, moe_sub, p_dtype):
    cos_t, sin_t = rope
    expand = _head_expand()
    prompt = caches is None
    bufs_out = [[] for _ in range(N_GROUPS)]
    cache_out = [None] * N_GROUPS
    states_out = []
    for l in range(DEPTH):
        w = lw[l]
        g_row = g_mix[l][None, :]
        rest = _inproj_rest(x, g_row, w["w_rest"], tm, p_dtype)
        outs, maxes, dens = [], [], []
        for g, (window, dil) in enumerate(DSWA_GROUPS):
            if prompt:
                qkv = _inproj_attn(x, g_row, w["w_attn"][True], cos_t, sin_t, g, batch, seq, dil, tm, p_dtype, True)
                o, mx, den = _attn_prompt(qkv, g)
                keep = min(window, seq)
                kv = jnp.stack([_from_split_half(_token_order(qkv, 1, keep)), _token_order(qkv, 2, keep)], axis=2)
                bufs_out[g].append(kv.astype(F32).reshape(batch, keep, 2, HEADS_PER_GROUP, HEAD_DIM))
            else:
                qkv = _inproj_attn(x, g_row, w["w_attn"][False], cos_t, sin_t, g, 1, batch * seq, 1, tm, p_dtype, False)
                o, lse, cache_out[g] = _attn_sample(qkv.reshape(3, batch * seq, COL_BLOCK), caches[g], cache_out[g],
                                                    l, g, seq)
                o = o.reshape(1, 1, batch * seq, A_OUT)
                mx = lse.reshape(1, 1, batch * seq, LANE)
                den = jnp.ones_like(mx)
            outs.append(o)
            maxes.append(mx)
            dens.append(den)
        if prompt:
            ob, s_new = _gla(rest, w["wa"], w["ba"], w["gg"], None, l, batch, seq, GLA_CHUNK, 8, BF16)
            x = _merge(x, outs, maxes, dens, ob, rest, w["wpa"], w["wpb"], w["wo"], expand, seq, min(tm, 512))
        else:
            ob, s_new = _gla(rest, w["wa"], w["ba"], w["gg"], gla_state, l, batch, seq, seq, 1, F32)
            x = _merge(x, outs, maxes, dens, ob, rest, w["wpa"], w["wpb"], w["wo"], expand, batch * seq, tm)
        states_out.append(s_new)
        i = l // 2
        if l % 2 == 0:
            x = _ffn(x, g_ffn[l][None, :], ffw["wg"][i], ffw["wu"][i], ffw["wd"][i], tm, D_FF // 2)
        else:
            comb = _router(x, g_ffn[l][None, :], ffw["wr"][i], tm)
            x = _moe(x, g_ffn[l][None, :], comb, ffw["eg"][i], ffw["eu"][i], ffw["ed"][i],
                     moe_tile, moe_sub, D_FF_EXPERT // 4)
    y = _final_norm(x, g_final[None, :], tm)
    if prompt:
        bufs = [jnp.stack(b) for b in bufs_out]
    else:
        bufs = [jnp.transpose(c.reshape(DEPTH, batch, 2, HEADS_PER_GROUP, HEAD_DIM, c.shape[-1]), (0, 1, 5, 2, 3, 4))
                for c in cache_out]
    return (y, bufs[0], bufs[1], bufs[2], jnp.stack(states_out))


def kernel(x_prompt, x_sample, cache_kv_w128, cache_kv_w512, cache_kv_w2048, state_gla, g_mix, w_in, w_gate_a2, b_gate_a, g_gla, w_branch_a, w_branch_b, w_out, g_ffn, w_ff_gate, w_ff_up, w_ff_down, w_router, w_exp_gate, w_exp_up, w_exp_down, g_final):
    batch, seq, _ = x_prompt.shape
    dec_batch, dec_seq, _ = x_sample.shape
    assert seq % (DSWA_GROUPS[-1][1] * Q_BLOCK) == 0 and seq % (8 * GLA_CHUNK) == 0
    assert dec_seq % GLA_CHUNK != 0 and dec_seq % 8 == 0
    assert all(c.shape[2] == w for c, (w, _) in zip((cache_kv_w128, cache_kv_w512, cache_kv_w2048), DSWA_GROUPS))

    lw = [_layer_weights(l, w_in, w_gate_a2, b_gate_a, g_gla, w_branch_a, w_branch_b, w_out) for l in range(DEPTH)]
    ffw = dict(wg=w_ff_gate.astype(BF16), wu=w_ff_up.astype(BF16), wd=w_ff_down.astype(BF16),
               wr=jnp.pad(w_router, ((0, 0), (0, 0), (0, LANE - N_EXPERTS))),
               eg=w_exp_gate.astype(BF16), eu=w_exp_up.astype(BF16), ed=w_exp_down.astype(BF16))

    y_p, kv128_p, kv512_p, kv2048_p, gla_p = _trunk(
        x_prompt.reshape(batch * seq, D_MODEL), _rope_tables(jnp.arange(seq, dtype=jnp.int32), True), None, None,
        lw, ffw, g_mix, g_ffn, g_final, batch, seq, 1024, 1024, 288, BF16)

    n_s = dec_batch * dec_seq
    pos_s = jnp.tile(PAST_LEN + jnp.arange(dec_seq, dtype=jnp.int32), dec_batch)
    caches = [jnp.transpose(c, (0, 1, 3, 4, 5, 2)).reshape(DEPTH, dec_batch, 2, A_OUT, c.shape[2])
              for c in (cache_kv_w128, cache_kv_w512, cache_kv_w2048)]
    y_s, kv128_s, kv512_s, kv2048_s, gla_s = _trunk(
        x_sample.reshape(n_s, D_MODEL), _rope_tables(pos_s, False), caches, state_gla,
        lw, ffw, g_mix, g_ffn, g_final, dec_batch, dec_seq, n_s, n_s, 96, F32)

    return (y_p.reshape(batch, seq, D_MODEL), y_s.reshape(dec_batch, dec_seq, D_MODEL),
            kv128_p, kv512_p, kv2048_p, gla_p, kv128_s, kv512_s, kv2048_s, gla_s)
```

```python
import functools

import jax
import jax.numpy as jnp
import numpy as np
from jax import lax
from jax.experimental import pallas as pl
from jax.experimental.pallas import tpu as pltpu

F32 = jnp.float32
BF16 = jnp.bfloat16

D_MODEL = 1024
DEPTH = 2
PAST_LEN = 16384
HEAD_DIM = 64
DSWA_GROUPS = ((128, 1), (512, 4), (2048, 16))
N_GROUPS = 3
HEADS_PER_GROUP = 8
A_WIDTH = N_GROUPS * HEADS_PER_GROUP * HEAD_DIM
A_OUT = HEADS_PER_GROUP * HEAD_DIM
Q_BLOCK = 128
GLA_HEADS = 4
GLA_KEY = 512
GLA_VAL = 1024
GLA_DK = 128
GLA_DV = 256
GATE_RANK = 16
GATE_TAU = 16.0
GLA_CHUNK = 64
D_FF = 2816
N_EXPERTS = 8
D_FF_EXPERT = 3584
ROPE_THETA = 10000.0
EPS = 1e-6
LOG2_E = 1.4426950408889634

LANE = 128
COL_BLOCK = 512
R_QB, R_KB, R_VB, R_RB, R_GA, R_GB, R_GLR, N_REST = 0, 1, 2, 4, 6, 8, 10, 11
ROW_CHUNK = 256
VMEM_LIMIT = 56 * 1024 * 1024


def _cparams(sem):
    return pltpu.CompilerParams(dimension_semantics=sem, vmem_limit_bytes=VMEM_LIMIT)


def _rms(xf, g):
    return xf * lax.rsqrt(jnp.mean(xf * xf, axis=-1, keepdims=True) + EPS) * g


def _dot(a, b):
    return jnp.dot(a, b, preferred_element_type=F32)


def _dot_nt(a, b):
    return lax.dot_general(a, b, (((1,), (1,)), ((), ())), preferred_element_type=F32)


def _dot_tn(a, b):
    return lax.dot_general(a, b, (((0,), (0,)), ((), ())), preferred_element_type=F32)


def _div_pow2(x, d):
    assert d & (d - 1) == 0
    return lax.shift_right_logical(x, int(d).bit_length() - 1)


def _mod_pow2(x, d):
    assert d & (d - 1) == 0
    return x & (d - 1)


def _split_dot(a_f32, b_bf16, dot=_dot):
    hi = a_f32.astype(BF16)
    lo = (a_f32 - hi.astype(F32)).astype(BF16)
    return dot(hi, b_bf16) + dot(lo, b_bf16)


def _inproj_attn_kernel(h_ref, w_ref, cos_ref, sin_ref, o_ref, *scratch, dil, split_half, q_scale):
    part = pl.program_id(1)
    scale = jnp.where(part == 0, q_scale, 1.0).astype(F32)
    tm = h_ref.shape[0]
    chunk = min(tm, ROW_CHUNK)
    n_lane_groups = COL_BLOCK // LANE
    for rc in range(tm // chunk):
        rows = slice(rc * chunk, (rc + 1) * chunk)
        acc = _dot(h_ref[rows, :], w_ref[...])
        cos = jnp.where(part == 2, 1.0, cos_ref[rows, :] * scale)
        sin = jnp.where(part == 2, 0.0, sin_ref[rows, :] * scale)
        ys = []
        if split_half:
            for c in range(0, n_lane_groups, 2):
                a = acc[:, c * LANE:(c + 1) * LANE]
                b = acc[:, (c + 1) * LANE:(c + 2) * LANE]
                ys += [a * cos - b * sin, a * sin + b * cos]
        else:
            first_half = _mod_pow2(lax.broadcasted_iota(jnp.int32, cos.shape, 1), HEAD_DIM) < HEAD_DIM // 2
            for c in range(n_lane_groups):
                xs = acc[:, c * LANE:(c + 1) * LANE]
                swapped = jnp.where(first_half, pltpu.roll(xs, LANE - HEAD_DIM // 2, 1), pltpu.roll(xs, HEAD_DIM // 2, 1))
                ys.append(xs * cos + swapped * sin)
        for c, y in enumerate(ys):
            if dil == 1:
                o_ref[0, rows, c * LANE:(c + 1) * LANE] = y.astype(o_ref.dtype)
            else:
                scratch[0][c, rows, :] = y
        if dil > 1:
            sub = chunk // dil
            for r in range(dil):
                for c in range(n_lane_groups):
                    o_ref[r, rc * sub:(rc + 1) * sub, c * LANE:(c + 1) * LANE] = (
                        scratch[0][c, pl.ds(rc * chunk + r, sub, stride=dil), :].astype(o_ref.dtype))


def _inproj_attn(h, w, cos_t, sin_t, group, batch, seq, dil, tm, out_dtype, split_half):
    tiles = seq // tm
    scratch = [pltpu.VMEM((COL_BLOCK // LANE, tm, LANE), F32)] if dil > 1 else []
    q_scale = HEAD_DIM ** -0.5 * (LOG2_E if split_half else 1.0)
    return pl.pallas_call(
        functools.partial(_inproj_attn_kernel, dil=dil, split_half=split_half, q_scale=q_scale),
        grid=(batch * tiles, 3),
        in_specs=[
            pl.BlockSpec((tm, D_MODEL), lambda i, j: (i, 0)),
            pl.BlockSpec((None, None, D_MODEL, COL_BLOCK), lambda i, j: (group, j, 0, 0)),
            pl.BlockSpec((tm, LANE), lambda i, j: (i % tiles, 0)),
            pl.BlockSpec((tm, LANE), lambda i, j: (i % tiles, 0)),
        ],
        out_specs=pl.BlockSpec((None, None, dil, tm // dil, COL_BLOCK), lambda i, j: (j, i // tiles, 0, i % tiles, 0)),
        out_shape=jax.ShapeDtypeStruct((3, batch, dil, seq // dil, COL_BLOCK), out_dtype),
        scratch_shapes=scratch,
        compiler_params=_cparams(("parallel", "arbitrary")),
        name=f"inproj_attn_g{group}",
    )(h, w, cos_t, sin_t)


def _inproj_rest_kernel(h_ref, w_ref, o_ref):
    tm = h_ref.shape[0]
    chunk = min(tm, ROW_CHUNK)
    for rc in range(tm // chunk):
        rows = slice(rc * chunk, (rc + 1) * chunk)
        o_ref[rows, :] = _dot(h_ref[rows, :], w_ref[...]).astype(o_ref.dtype)


def _inproj_rest(h, w, tm, out_dtype):
    n = h.shape[0]
    return pl.pallas_call(
        _inproj_rest_kernel,
        grid=(n // tm, N_REST),
        in_specs=[
            pl.BlockSpec((tm, D_MODEL), lambda i, j: (i, 0)),
            pl.BlockSpec((None, D_MODEL, COL_BLOCK), lambda i, j: (j, 0, 0)),
        ],
        out_specs=pl.BlockSpec((None, tm, COL_BLOCK), lambda i, j: (j, i, 0)),
        out_shape=jax.ShapeDtypeStruct((N_REST, n, COL_BLOCK), out_dtype),
        compiler_params=_cparams(("parallel", "arbitrary")),
        name="inproj_rest",
    )(h, w)


QUAD = 4
QUAD_WIDTH = QUAD * HEAD_DIM
MAX_Q_BLOCKS_PER_STEP = 4


def _attn_prompt_kernel(bias_ref, q_ref, kp_ref, kc_ref, vp_ref, vc_ref, o_ref, m_ref, l_ref):
    n = pl.program_id(2)
    rows = QUAD * Q_BLOCK
    row_head = _div_pow2(lax.broadcasted_iota(jnp.int32, (rows, QUAD_WIDTH), 0), Q_BLOCK)
    col = lax.broadcasted_iota(jnp.int32, (rows, QUAD_WIDTH), 1)
    q_lanes = _div_pow2(_mod_pow2(col, LANE), HEAD_DIM // 2) == row_head
    v_lanes = _div_pow2(col, HEAD_DIM) == row_head
    lane = lax.broadcasted_iota(jnp.int32, (Q_BLOCK, LANE), 1)
    for qb in range(q_ref.shape[0] // Q_BLOCK):
        cur = slice(qb * Q_BLOCK, (qb + 1) * Q_BLOCK)
        if qb == 0:
            bias = bias_ref[jnp.where(n == 0, 1, 0)]
        else:
            bias = bias_ref[0]
            prev = slice((qb - 1) * Q_BLOCK, qb * Q_BLOCK)
        m_all = jnp.zeros((Q_BLOCK, LANE), F32)
        l_all = jnp.ones((Q_BLOCK, LANE), F32)
        for c in range(A_OUT // QUAD_WIDTH):
            cs = slice(c * QUAD_WIDTH, (c + 1) * QUAD_WIDTH)
            q = q_ref[cur, cs].astype(F32)
            q4 = jnp.where(q_lanes, jnp.concatenate([q] * QUAD, axis=0), 0.0).astype(BF16)
            k_prev = kp_ref[:, cs] if qb == 0 else kc_ref[prev, cs]
            v_prev = vp_ref[:, cs] if qb == 0 else vc_ref[prev, cs]
            k = jnp.concatenate([k_prev, kc_ref[cur, cs]], axis=0)
            v = jnp.concatenate([v_prev, vc_ref[cur, cs]], axis=0)
            s = _dot_nt(q4, k) + bias
            m = jnp.max(s, axis=-1, keepdims=True)
            p = jnp.exp2(s - m)
            l = jnp.sum(p, axis=-1, keepdims=True)
            o4 = jnp.where(v_lanes, _dot(p.astype(BF16), v), 0.0)
            o = o4[0:Q_BLOCK]
            for j in range(QUAD):
                hs = slice(j * Q_BLOCK, (j + 1) * Q_BLOCK)
                if j > 0:
                    o = o + o4[hs]
                m_all = jnp.where(lane == c * QUAD + j, m[hs], m_all)
                l_all = jnp.where(lane == c * QUAD + j, l[hs], l_all)
            o_ref[cur, cs] = o.astype(o_ref.dtype)
        m_ref[cur, :] = m_all
        l_ref[cur, :] = l_all


def _band_bias(span):
    qi = np.arange(QUAD * Q_BLOCK)[:, None] % Q_BLOCK
    kj = np.arange(2 * Q_BLOCK)[None, :]
    dist = Q_BLOCK + qi - kj
    band = (dist >= 0) & (dist <= span)
    first = band & (kj >= Q_BLOCK)
    return jnp.asarray(np.where(np.stack([band, first]), 0.0, -np.inf), F32)


def _attn_prompt(qkv, group):
    window, dil = DSWA_GROUPS[group]
    _, batch, _, sub_len, _ = qkv.shape
    q_blocks = min(MAX_Q_BLOCKS_PER_STEP, sub_len // Q_BLOCK)
    step_rows = q_blocks * Q_BLOCK
    nb = sub_len // step_rows

    def spec(part, prev):
        if prev:
            return pl.BlockSpec((None, None, None, Q_BLOCK, COL_BLOCK),
                                lambda b, r, n: (part, b, r, jnp.maximum(n * q_blocks - 1, 0), 0))
        return pl.BlockSpec((None, None, None, step_rows, COL_BLOCK), lambda b, r, n: (part, b, r, n, 0))

    bias = _band_bias(window // dil)
    stat_spec = pl.BlockSpec((None, None, step_rows, LANE), lambda b, r, n: (b, r, n, 0))
    stat_shape = jax.ShapeDtypeStruct((batch, dil, sub_len, LANE), F32)
    return pl.pallas_call(
        _attn_prompt_kernel,
        grid=(batch, dil, nb),
        in_specs=[pl.BlockSpec(bias.shape, lambda b, r, n: (0, 0, 0)),
                  spec(0, False), spec(1, True), spec(1, False), spec(2, True), spec(2, False)],
        out_specs=[pl.BlockSpec((None, None, step_rows, A_OUT), lambda b, r, n: (b, r, n, 0)), stat_spec, stat_spec],
        out_shape=[jax.ShapeDtypeStruct((batch, dil, sub_len, A_OUT), BF16), stat_shape, stat_shape],
        compiler_params=_cparams(("parallel", "parallel", "arbitrary")),
        name=f"attn_prompt_g{group}",
    )(bias, qkv, qkv, qkv, qkv, qkv)


def _kv_tail_kernel(*refs, dil, aliased):
    if aliased:
        k_ref, v_ref, _, out_ref, scr = refs
        _kv_tail_body(k_ref, v_ref, out_ref, scr, dil)
        return
    k_ref, v_ref, out_ref, scr = refs

    @pl.when(pl.program_id(0) == 0)
    def _():
        _kv_tail_body(k_ref, v_ref, out_ref, scr, dil)

    @pl.when(pl.program_id(0) > 0)
    def _():
        out_ref[...] = jnp.zeros_like(out_ref)


def _kv_tail_body(k_ref, v_ref, out_ref, scr, dil):
    keep = out_ref.shape[2]
    sub = keep // dil
    half_w = HEAD_DIM // 2
    for kv, ref in enumerate((k_ref, v_ref)):
        for c in range(COL_BLOCK // LANE):
            cs = slice(c * LANE, (c + 1) * LANE)
            if dil == 1:
                tok = ref[0, :, cs].astype(F32)
            else:
                for r in range(dil):
                    scr[pl.ds(r, sub, stride=dil), :] = ref[r, :, cs].astype(F32)
                tok = scr[...]
            chan = tok.T
            if kv == 1:
                out_ref[kv, cs, :] = chan
            else:
                for j in range(QUAD):
                    dst = (QUAD * (c // 2) + j) * HEAD_DIM + (c % 2) * half_w
                    out_ref[kv, dst:dst + half_w, :] = chan[j * half_w:(j + 1) * half_w, :]


def _kv_tail(qkv, prev_out, layer, keep):
    _, batch, dil, sub_len, _ = qkv.shape
    last = sub_len // (keep // dil) - 1
    first = prev_out is None
    assert first == (layer == 0)
    phases = DEPTH if first else 1

    def spec(part):
        return pl.BlockSpec((None, None, dil, keep // dil, COL_BLOCK),
                            lambda p, b: (part, jnp.where(p == 0, b, batch - 1), 0, last, 0))

    in_specs = [spec(1), spec(2)]
    args = [qkv, qkv]
    aliases = {}
    if not first:
        in_specs.append(pl.BlockSpec(memory_space=pl.ANY))
        args.append(prev_out)
        aliases = {2: 0}
    return pl.pallas_call(
        functools.partial(_kv_tail_kernel, dil=dil, aliased=not first),
        grid=(phases, batch),
        in_specs=in_specs,
        out_specs=pl.BlockSpec((None, None, 2, A_OUT, keep), lambda p, b: (layer + p, b, 0, 0, 0)),
        out_shape=jax.ShapeDtypeStruct((DEPTH, batch, 2, A_OUT, keep), F32),
        scratch_shapes=[pltpu.VMEM((keep, LANE), F32)],
        input_output_aliases=aliases,
        compiler_params=_cparams(("arbitrary", "arbitrary")),
        name=f"kv_tail_{keep}",
    )(*args)


SHIFT_ROWS = 128


def _attn_sample_kernel(*refs, dil, span, aliased):
    if aliased:
        q_ref, k_ref, v_ref, c_ref, _, o_ref, lse_ref, cout_ref = refs
        _attn_sample_body(q_ref, k_ref, v_ref, c_ref, o_ref, lse_ref, cout_ref, dil, span)
        return
    q_ref, k_ref, v_ref, c_ref, o_ref, lse_ref, cout_ref = refs

    @pl.when(pl.program_id(0) == 0)
    def _():
        _attn_sample_body(q_ref, k_ref, v_ref, c_ref, o_ref, lse_ref, cout_ref, dil, span)

    @pl.when(pl.program_id(0) > 0)
    def _():
        cout_ref[...] = jnp.zeros_like(cout_ref)


def _attn_sample_body(q_ref, k_ref, v_ref, c_ref, o_ref, lse_ref, cout_ref, dil, span):
    t_new = q_ref.shape[0]
    lb = c_ref.shape[2]
    rows = HEADS_PER_GROUP * t_new
    q = q_ref[...].astype(F32)
    k_new = k_ref[...].astype(F32)
    v_new = v_ref[...].astype(F32)

    lane = lax.broadcasted_iota(jnp.int32, (SHIFT_ROWS, LANE), 1)
    zeros = jnp.zeros((LANE - t_new, A_OUT), F32)
    for kv, new in enumerate((k_new, v_new)):
        new_t = pltpu.roll(jnp.concatenate([new, zeros], axis=0).T, LANE - t_new, 1)
        for rb in range(A_OUT // SHIFT_ROWS):
            rs = slice(rb * SHIFT_ROWS, (rb + 1) * SHIFT_ROWS)
            shifted = pltpu.roll(c_ref[kv, rs, :], lb - t_new, 1)
            if lb > LANE:
                cout_ref[kv, rs, 0:lb - LANE] = shifted[:, 0:lb - LANE]
            cout_ref[kv, rs, lb - LANE:lb] = jnp.where(lane >= LANE - t_new, new_t[rs, :], shifted[:, lb - LANE:lb])

    q_rep = jnp.concatenate([q] * HEADS_PER_GROUP, axis=0)
    row_head = _div_pow2(lax.broadcasted_iota(jnp.int32, (rows, A_OUT), 0), t_new)
    col_head = _div_pow2(lax.broadcasted_iota(jnp.int32, (rows, A_OUT), 1), HEAD_DIM)
    diag = row_head == col_head
    q_bd = jnp.where(diag, q_rep, 0.0).astype(BF16)

    s_buf = _dot(q_bd, c_ref[0].astype(BF16))
    s_new = _dot_nt(q_bd, k_new.astype(BF16))

    tok_b = _mod_pow2(lax.broadcasted_iota(jnp.int32, (rows, lb), 0), t_new)
    pos_b = lax.broadcasted_iota(jnp.int32, (rows, lb), 1)
    dist_b = lb + tok_b - pos_b
    ok_b = (_mod_pow2(dist_b, dil) == 0) & (dist_b <= span * dil)
    tok_n = _mod_pow2(lax.broadcasted_iota(jnp.int32, (rows, t_new), 0), t_new)
    pos_n = lax.broadcasted_iota(jnp.int32, (rows, t_new), 1)
    dist_n = tok_n - pos_n
    ok_n = (dist_n >= 0) & (_mod_pow2(dist_n, dil) == 0) & (dist_n <= span * dil)
    s_buf = jnp.where(ok_b, s_buf, -jnp.inf)
    s_new = jnp.where(ok_n, s_new, -jnp.inf)

    m = jnp.maximum(jnp.max(s_buf, axis=-1, keepdims=True), jnp.max(s_new, axis=-1, keepdims=True))
    p_buf = jnp.exp(s_buf - m)
    p_new = jnp.exp(s_new - m)
    l = jnp.sum(p_buf, axis=-1, keepdims=True) + jnp.sum(p_new, axis=-1, keepdims=True)
    o_full = (_dot_nt(p_buf.astype(BF16), c_ref[1].astype(BF16)) + _dot(p_new.astype(BF16), v_new.astype(BF16))) / l
    o_full = jnp.where(diag, o_full, 0.0)
    lse = (m + jnp.log(l)) * LOG2_E

    o = o_full[0:t_new, :]
    lane_t = lax.broadcasted_iota(jnp.int32, (t_new, LANE), 1)
    lse_all = jnp.zeros((t_new, LANE), F32)
    for h in range(HEADS_PER_GROUP):
        if h > 0:
            o = o + o_full[h * t_new:(h + 1) * t_new, :]
        lse_all = jnp.where(lane_t == h, lse[h * t_new:(h + 1) * t_new, :], lse_all)
    o_ref[...] = o.astype(o_ref.dtype)
    lse_ref[...] = lse_all


def _attn_sample(qkv, cache, prev_out, layer, group, t_new):
    window, dil = DSWA_GROUPS[group]
    n_rows = qkv.shape[1]
    batch = n_rows // t_new
    lb = cache.shape[-1]

    first = prev_out is None
    assert first == (layer == 0)
    phases = DEPTH if first else 1

    def row(p, b):
        return jnp.where(p == 0, b, batch - 1)

    def pspec(part):
        return pl.BlockSpec((None, t_new, COL_BLOCK), lambda p, b: (part, row(p, b), 0))

    in_specs = [pspec(0), pspec(1), pspec(2),
                pl.BlockSpec((None, None, 2, A_OUT, lb), lambda p, b: (layer, row(p, b), 0, 0, 0))]
    args = [qkv, qkv, qkv, cache]
    aliases = {}
    if not first:
        in_specs.append(pl.BlockSpec(memory_space=pl.ANY))
        args.append(prev_out)
        aliases = {4: 2}
    return pl.pallas_call(
        functools.partial(_attn_sample_kernel, dil=dil, span=window // dil, aliased=not first),
        grid=(phases, batch),
        in_specs=in_specs,
        out_specs=[
            pl.BlockSpec((t_new, A_OUT), lambda p, b: (row(p, b), 0)),
            pl.BlockSpec((t_new, LANE), lambda p, b: (row(p, b), 0)),
            pl.BlockSpec((None, None, 2, A_OUT, lb), lambda p, b: (layer + p, b, 0, 0, 0)),
        ],
        out_shape=[
            jax.ShapeDtypeStruct((n_rows, A_OUT), F32),
            jax.ShapeDtypeStruct((n_rows, LANE), F32),
            jax.ShapeDtypeStruct(cache.shape, F32),
        ],
        input_output_aliases=aliases,
        compiler_params=_cparams(("arbitrary", "arbitrary")),
        name=f"attn_sample_g{group}",
    )(*args)


def _gla_kernel(*refs, chunk, n_chunks, has_s0):
    if has_s0:
        q_ref, k_ref, v_ref, r_ref, glr_ref, wa_ref, ba_ref, gg_ref, s0_ref, o_ref, sout_ref, s_ref = refs
    else:
        q_ref, k_ref, v_ref, r_ref, glr_ref, wa_ref, ba_ref, gg_ref, o_ref, sout_ref, s_ref = refs
    step = pl.program_id(2)

    @pl.when(step == 0)
    def _():
        if has_s0:
            s_ref[...] = s0_ref[...].astype(F32)
        else:
            s_ref[...] = jnp.zeros_like(s_ref)

    ri = lax.broadcasted_iota(jnp.int32, (chunk, chunk), 0)
    ci = lax.broadcasted_iota(jnp.int32, (chunk, chunk), 1)
    causal = ri >= ci
    tri = causal.astype(BF16)
    gg = gg_ref[...]

    gate = _dot(glr_ref[...].astype(BF16), wa_ref[...]) + ba_ref[...]
    log_a = jax.nn.log_sigmoid(gate) * (1.0 / GATE_TAU)
    log_hi = log_a.astype(BF16)
    log_lo = (log_a - log_hi.astype(F32)).astype(BF16)
    chunks = [slice(c * chunk, (c + 1) * chunk) for c in range(n_chunks)]
    cums = [_dot(tri, log_hi[rs, :]) + _dot(tri, log_lo[rs, :]) for rs in chunks]
    lasts = [cum[chunk - 1:chunk, :] for cum in cums]
    cum = jnp.concatenate(cums, axis=0)
    last = jnp.concatenate([jnp.broadcast_to(l, (chunk, GLA_DK)) for l in lasts], axis=0)
    grow = jnp.exp(cum)
    q_in = (q_ref[...].astype(F32) * (GLA_DK ** -0.5) * grow).astype(BF16)
    k = k_ref[...].astype(F32)
    k_in = (k * jnp.exp(-cum)).astype(BF16)
    k_out = (k * jnp.exp(last - cum)).astype(BF16)
    v = v_ref[...].astype(BF16)
    o_intra, kv, decay = [], [], []
    for rs, l in zip(chunks, lasts):
        att = jnp.where(causal, _dot_nt(q_in[rs, :], k_in[rs, :]), 0.0).astype(BF16)
        o_intra.append(_dot(att, v[rs, :]))
        kv.append(_dot_tn(k_out[rs, :], v[rs, :]))
        d = jnp.broadcast_to(jnp.exp(l), (GLA_DK, GLA_DK)).T
        decay.append(jnp.concatenate([d] * (GLA_DV // GLA_DK), axis=1))

    state = s_ref[...]
    for c, rs in enumerate(chunks):
        o = o_intra[c] + _dot(q_in[rs, :], state.astype(BF16))
        state = decay[c] * state + kv[c]
        y = _rms(o, gg) * jax.nn.silu(r_ref[rs, :].astype(F32))
        o_ref[rs, :] = y.astype(o_ref.dtype)
    s_ref[...] = state

    @pl.when(step == pl.num_programs(2) - 1)
    def _():
        sout_ref[...] = state


def _gla(rest, wa, ba, gg, s0, layer, batch, seq, chunk, n_chunks, out_dtype):
    n_rows = rest.shape[1]
    lblk = chunk * n_chunks
    steps = seq // lblk

    def pspec(width, block):
        per = COL_BLOCK // width
        return pl.BlockSpec((None, lblk, width), lambda b, h, n: (block + h // per, b * steps + n, h % per))

    in_specs = [
        pspec(GLA_DK, R_QB), pspec(GLA_DK, R_KB), pspec(GLA_DV, R_VB), pspec(GLA_DV, R_RB),
        pl.BlockSpec((None, lblk, LANE), lambda b, h, n: (R_GLR, b * steps + n, 0)),
        pl.BlockSpec((LANE, GLA_DK), lambda b, h, n: (0, h)),
        pl.BlockSpec((1, GLA_DK), lambda b, h, n: (0, h)),
        pl.BlockSpec((1, GLA_DV), lambda b, h, n: (0, h)),
    ]
    args = [rest, rest, rest, rest, rest, wa, ba, gg]
    if s0 is not None:
        in_specs.append(pl.BlockSpec((None, None, None, GLA_DK, GLA_DV), lambda b, h, n: (layer, b, h, 0, 0)))
        args.append(s0)
    return pl.pallas_call(
        functools.partial(_gla_kernel, chunk=chunk, n_chunks=n_chunks, has_s0=s0 is not None),
        grid=(batch, GLA_HEADS, steps),
        in_specs=in_specs,
        out_specs=[
            pl.BlockSpec((lblk, GLA_DV), lambda b, h, n: (b * steps + n, h)),
            pl.BlockSpec((None, None, GLA_DK, GLA_DV), lambda b, h, n: (b, h, 0, 0)),
        ],
        out_shape=[
            jax.ShapeDtypeStruct((n_rows, GLA_VAL), out_dtype),
            jax.ShapeDtypeStruct((batch, GLA_HEADS, GLA_DK, GLA_DV), F32),
        ],
        scratch_shapes=[pltpu.VMEM((GLA_DK, GLA_DV), F32)],
        compiler_params=_cparams(("parallel", "parallel", "arbitrary")),
        name="gla",
    )(*args)


def _merge_kernel(x_ref, o1_ref, o2_ref, o3_ref, m1_ref, m2_ref, m3_ref, l1_ref, l2_ref, l3_ref, ob_ref, ga_ref, gb_ref,
                  wpa_ref, wpb_ref, wo_ref, e_ref, out_ref, o_scr, l_scr, *, dils):
    tm = x_ref.shape[0]

    def token_order(ref, scr, dil):
        if dil == 1:
            return ref[0].astype(F32)
        chunks = ref.shape[2] // LANE
        for r in range(dil):
            for c in range(chunks):
                scr[c, pl.ds(r, tm // dil, stride=dil), :] = ref[r, :, c * LANE:(c + 1) * LANE].astype(F32)
        return jnp.concatenate([scr[c] for c in range(chunks)], axis=1)

    ms = [token_order(m_ref, l_scr, d) for m_ref, d in zip((m1_ref, m2_ref, m3_ref), dils)]
    ls = [token_order(l_ref, l_scr, d) for l_ref, d in zip((l1_ref, l2_ref, l3_ref), dils)]
    m = jnp.maximum(jnp.maximum(ms[0], ms[1]), ms[2])
    ws = [jnp.exp2(mg - m) for mg in ms]
    inv = 1.0 / (ws[0] * ls[0] + ws[1] * ls[1] + ws[2] * ls[2])
    expand = e_ref[...]
    o_a = jnp.zeros((tm, A_OUT), F32)
    for w, o_ref, d in zip(ws, (o1_ref, o2_ref, o3_ref), dils):
        o_a = o_a + _split_dot(w * inv, expand) * token_order(o_ref, o_scr, d)
    ya = _dot(o_a.astype(BF16), wpa_ref[...])
    yb = _dot(ob_ref[...].astype(BF16), wpb_ref[...])
    gate_a = jnp.concatenate([ga_ref[0], ga_ref[1]], axis=1).astype(F32)
    gate_b = jnp.concatenate([gb_ref[0], gb_ref[1]], axis=1).astype(F32)
    merged = jax.nn.sigmoid(gate_a) * ya + jax.nn.sigmoid(gate_b) * yb
    out_ref[...] = x_ref[...] + _dot(merged.astype(BF16), wo_ref[...])


def _merge(x, outs, maxes, dens, ob, rest, wpa, wpb, wo, expand, seq, tm):
    n = x.shape[0]
    tiles = seq // tm
    dils = tuple(o.shape[1] for o in outs)

    def rows(width):
        return pl.BlockSpec((tm, width), lambda i: (i, 0))

    def dilated(a):
        dil, width = a.shape[1], a.shape[3]
        return pl.BlockSpec((None, dil, tm // dil, width), lambda i: (i // tiles, 0, i % tiles, 0))

    def gate(block):
        return pl.BlockSpec((2, tm, COL_BLOCK), lambda i: (block // 2, i, 0))

    def full(a):
        return pl.BlockSpec(a.shape, lambda i: (0, 0))

    return pl.pallas_call(
        functools.partial(_merge_kernel, dils=dils),
        grid=(n // tm,),
        in_specs=[rows(D_MODEL)] + [dilated(a) for a in (*outs, *maxes, *dens)] + [rows(GLA_VAL)]
                 + [gate(R_GA), gate(R_GB)] + [full(wpa), full(wpb), full(wo), full(expand)],
        out_specs=rows(D_MODEL),
        out_shape=jax.ShapeDtypeStruct((n, D_MODEL), F32),
        scratch_shapes=[pltpu.VMEM((A_OUT // LANE, tm, LANE), F32), pltpu.VMEM((1, tm, LANE), F32)],
        compiler_params=_cparams(("parallel",)),
        name="merge",
    )(x, *outs, *maxes, *dens, ob, rest, rest, wpa, wpb, wo, expand)


def _ffn_kernel(x_ref, g_ref, wg_ref, wu_ref, wd_ref, o_ref, h_ref, acc_ref):
    f = pl.program_id(1)

    @pl.when(f == 0)
    def _():
        h_ref[...] = _rms(x_ref[...], g_ref[...]).astype(BF16)
        acc_ref[...] = x_ref[...]

    h = h_ref[...]
    mid = jax.nn.silu(_dot(h, wg_ref[...])) * _dot(h, wu_ref[...])
    acc_ref[...] += _dot(mid.astype(BF16), wd_ref[...])

    @pl.when(f == pl.num_programs(1) - 1)
    def _():
        o_ref[...] = acc_ref[...]


def _ffn(x, g, wg, wu, wd, tm, tf):
    n = x.shape[0]
    d_ff = wg.shape[1]
    return pl.pallas_call(
        _ffn_kernel,
        grid=(n // tm, d_ff // tf),
        in_specs=[
            pl.BlockSpec((tm, D_MODEL), lambda i, f: (i, 0)),
            pl.BlockSpec((1, D_MODEL), lambda i, f: (0, 0)),
            pl.BlockSpec((D_MODEL, tf), lambda i, f: (0, f)),
            pl.BlockSpec((D_MODEL, tf), lambda i, f: (0, f)),
            pl.BlockSpec((tf, D_MODEL), lambda i, f: (f, 0)),
        ],
        out_specs=pl.BlockSpec((tm, D_MODEL), lambda i, f: (i, 0)),
        out_shape=jax.ShapeDtypeStruct((n, D_MODEL), F32),
        scratch_shapes=[pltpu.VMEM((tm, D_MODEL), BF16), pltpu.VMEM((tm, D_MODEL), F32)],
        compiler_params=_cparams(("parallel", "arbitrary")),
        name="ffn",
    )(x, g, wg, wu, wd)


def _router_kernel(x_ref, g_ref, wr_ref, comb_ref):
    h = _rms(x_ref[...], g_ref[...])
    logits = jnp.dot(h, wr_ref[...], preferred_element_type=F32, precision=lax.Precision.HIGHEST)
    lane = lax.broadcasted_iota(jnp.int32, logits.shape, 1)
    logits = jnp.where(lane < N_EXPERTS, logits, -jnp.inf)
    v1 = jnp.max(logits, axis=-1, keepdims=True)
    i1 = jnp.min(jnp.where(logits == v1, lane, LANE), axis=-1, keepdims=True)
    rest = jnp.where(lane == i1, -jnp.inf, logits)
    v2 = jnp.max(rest, axis=-1, keepdims=True)
    i2 = jnp.min(jnp.where(rest == v2, lane, LANE), axis=-1, keepdims=True)
    e2 = jnp.exp(v2 - v1)
    g1 = 1.0 / (1.0 + e2)
    g2 = e2 / (1.0 + e2)
    comb_ref[...] = jnp.where(lane == i1, g1, 0.0) + jnp.where(lane == i2, g2, 0.0)


def _router(x, g, wr, tm):
    n = x.shape[0]
    return pl.pallas_call(
        _router_kernel,
        grid=(n // tm,),
        in_specs=[
            pl.BlockSpec((tm, D_MODEL), lambda i: (i, 0)),
            pl.BlockSpec((1, D_MODEL), lambda i: (0, 0)),
            pl.BlockSpec((D_MODEL, LANE), lambda i: (0, 0)),
        ],
        out_specs=pl.BlockSpec((tm, LANE), lambda i: (i, 0)),
        out_shape=jax.ShapeDtypeStruct((n, LANE), F32),
        compiler_params=_cparams(("parallel",)),
        name="router",
    )(x, g, wr)


def _route_tables(comb, tile):
    n = comb.shape[0]
    tiles = n // tile
    sel = (comb[:, :N_EXPERTS] > 0).reshape(tiles, tile, N_EXPERTS)
    seli = sel.astype(jnp.int32)
    slot = jnp.where(sel, jnp.cumsum(seli, axis=1) - seli, -1).astype(F32)
    counts = jnp.sum(seli, axis=1).reshape(tiles * N_EXPERTS)
    slot_col = jnp.pad(slot.reshape(n, N_EXPERTS), ((0, 0), (0, LANE - N_EXPERTS)), constant_values=-1.0)
    slot_row = jnp.swapaxes(slot, 1, 2).reshape(tiles * N_EXPERTS, 1, tile)
    return counts, slot_col, slot_row


def _moe_kernel(cnt_ref, x_ref, g_ref, comb_ref, scol_ref, srow_ref, wg_ref, wu_ref, wd_ref, o_ref,
                h_ref, xc_ref, y_ref, *, sub, n_sub):
    i, e, f = pl.program_id(0), pl.program_id(1), pl.program_id(2)
    last_f = pl.num_programs(2) - 1
    tile = x_ref.shape[0]
    sub_pad = y_ref.shape[1]
    count = cnt_ref[i * N_EXPERTS + e]

    @pl.when((e == 0) & (f == 0))
    def _():
        h_ref[...] = _rms(x_ref[...], g_ref[...]).astype(BF16)
        o_ref[...] = x_ref[...]

    for s in range(n_sub):
        @pl.when((f == 0) & (s * sub < count))
        def _():
            slots = (lax.broadcasted_iota(jnp.int32, (sub, tile), 0) + s * sub).astype(F32)
            pick = (srow_ref[...] == slots).astype(BF16)
            xc_ref[s] = _dot(pick, h_ref[...]).astype(BF16)
            y_ref[s] = jnp.zeros((sub_pad, D_MODEL), F32)

        @pl.when(s * sub < count)
        def _():
            xs = xc_ref[s]
            mid = jax.nn.silu(_dot(xs, wg_ref[...])) * _dot(xs, wu_ref[...])
            y_ref[s, 0:sub, :] += _dot(mid.astype(BF16), wd_ref[...])

        @pl.when((f == last_f) & (s * sub < count))
        def _():
            mine = lax.broadcasted_iota(jnp.int32, (tile, LANE), 1) == e
            slot = jnp.sum(jnp.where(mine, scol_ref[...], 0.0), axis=-1, keepdims=True)
            gate = jnp.sum(jnp.where(mine, comb_ref[...], 0.0), axis=-1, keepdims=True)
            slots = (lax.broadcasted_iota(jnp.int32, (tile, sub_pad), 1) + s * sub).astype(F32)
            place = (slot == slots).astype(BF16)
            o_ref[...] += gate * _dot(place, y_ref[s].astype(BF16))


def _moe(x, g, comb, wg, wu, wd, tile, sub, tf):
    n = x.shape[0]
    n_sub = -(-tile // sub)
    sub_pad = -(-sub // LANE) * LANE
    counts, slot_col, slot_row = _route_tables(comb, tile)
    once = pl.Buffered(1)
    grid_spec = pltpu.PrefetchScalarGridSpec(
        num_scalar_prefetch=1,
        grid=(n // tile, N_EXPERTS, D_FF_EXPERT // tf),
        in_specs=[
            pl.BlockSpec((tile, D_MODEL), lambda i, e, f, c: (i, 0), pipeline_mode=once),
            pl.BlockSpec((1, D_MODEL), lambda i, e, f, c: (0, 0)),
            pl.BlockSpec((tile, LANE), lambda i, e, f, c: (i, 0)),
            pl.BlockSpec((tile, LANE), lambda i, e, f, c: (i, 0)),
            pl.BlockSpec((None, 1, tile), lambda i, e, f, c: (i * N_EXPERTS + e, 0, 0)),
            pl.BlockSpec((None, D_MODEL, tf), lambda i, e, f, c: (e, 0, f)),
            pl.BlockSpec((None, D_MODEL, tf), lambda i, e, f, c: (e, 0, f)),
            pl.BlockSpec((None, tf, D_MODEL), lambda i, e, f, c: (e, f, 0)),
        ],
        out_specs=pl.BlockSpec((tile, D_MODEL), lambda i, e, f, c: (i, 0), pipeline_mode=once),
        scratch_shapes=[
            pltpu.VMEM((tile, D_MODEL), BF16),
            pltpu.VMEM((n_sub, sub, D_MODEL), BF16),
            pltpu.VMEM((n_sub, sub_pad, D_MODEL), F32),
        ],
    )
    return pl.pallas_call(
        functools.partial(_moe_kernel, sub=sub, n_sub=n_sub),
        grid_spec=grid_spec,
        out_shape=jax.ShapeDtypeStruct((n, D_MODEL), F32),
        compiler_params=_cparams(("parallel", "arbitrary", "arbitrary")),
        name="moe",
    )(counts, x, g, comb, slot_col, slot_row, wg, wu, wd)


def _norm_kernel(x_ref, g_ref, o_ref):
    o_ref[...] = _rms(x_ref[...], g_ref[...]).astype(o_ref.dtype)


def _norm(x, g, tm, out_dtype):
    n = x.shape[0]
    return pl.pallas_call(
        _norm_kernel,
        grid=(n // tm,),
        in_specs=[pl.BlockSpec((tm, D_MODEL), lambda i: (i, 0)), pl.BlockSpec((1, D_MODEL), lambda i: (0, 0))],
        out_specs=pl.BlockSpec((tm, D_MODEL), lambda i: (i, 0)),
        out_shape=jax.ShapeDtypeStruct((n, D_MODEL), out_dtype),
        compiler_params=_cparams(("parallel",)),
        name="norm",
    )(x, g)


def _rope_tables(pos, split_half):
    half = HEAD_DIM // 2
    inv = ROPE_THETA ** (-jnp.arange(half, dtype=F32) / half)
    ang = pos.astype(F32)[:, None] * inv[None, :]
    cos = jnp.cos(ang)
    sin = jnp.sin(ang)
    if split_half:
        return jnp.tile(cos, (1, LANE // half)), jnp.tile(sin, (1, LANE // half))
    return (jnp.tile(jnp.concatenate([cos, cos], axis=-1), (1, LANE // HEAD_DIM)),
            jnp.tile(jnp.concatenate([-sin, sin], axis=-1), (1, LANE // HEAD_DIM)))


def _to_split_half(a):
    lead = a.shape[:-1]
    a = a.reshape(*lead, HEADS_PER_GROUP // QUAD, QUAD, 2, HEAD_DIM // 2)
    return jnp.swapaxes(a, -2, -3).reshape(*lead, A_OUT)


def _layer_weights(l, w_in, w_gate_a2, b_gate_a, g_gla, w_branch_a, w_branch_b, w_out):
    w = w_in[l]
    offs = np.cumsum((A_WIDTH, A_WIDTH, A_WIDTH, GLA_KEY, GLA_KEY, GLA_VAL, GLA_VAL, GATE_RANK, D_MODEL, D_MODEL))
    qa, ka, va, qb, kb, vb, rb, glr, ga, gb = jnp.split(w, [int(o) for o in offs[:-1]], axis=1)
    def attn_weights(split_half):
        relay = _to_split_half if split_half else (lambda a: a)
        return jnp.stack([jnp.stack([relay(qa[:, g * A_OUT:(g + 1) * A_OUT]), relay(ka[:, g * A_OUT:(g + 1) * A_OUT]),
                                     va[:, g * A_OUT:(g + 1) * A_OUT]]) for g in range(N_GROUPS)]).astype(BF16)

    w_attn = {False: attn_weights(False), True: attn_weights(True)}
    glr = jnp.pad(glr, ((0, 0), (0, COL_BLOCK - GATE_RANK)))
    rest = jnp.concatenate([qb, kb, vb, rb, ga, gb, glr], axis=1)
    w_rest = jnp.swapaxes(rest.reshape(D_MODEL, N_REST, COL_BLOCK), 0, 1).astype(BF16)
    wa = jnp.pad(w_gate_a2[l], ((0, LANE - GATE_RANK), (0, 0))).astype(BF16)
    return dict(w_attn=w_attn, w_rest=w_rest, wa=wa, ba=b_gate_a[l][None, :], gg=g_gla[l][None, :],
                wpa=w_branch_a[l].astype(BF16), wpb=w_branch_b[l].astype(BF16), wo=w_out[l].astype(BF16))


def _head_expand():
    e = np.zeros((LANE, A_OUT), np.float32)
    for h in range(HEADS_PER_GROUP):
        e[h, h * HEAD_DIM:(h + 1) * HEAD_DIM] = 1.0
    return jnp.asarray(e, BF16)


def _trunk(x, rope, caches, gla_state, lw, ffw, g_mix, g_ffn, g_final, batch, seq, tm, tm_proj, moe_tile, moe_sub,
           p_dtype):
    cos_t, sin_t = rope
    expand = _head_expand()
    prompt = caches is None
    cache_out = [None] * N_GROUPS
    states_out = []
    for l in range(DEPTH):
        w = lw[l]
        h = _norm(x, g_mix[l][None, :], tm, BF16)
        rest = _inproj_rest(h, w["w_rest"], tm_proj, p_dtype)
        outs, maxes, dens = [], [], []
        for g, (window, dil) in enumerate(DSWA_GROUPS):
            if prompt:
                qkv = _inproj_attn(h, w["w_attn"][True], cos_t, sin_t, g, batch, seq, dil, tm_proj, p_dtype, True)
                o, mx, den = _attn_prompt(qkv, g)
                cache_out[g] = _kv_tail(qkv, cache_out[g], l, min(window, seq))
            else:
                qkv = _inproj_attn(h, w["w_attn"][False], cos_t, sin_t, g, 1, batch * seq, 1, tm_proj, p_dtype, False)
                o, lse, cache_out[g] = _attn_sample(qkv.reshape(3, batch * seq, COL_BLOCK), caches[g], cache_out[g],
                                                    l, g, seq)
                o = o.reshape(1, 1, batch * seq, A_OUT)
                mx = lse.reshape(1, 1, batch * seq, LANE)
                den = jnp.ones_like(mx)
            outs.append(o)
            maxes.append(mx)
            dens.append(den)
        if prompt:
            ob, s_new = _gla(rest, w["wa"], w["ba"], w["gg"], None, l, batch, seq, GLA_CHUNK, 8, BF16)
            x = _merge(x, outs, maxes, dens, ob, rest, w["wpa"], w["wpb"], w["wo"], expand, seq, min(tm, 512))
        else:
            ob, s_new = _gla(rest, w["wa"], w["ba"], w["gg"], gla_state, l, batch, seq, seq, 1, F32)
            x = _merge(x, outs, maxes, dens, ob, rest, w["wpa"], w["wpb"], w["wo"], expand, batch * seq, tm)
        states_out.append(s_new)
        i = l // 2
        if l % 2 == 0:
            x = _ffn(x, g_ffn[l][None, :], ffw["wg"][i], ffw["wu"][i], ffw["wd"][i], tm, D_FF // 2)
        else:
            comb = _router(x, g_ffn[l][None, :], ffw["wr"][i], tm)
            x = _moe(x, g_ffn[l][None, :], comb, ffw["eg"][i], ffw["eu"][i], ffw["ed"][i],
                     moe_tile, moe_sub, D_FF_EXPERT // 2)
    y = _norm(x, g_final[None, :], tm, F32)
    bufs = [jnp.transpose(c.reshape(DEPTH, batch, 2, HEADS_PER_GROUP, HEAD_DIM, c.shape[-1]), (0, 1, 5, 2, 3, 4))
            for c in cache_out]
    return (y, bufs[0], bufs[1], bufs[2], jnp.stack(states_out))


def kernel(x_prompt, x_sample, cache_kv_w128, cache_kv_w512, cache_kv_w2048, state_gla, g_mix, w_in, w_gate_a2, b_gate_a, g_gla, w_branch_a, w_branch_b, w_out, g_ffn, w_ff_gate, w_ff_up, w_ff_down, w_router, w_exp_gate, w_exp_up, w_exp_down, g_final):
    batch, seq, _ = x_prompt.shape
    dec_batch, dec_seq, _ = x_sample.shape
    assert seq % (DSWA_GROUPS[-1][1] * Q_BLOCK) == 0 and seq % (8 * GLA_CHUNK) == 0
    assert dec_seq % GLA_CHUNK != 0 and dec_seq % 8 == 0
    assert all(c.shape[2] == w for c, (w, _) in zip((cache_kv_w128, cache_kv_w512, cache_kv_w2048), DSWA_GROUPS))

    lw = [_layer_weights(l, w_in, w_gate_a2, b_gate_a, g_gla, w_branch_a, w_branch_b, w_out) for l in range(DEPTH)]
    ffw = dict(wg=w_ff_gate.astype(BF16), wu=w_ff_up.astype(BF16), wd=w_ff_down.astype(BF16),
               wr=jnp.pad(w_router, ((0, 0), (0, 0), (0, LANE - N_EXPERTS))),
               eg=w_exp_gate.astype(BF16), eu=w_exp_up.astype(BF16), ed=w_exp_down.astype(BF16))

    y_p, kv128_p, kv512_p, kv2048_p, gla_p = _trunk(
        x_prompt.reshape(batch * seq, D_MODEL), _rope_tables(jnp.arange(seq, dtype=jnp.int32), True), None, None,
        lw, ffw, g_mix, g_ffn, g_final, batch, seq, 1024, 2048, 1024, 288, BF16)

    n_s = dec_batch * dec_seq
    pos_s = jnp.tile(PAST_LEN + jnp.arange(dec_seq, dtype=jnp.int32), dec_batch)
    caches = [jnp.transpose(c, (0, 1, 3, 4, 5, 2)).reshape(DEPTH, dec_batch, 2, A_OUT, c.shape[2])
              for c in (cache_kv_w128, cache_kv_w512, cache_kv_w2048)]
    y_s, kv128_s, kv512_s, kv2048_s, gla_s = _trunk(
        x_sample.reshape(n_s, D_MODEL), _rope_tables(pos_s, False), caches, state_gla,
        lw, ffw, g_mix, g_ffn, g_final, dec_batch, dec_seq, n_s, n_s, n_s, 96, F32)

    return (y_p.reshape(batch, seq, D_MODEL), y_s.reshape(dec_batch, dec_seq, D_MODEL),
            kv128_p, kv512_p, kv2048_p, gla_p, kv128_s, kv512_s, kv2048_s, gla_s)
```

```python
import functools

import jax
import jax.numpy as jnp
import numpy as np
from jax import lax
from jax.experimental import pallas as pl
from jax.experimental.pallas import tpu as pltpu

F32 = jnp.float32
BF16 = jnp.bfloat16

D_MODEL = 1024
DEPTH = 2
PAST_LEN = 16384
HEAD_DIM = 64
DSWA_GROUPS = ((128, 1), (512, 4), (2048, 16))
N_GROUPS = 3
HEADS_PER_GROUP = 8
A_WIDTH = N_GROUPS * HEADS_PER_GROUP * HEAD_DIM
A_OUT = HEADS_PER_GROUP * HEAD_DIM
Q_BLOCK = 128
GLA_HEADS = 4
GLA_KEY = 512
GLA_VAL = 1024
GLA_DK = 128
GLA_DV = 256
GATE_RANK = 16
GATE_TAU = 16.0
GLA_CHUNK = 64
D_FF = 2816
N_EXPERTS = 8
D_FF_EXPERT = 3584
ROPE_THETA = 10000.0
EPS = 1e-6
LOG2_E = 1.4426950408889634

LANE = 128
COL_BLOCK = 512
R_QB, R_KB, R_VB, R_RB, R_GA, R_GB, R_GLR, N_REST = 0, 1, 2, 4, 6, 8, 10, 11
ROW_CHUNK = 256
VMEM_LIMIT = 56 * 1024 * 1024


def _cparams(sem):
    return pltpu.CompilerParams(dimension_semantics=sem, vmem_limit_bytes=VMEM_LIMIT)


def _rms(xf, g):
    return xf * lax.rsqrt(jnp.mean(xf * xf, axis=-1, keepdims=True) + EPS) * g


def _dot(a, b):
    return jnp.dot(a, b, preferred_element_type=F32)


def _dot_nt(a, b):
    return lax.dot_general(a, b, (((1,), (1,)), ((), ())), preferred_element_type=F32)


def _dot_tn(a, b):
    return lax.dot_general(a, b, (((0,), (0,)), ((), ())), preferred_element_type=F32)


def _div_pow2(x, d):
    assert d & (d - 1) == 0
    return lax.shift_right_logical(x, int(d).bit_length() - 1)


def _mod_pow2(x, d):
    assert d & (d - 1) == 0
    return x & (d - 1)


def _split_dot(a_f32, b_bf16, dot=_dot):
    hi = a_f32.astype(BF16)
    lo = (a_f32 - hi.astype(F32)).astype(BF16)
    return dot(hi, b_bf16) + dot(lo, b_bf16)


def _inproj_attn_kernel(h_ref, w_ref, cos_ref, sin_ref, o_ref, *scratch, dil, split_half, q_scale):
    part = pl.program_id(1)
    scale = jnp.where(part == 0, q_scale, 1.0).astype(F32)
    tm = h_ref.shape[0]
    chunk = min(tm, ROW_CHUNK)
    n_lane_groups = COL_BLOCK // LANE
    for rc in range(tm // chunk):
        rows = slice(rc * chunk, (rc + 1) * chunk)
        acc = _dot(h_ref[rows, :], w_ref[...])
        cos = jnp.where(part == 2, 1.0, cos_ref[rows, :] * scale)
        sin = jnp.where(part == 2, 0.0, sin_ref[rows, :] * scale)
        ys = []
        if split_half:
            for c in range(0, n_lane_groups, 2):
                a = acc[:, c * LANE:(c + 1) * LANE]
                b = acc[:, (c + 1) * LANE:(c + 2) * LANE]
                ys += [a * cos - b * sin, a * sin + b * cos]
        else:
            first_half = _mod_pow2(lax.broadcasted_iota(jnp.int32, cos.shape, 1), HEAD_DIM) < HEAD_DIM // 2
            for c in range(n_lane_groups):
                xs = acc[:, c * LANE:(c + 1) * LANE]
                swapped = jnp.where(first_half, pltpu.roll(xs, LANE - HEAD_DIM // 2, 1), pltpu.roll(xs, HEAD_DIM // 2, 1))
                ys.append(xs * cos + swapped * sin)
        if dil == 1:
            for c, y in enumerate(ys):
                o_ref[0, rows, c * LANE:(c + 1) * LANE] = y.astype(o_ref.dtype)
        else:
            sub = chunk // dil
            pitch = _residue_pitch(dil)
            base = rc * sub * pitch
            for c, y in enumerate(ys):
                if pitch == dil:
                    scratch[0][c, rows, :] = y
                else:
                    for u in range(sub):
                        scratch[0][c, base + u * pitch:base + u * pitch + dil, :] = y[u * dil:(u + 1) * dil, :]
            for r in range(dil):
                for c in range(n_lane_groups):
                    o_ref[r, rc * sub:(rc + 1) * sub, c * LANE:(c + 1) * LANE] = (
                        scratch[0][c, pl.ds(base + r, sub, stride=pitch), :].astype(o_ref.dtype))


def _residue_pitch(dil):
    return dil + 8 if dil % 16 == 0 else dil


def _inproj_attn(h, w, cos_t, sin_t, group, batch, seq, dil, tm, out_dtype, split_half):
    tiles = seq // tm
    scratch = [pltpu.VMEM((COL_BLOCK // LANE, tm // dil * _residue_pitch(dil), LANE), F32)] if dil > 1 else []
    q_scale = HEAD_DIM ** -0.5 * (LOG2_E if split_half else 1.0)
    return pl.pallas_call(
        functools.partial(_inproj_attn_kernel, dil=dil, split_half=split_half, q_scale=q_scale),
        grid=(batch * tiles, 3),
        in_specs=[
            pl.BlockSpec((tm, D_MODEL), lambda i, j: (i, 0)),
            pl.BlockSpec((None, None, D_MODEL, COL_BLOCK), lambda i, j: (group, j, 0, 0)),
            pl.BlockSpec((tm, LANE), lambda i, j: (i % tiles, 0)),
            pl.BlockSpec((tm, LANE), lambda i, j: (i % tiles, 0)),
        ],
        out_specs=pl.BlockSpec((None, None, dil, tm // dil, COL_BLOCK), lambda i, j: (j, i // tiles, 0, i % tiles, 0)),
        out_shape=jax.ShapeDtypeStruct((3, batch, dil, seq // dil, COL_BLOCK), out_dtype),
        scratch_shapes=scratch,
        compiler_params=_cparams(("parallel", "arbitrary")),
        name=f"inproj_attn_g{group}",
    )(h, w, cos_t, sin_t)


def _inproj_rest_kernel(h_ref, w_ref, o_ref):
    tm = h_ref.shape[0]
    chunk = min(tm, ROW_CHUNK)
    for rc in range(tm // chunk):
        rows = slice(rc * chunk, (rc + 1) * chunk)
        o_ref[rows, :] = _dot(h_ref[rows, :], w_ref[...]).astype(o_ref.dtype)


def _inproj_rest(h, w, tm, out_dtype):
    n = h.shape[0]
    return pl.pallas_call(
        _inproj_rest_kernel,
        grid=(n // tm, N_REST),
        in_specs=[
            pl.BlockSpec((tm, D_MODEL), lambda i, j: (i, 0)),
            pl.BlockSpec((None, D_MODEL, COL_BLOCK), lambda i, j: (j, 0, 0)),
        ],
        out_specs=pl.BlockSpec((None, tm, COL_BLOCK), lambda i, j: (j, i, 0)),
        out_shape=jax.ShapeDtypeStruct((N_REST, n, COL_BLOCK), out_dtype),
        compiler_params=_cparams(("parallel", "arbitrary")),
        name="inproj_rest",
    )(h, w)


QUAD = 4
QUAD_WIDTH = QUAD * HEAD_DIM
MAX_Q_BLOCKS_PER_STEP = 4


def _attn_prompt_kernel(bias_ref, q_ref, kp_ref, kc_ref, vp_ref, vc_ref, o_ref, m_ref, l_ref):
    n = pl.program_id(2)
    rows = QUAD * Q_BLOCK
    row_head = _div_pow2(lax.broadcasted_iota(jnp.int32, (rows, QUAD_WIDTH), 0), Q_BLOCK)
    col = lax.broadcasted_iota(jnp.int32, (rows, QUAD_WIDTH), 1)
    q_lanes = _div_pow2(_mod_pow2(col, LANE), HEAD_DIM // 2) == row_head
    v_lanes = _div_pow2(col, HEAD_DIM) == row_head
    lane = lax.broadcasted_iota(jnp.int32, (Q_BLOCK, LANE), 1)
    for qb in range(q_ref.shape[0] // Q_BLOCK):
        cur = slice(qb * Q_BLOCK, (qb + 1) * Q_BLOCK)
        if qb == 0:
            bias = bias_ref[jnp.where(n == 0, 1, 0)]
        else:
            bias = bias_ref[0]
            prev = slice((qb - 1) * Q_BLOCK, qb * Q_BLOCK)
        m_all = jnp.zeros((Q_BLOCK, LANE), F32)
        l_all = jnp.ones((Q_BLOCK, LANE), F32)
        for c in range(A_OUT // QUAD_WIDTH):
            cs = slice(c * QUAD_WIDTH, (c + 1) * QUAD_WIDTH)
            q = q_ref[cur, cs].astype(F32)
            q4 = jnp.where(q_lanes, jnp.concatenate([q] * QUAD, axis=0), 0.0).astype(BF16)
            k_prev = kp_ref[:, cs] if qb == 0 else kc_ref[prev, cs]
            v_prev = vp_ref[:, cs] if qb == 0 else vc_ref[prev, cs]
            k = jnp.concatenate([k_prev, kc_ref[cur, cs]], axis=0)
            v = jnp.concatenate([v_prev, vc_ref[cur, cs]], axis=0)
            s = _dot_nt(q4, k) + bias
            m = jnp.max(s, axis=-1, keepdims=True)
            p = jnp.exp2(s - m)
            l = jnp.sum(p, axis=-1, keepdims=True)
            o4 = jnp.where(v_lanes, _dot(p.astype(BF16), v), 0.0)
            o = o4[0:Q_BLOCK]
            for j in range(QUAD):
                hs = slice(j * Q_BLOCK, (j + 1) * Q_BLOCK)
                if j > 0:
                    o = o + o4[hs]
                m_all = jnp.where(lane == c * QUAD + j, m[hs], m_all)
                l_all = jnp.where(lane == c * QUAD + j, l[hs], l_all)
            o_ref[cur, cs] = o.astype(o_ref.dtype)
        m_ref[cur, :] = m_all
        l_ref[cur, :] = l_all


def _band_bias(span):
    qi = np.arange(QUAD * Q_BLOCK)[:, None] % Q_BLOCK
    kj = np.arange(2 * Q_BLOCK)[None, :]
    dist = Q_BLOCK + qi - kj
    band = (dist >= 0) & (dist <= span)
    first = band & (kj >= Q_BLOCK)
    return jnp.asarray(np.where(np.stack([band, first]), 0.0, -np.inf), F32)


def _attn_prompt(qkv, group):
    window, dil = DSWA_GROUPS[group]
    _, batch, _, sub_len, _ = qkv.shape
    q_blocks = min(MAX_Q_BLOCKS_PER_STEP, sub_len // Q_BLOCK)
    step_rows = q_blocks * Q_BLOCK
    nb = sub_len // step_rows

    def spec(part, prev):
        if prev:
            return pl.BlockSpec((None, None, None, Q_BLOCK, COL_BLOCK),
                                lambda b, r, n: (part, b, r, jnp.maximum(n * q_blocks - 1, 0), 0))
        return pl.BlockSpec((None, None, None, step_rows, COL_BLOCK), lambda b, r, n: (part, b, r, n, 0))

    bias = _band_bias(window // dil)
    stat_spec = pl.BlockSpec((None, None, step_rows, LANE), lambda b, r, n: (b, r, n, 0))
    stat_shape = jax.ShapeDtypeStruct((batch, dil, sub_len, LANE), F32)
    return pl.pallas_call(
        _attn_prompt_kernel,
        grid=(batch, dil, nb),
        in_specs=[pl.BlockSpec(bias.shape, lambda b, r, n: (0, 0, 0)),
                  spec(0, False), spec(1, True), spec(1, False), spec(2, True), spec(2, False)],
        out_specs=[pl.BlockSpec((None, None, step_rows, A_OUT), lambda b, r, n: (b, r, n, 0)), stat_spec, stat_spec],
        out_shape=[jax.ShapeDtypeStruct((batch, dil, sub_len, A_OUT), BF16), stat_shape, stat_shape],
        compiler_params=_cparams(("parallel", "parallel", "arbitrary")),
        name=f"attn_prompt_g{group}",
    )(bias, qkv, qkv, qkv, qkv, qkv)


def _kv_tail_kernel(*refs, dil, aliased):
    if aliased:
        k_ref, v_ref, _, out_ref, scr = refs
        _kv_tail_body(k_ref, v_ref, out_ref, scr, dil)
        return
    k_ref, v_ref, out_ref, scr = refs

    @pl.when(pl.program_id(0) == 0)
    def _():
        _kv_tail_body(k_ref, v_ref, out_ref, scr, dil)

    @pl.when(pl.program_id(0) > 0)
    def _():
        out_ref[...] = jnp.zeros_like(out_ref)


def _kv_tail_body(k_ref, v_ref, out_ref, scr, dil):
    keep = out_ref.shape[2]
    sub = keep // dil
    half_w = HEAD_DIM // 2
    for kv, ref in enumerate((k_ref, v_ref)):
        for c in range(COL_BLOCK // LANE):
            cs = slice(c * LANE, (c + 1) * LANE)
            if dil == 1:
                tok = ref[0, :, cs].astype(F32)
            else:
                for r in range(dil):
                    scr[pl.ds(r, sub, stride=dil), :] = ref[r, :, cs].astype(F32)
                tok = scr[...]
            chan = tok.T
            if kv == 1:
                out_ref[kv, cs, :] = chan
            else:
                for j in range(QUAD):
                    dst = (QUAD * (c // 2) + j) * HEAD_DIM + (c % 2) * half_w
                    out_ref[kv, dst:dst + half_w, :] = chan[j * half_w:(j + 1) * half_w, :]


def _kv_tail(qkv, prev_out, layer, keep):
    _, batch, dil, sub_len, _ = qkv.shape
    last = sub_len // (keep // dil) - 1
    first = prev_out is None
    assert first == (layer == 0)
    phases = DEPTH if first else 1

    def spec(part):
        return pl.BlockSpec((None, None, dil, keep // dil, COL_BLOCK),
                            lambda p, b: (part, jnp.where(p == 0, b, batch - 1), 0, last, 0))

    in_specs = [spec(1), spec(2)]
    args = [qkv, qkv]
    aliases = {}
    if not first:
        in_specs.append(pl.BlockSpec(memory_space=pl.ANY))
        args.append(prev_out)
        aliases = {2: 0}
    return pl.pallas_call(
        functools.partial(_kv_tail_kernel, dil=dil, aliased=not first),
        grid=(phases, batch),
        in_specs=in_specs,
        out_specs=pl.BlockSpec((None, None, 2, A_OUT, keep), lambda p, b: (layer + p, b, 0, 0, 0)),
        out_shape=jax.ShapeDtypeStruct((DEPTH, batch, 2, A_OUT, keep), F32),
        scratch_shapes=[pltpu.VMEM((keep, LANE), F32)],
        input_output_aliases=aliases,
        compiler_params=_cparams(("arbitrary", "arbitrary")),
        name=f"kv_tail_{keep}",
    )(*args)


SHIFT_ROWS = 128


def _attn_sample_kernel(*refs, dil, span, first):
    if first:
        q_ref, k_ref, v_ref, c_ref, o_ref, lse_ref, cout_ref = refs

        @pl.when(pl.program_id(0) == 0)
        def _():
            _sample_attention(q_ref, k_ref, v_ref, c_ref, o_ref, lse_ref, dil, span)
            _append_new(c_ref, k_ref, v_ref, cout_ref)

        @pl.when(pl.program_id(0) > 0)
        def _():
            shift = q_ref.shape[0]
            for kv in range(2):
                for rb in range(A_OUT // SHIFT_ROWS):
                    rs = slice(rb * SHIFT_ROWS, (rb + 1) * SHIFT_ROWS)
                    cout_ref[kv, rs, :] = pltpu.roll(c_ref[kv, rs, :], c_ref.shape[2] - shift, 1)
    else:
        q_ref, k_ref, v_ref, c_ref, _, o_ref, lse_ref, cout_ref = refs
        _sample_attention(q_ref, k_ref, v_ref, c_ref, o_ref, lse_ref, dil, span)
        _append_new(c_ref, k_ref, v_ref, cout_ref)


def _append_new(c_ref, k_ref, v_ref, cout_ref):
    t_new = k_ref.shape[0]
    lb = c_ref.shape[2]
    width = cout_ref.shape[2]
    lane = lax.broadcasted_iota(jnp.int32, (SHIFT_ROWS, LANE), 1)
    zeros = jnp.zeros((LANE - t_new, A_OUT), F32)
    for kv, ref in enumerate((k_ref, v_ref)):
        new = ref[...].astype(F32)
        new_t = pltpu.roll(jnp.concatenate([new, zeros], axis=0).T, LANE - t_new, 1)
        for rb in range(A_OUT // SHIFT_ROWS):
            rs = slice(rb * SHIFT_ROWS, (rb + 1) * SHIFT_ROWS)
            shifted = pltpu.roll(c_ref[kv, rs, lb - width:lb], width - t_new, 1)
            if width > LANE:
                cout_ref[kv, rs, 0:width - LANE] = shifted[:, 0:width - LANE]
            cout_ref[kv, rs, width - LANE:width] = jnp.where(lane >= LANE - t_new, new_t[rs, :], shifted[:, width - LANE:width])


def _sample_attention(q_ref, k_ref, v_ref, c_ref, o_ref, lse_ref, dil, span):
    t_new = q_ref.shape[0]
    lb = c_ref.shape[2]
    rows = HEADS_PER_GROUP * t_new
    q = q_ref[...].astype(F32)
    k_new = k_ref[...].astype(F32)
    v_new = v_ref[...].astype(F32)

    q_rep = jnp.concatenate([q] * HEADS_PER_GROUP, axis=0)
    row_head = _div_pow2(lax.broadcasted_iota(jnp.int32, (rows, A_OUT), 0), t_new)
    col_head = _div_pow2(lax.broadcasted_iota(jnp.int32, (rows, A_OUT), 1), HEAD_DIM)
    diag = row_head == col_head
    q_bd = jnp.where(diag, q_rep, 0.0).astype(BF16)

    s_buf = _dot(q_bd, c_ref[0].astype(BF16))
    s_new = _dot_nt(q_bd, k_new.astype(BF16))

    tok_b = _mod_pow2(lax.broadcasted_iota(jnp.int32, (rows, lb), 0), t_new)
    pos_b = lax.broadcasted_iota(jnp.int32, (rows, lb), 1)
    dist_b = lb + tok_b - pos_b
    ok_b = (_mod_pow2(dist_b, dil) == 0) & (dist_b <= span * dil)
    tok_n = _mod_pow2(lax.broadcasted_iota(jnp.int32, (rows, t_new), 0), t_new)
    pos_n = lax.broadcasted_iota(jnp.int32, (rows, t_new), 1)
    dist_n = tok_n - pos_n
    ok_n = (dist_n >= 0) & (_mod_pow2(dist_n, dil) == 0) & (dist_n <= span * dil)
    s_buf = jnp.where(ok_b, s_buf, -jnp.inf)
    s_new = jnp.where(ok_n, s_new, -jnp.inf)

    m = jnp.maximum(jnp.max(s_buf, axis=-1, keepdims=True), jnp.max(s_new, axis=-1, keepdims=True))
    p_buf = jnp.exp(s_buf - m)
    p_new = jnp.exp(s_new - m)
    l = jnp.sum(p_buf, axis=-1, keepdims=True) + jnp.sum(p_new, axis=-1, keepdims=True)
    o_full = (_dot_nt(p_buf.astype(BF16), c_ref[1].astype(BF16)) + _dot(p_new.astype(BF16), v_new.astype(BF16))) / l
    o_full = jnp.where(diag, o_full, 0.0)
    lse = (m + jnp.log(l)) * LOG2_E

    o = o_full[0:t_new, :]
    lane_t = lax.broadcasted_iota(jnp.int32, (t_new, LANE), 1)
    lse_all = jnp.zeros((t_new, LANE), F32)
    for h in range(HEADS_PER_GROUP):
        if h > 0:
            o = o + o_full[h * t_new:(h + 1) * t_new, :]
        lse_all = jnp.where(lane_t == h, lse[h * t_new:(h + 1) * t_new, :], lse_all)
    o_ref[...] = o.astype(o_ref.dtype)
    lse_ref[...] = lse_all


def _attn_sample(qkv, cache, prev_out, layer, group, t_new):
    window, dil = DSWA_GROUPS[group]
    n_rows = qkv.shape[1]
    batch = n_rows // t_new
    lb = cache.shape[-1]

    first = prev_out is None
    assert first == (layer == 0) and t_new <= LANE
    phases = DEPTH if first else 1

    def row(p, b):
        return jnp.where(p == 0, b, batch - 1)

    def pspec(part):
        return pl.BlockSpec((None, t_new, COL_BLOCK), lambda p, b: (part, row(p, b), 0))

    in_specs = [pspec(0), pspec(1), pspec(2),
                pl.BlockSpec((None, None, 2, A_OUT, lb), lambda p, b: (layer + p, b, 0, 0, 0))]
    args = [qkv, qkv, qkv, cache]
    aliases = {}
    if first:
        cache_out = pl.BlockSpec((None, None, 2, A_OUT, lb), lambda p, b: (layer + p, b, 0, 0, 0))
    else:
        in_specs.append(pl.BlockSpec(memory_space=pl.ANY))
        args.append(prev_out)
        aliases = {4: 2}
        cache_out = pl.BlockSpec((None, None, 2, A_OUT, LANE), lambda p, b: (layer, b, 0, 0, lb // LANE - 1))
    return pl.pallas_call(
        functools.partial(_attn_sample_kernel, dil=dil, span=window // dil, first=first),
        grid=(phases, batch),
        in_specs=in_specs,
        out_specs=[
            pl.BlockSpec((t_new, A_OUT), lambda p, b: (row(p, b), 0)),
            pl.BlockSpec((t_new, LANE), lambda p, b: (row(p, b), 0)),
            cache_out,
        ],
        out_shape=[
            jax.ShapeDtypeStruct((n_rows, A_OUT), F32),
            jax.ShapeDtypeStruct((n_rows, LANE), F32),
            jax.ShapeDtypeStruct(cache.shape, F32),
        ],
        input_output_aliases=aliases,
        compiler_params=_cparams(("arbitrary", "arbitrary")),
        name=f"attn_sample_g{group}",
    )(*args)


def _gla_kernel(q_ref, k_ref, v_ref, r_ref, glr_ref, wa_ref, ba_ref, gg_ref, o_ref, sout_ref, s_ref, *, chunk, n_chunks):
    step = pl.program_id(2)

    @pl.when(step == 0)
    def _():
        s_ref[...] = jnp.zeros_like(s_ref)

    ri = lax.broadcasted_iota(jnp.int32, (chunk, chunk), 0)
    ci = lax.broadcasted_iota(jnp.int32, (chunk, chunk), 1)
    causal = ri >= ci
    tri = causal.astype(BF16)
    gg = gg_ref[...]

    gate = _dot(glr_ref[...].astype(BF16), wa_ref[...]) + ba_ref[...]
    log_a = jax.nn.log_sigmoid(gate) * (1.0 / GATE_TAU)
    log_hi = log_a.astype(BF16)
    log_lo = (log_a - log_hi.astype(F32)).astype(BF16)
    chunks = [slice(c * chunk, (c + 1) * chunk) for c in range(n_chunks)]
    cums = [_dot(tri, log_hi[rs, :]) + _dot(tri, log_lo[rs, :]) for rs in chunks]
    lasts = [cum[chunk - 1:chunk, :] for cum in cums]
    cum = jnp.concatenate(cums, axis=0)
    last = jnp.concatenate([jnp.broadcast_to(l, (chunk, GLA_DK)) for l in lasts], axis=0)
    grow = jnp.exp(cum)
    q_in = (q_ref[...].astype(F32) * (GLA_DK ** -0.5) * grow).astype(BF16)
    k = k_ref[...].astype(F32)
    k_in = (k * jnp.exp(-cum)).astype(BF16)
    k_out = (k * jnp.exp(last - cum)).astype(BF16)
    v = v_ref[...].astype(BF16)
    o_intra, kv, decay = [], [], []
    for rs, l in zip(chunks, lasts):
        att = jnp.where(causal, _dot_nt(q_in[rs, :], k_in[rs, :]), 0.0).astype(BF16)
        o_intra.append(_dot(att, v[rs, :]))
        kv.append(_dot_tn(k_out[rs, :], v[rs, :]))
        d = jnp.broadcast_to(jnp.exp(l), (GLA_DK, GLA_DK)).T
        decay.append(jnp.concatenate([d] * (GLA_DV // GLA_DK), axis=1))

    state = s_ref[...]
    for c, rs in enumerate(chunks):
        o = o_intra[c] + _dot(q_in[rs, :], state.astype(BF16))
        state = decay[c] * state + kv[c]
        y = _rms(o, gg) * jax.nn.silu(r_ref[rs, :].astype(F32))
        o_ref[rs, :] = y.astype(o_ref.dtype)
    s_ref[...] = state

    @pl.when(step == pl.num_programs(2) - 1)
    def _():
        sout_ref[...] = state


def _gla_decode_kernel(q_ref, k_ref, v_ref, r_ref, glr_ref, wa_ref, ba_ref, gg_ref, s0_ref, o_ref, sout_ref):
    t_new = q_ref.shape[0]
    ri = lax.broadcasted_iota(jnp.int32, (t_new, t_new), 0)
    ci = lax.broadcasted_iota(jnp.int32, (t_new, t_new), 1)
    causal = ri >= ci
    tri = causal.astype(BF16)
    glr = glr_ref[...].astype(BF16)
    heads_per_block = COL_BLOCK // GLA_DV
    for h in range(GLA_HEADS):
        ks = slice(h * GLA_DK, (h + 1) * GLA_DK)
        vb, vs = h // heads_per_block, slice((h % heads_per_block) * GLA_DV, (h % heads_per_block + 1) * GLA_DV)
        gate = _dot(glr, wa_ref[:, ks]) + ba_ref[:, ks]
        log_a = jax.nn.log_sigmoid(gate) * (1.0 / GATE_TAU)
        log_hi = log_a.astype(BF16)
        log_lo = (log_a - log_hi.astype(F32)).astype(BF16)
        cum = _dot(tri, log_hi) + _dot(tri, log_lo)
        last = cum[t_new - 1:t_new, :]
        q_in = (q_ref[:, ks].astype(F32) * (GLA_DK ** -0.5) * jnp.exp(cum)).astype(BF16)
        k = k_ref[:, ks].astype(F32)
        k_in = (k * jnp.exp(-cum)).astype(BF16)
        k_out = (k * jnp.exp(last - cum)).astype(BF16)
        v = v_ref[vb, :, vs].astype(BF16)
        att = jnp.where(causal, _dot_nt(q_in, k_in), 0.0).astype(BF16)
        state = s0_ref[h].astype(F32)
        o = _dot(att, v) + _dot(q_in, state.astype(BF16))
        d = jnp.broadcast_to(jnp.exp(last), (GLA_DK, GLA_DK)).T
        sout_ref[h] = jnp.concatenate([d] * (GLA_DV // GLA_DK), axis=1) * state + _dot_tn(k_out, v)
        os_ = slice(h * GLA_DV, (h + 1) * GLA_DV)
        y = _rms(o, gg_ref[:, os_]) * jax.nn.silu(r_ref[vb, :, vs].astype(F32))
        o_ref[:, os_] = y.astype(o_ref.dtype)


def _gla_decode(rest, wa, ba, gg, s0, layer, batch, t_new, out_dtype):
    def rows(block, n_blocks=None):
        if n_blocks is None:
            return pl.BlockSpec((None, t_new, COL_BLOCK), lambda b: (block, b, 0))
        return pl.BlockSpec((n_blocks, t_new, COL_BLOCK), lambda b: (block // n_blocks, b, 0))

    def full(a):
        return pl.BlockSpec(a.shape, lambda b: (0, 0))

    v_blocks = GLA_VAL // COL_BLOCK
    return pl.pallas_call(
        _gla_decode_kernel,
        grid=(batch,),
        in_specs=[rows(R_QB), rows(R_KB), rows(R_VB, v_blocks), rows(R_RB, v_blocks),
                  pl.BlockSpec((None, t_new, LANE), lambda b: (R_GLR, b, 0)),
                  full(wa), full(ba), full(gg),
                  pl.BlockSpec((None, None, GLA_HEADS, GLA_DK, GLA_DV), lambda b: (layer, b, 0, 0, 0))],
        out_specs=[pl.BlockSpec((t_new, GLA_VAL), lambda b: (b, 0)),
                   pl.BlockSpec((None, GLA_HEADS, GLA_DK, GLA_DV), lambda b: (b, 0, 0, 0))],
        out_shape=[jax.ShapeDtypeStruct((batch * t_new, GLA_VAL), out_dtype),
                   jax.ShapeDtypeStruct((batch, GLA_HEADS, GLA_DK, GLA_DV), F32)],
        compiler_params=_cparams(("parallel",)),
        name="gla_decode",
    )(rest, rest, rest, rest, rest, wa, ba, gg, s0)


def _gla(rest, wa, ba, gg, batch, seq, chunk, n_chunks, out_dtype):
    n_rows = rest.shape[1]
    lblk = chunk * n_chunks
    steps = seq // lblk

    def pspec(width, block):
        per = COL_BLOCK // width
        return pl.BlockSpec((None, lblk, width), lambda b, h, n: (block + h // per, b * steps + n, h % per))

    in_specs = [
        pspec(GLA_DK, R_QB), pspec(GLA_DK, R_KB), pspec(GLA_DV, R_VB), pspec(GLA_DV, R_RB),
        pl.BlockSpec((None, lblk, LANE), lambda b, h, n: (R_GLR, b * steps + n, 0)),
        pl.BlockSpec((LANE, GLA_DK), lambda b, h, n: (0, h)),
        pl.BlockSpec((1, GLA_DK), lambda b, h, n: (0, h)),
        pl.BlockSpec((1, GLA_DV), lambda b, h, n: (0, h)),
    ]
    args = [rest, rest, rest, rest, rest, wa, ba, gg]
    return pl.pallas_call(
        functools.partial(_gla_kernel, chunk=chunk, n_chunks=n_chunks),
        grid=(batch, GLA_HEADS, steps),
        in_specs=in_specs,
        out_specs=[
            pl.BlockSpec((lblk, GLA_DV), lambda b, h, n: (b * steps + n, h)),
            pl.BlockSpec((None, None, GLA_DK, GLA_DV), lambda b, h, n: (b, h, 0, 0)),
        ],
        out_shape=[
            jax.ShapeDtypeStruct((n_rows, GLA_VAL), out_dtype),
            jax.ShapeDtypeStruct((batch, GLA_HEADS, GLA_DK, GLA_DV), F32),
        ],
        scratch_shapes=[pltpu.VMEM((GLA_DK, GLA_DV), F32)],
        compiler_params=_cparams(("parallel", "parallel", "arbitrary")),
        name="gla",
    )(*args)


def _merge_kernel(x_ref, o1_ref, o2_ref, o3_ref, m1_ref, m2_ref, m3_ref, l1_ref, l2_ref, l3_ref, ob_ref, ga_ref, gb_ref,
                  wpa_ref, wpb_ref, wo_ref, e_ref, out_ref, o_scr, l_scr, *, dils):
    tm = x_ref.shape[0]

    def token_order(ref, scr, dil):
        if dil == 1:
            return ref[0].astype(F32)
        chunks = ref.shape[2] // LANE
        for r in range(dil):
            for c in range(chunks):
                scr[c, pl.ds(r, tm // dil, stride=dil), :] = ref[r, :, c * LANE:(c + 1) * LANE].astype(F32)
        return jnp.concatenate([scr[c] for c in range(chunks)], axis=1)

    ms = [token_order(m_ref, l_scr, d) for m_ref, d in zip((m1_ref, m2_ref, m3_ref), dils)]
    ls = [token_order(l_ref, l_scr, d) for l_ref, d in zip((l1_ref, l2_ref, l3_ref), dils)]
    m = jnp.maximum(jnp.maximum(ms[0], ms[1]), ms[2])
    ws = [jnp.exp2(mg - m) for mg in ms]
    inv = 1.0 / (ws[0] * ls[0] + ws[1] * ls[1] + ws[2] * ls[2])
    expand = e_ref[...]
    o_a = jnp.zeros((tm, A_OUT), F32)
    for w, o_ref, d in zip(ws, (o1_ref, o2_ref, o3_ref), dils):
        o_a = o_a + _split_dot(w * inv, expand) * token_order(o_ref, o_scr, d)
    ya = _dot(o_a.astype(BF16), wpa_ref[...])
    yb = _dot(ob_ref[...].astype(BF16), wpb_ref[...])
    gate_a = jnp.concatenate([ga_ref[0], ga_ref[1]], axis=1).astype(F32)
    gate_b = jnp.concatenate([gb_ref[0], gb_ref[1]], axis=1).astype(F32)
    merged = jax.nn.sigmoid(gate_a) * ya + jax.nn.sigmoid(gate_b) * yb
    out_ref[...] = x_ref[...] + _dot(merged.astype(BF16), wo_ref[...])


def _merge(x, outs, maxes, dens, ob, rest, wpa, wpb, wo, expand, seq, tm):
    n = x.shape[0]
    tiles = seq // tm
    dils = tuple(o.shape[1] for o in outs)

    def rows(width):
        return pl.BlockSpec((tm, width), lambda i: (i, 0))

    def dilated(a):
        dil, width = a.shape[1], a.shape[3]
        return pl.BlockSpec((None, dil, tm // dil, width), lambda i: (i // tiles, 0, i % tiles, 0))

    def gate(block):
        return pl.BlockSpec((2, tm, COL_BLOCK), lambda i: (block // 2, i, 0))

    def full(a):
        return pl.BlockSpec(a.shape, lambda i: (0, 0))

    return pl.pallas_call(
        functools.partial(_merge_kernel, dils=dils),
        grid=(n // tm,),
        in_specs=[rows(D_MODEL)] + [dilated(a) for a in (*outs, *maxes, *dens)] + [rows(GLA_VAL)]
                 + [gate(R_GA), gate(R_GB)] + [full(wpa), full(wpb), full(wo), full(expand)],
        out_specs=rows(D_MODEL),
        out_shape=jax.ShapeDtypeStruct((n, D_MODEL), F32),
        scratch_shapes=[pltpu.VMEM((A_OUT // LANE, tm, LANE), F32), pltpu.VMEM((1, tm, LANE), F32)],
        compiler_params=_cparams(("parallel",)),
        name="merge",
    )(x, *outs, *maxes, *dens, ob, rest, rest, wpa, wpb, wo, expand)


def _ffn_kernel(x_ref, g_ref, wg_ref, wu_ref, wd_ref, o_ref, h_ref, acc_ref):
    f = pl.program_id(1)

    @pl.when(f == 0)
    def _():
        h_ref[...] = _rms(x_ref[...], g_ref[...]).astype(BF16)
        acc_ref[...] = x_ref[...]

    h = h_ref[...]
    mid = jax.nn.silu(_dot(h, wg_ref[...])) * _dot(h, wu_ref[...])
    acc_ref[...] += _dot(mid.astype(BF16), wd_ref[...])

    @pl.when(f == pl.num_programs(1) - 1)
    def _():
        o_ref[...] = acc_ref[...]


def _ffn(x, g, wg, wu, wd, tm, tf):
    n = x.shape[0]
    d_ff = wg.shape[1]
    return pl.pallas_call(
        _ffn_kernel,
        grid=(n // tm, d_ff // tf),
        in_specs=[
            pl.BlockSpec((tm, D_MODEL), lambda i, f: (i, 0)),
            pl.BlockSpec((1, D_MODEL), lambda i, f: (0, 0)),
            pl.BlockSpec((D_MODEL, tf), lambda i, f: (0, f)),
            pl.BlockSpec((D_MODEL, tf), lambda i, f: (0, f)),
            pl.BlockSpec((tf, D_MODEL), lambda i, f: (f, 0)),
        ],
        out_specs=pl.BlockSpec((tm, D_MODEL), lambda i, f: (i, 0)),
        out_shape=jax.ShapeDtypeStruct((n, D_MODEL), F32),
        scratch_shapes=[pltpu.VMEM((tm, D_MODEL), BF16), pltpu.VMEM((tm, D_MODEL), F32)],
        compiler_params=_cparams(("parallel", "arbitrary")),
        name="ffn",
    )(x, g, wg, wu, wd)


def _router_kernel(x_ref, g_ref, wr_ref, comb_ref, scol_ref, srow_ref, cnt_ref):
    h = _rms(x_ref[...], g_ref[...])
    w = wr_ref[...]
    h_hi, w_hi = h.astype(BF16), w.astype(BF16)
    h_lo, w_lo = (h - h_hi.astype(F32)).astype(BF16), (w - w_hi.astype(F32)).astype(BF16)
    logits = _dot(h_hi, w_hi) + (_dot(h_hi, w_lo) + _dot(h_lo, w_hi))
    lane = lax.broadcasted_iota(jnp.int32, logits.shape, 1)
    logits = jnp.where(lane < N_EXPERTS, logits, -jnp.inf)
    v1 = jnp.max(logits, axis=-1, keepdims=True)
    i1 = jnp.min(jnp.where(logits == v1, lane, LANE), axis=-1, keepdims=True)
    rest = jnp.where(lane == i1, -jnp.inf, logits)
    v2 = jnp.max(rest, axis=-1, keepdims=True)
    i2 = jnp.min(jnp.where(rest == v2, lane, LANE), axis=-1, keepdims=True)
    e2 = jnp.exp(v2 - v1)
    g1 = 1.0 / (1.0 + e2)
    g2 = e2 / (1.0 + e2)
    comb = jnp.where(lane == i1, g1, 0.0) + jnp.where(lane == i2, g2, 0.0)
    comb_ref[...] = comb

    tile = comb.shape[0]
    routed = comb > 0.0
    earlier = lax.broadcasted_iota(jnp.int32, (tile, tile), 1) < lax.broadcasted_iota(jnp.int32, (tile, tile), 0)
    rank = _dot(earlier.astype(BF16), routed.astype(BF16))
    slot = jnp.where(routed, rank, -1.0)
    scol_ref[...] = slot
    srow_ref[...] = slot.T[0:N_EXPERTS, :]
    count = jnp.sum(routed.astype(F32), axis=0, keepdims=True)
    cnt_ref[...] = jnp.broadcast_to(count, cnt_ref.shape).astype(jnp.int32)


def _router(x, g, wr, tile):
    n = x.shape[0]
    tiles = n // tile
    comb, slot_col, slot_row, counts = pl.pallas_call(
        _router_kernel,
        grid=(tiles,),
        in_specs=[
            pl.BlockSpec((tile, D_MODEL), lambda i: (i, 0)),
            pl.BlockSpec((1, D_MODEL), lambda i: (0, 0)),
            pl.BlockSpec((D_MODEL, LANE), lambda i: (0, 0)),
        ],
        out_specs=[
            pl.BlockSpec((tile, LANE), lambda i: (i, 0)),
            pl.BlockSpec((tile, LANE), lambda i: (i, 0)),
            pl.BlockSpec((None, N_EXPERTS, tile), lambda i: (i, 0, 0)),
            pl.BlockSpec((None, 8, LANE), lambda i: (i, 0, 0)),
        ],
        out_shape=[
            jax.ShapeDtypeStruct((n, LANE), F32),
            jax.ShapeDtypeStruct((n, LANE), F32),
            jax.ShapeDtypeStruct((tiles, N_EXPERTS, tile), F32),
            jax.ShapeDtypeStruct((tiles, 8, LANE), jnp.int32),
        ],
        compiler_params=_cparams(("parallel",)),
        name="router",
    )(x, g, wr)
    return (comb, counts[:, 0, :N_EXPERTS].reshape(tiles * N_EXPERTS), slot_col,
            slot_row.reshape(tiles * N_EXPERTS, 1, tile))


def _moe_kernel(cnt_ref, x_ref, g_ref, comb_ref, scol_ref, srow_ref, wg_ref, wu_ref, wd_ref, o_ref,
                h_ref, xc_ref, y_ref, *, sub, n_sub):
    i, e, f = pl.program_id(0), pl.program_id(1), pl.program_id(2)
    last_f = pl.num_programs(2) - 1
    tile = x_ref.shape[0]
    sub_pad = y_ref.shape[1]
    count = cnt_ref[i * N_EXPERTS + e]

    @pl.when((e == 0) & (f == 0))
    def _():
        h_ref[...] = _rms(x_ref[...], g_ref[...]).astype(BF16)
        o_ref[...] = x_ref[...]

    for s in range(n_sub):
        @pl.when((f == 0) & (s * sub < count))
        def _():
            slots = (lax.broadcasted_iota(jnp.int32, (sub, tile), 0) + s * sub).astype(F32)
            pick = (srow_ref[...] == slots).astype(BF16)
            xc_ref[s] = _dot(pick, h_ref[...]).astype(BF16)
            y_ref[s] = jnp.zeros((sub_pad, D_MODEL), F32)

        @pl.when(s * sub < count)
        def _():
            xs = xc_ref[s]
            mid = jax.nn.silu(_dot(xs, wg_ref[...])) * _dot(xs, wu_ref[...])
            y_ref[s, 0:sub, :] += _dot(mid.astype(BF16), wd_ref[...])

        @pl.when((f == last_f) & (s * sub < count))
        def _():
            mine = lax.broadcasted_iota(jnp.int32, (tile, LANE), 1) == e
            slot = jnp.sum(jnp.where(mine, scol_ref[...], 0.0), axis=-1, keepdims=True)
            gate = jnp.sum(jnp.where(mine, comb_ref[...], 0.0), axis=-1, keepdims=True)
            slots = (lax.broadcasted_iota(jnp.int32, (tile, sub_pad), 1) + s * sub).astype(F32)
            place = (slot == slots).astype(BF16)
            o_ref[...] += gate * _dot(place, y_ref[s].astype(BF16))


def _moe(x, g, comb, counts, slot_col, slot_row, wg, wu, wd, sub, tf):
    n = x.shape[0]
    tile = slot_row.shape[-1]
    n_sub = -(-tile // sub)
    sub_pad = -(-sub // LANE) * LANE
    grid_spec = pltpu.PrefetchScalarGridSpec(
        num_scalar_prefetch=1,
        grid=(n // tile, N_EXPERTS, D_FF_EXPERT // tf),
        in_specs=[
            pl.BlockSpec((tile, D_MODEL), lambda i, e, f, c: (i, 0)),
            pl.BlockSpec((1, D_MODEL), lambda i, e, f, c: (0, 0)),
            pl.BlockSpec((tile, LANE), lambda i, e, f, c: (i, 0)),
            pl.BlockSpec((tile, LANE), lambda i, e, f, c: (i, 0)),
            pl.BlockSpec((None, 1, tile), lambda i, e, f, c: (i * N_EXPERTS + e, 0, 0)),
            pl.BlockSpec((None, D_MODEL, tf), lambda i, e, f, c: (e, 0, f)),
            pl.BlockSpec((None, D_MODEL, tf), lambda i, e, f, c: (e, 0, f)),
            pl.BlockSpec((None, tf, D_MODEL), lambda i, e, f, c: (e, f, 0)),
        ],
        out_specs=pl.BlockSpec((tile, D_MODEL), lambda i, e, f, c: (i, 0)),
        scratch_shapes=[
            pltpu.VMEM((tile, D_MODEL), BF16),
            pltpu.VMEM((n_sub, sub, D_MODEL), BF16),
            pltpu.VMEM((n_sub, sub_pad, D_MODEL), F32),
        ],
    )
    return pl.pallas_call(
        functools.partial(_moe_kernel, sub=sub, n_sub=n_sub),
        grid_spec=grid_spec,
        out_shape=jax.ShapeDtypeStruct((n, D_MODEL), F32),
        compiler_params=_cparams(("parallel", "arbitrary", "arbitrary")),
        name="moe",
    )(counts, x, g, comb, slot_col, slot_row, wg, wu, wd)


def _norm_kernel(x_ref, g_ref, o_ref):
    o_ref[...] = _rms(x_ref[...], g_ref[...]).astype(o_ref.dtype)


def _norm(x, g, tm, out_dtype):
    n = x.shape[0]
    return pl.pallas_call(
        _norm_kernel,
        grid=(n // tm,),
        in_specs=[pl.BlockSpec((tm, D_MODEL), lambda i: (i, 0)), pl.BlockSpec((1, D_MODEL), lambda i: (0, 0))],
        out_specs=pl.BlockSpec((tm, D_MODEL), lambda i: (i, 0)),
        out_shape=jax.ShapeDtypeStruct((n, D_MODEL), out_dtype),
        compiler_params=_cparams(("parallel",)),
        name="norm",
    )(x, g)


def _rope_tables(pos, split_half):
    half = HEAD_DIM // 2
    inv = ROPE_THETA ** (-jnp.arange(half, dtype=F32) / half)
    ang = pos.astype(F32)[:, None] * inv[None, :]
    cos = jnp.cos(ang)
    sin = jnp.sin(ang)
    if split_half:
        return jnp.tile(cos, (1, LANE // half)), jnp.tile(sin, (1, LANE // half))
    return (jnp.tile(jnp.concatenate([cos, cos], axis=-1), (1, LANE // HEAD_DIM)),
            jnp.tile(jnp.concatenate([-sin, sin], axis=-1), (1, LANE // HEAD_DIM)))


def _to_split_half(a):
    lead = a.shape[:-1]
    a = a.reshape(*lead, HEADS_PER_GROUP // QUAD, QUAD, 2, HEAD_DIM // 2)
    return jnp.swapaxes(a, -2, -3).reshape(*lead, A_OUT)


def _layer_weights(l, w_in, w_gate_a2, b_gate_a, g_gla, w_branch_a, w_branch_b, w_out):
    w = w_in[l]
    offs = np.cumsum((A_WIDTH, A_WIDTH, A_WIDTH, GLA_KEY, GLA_KEY, GLA_VAL, GLA_VAL, GATE_RANK, D_MODEL, D_MODEL))
    qa, ka, va, qb, kb, vb, rb, glr, ga, gb = jnp.split(w, [int(o) for o in offs[:-1]], axis=1)
    def attn_weights(split_half):
        relay = _to_split_half if split_half else (lambda a: a)
        return jnp.stack([jnp.stack([relay(qa[:, g * A_OUT:(g + 1) * A_OUT]), relay(ka[:, g * A_OUT:(g + 1) * A_OUT]),
                                     va[:, g * A_OUT:(g + 1) * A_OUT]]) for g in range(N_GROUPS)]).astype(BF16)

    w_attn = {False: attn_weights(False), True: attn_weights(True)}
    glr = jnp.pad(glr, ((0, 0), (0, COL_BLOCK - GATE_RANK)))
    rest = jnp.concatenate([qb, kb, vb, rb, ga, gb, glr], axis=1)
    w_rest = jnp.swapaxes(rest.reshape(D_MODEL, N_REST, COL_BLOCK), 0, 1).astype(BF16)
    wa = jnp.pad(w_gate_a2[l], ((0, LANE - GATE_RANK), (0, 0))).astype(BF16)
    return dict(w_attn=w_attn, w_rest=w_rest, wa=wa, ba=b_gate_a[l][None, :], gg=g_gla[l][None, :],
                wpa=w_branch_a[l].astype(BF16), wpb=w_branch_b[l].astype(BF16), wo=w_out[l].astype(BF16))


def _head_expand():
    e = np.zeros((LANE, A_OUT), np.float32)
    for h in range(HEADS_PER_GROUP):
        e[h, h * HEAD_DIM:(h + 1) * HEAD_DIM] = 1.0
    return jnp.asarray(e, BF16)


def _trunk(x, rope, caches, gla_state, lw, ffw, g_mix, g_ffn, g_final, batch, seq, tm, tm_proj, moe_tile, moe_sub,
           p_dtype):
    cos_t, sin_t = rope
    expand = _head_expand()
    prompt = caches is None
    cache_out = [None] * N_GROUPS
    states_out = []
    for l in range(DEPTH):
        w = lw[l]
        h = _norm(x, g_mix[l][None, :], tm, BF16)
        rest = _inproj_rest(h, w["w_rest"], tm_proj, p_dtype)
        outs, maxes, dens = [], [], []
        for g, (window, dil) in enumerate(DSWA_GROUPS):
            if prompt:
                qkv = _inproj_attn(h, w["w_attn"][True], cos_t, sin_t, g, batch, seq, dil, tm_proj, p_dtype, True)
                o, mx, den = _attn_prompt(qkv, g)
                cache_out[g] = _kv_tail(qkv, cache_out[g], l, min(window, seq))
            else:
                qkv = _inproj_attn(h, w["w_attn"][False], cos_t, sin_t, g, 1, batch * seq, 1, tm_proj, p_dtype, False)
                o, lse, cache_out[g] = _attn_sample(qkv.reshape(3, batch * seq, COL_BLOCK), caches[g], cache_out[g],
                                                    l, g, seq)
                o = o.reshape(1, 1, batch * seq, A_OUT)
                mx = lse.reshape(1, 1, batch * seq, LANE)
                den = jnp.ones_like(mx)
            outs.append(o)
            maxes.append(mx)
            dens.append(den)
        if prompt:
            ob, s_new = _gla(rest, w["wa"], w["ba"], w["gg"], batch, seq, GLA_CHUNK, 8, BF16)
            x = _merge(x, outs, maxes, dens, ob, rest, w["wpa"], w["wpb"], w["wo"], expand, seq, min(tm, 512))
        else:
            ob, s_new = _gla_decode(rest, w["wa"], w["ba"], w["gg"], gla_state, l, batch, seq, F32)
            x = _merge(x, outs, maxes, dens, ob, rest, w["wpa"], w["wpb"], w["wo"], expand, batch * seq, tm)
        states_out.append(s_new)
        i = l // 2
        if l % 2 == 0:
            x = _ffn(x, g_ffn[l][None, :], ffw["wg"][i], ffw["wu"][i], ffw["wd"][i], tm, D_FF // 2)
        else:
            routing = _router(x, g_ffn[l][None, :], ffw["wr"][i], moe_tile)
            x = _moe(x, g_ffn[l][None, :], *routing, ffw["eg"][i], ffw["eu"][i], ffw["ed"][i],
                     moe_sub, D_FF_EXPERT // 2)
    y = _norm(x, g_final[None, :], tm, F32)
    bufs = [jnp.transpose(c.reshape(DEPTH, batch, 2, HEADS_PER_GROUP, HEAD_DIM, c.shape[-1]), (0, 1, 5, 2, 3, 4))
            for c in cache_out]
    return (y, bufs[0], bufs[1], bufs[2], jnp.stack(states_out))


def kernel(x_prompt, x_sample, cache_kv_w128, cache_kv_w512, cache_kv_w2048, state_gla, g_mix, w_in, w_gate_a2, b_gate_a, g_gla, w_branch_a, w_branch_b, w_out, g_ffn, w_ff_gate, w_ff_up, w_ff_down, w_router, w_exp_gate, w_exp_up, w_exp_down, g_final):
    batch, seq, _ = x_prompt.shape
    dec_batch, dec_seq, _ = x_sample.shape
    assert seq % (DSWA_GROUPS[-1][1] * Q_BLOCK) == 0 and seq % (8 * GLA_CHUNK) == 0
    assert dec_seq % GLA_CHUNK != 0 and dec_seq % 8 == 0
    assert all(c.shape[2] == w for c, (w, _) in zip((cache_kv_w128, cache_kv_w512, cache_kv_w2048), DSWA_GROUPS))

    lw = [_layer_weights(l, w_in, w_gate_a2, b_gate_a, g_gla, w_branch_a, w_branch_b, w_out) for l in range(DEPTH)]
    ffw = dict(wg=w_ff_gate.astype(BF16), wu=w_ff_up.astype(BF16), wd=w_ff_down.astype(BF16),
               wr=jnp.pad(w_router, ((0, 0), (0, 0), (0, LANE - N_EXPERTS))),
               eg=w_exp_gate.astype(BF16), eu=w_exp_up.astype(BF16), ed=w_exp_down.astype(BF16))

    y_p, kv128_p, kv512_p, kv2048_p, gla_p = _trunk(
        x_prompt.reshape(batch * seq, D_MODEL), _rope_tables(jnp.arange(seq, dtype=jnp.int32), True), None, None,
        lw, ffw, g_mix, g_ffn, g_final, batch, seq, 1024, 2048, 1024, 288, BF16)

    n_s = dec_batch * dec_seq
    pos_s = jnp.tile(PAST_LEN + jnp.arange(dec_seq, dtype=jnp.int32), dec_batch)
    caches = [jnp.transpose(c, (0, 1, 3, 4, 5, 2)).reshape(DEPTH, dec_batch, 2, A_OUT, c.shape[2])
              for c in (cache_kv_w128, cache_kv_w512, cache_kv_w2048)]
    y_s, kv128_s, kv512_s, kv2048_s, gla_s = _trunk(
        x_sample.reshape(n_s, D_MODEL), _rope_tables(pos_s, False), caches, state_gla,
        lw, ffw, g_mix, g_ffn, g_final, dec_batch, dec_seq, n_s, n_s, n_s, 96, F32)

    return (y_p.reshape(batch, seq, D_MODEL), y_s.reshape(dec_batch, dec_seq, D_MODEL),
            kv128_p, kv512_p, kv2048_p, gla_p, kv128_s, kv512_s, kv2048_s, gla_s)
```

```python
import functools

import jax
import jax.numpy as jnp
import numpy as np
from jax import lax
from jax.experimental import pallas as pl
from jax.experimental.pallas import tpu as pltpu

F32 = jnp.float32
BF16 = jnp.bfloat16

D_MODEL = 1024
DEPTH = 2
PAST_LEN = 16384
HEAD_DIM = 64
DSWA_GROUPS = ((128, 1), (512, 4), (2048, 16))
N_GROUPS = 3
HEADS_PER_GROUP = 8
A_WIDTH = N_GROUPS * HEADS_PER_GROUP * HEAD_DIM
A_OUT = HEADS_PER_GROUP * HEAD_DIM
Q_BLOCK = 128
GLA_HEADS = 4
GLA_KEY = 512
GLA_VAL = 1024
GLA_DK = 128
GLA_DV = 256
GATE_RANK = 16
GATE_TAU = 16.0
GLA_CHUNK = 64
D_FF = 2816
N_EXPERTS = 8
D_FF_EXPERT = 3584
ROPE_THETA = 10000.0
EPS = 1e-6
LOG2_E = 1.4426950408889634

LANE = 128
COL_BLOCK = 512
R_QB, R_KB, R_VB, R_RB, R_GA, R_GB, R_GLR, N_REST = 0, 1, 2, 4, 6, 8, 10, 11
ROW_CHUNK = 256
GLA_CHUNKS_PER_STEP = 16
VMEM_LIMIT = 56 * 1024 * 1024


def _cparams(sem):
    return pltpu.CompilerParams(dimension_semantics=sem, vmem_limit_bytes=VMEM_LIMIT)


def _rms(xf, g):
    return xf * lax.rsqrt(jnp.mean(xf * xf, axis=-1, keepdims=True) + EPS) * g


def _dot(a, b):
    return jnp.dot(a, b, preferred_element_type=F32)


def _dot_nt(a, b):
    return lax.dot_general(a, b, (((1,), (1,)), ((), ())), preferred_element_type=F32)


def _dot_tn(a, b):
    return lax.dot_general(a, b, (((0,), (0,)), ((), ())), preferred_element_type=F32)


def _div_pow2(x, d):
    assert d & (d - 1) == 0
    return lax.shift_right_logical(x, int(d).bit_length() - 1)


def _mod_pow2(x, d):
    assert d & (d - 1) == 0
    return x & (d - 1)


def _split_dot(a_f32, b_bf16, dot=_dot):
    hi = a_f32.astype(BF16)
    lo = (a_f32 - hi.astype(F32)).astype(BF16)
    return dot(hi, b_bf16) + dot(lo, b_bf16)


def _inproj_attn_kernel(h_ref, w_ref, cos_ref, sin_ref, o_ref, *scratch, dil, split_half, q_scale):
    part = pl.program_id(1)
    scale = jnp.where(part == 0, q_scale, 1.0).astype(F32)
    tm = h_ref.shape[0]
    chunk = min(tm, ROW_CHUNK)
    n_lane_groups = COL_BLOCK // LANE
    for rc in range(tm // chunk):
        rows = slice(rc * chunk, (rc + 1) * chunk)
        acc = _dot(h_ref[rows, :], w_ref[...])
        cos = jnp.where(part == 2, 1.0, cos_ref[rows, :] * scale)
        sin = jnp.where(part == 2, 0.0, sin_ref[rows, :] * scale)
        ys = []
        if split_half:
            for c in range(0, n_lane_groups, 2):
                a = acc[:, c * LANE:(c + 1) * LANE]
                b = acc[:, (c + 1) * LANE:(c + 2) * LANE]
                ys += [a * cos - b * sin, a * sin + b * cos]
        else:
            first_half = _mod_pow2(lax.broadcasted_iota(jnp.int32, cos.shape, 1), HEAD_DIM) < HEAD_DIM // 2
            for c in range(n_lane_groups):
                xs = acc[:, c * LANE:(c + 1) * LANE]
                swapped = jnp.where(first_half, pltpu.roll(xs, LANE - HEAD_DIM // 2, 1), pltpu.roll(xs, HEAD_DIM // 2, 1))
                ys.append(xs * cos + swapped * sin)
        if dil == 1:
            for c, y in enumerate(ys):
                o_ref[0, rows, c * LANE:(c + 1) * LANE] = y.astype(o_ref.dtype)
        else:
            sub = chunk // dil
            pitch = _residue_pitch(dil)
            base = rc * sub * pitch
            for c, y in enumerate(ys):
                if pitch == dil:
                    scratch[0][c, rows, :] = y
                else:
                    for u in range(sub):
                        scratch[0][c, base + u * pitch:base + u * pitch + dil, :] = y[u * dil:(u + 1) * dil, :]
            for r in range(dil):
                for c in range(n_lane_groups):
                    o_ref[r, rc * sub:(rc + 1) * sub, c * LANE:(c + 1) * LANE] = (
                        scratch[0][c, pl.ds(base + r, sub, stride=pitch), :].astype(o_ref.dtype))


def _residue_pitch(dil):
    return dil + 8 if dil % 16 == 0 else dil


def _inproj_attn(h, w, cos_t, sin_t, group, batch, seq, dil, tm, out_dtype, split_half):
    tiles = seq // tm
    scratch = [pltpu.VMEM((COL_BLOCK // LANE, tm // dil * _residue_pitch(dil), LANE), F32)] if dil > 1 else []
    q_scale = HEAD_DIM ** -0.5 * (LOG2_E if split_half else 1.0)
    return pl.pallas_call(
        functools.partial(_inproj_attn_kernel, dil=dil, split_half=split_half, q_scale=q_scale),
        grid=(batch * tiles, 3),
        in_specs=[
            pl.BlockSpec((tm, D_MODEL), lambda i, j: (i, 0)),
            pl.BlockSpec((None, None, D_MODEL, COL_BLOCK), lambda i, j: (group, j, 0, 0)),
            pl.BlockSpec((tm, LANE), lambda i, j: (i % tiles, 0)),
            pl.BlockSpec((tm, LANE), lambda i, j: (i % tiles, 0)),
        ],
        out_specs=pl.BlockSpec((None, None, dil, tm // dil, COL_BLOCK), lambda i, j: (j, i // tiles, 0, i % tiles, 0)),
        out_shape=jax.ShapeDtypeStruct((3, batch, dil, seq // dil, COL_BLOCK), out_dtype),
        scratch_shapes=scratch,
        compiler_params=_cparams(("parallel", "arbitrary")),
        name=f"inproj_attn_g{group}",
    )(h, w, cos_t, sin_t)


def _inproj_rest_kernel(h_ref, w_ref, o_ref):
    tm = h_ref.shape[0]
    chunk = min(tm, ROW_CHUNK)
    for rc in range(tm // chunk):
        rows = slice(rc * chunk, (rc + 1) * chunk)
        o_ref[rows, :] = _dot(h_ref[rows, :], w_ref[...]).astype(o_ref.dtype)


def _inproj_rest(h, w, tm, out_dtype):
    n = h.shape[0]
    return pl.pallas_call(
        _inproj_rest_kernel,
        grid=(n // tm, N_REST),
        in_specs=[
            pl.BlockSpec((tm, D_MODEL), lambda i, j: (i, 0)),
            pl.BlockSpec((None, D_MODEL, COL_BLOCK), lambda i, j: (j, 0, 0)),
        ],
        out_specs=pl.BlockSpec((None, tm, COL_BLOCK), lambda i, j: (j, i, 0)),
        out_shape=jax.ShapeDtypeStruct((N_REST, n, COL_BLOCK), out_dtype),
        compiler_params=_cparams(("parallel", "arbitrary")),
        name="inproj_rest",
    )(h, w)


QUAD = 4
QUAD_WIDTH = QUAD * HEAD_DIM
MAX_Q_BLOCKS_PER_STEP = 4


def _attn_prompt_kernel(bias_ref, q_ref, kp_ref, kc_ref, vp_ref, vc_ref, o_ref, m_ref, l_ref):
    n = pl.program_id(2)
    rows = QUAD * Q_BLOCK
    row_head = _div_pow2(lax.broadcasted_iota(jnp.int32, (rows, QUAD_WIDTH), 0), Q_BLOCK)
    col = lax.broadcasted_iota(jnp.int32, (rows, QUAD_WIDTH), 1)
    q_lanes = _div_pow2(_mod_pow2(col, LANE), HEAD_DIM // 2) == row_head
    v_lanes = _div_pow2(col, HEAD_DIM) == row_head
    lane = lax.broadcasted_iota(jnp.int32, (Q_BLOCK, LANE), 1)
    for qb in range(q_ref.shape[0] // Q_BLOCK):
        cur = slice(qb * Q_BLOCK, (qb + 1) * Q_BLOCK)
        if qb == 0:
            bias = bias_ref[jnp.where(n == 0, 1, 0)]
        else:
            bias = bias_ref[0]
            prev = slice((qb - 1) * Q_BLOCK, qb * Q_BLOCK)
        m_all = jnp.zeros((Q_BLOCK, LANE), F32)
        l_all = jnp.ones((Q_BLOCK, LANE), F32)
        for c in range(A_OUT // QUAD_WIDTH):
            cs = slice(c * QUAD_WIDTH, (c + 1) * QUAD_WIDTH)
            q = q_ref[cur, cs].astype(F32)
            q4 = jnp.where(q_lanes, jnp.concatenate([q] * QUAD, axis=0), 0.0).astype(BF16)
            k_prev = kp_ref[:, cs] if qb == 0 else kc_ref[prev, cs]
            v_prev = vp_ref[:, cs] if qb == 0 else vc_ref[prev, cs]
            k = jnp.concatenate([k_prev, kc_ref[cur, cs]], axis=0)
            v = jnp.concatenate([v_prev, vc_ref[cur, cs]], axis=0)
            s = _dot_nt(q4, k) + bias
            m = jnp.max(s, axis=-1, keepdims=True)
            p = jnp.exp2(s - m)
            l = jnp.sum(p, axis=-1, keepdims=True)
            o4 = jnp.where(v_lanes, _dot(p.astype(BF16), v), 0.0)
            o = o4[0:Q_BLOCK]
            for j in range(QUAD):
                hs = slice(j * Q_BLOCK, (j + 1) * Q_BLOCK)
                if j > 0:
                    o = o + o4[hs]
                m_all = jnp.where(lane == c * QUAD + j, m[hs], m_all)
                l_all = jnp.where(lane == c * QUAD + j, l[hs], l_all)
            o_ref[cur, cs] = o.astype(o_ref.dtype)
        m_ref[cur, :] = m_all
        l_ref[cur, :] = l_all


def _band_bias(span):
    qi = np.arange(QUAD * Q_BLOCK)[:, None] % Q_BLOCK
    kj = np.arange(2 * Q_BLOCK)[None, :]
    dist = Q_BLOCK + qi - kj
    band = (dist >= 0) & (dist <= span)
    first = band & (kj >= Q_BLOCK)
    return jnp.asarray(np.where(np.stack([band, first]), 0.0, -np.inf), F32)


def _attn_prompt(qkv, group):
    window, dil = DSWA_GROUPS[group]
    _, batch, _, sub_len, _ = qkv.shape
    q_blocks = min(MAX_Q_BLOCKS_PER_STEP, sub_len // Q_BLOCK)
    step_rows = q_blocks * Q_BLOCK
    nb = sub_len // step_rows

    def spec(part, prev):
        if prev:
            return pl.BlockSpec((None, None, None, Q_BLOCK, COL_BLOCK),
                                lambda b, r, n: (part, b, r, jnp.maximum(n * q_blocks - 1, 0), 0))
        return pl.BlockSpec((None, None, None, step_rows, COL_BLOCK), lambda b, r, n: (part, b, r, n, 0))

    bias = _band_bias(window // dil)
    stat_spec = pl.BlockSpec((None, None, step_rows, LANE), lambda b, r, n: (b, r, n, 0))
    stat_shape = jax.ShapeDtypeStruct((batch, dil, sub_len, LANE), F32)
    return pl.pallas_call(
        _attn_prompt_kernel,
        grid=(batch, dil, nb),
        in_specs=[pl.BlockSpec(bias.shape, lambda b, r, n: (0, 0, 0)),
                  spec(0, False), spec(1, True), spec(1, False), spec(2, True), spec(2, False)],
        out_specs=[pl.BlockSpec((None, None, step_rows, A_OUT), lambda b, r, n: (b, r, n, 0)), stat_spec, stat_spec],
        out_shape=[jax.ShapeDtypeStruct((batch, dil, sub_len, A_OUT), BF16), stat_shape, stat_shape],
        compiler_params=_cparams(("parallel", "parallel", "arbitrary")),
        name=f"attn_prompt_g{group}",
    )(bias, qkv, qkv, qkv, qkv, qkv)


def _kv_tail_kernel(*refs, dil, aliased):
    if aliased:
        k_ref, v_ref, _, out_ref, scr = refs
        _kv_tail_body(k_ref, v_ref, out_ref, scr, dil)
        return
    k_ref, v_ref, out_ref, scr = refs

    @pl.when(pl.program_id(0) == 0)
    def _():
        _kv_tail_body(k_ref, v_ref, out_ref, scr, dil)

    @pl.when(pl.program_id(0) > 0)
    def _():
        out_ref[...] = jnp.zeros_like(out_ref)


def _kv_tail_body(k_ref, v_ref, out_ref, scr, dil):
    keep = out_ref.shape[2]
    sub = keep // dil
    half_w = HEAD_DIM // 2
    for kv, ref in enumerate((k_ref, v_ref)):
        for c in range(COL_BLOCK // LANE):
            cs = slice(c * LANE, (c + 1) * LANE)
            if dil == 1:
                tok = ref[0, :, cs].astype(F32)
            else:
                for r in range(dil):
                    scr[pl.ds(r, sub, stride=dil), :] = ref[r, :, cs].astype(F32)
                tok = scr[...]
            chan = tok.T
            if kv == 1:
                out_ref[kv, cs, :] = chan
            else:
                for j in range(QUAD):
                    dst = (QUAD * (c // 2) + j) * HEAD_DIM + (c % 2) * half_w
                    out_ref[kv, dst:dst + half_w, :] = chan[j * half_w:(j + 1) * half_w, :]


def _kv_tail(qkv, prev_out, layer, keep):
    _, batch, dil, sub_len, _ = qkv.shape
    last = sub_len // (keep // dil) - 1
    first = prev_out is None
    assert first == (layer == 0)
    phases = DEPTH if first else 1

    def spec(part):
        return pl.BlockSpec((None, None, dil, keep // dil, COL_BLOCK),
                            lambda p, b: (part, jnp.where(p == 0, b, batch - 1), 0, last, 0))

    in_specs = [spec(1), spec(2)]
    args = [qkv, qkv]
    aliases = {}
    if not first:
        in_specs.append(pl.BlockSpec(memory_space=pl.ANY))
        args.append(prev_out)
        aliases = {2: 0}
    return pl.pallas_call(
        functools.partial(_kv_tail_kernel, dil=dil, aliased=not first),
        grid=(phases, batch),
        in_specs=in_specs,
        out_specs=pl.BlockSpec((None, None, 2, A_OUT, keep), lambda p, b: (layer + p, b, 0, 0, 0)),
        out_shape=jax.ShapeDtypeStruct((DEPTH, batch, 2, A_OUT, keep), F32),
        scratch_shapes=[pltpu.VMEM((keep, LANE), F32)],
        input_output_aliases=aliases,
        compiler_params=_cparams(("arbitrary", "arbitrary")),
        name=f"kv_tail_{keep}",
    )(*args)


SHIFT_ROWS = 128


def _attn_sample_kernel(*refs, dil, span, first):
    if first:
        q_ref, k_ref, v_ref, c_ref, o_ref, lse_ref, cout_ref = refs

        @pl.when(pl.program_id(0) == 0)
        def _():
            _sample_attention(q_ref, k_ref, v_ref, c_ref, o_ref, lse_ref, dil, span)
            _append_new(c_ref, k_ref, v_ref, cout_ref)

        @pl.when(pl.program_id(0) > 0)
        def _():
            shift = q_ref.shape[0]
            for kv in range(2):
                for rb in range(A_OUT // SHIFT_ROWS):
                    rs = slice(rb * SHIFT_ROWS, (rb + 1) * SHIFT_ROWS)
                    cout_ref[kv, rs, :] = pltpu.roll(c_ref[kv, rs, :], c_ref.shape[2] - shift, 1)
    else:
        q_ref, k_ref, v_ref, c_ref, _, o_ref, lse_ref, cout_ref = refs
        _sample_attention(q_ref, k_ref, v_ref, c_ref, o_ref, lse_ref, dil, span)
        _append_new(c_ref, k_ref, v_ref, cout_ref)


def _append_new(c_ref, k_ref, v_ref, cout_ref):
    t_new = k_ref.shape[0]
    lb = c_ref.shape[2]
    width = cout_ref.shape[2]
    lane = lax.broadcasted_iota(jnp.int32, (SHIFT_ROWS, LANE), 1)
    zeros = jnp.zeros((LANE - t_new, A_OUT), F32)
    for kv, ref in enumerate((k_ref, v_ref)):
        new = ref[...].astype(F32)
        new_t = pltpu.roll(jnp.concatenate([new, zeros], axis=0).T, LANE - t_new, 1)
        for rb in range(A_OUT // SHIFT_ROWS):
            rs = slice(rb * SHIFT_ROWS, (rb + 1) * SHIFT_ROWS)
            shifted = pltpu.roll(c_ref[kv, rs, lb - width:lb], width - t_new, 1)
            if width > LANE:
                cout_ref[kv, rs, 0:width - LANE] = shifted[:, 0:width - LANE]
            cout_ref[kv, rs, width - LANE:width] = jnp.where(lane >= LANE - t_new, new_t[rs, :], shifted[:, width - LANE:width])


def _sample_attention(q_ref, k_ref, v_ref, c_ref, o_ref, lse_ref, dil, span):
    t_new = q_ref.shape[0]
    lb = c_ref.shape[2]
    rows = HEADS_PER_GROUP * t_new
    q = q_ref[...].astype(F32)
    k_new = k_ref[...].astype(F32)
    v_new = v_ref[...].astype(F32)

    q_rep = jnp.concatenate([q] * HEADS_PER_GROUP, axis=0)
    row_head = _div_pow2(lax.broadcasted_iota(jnp.int32, (rows, A_OUT), 0), t_new)
    col_head = _div_pow2(lax.broadcasted_iota(jnp.int32, (rows, A_OUT), 1), HEAD_DIM)
    diag = row_head == col_head
    q_bd = jnp.where(diag, q_rep, 0.0).astype(BF16)

    s_buf = _dot(q_bd, c_ref[0].astype(BF16))
    s_new = _dot_nt(q_bd, k_new.astype(BF16))

    tok_b = _mod_pow2(lax.broadcasted_iota(jnp.int32, (rows, lb), 0), t_new)
    pos_b = lax.broadcasted_iota(jnp.int32, (rows, lb), 1)
    dist_b = lb + tok_b - pos_b
    ok_b = (_mod_pow2(dist_b, dil) == 0) & (dist_b <= span * dil)
    tok_n = _mod_pow2(lax.broadcasted_iota(jnp.int32, (rows, t_new), 0), t_new)
    pos_n = lax.broadcasted_iota(jnp.int32, (rows, t_new), 1)
    dist_n = tok_n - pos_n
    ok_n = (dist_n >= 0) & (_mod_pow2(dist_n, dil) == 0) & (dist_n <= span * dil)
    s_buf = jnp.where(ok_b, s_buf, -jnp.inf)
    s_new = jnp.where(ok_n, s_new, -jnp.inf)

    m = jnp.maximum(jnp.max(s_buf, axis=-1, keepdims=True), jnp.max(s_new, axis=-1, keepdims=True))
    p_buf = jnp.exp(s_buf - m)
    p_new = jnp.exp(s_new - m)
    l = jnp.sum(p_buf, axis=-1, keepdims=True) + jnp.sum(p_new, axis=-1, keepdims=True)
    o_full = (_dot_nt(p_buf.astype(BF16), c_ref[1].astype(BF16)) + _dot(p_new.astype(BF16), v_new.astype(BF16))) / l
    o_full = jnp.where(diag, o_full, 0.0)
    lse = (m + jnp.log(l)) * LOG2_E

    o = o_full[0:t_new, :]
    lane_t = lax.broadcasted_iota(jnp.int32, (t_new, LANE), 1)
    lse_all = jnp.zeros((t_new, LANE), F32)
    for h in range(HEADS_PER_GROUP):
        if h > 0:
            o = o + o_full[h * t_new:(h + 1) * t_new, :]
        lse_all = jnp.where(lane_t == h, lse[h * t_new:(h + 1) * t_new, :], lse_all)
    o_ref[...] = o.astype(o_ref.dtype)
    lse_ref[...] = lse_all


def _attn_sample(qkv, cache, prev_out, layer, group, t_new):
    window, dil = DSWA_GROUPS[group]
    n_rows = qkv.shape[1]
    batch = n_rows // t_new
    lb = cache.shape[-1]

    first = prev_out is None
    assert first == (layer == 0) and t_new <= LANE
    phases = DEPTH if first else 1

    def row(p, b):
        return jnp.where(p == 0, b, batch - 1)

    def pspec(part):
        return pl.BlockSpec((None, t_new, COL_BLOCK), lambda p, b: (part, row(p, b), 0))

    in_specs = [pspec(0), pspec(1), pspec(2),
                pl.BlockSpec((None, None, 2, A_OUT, lb), lambda p, b: (layer + p, b, 0, 0, 0))]
    args = [qkv, qkv, qkv, cache]
    aliases = {}
    if first:
        cache_out = pl.BlockSpec((None, None, 2, A_OUT, lb), lambda p, b: (layer + p, b, 0, 0, 0))
    else:
        in_specs.append(pl.BlockSpec(memory_space=pl.ANY))
        args.append(prev_out)
        aliases = {4: 2}
        cache_out = pl.BlockSpec((None, None, 2, A_OUT, LANE), lambda p, b: (layer, b, 0, 0, lb // LANE - 1))
    return pl.pallas_call(
        functools.partial(_attn_sample_kernel, dil=dil, span=window // dil, first=first),
        grid=(phases, batch),
        in_specs=in_specs,
        out_specs=[
            pl.BlockSpec((t_new, A_OUT), lambda p, b: (row(p, b), 0)),
            pl.BlockSpec((t_new, LANE), lambda p, b: (row(p, b), 0)),
            cache_out,
        ],
        out_shape=[
            jax.ShapeDtypeStruct((n_rows, A_OUT), F32),
            jax.ShapeDtypeStruct((n_rows, LANE), F32),
            jax.ShapeDtypeStruct(cache.shape, F32),
        ],
        input_output_aliases=aliases,
        compiler_params=_cparams(("arbitrary", "arbitrary")),
        name=f"attn_sample_g{group}",
    )(*args)


def _gla_kernel(q_ref, k_ref, v_ref, r_ref, glr_ref, wa_ref, ba_ref, gg_ref, o_ref, sout_ref, s_ref, *, chunk, n_chunks):
    step = pl.program_id(2)

    @pl.when(step == 0)
    def _():
        s_ref[...] = jnp.zeros_like(s_ref)

    ri = lax.broadcasted_iota(jnp.int32, (chunk, chunk), 0)
    ci = lax.broadcasted_iota(jnp.int32, (chunk, chunk), 1)
    causal = ri >= ci
    tri = causal.astype(BF16)
    gg = gg_ref[...]

    gate = _dot(glr_ref[...].astype(BF16), wa_ref[...]) + ba_ref[...]
    log_a = jax.nn.log_sigmoid(gate) * (1.0 / GATE_TAU)
    log_hi = log_a.astype(BF16)
    log_lo = (log_a - log_hi.astype(F32)).astype(BF16)
    chunks = [slice(c * chunk, (c + 1) * chunk) for c in range(n_chunks)]
    cums = [_dot(tri, log_hi[rs, :]) + _dot(tri, log_lo[rs, :]) for rs in chunks]
    lasts = [cum[chunk - 1:chunk, :] for cum in cums]
    cum = jnp.concatenate(cums, axis=0)
    last = jnp.concatenate([jnp.broadcast_to(l, (chunk, GLA_DK)) for l in lasts], axis=0)
    grow = jnp.exp(cum)
    q_in = (q_ref[...].astype(F32) * (GLA_DK ** -0.5) * grow).astype(BF16)
    k = k_ref[...].astype(F32)
    k_in = (k * jnp.exp(-cum)).astype(BF16)
    k_out = (k * jnp.exp(last - cum)).astype(BF16)
    v = v_ref[...].astype(BF16)
    o_intra, kv, decay = [], [], []
    for rs, l in zip(chunks, lasts):
        att = jnp.where(causal, _dot_nt(q_in[rs, :], k_in[rs, :]), 0.0).astype(BF16)
        o_intra.append(_dot(att, v[rs, :]))
        kv.append(_dot_tn(k_out[rs, :], v[rs, :]))
        d = jnp.broadcast_to(jnp.exp(l), (GLA_DK, GLA_DK)).T
        decay.append(jnp.concatenate([d] * (GLA_DV // GLA_DK), axis=1))

    state = s_ref[...]
    for c, rs in enumerate(chunks):
        o = o_intra[c] + _dot(q_in[rs, :], state.astype(BF16))
        state = decay[c] * state + kv[c]
        y = _rms(o, gg) * jax.nn.silu(r_ref[rs, :].astype(F32))
        o_ref[rs, :] = y.astype(o_ref.dtype)
    s_ref[...] = state

    @pl.when(step == pl.num_programs(2) - 1)
    def _():
        sout_ref[...] = state


def _gla_decode_kernel(q_ref, k_ref, v_ref, r_ref, glr_ref, wa_ref, ba_ref, gg_ref, s0_ref, o_ref, sout_ref):
    t_new = q_ref.shape[0]
    ri = lax.broadcasted_iota(jnp.int32, (t_new, t_new), 0)
    ci = lax.broadcasted_iota(jnp.int32, (t_new, t_new), 1)
    causal = ri >= ci
    tri = causal.astype(BF16)
    glr = glr_ref[...].astype(BF16)
    heads_per_block = COL_BLOCK // GLA_DV
    for h in range(GLA_HEADS):
        ks = slice(h * GLA_DK, (h + 1) * GLA_DK)
        vb, vs = h // heads_per_block, slice((h % heads_per_block) * GLA_DV, (h % heads_per_block + 1) * GLA_DV)
        gate = _dot(glr, wa_ref[:, ks]) + ba_ref[:, ks]
        log_a = jax.nn.log_sigmoid(gate) * (1.0 / GATE_TAU)
        log_hi = log_a.astype(BF16)
        log_lo = (log_a - log_hi.astype(F32)).astype(BF16)
        cum = _dot(tri, log_hi) + _dot(tri, log_lo)
        last = cum[t_new - 1:t_new, :]
        q_in = (q_ref[:, ks].astype(F32) * (GLA_DK ** -0.5) * jnp.exp(cum)).astype(BF16)
        k = k_ref[:, ks].astype(F32)
        k_in = (k * jnp.exp(-cum)).astype(BF16)
        k_out = (k * jnp.exp(last - cum)).astype(BF16)
        v = v_ref[vb, :, vs].astype(BF16)
        att = jnp.where(causal, _dot_nt(q_in, k_in), 0.0).astype(BF16)
        state = s0_ref[h].astype(F32)
        o = _dot(att, v) + _dot(q_in, state.astype(BF16))
        d = jnp.broadcast_to(jnp.exp(last), (GLA_DK, GLA_DK)).T
        sout_ref[h] = jnp.concatenate([d] * (GLA_DV // GLA_DK), axis=1) * state + _dot_tn(k_out, v)
        os_ = slice(h * GLA_DV, (h + 1) * GLA_DV)
        y = _rms(o, gg_ref[:, os_]) * jax.nn.silu(r_ref[vb, :, vs].astype(F32))
        o_ref[:, os_] = y.astype(o_ref.dtype)


def _gla_decode(rest, wa, ba, gg, s0, layer, batch, t_new, out_dtype):
    def rows(block, n_blocks=None):
        if n_blocks is None:
            return pl.BlockSpec((None, t_new, COL_BLOCK), lambda b: (block, b, 0))
        return pl.BlockSpec((n_blocks, t_new, COL_BLOCK), lambda b: (block // n_blocks, b, 0))

    def full(a):
        return pl.BlockSpec(a.shape, lambda b: (0, 0))

    v_blocks = GLA_VAL // COL_BLOCK
    return pl.pallas_call(
        _gla_decode_kernel,
        grid=(batch,),
        in_specs=[rows(R_QB), rows(R_KB), rows(R_VB, v_blocks), rows(R_RB, v_blocks),
                  pl.BlockSpec((None, t_new, LANE), lambda b: (R_GLR, b, 0)),
                  full(wa), full(ba), full(gg),
                  pl.BlockSpec((None, None, GLA_HEADS, GLA_DK, GLA_DV), lambda b: (layer, b, 0, 0, 0))],
        out_specs=[pl.BlockSpec((t_new, GLA_VAL), lambda b: (b, 0)),
                   pl.BlockSpec((None, GLA_HEADS, GLA_DK, GLA_DV), lambda b: (b, 0, 0, 0))],
        out_shape=[jax.ShapeDtypeStruct((batch * t_new, GLA_VAL), out_dtype),
                   jax.ShapeDtypeStruct((batch, GLA_HEADS, GLA_DK, GLA_DV), F32)],
        compiler_params=_cparams(("parallel",)),
        name="gla_decode",
    )(rest, rest, rest, rest, rest, wa, ba, gg, s0)


def _gla(rest, wa, ba, gg, batch, seq, chunk, n_chunks, out_dtype):
    n_rows = rest.shape[1]
    lblk = chunk * n_chunks
    steps = seq // lblk

    def pspec(width, block):
        per = COL_BLOCK // width
        return pl.BlockSpec((None, lblk, width), lambda b, h, n: (block + h // per, b * steps + n, h % per))

    in_specs = [
        pspec(GLA_DK, R_QB), pspec(GLA_DK, R_KB), pspec(GLA_DV, R_VB), pspec(GLA_DV, R_RB),
        pl.BlockSpec((None, lblk, LANE), lambda b, h, n: (R_GLR, b * steps + n, 0)),
        pl.BlockSpec((LANE, GLA_DK), lambda b, h, n: (0, h)),
        pl.BlockSpec((1, GLA_DK), lambda b, h, n: (0, h)),
        pl.BlockSpec((1, GLA_DV), lambda b, h, n: (0, h)),
    ]
    args = [rest, rest, rest, rest, rest, wa, ba, gg]
    return pl.pallas_call(
        functools.partial(_gla_kernel, chunk=chunk, n_chunks=n_chunks),
        grid=(batch, GLA_HEADS, steps),
        in_specs=in_specs,
        out_specs=[
            pl.BlockSpec((lblk, GLA_DV), lambda b, h, n: (b * steps + n, h)),
            pl.BlockSpec((None, None, GLA_DK, GLA_DV), lambda b, h, n: (b, h, 0, 0)),
        ],
        out_shape=[
            jax.ShapeDtypeStruct((n_rows, GLA_VAL), out_dtype),
            jax.ShapeDtypeStruct((batch, GLA_HEADS, GLA_DK, GLA_DV), F32),
        ],
        scratch_shapes=[pltpu.VMEM((GLA_DK, GLA_DV), F32)],
        compiler_params=_cparams(("parallel", "parallel", "arbitrary")),
        name="gla",
    )(*args)


def _merge_kernel(x_ref, o1_ref, o2_ref, o3_ref, m1_ref, m2_ref, m3_ref, l1_ref, l2_ref, l3_ref, ob_ref, ga_ref, gb_ref,
                  wpa_ref, wpb_ref, wo_ref, e_ref, out_ref, o_scr, l_scr, *, dils):
    tm = x_ref.shape[0]

    def token_order(ref, scr, dil):
        if dil == 1:
            return ref[0].astype(F32)
        chunks = ref.shape[2] // LANE
        for r in range(dil):
            for c in range(chunks):
                scr[c, pl.ds(r, tm // dil, stride=dil), :] = ref[r, :, c * LANE:(c + 1) * LANE].astype(F32)
        return jnp.concatenate([scr[c] for c in range(chunks)], axis=1)

    ms = [token_order(m_ref, l_scr, d) for m_ref, d in zip((m1_ref, m2_ref, m3_ref), dils)]
    ls = [token_order(l_ref, l_scr, d) for l_ref, d in zip((l1_ref, l2_ref, l3_ref), dils)]
    m = jnp.maximum(jnp.maximum(ms[0], ms[1]), ms[2])
    ws = [jnp.exp2(mg - m) for mg in ms]
    inv = 1.0 / (ws[0] * ls[0] + ws[1] * ls[1] + ws[2] * ls[2])
    expand = e_ref[...]
    o_a = jnp.zeros((tm, A_OUT), F32)
    for w, o_ref, d in zip(ws, (o1_ref, o2_ref, o3_ref), dils):
        o_a = o_a + _split_dot(w * inv, expand) * token_order(o_ref, o_scr, d)
    ya = _dot(o_a.astype(BF16), wpa_ref[...])
    yb = _dot(ob_ref[...].astype(BF16), wpb_ref[...])
    gate_a = jnp.concatenate([ga_ref[0], ga_ref[1]], axis=1).astype(F32)
    gate_b = jnp.concatenate([gb_ref[0], gb_ref[1]], axis=1).astype(F32)
    merged = jax.nn.sigmoid(gate_a) * ya + jax.nn.sigmoid(gate_b) * yb
    out_ref[...] = x_ref[...] + _dot(merged.astype(BF16), wo_ref[...])


def _merge(x, outs, maxes, dens, ob, rest, wpa, wpb, wo, expand, seq, tm):
    n = x.shape[0]
    tiles = seq // tm
    dils = tuple(o.shape[1] for o in outs)

    def rows(width):
        return pl.BlockSpec((tm, width), lambda i: (i, 0))

    def dilated(a):
        dil, width = a.shape[1], a.shape[3]
        return pl.BlockSpec((None, dil, tm // dil, width), lambda i: (i // tiles, 0, i % tiles, 0))

    def gate(block):
        return pl.BlockSpec((2, tm, COL_BLOCK), lambda i: (block // 2, i, 0))

    def full(a):
        return pl.BlockSpec(a.shape, lambda i: (0, 0))

    return pl.pallas_call(
        functools.partial(_merge_kernel, dils=dils),
        grid=(n // tm,),
        in_specs=[rows(D_MODEL)] + [dilated(a) for a in (*outs, *maxes, *dens)] + [rows(GLA_VAL)]
                 + [gate(R_GA), gate(R_GB)] + [full(wpa), full(wpb), full(wo), full(expand)],
        out_specs=rows(D_MODEL),
        out_shape=jax.ShapeDtypeStruct((n, D_MODEL), F32),
        scratch_shapes=[pltpu.VMEM((A_OUT // LANE, tm, LANE), F32), pltpu.VMEM((1, tm, LANE), F32)],
        compiler_params=_cparams(("parallel",)),
        name="merge",
    )(x, *outs, *maxes, *dens, ob, rest, rest, wpa, wpb, wo, expand)


def _ffn_kernel(x_ref, g_ref, gn_ref, wg_ref, wu_ref, wd_ref, o_ref, hn_ref):
    tm = x_ref.shape[0]
    chunk = min(tm, ROW_CHUNK)
    for rc in range(tm // chunk):
        rows = slice(rc * chunk, (rc + 1) * chunk)
        x = x_ref[rows, :]
        h = _rms(x, g_ref[...]).astype(BF16)
        mid = jax.nn.silu(_dot(h, wg_ref[...])) * _dot(h, wu_ref[...])
        y = x + _dot(mid.astype(BF16), wd_ref[...])
        o_ref[rows, :] = y
        hn_ref[rows, :] = _rms(y, gn_ref[...]).astype(hn_ref.dtype)


def _ffn(x, g, g_next, wg, wu, wd, tm):
    n = x.shape[0]

    def resident(a):
        return pl.BlockSpec(a.shape, lambda i: (0, 0), pipeline_mode=pl.Buffered(1))

    def rows():
        return pl.BlockSpec((tm, D_MODEL), lambda i: (i, 0))

    return pl.pallas_call(
        _ffn_kernel,
        grid=(n // tm,),
        in_specs=[rows(), resident(g), resident(g_next), resident(wg), resident(wu), resident(wd)],
        out_specs=[rows(), rows()],
        out_shape=[jax.ShapeDtypeStruct((n, D_MODEL), F32), jax.ShapeDtypeStruct((n, D_MODEL), BF16)],
        compiler_params=_cparams(("parallel",)),
        name="ffn",
    )(x, g, g_next, wg, wu, wd)


def _router_kernel(x_ref, g_ref, wr_ref, comb_ref, scol_ref, srow_ref, cnt_ref):
    h = _rms(x_ref[...], g_ref[...])
    w = wr_ref[...]
    h_hi, w_hi = h.astype(BF16), w.astype(BF16)
    h_lo, w_lo = (h - h_hi.astype(F32)).astype(BF16), (w - w_hi.astype(F32)).astype(BF16)
    logits = _dot(h_hi, w_hi) + (_dot(h_hi, w_lo) + _dot(h_lo, w_hi))
    lane = lax.broadcasted_iota(jnp.int32, logits.shape, 1)
    logits = jnp.where(lane < N_EXPERTS, logits, -jnp.inf)
    v1 = jnp.max(logits, axis=-1, keepdims=True)
    i1 = jnp.min(jnp.where(logits == v1, lane, LANE), axis=-1, keepdims=True)
    rest = jnp.where(lane == i1, -jnp.inf, logits)
    v2 = jnp.max(rest, axis=-1, keepdims=True)
    i2 = jnp.min(jnp.where(rest == v2, lane, LANE), axis=-1, keepdims=True)
    e2 = jnp.exp(v2 - v1)
    g1 = 1.0 / (1.0 + e2)
    g2 = e2 / (1.0 + e2)
    comb = jnp.where(lane == i1, g1, 0.0) + jnp.where(lane == i2, g2, 0.0)
    comb_ref[...] = comb

    tile = comb.shape[0]
    routed = comb > 0.0
    earlier = lax.broadcasted_iota(jnp.int32, (tile, tile), 1) < lax.broadcasted_iota(jnp.int32, (tile, tile), 0)
    rank = _dot(earlier.astype(BF16), routed.astype(BF16))
    slot = jnp.where(routed, rank, -1.0)
    scol_ref[...] = slot
    srow_ref[...] = slot.T[0:N_EXPERTS, :]
    count = jnp.sum(routed.astype(F32), axis=0, keepdims=True)
    cnt_ref[...] = jnp.broadcast_to(count, cnt_ref.shape).astype(jnp.int32)


def _router(x, g, wr, tile):
    n = x.shape[0]
    tiles = n // tile
    comb, slot_col, slot_row, counts = pl.pallas_call(
        _router_kernel,
        grid=(tiles,),
        in_specs=[
            pl.BlockSpec((tile, D_MODEL), lambda i: (i, 0)),
            pl.BlockSpec((1, D_MODEL), lambda i: (0, 0)),
            pl.BlockSpec((D_MODEL, LANE), lambda i: (0, 0)),
        ],
        out_specs=[
            pl.BlockSpec((tile, LANE), lambda i: (i, 0)),
            pl.BlockSpec((tile, LANE), lambda i: (i, 0)),
            pl.BlockSpec((None, N_EXPERTS, tile), lambda i: (i, 0, 0)),
            pl.BlockSpec((None, 8, LANE), lambda i: (i, 0, 0)),
        ],
        out_shape=[
            jax.ShapeDtypeStruct((n, LANE), F32),
            jax.ShapeDtypeStruct((n, LANE), F32),
            jax.ShapeDtypeStruct((tiles, N_EXPERTS, tile), F32),
            jax.ShapeDtypeStruct((tiles, 8, LANE), jnp.int32),
        ],
        compiler_params=_cparams(("parallel",)),
        name="router",
    )(x, g, wr)
    return (comb, counts[:, 0, :N_EXPERTS].reshape(tiles * N_EXPERTS), slot_col,
            slot_row.reshape(tiles * N_EXPERTS, 1, tile))


def _moe_kernel(cnt_ref, x_ref, g_ref, gn_ref, comb_ref, scol_ref, srow_ref, wg_ref, wu_ref, wd_ref, o_ref,
                h_ref, xc_ref, y_ref, *, sub, n_sub):
    i, e, f = pl.program_id(0), pl.program_id(1), pl.program_id(2)
    last_f = pl.num_programs(2) - 1
    tile = x_ref.shape[0]
    sub_pad = y_ref.shape[1]
    count = cnt_ref[i * N_EXPERTS + e]

    @pl.when((e == 0) & (f == 0))
    def _():
        h_ref[...] = _rms(x_ref[...], g_ref[...]).astype(BF16)
        o_ref[...] = x_ref[...]

    for s in range(n_sub):
        @pl.when((f == 0) & (s * sub < count))
        def _():
            slots = (lax.broadcasted_iota(jnp.int32, (sub, tile), 0) + s * sub).astype(F32)
            pick = (srow_ref[...] == slots).astype(BF16)
            xc_ref[s] = _dot(pick, h_ref[...]).astype(BF16)
            y_ref[s] = jnp.zeros((sub_pad, D_MODEL), F32)

        @pl.when(s * sub < count)
        def _():
            xs = xc_ref[s]
            mid = jax.nn.silu(_dot(xs, wg_ref[...])) * _dot(xs, wu_ref[...])
            y_ref[s, 0:sub, :] += _dot(mid.astype(BF16), wd_ref[...])

        @pl.when((f == last_f) & (s * sub < count))
        def _():
            mine = lax.broadcasted_iota(jnp.int32, (tile, LANE), 1) == e
            slot = jnp.sum(jnp.where(mine, scol_ref[...], 0.0), axis=-1, keepdims=True)
            gate = jnp.sum(jnp.where(mine, comb_ref[...], 0.0), axis=-1, keepdims=True)
            slots = (lax.broadcasted_iota(jnp.int32, (tile, sub_pad), 1) + s * sub).astype(F32)
            place = (slot == slots).astype(BF16)
            o_ref[...] += gate * _dot(place, y_ref[s].astype(BF16))

    @pl.when((e == pl.num_programs(1) - 1) & (f == last_f))
    def _():
        o_ref[...] = _rms(o_ref[...], gn_ref[...])


def _moe(x, g, g_out, comb, counts, slot_col, slot_row, wg, wu, wd, sub, tf):
    n = x.shape[0]
    tile = slot_row.shape[-1]
    n_sub = -(-tile // sub)
    sub_pad = -(-sub // LANE) * LANE
    grid_spec = pltpu.PrefetchScalarGridSpec(
        num_scalar_prefetch=1,
        grid=(n // tile, N_EXPERTS, D_FF_EXPERT // tf),
        in_specs=[
            pl.BlockSpec((tile, D_MODEL), lambda i, e, f, c: (i, 0)),
            pl.BlockSpec((1, D_MODEL), lambda i, e, f, c: (0, 0)),
            pl.BlockSpec((1, D_MODEL), lambda i, e, f, c: (0, 0)),
            pl.BlockSpec((tile, LANE), lambda i, e, f, c: (i, 0)),
            pl.BlockSpec((tile, LANE), lambda i, e, f, c: (i, 0)),
            pl.BlockSpec((None, 1, tile), lambda i, e, f, c: (i * N_EXPERTS + e, 0, 0)),
            pl.BlockSpec((None, D_MODEL, tf), lambda i, e, f, c: (e, 0, f)),
            pl.BlockSpec((None, D_MODEL, tf), lambda i, e, f, c: (e, 0, f)),
            pl.BlockSpec((None, tf, D_MODEL), lambda i, e, f, c: (e, f, 0)),
        ],
        out_specs=pl.BlockSpec((tile, D_MODEL), lambda i, e, f, c: (i, 0)),
        scratch_shapes=[
            pltpu.VMEM((tile, D_MODEL), BF16),
            pltpu.VMEM((n_sub, sub, D_MODEL), BF16),
            pltpu.VMEM((n_sub, sub_pad, D_MODEL), F32),
        ],
    )
    return pl.pallas_call(
        functools.partial(_moe_kernel, sub=sub, n_sub=n_sub),
        grid_spec=grid_spec,
        out_shape=jax.ShapeDtypeStruct((n, D_MODEL), F32),
        compiler_params=_cparams(("parallel", "arbitrary", "arbitrary")),
        name="moe",
    )(counts, x, g, g_out, comb, slot_col, slot_row, wg, wu, wd)


def _norm_kernel(x_ref, g_ref, o_ref):
    o_ref[...] = _rms(x_ref[...], g_ref[...]).astype(o_ref.dtype)


def _norm(x, g, tm, out_dtype):
    n = x.shape[0]
    return pl.pallas_call(
        _norm_kernel,
        grid=(n // tm,),
        in_specs=[pl.BlockSpec((tm, D_MODEL), lambda i: (i, 0)), pl.BlockSpec((1, D_MODEL), lambda i: (0, 0))],
        out_specs=pl.BlockSpec((tm, D_MODEL), lambda i: (i, 0)),
        out_shape=jax.ShapeDtypeStruct((n, D_MODEL), out_dtype),
        compiler_params=_cparams(("parallel",)),
        name="norm",
    )(x, g)


def _rope_tables(pos, split_half):
    half = HEAD_DIM // 2
    inv = ROPE_THETA ** (-jnp.arange(half, dtype=F32) / half)
    ang = pos.astype(F32)[:, None] * inv[None, :]
    cos = jnp.cos(ang)
    sin = jnp.sin(ang)
    if split_half:
        return jnp.tile(cos, (1, LANE // half)), jnp.tile(sin, (1, LANE // half))
    return (jnp.tile(jnp.concatenate([cos, cos], axis=-1), (1, LANE // HEAD_DIM)),
            jnp.tile(jnp.concatenate([-sin, sin], axis=-1), (1, LANE // HEAD_DIM)))


def _to_split_half(a):
    lead = a.shape[:-1]
    a = a.reshape(*lead, HEADS_PER_GROUP // QUAD, QUAD, 2, HEAD_DIM // 2)
    return jnp.swapaxes(a, -2, -3).reshape(*lead, A_OUT)


def _layer_weights(l, w_in, w_gate_a2, b_gate_a, g_gla, w_branch_a, w_branch_b, w_out):
    w = w_in[l]
    offs = np.cumsum((A_WIDTH, A_WIDTH, A_WIDTH, GLA_KEY, GLA_KEY, GLA_VAL, GLA_VAL, GATE_RANK, D_MODEL, D_MODEL))
    qa, ka, va, qb, kb, vb, rb, glr, ga, gb = jnp.split(w, [int(o) for o in offs[:-1]], axis=1)
    def attn_weights(split_half):
        relay = _to_split_half if split_half else (lambda a: a)
        return jnp.stack([jnp.stack([relay(qa[:, g * A_OUT:(g + 1) * A_OUT]), relay(ka[:, g * A_OUT:(g + 1) * A_OUT]),
                                     va[:, g * A_OUT:(g + 1) * A_OUT]]) for g in range(N_GROUPS)]).astype(BF16)

    w_attn = {False: attn_weights(False), True: attn_weights(True)}
    glr = jnp.pad(glr, ((0, 0), (0, COL_BLOCK - GATE_RANK)))
    rest = jnp.concatenate([qb, kb, vb, rb, ga, gb, glr], axis=1)
    w_rest = jnp.swapaxes(rest.reshape(D_MODEL, N_REST, COL_BLOCK), 0, 1).astype(BF16)
    wa = jnp.pad(w_gate_a2[l], ((0, LANE - GATE_RANK), (0, 0))).astype(BF16)
    return dict(w_attn=w_attn, w_rest=w_rest, wa=wa, ba=b_gate_a[l][None, :], gg=g_gla[l][None, :],
                wpa=w_branch_a[l].astype(BF16), wpb=w_branch_b[l].astype(BF16), wo=w_out[l].astype(BF16))


def _head_expand():
    e = np.zeros((LANE, A_OUT), np.float32)
    for h in range(HEADS_PER_GROUP):
        e[h, h * HEAD_DIM:(h + 1) * HEAD_DIM] = 1.0
    return jnp.asarray(e, BF16)


def _trunk(x, rope, caches, gla_state, lw, ffw, g_mix, g_ffn, g_final, batch, seq, tm, tm_proj, moe_tile, moe_sub,
           p_dtype):
    cos_t, sin_t = rope
    expand = _head_expand()
    prompt = caches is None
    cache_out = [None] * N_GROUPS
    states_out = []
    assert DEPTH % 2 == 0
    h = _norm(x, g_mix[0][None, :], tm, BF16)
    for l in range(DEPTH):
        w = lw[l]
        rest = _inproj_rest(h, w["w_rest"], tm_proj, p_dtype)
        outs, maxes, dens = [], [], []
        for g, (window, dil) in enumerate(DSWA_GROUPS):
            if prompt:
                qkv = _inproj_attn(h, w["w_attn"][True], cos_t, sin_t, g, batch, seq, dil, tm_proj, p_dtype, True)
                o, mx, den = _attn_prompt(qkv, g)
                cache_out[g] = _kv_tail(qkv, cache_out[g], l, min(window, seq))
            else:
                qkv = _inproj_attn(h, w["w_attn"][False], cos_t, sin_t, g, 1, batch * seq, 1, tm_proj, p_dtype, False)
                o, lse, cache_out[g] = _attn_sample(qkv.reshape(3, batch * seq, COL_BLOCK), caches[g], cache_out[g],
                                                    l, g, seq)
                o = o.reshape(1, 1, batch * seq, A_OUT)
                mx = lse.reshape(1, 1, batch * seq, LANE)
                den = jnp.ones_like(mx)
            outs.append(o)
            maxes.append(mx)
            dens.append(den)
        if prompt:
            ob, s_new = _gla(rest, w["wa"], w["ba"], w["gg"], batch, seq, GLA_CHUNK, GLA_CHUNKS_PER_STEP, BF16)
            x = _merge(x, outs, maxes, dens, ob, rest, w["wpa"], w["wpb"], w["wo"], expand, seq, min(tm, 512))
        else:
            ob, s_new = _gla_decode(rest, w["wa"], w["ba"], w["gg"], gla_state, l, batch, seq, F32)
            x = _merge(x, outs, maxes, dens, ob, rest, w["wpa"], w["wpb"], w["wo"], expand, batch * seq, tm)
        states_out.append(s_new)
        i = l // 2
        if l % 2 == 0:
            x, h = _ffn(x, g_ffn[l][None, :], g_mix[l + 1][None, :], ffw["wg"][i], ffw["wu"][i], ffw["wd"][i], tm)
        else:
            routing = _router(x, g_ffn[l][None, :], ffw["wr"][i], moe_tile)
            g_out = g_final if l == DEPTH - 1 else g_mix[l + 1]
            x = _moe(x, g_ffn[l][None, :], g_out[None, :], *routing, ffw["eg"][i], ffw["eu"][i], ffw["ed"][i],
                     moe_sub, D_FF_EXPERT // 2)
            assert l == DEPTH - 1
    y = x
    bufs = [jnp.transpose(c.reshape(DEPTH, batch, 2, HEADS_PER_GROUP, HEAD_DIM, c.shape[-1]), (0, 1, 5, 2, 3, 4))
            for c in cache_out]
    return (y, bufs[0], bufs[1], bufs[2], jnp.stack(states_out))


def kernel(x_prompt, x_sample, cache_kv_w128, cache_kv_w512, cache_kv_w2048, state_gla, g_mix, w_in, w_gate_a2, b_gate_a, g_gla, w_branch_a, w_branch_b, w_out, g_ffn, w_ff_gate, w_ff_up, w_ff_down, w_router, w_exp_gate, w_exp_up, w_exp_down, g_final):
    batch, seq, _ = x_prompt.shape
    dec_batch, dec_seq, _ = x_sample.shape
    assert seq % (DSWA_GROUPS[-1][1] * Q_BLOCK) == 0 and seq % (GLA_CHUNKS_PER_STEP * GLA_CHUNK) == 0
    assert dec_seq % GLA_CHUNK != 0 and dec_seq % 8 == 0
    assert all(c.shape[2] == w for c, (w, _) in zip((cache_kv_w128, cache_kv_w512, cache_kv_w2048), DSWA_GROUPS))

    lw = [_layer_weights(l, w_in, w_gate_a2, b_gate_a, g_gla, w_branch_a, w_branch_b, w_out) for l in range(DEPTH)]
    ffw = dict(wg=w_ff_gate.astype(BF16), wu=w_ff_up.astype(BF16), wd=w_ff_down.astype(BF16),
               wr=jnp.pad(w_router, ((0, 0), (0, 0), (0, LANE - N_EXPERTS))),
               eg=w_exp_gate.astype(BF16), eu=w_exp_up.astype(BF16), ed=w_exp_down.astype(BF16))

    y_p, kv128_p, kv512_p, kv2048_p, gla_p = _trunk(
        x_prompt.reshape(batch * seq, D_MODEL), _rope_tables(jnp.arange(seq, dtype=jnp.int32), True), None, None,
        lw, ffw, g_mix, g_ffn, g_final, batch, seq, 1024, 2048, 1024, 288, BF16)

    n_s = dec_batch * dec_seq
    pos_s = jnp.tile(PAST_LEN + jnp.arange(dec_seq, dtype=jnp.int32), dec_batch)
    caches = [jnp.transpose(c, (0, 1, 3, 4, 5, 2)).reshape(DEPTH, dec_batch, 2, A_OUT, c.shape[2])
              for c in (cache_kv_w128, cache_kv_w512, cache_kv_w2048)]
    y_s, kv128_s, kv512_s, kv2048_s, gla_s = _trunk(
        x_sample.reshape(n_s, D_MODEL), _rope_tables(pos_s, False), caches, state_gla,
        lw, ffw, g_mix, g_ffn, g_final, dec_batch, dec_seq, n_s, n_s, n_s, 96, F32)

    return (y_p.reshape(batch, seq, D_MODEL), y_s.reshape(dec_batch, dec_seq, D_MODEL),
            kv128_p, kv512_p, kv2048_p, gla_p, kv128_s, kv512_s, kv2048_s, gla_s)
```

```python
import functools

import jax
import jax.numpy as jnp
import numpy as np
from jax import lax
from jax.experimental import pallas as pl
from jax.experimental.pallas import tpu as pltpu

F32 = jnp.float32
BF16 = jnp.bfloat16

D_MODEL = 1024
DEPTH = 2
PAST_LEN = 16384
HEAD_DIM = 64
DSWA_GROUPS = ((128, 1), (512, 4), (2048, 16))
N_GROUPS = 3
HEADS_PER_GROUP = 8
A_WIDTH = N_GROUPS * HEADS_PER_GROUP * HEAD_DIM
A_OUT = HEADS_PER_GROUP * HEAD_DIM
Q_BLOCK = 128
GLA_HEADS = 4
GLA_KEY = 512
GLA_VAL = 1024
GLA_DK = 128
GLA_DV = 256
GATE_RANK = 16
GATE_TAU = 16.0
GLA_CHUNK = 64
D_FF = 2816
N_EXPERTS = 8
D_FF_EXPERT = 3584
ROPE_THETA = 10000.0
EPS = 1e-6
LOG2_E = 1.4426950408889634

LANE = 128
COL_BLOCK = 512
R_QB, R_KB, R_VB, R_RB, R_GA, R_GB, R_GLR, N_REST = 0, 1, 2, 4, 6, 8, 10, 11
ROW_CHUNK = 256
GLA_CHUNKS_PER_STEP = 16
VMEM_LIMIT = 56 * 1024 * 1024


def _cparams(sem):
    return pltpu.CompilerParams(dimension_semantics=sem, vmem_limit_bytes=VMEM_LIMIT)


def _rms(xf, g):
    return xf * lax.rsqrt(jnp.mean(xf * xf, axis=-1, keepdims=True) + EPS) * g


def _dot(a, b):
    return jnp.dot(a, b, preferred_element_type=F32)


def _dot_nt(a, b):
    return lax.dot_general(a, b, (((1,), (1,)), ((), ())), preferred_element_type=F32)


def _dot_tn(a, b):
    return lax.dot_general(a, b, (((0,), (0,)), ((), ())), preferred_element_type=F32)


def _div_pow2(x, d):
    assert d & (d - 1) == 0
    return lax.shift_right_logical(x, int(d).bit_length() - 1)


def _mod_pow2(x, d):
    assert d & (d - 1) == 0
    return x & (d - 1)


def _split_dot(a_f32, b_bf16, dot=_dot):
    hi = a_f32.astype(BF16)
    lo = (a_f32 - hi.astype(F32)).astype(BF16)
    return dot(hi, b_bf16) + dot(lo, b_bf16)


def _inproj_attn_kernel(h_ref, w_ref, cos_ref, sin_ref, o_ref, *scratch, dil, split_half, q_scale):
    part = pl.program_id(1)
    scale = jnp.where(part == 0, q_scale, 1.0).astype(F32)
    tm = h_ref.shape[0]
    chunk = min(tm, ROW_CHUNK)
    n_lane_groups = COL_BLOCK // LANE
    for rc in range(tm // chunk):
        rows = slice(rc * chunk, (rc + 1) * chunk)
        acc = _dot(h_ref[rows, :], w_ref[...])
        cos = jnp.where(part == 2, 1.0, cos_ref[rows, :] * scale)
        sin = jnp.where(part == 2, 0.0, sin_ref[rows, :] * scale)
        ys = []
        if split_half:
            for c in range(0, n_lane_groups, 2):
                a = acc[:, c * LANE:(c + 1) * LANE]
                b = acc[:, (c + 1) * LANE:(c + 2) * LANE]
                ys += [a * cos - b * sin, a * sin + b * cos]
        else:
            first_half = _mod_pow2(lax.broadcasted_iota(jnp.int32, cos.shape, 1), HEAD_DIM) < HEAD_DIM // 2
            for c in range(n_lane_groups):
                xs = acc[:, c * LANE:(c + 1) * LANE]
                swapped = jnp.where(first_half, pltpu.roll(xs, LANE - HEAD_DIM // 2, 1), pltpu.roll(xs, HEAD_DIM // 2, 1))
                ys.append(xs * cos + swapped * sin)
        if dil == 1:
            for c, y in enumerate(ys):
                o_ref[0, rows, c * LANE:(c + 1) * LANE] = y.astype(o_ref.dtype)
        else:
            sub = chunk // dil
            pitch = _residue_pitch(dil)
            base = rc * sub * pitch
            for c, y in enumerate(ys):
                if pitch == dil:
                    scratch[0][c, rows, :] = y
                else:
                    for u in range(sub):
                        scratch[0][c, base + u * pitch:base + u * pitch + dil, :] = y[u * dil:(u + 1) * dil, :]
            for r in range(dil):
                for c in range(n_lane_groups):
                    o_ref[r, rc * sub:(rc + 1) * sub, c * LANE:(c + 1) * LANE] = (
                        scratch[0][c, pl.ds(base + r, sub, stride=pitch), :].astype(o_ref.dtype))


def _residue_pitch(dil):
    return dil + 8 if dil % 16 == 0 else dil


def _inproj_attn(h, w, cos_t, sin_t, group, batch, seq, dil, tm, out_dtype, split_half):
    tiles = seq // tm
    scratch = [pltpu.VMEM((COL_BLOCK // LANE, tm // dil * _residue_pitch(dil), LANE), F32)] if dil > 1 else []
    q_scale = HEAD_DIM ** -0.5 * (LOG2_E if split_half else 1.0)
    return pl.pallas_call(
        functools.partial(_inproj_attn_kernel, dil=dil, split_half=split_half, q_scale=q_scale),
        grid=(batch * tiles, 3),
        in_specs=[
            pl.BlockSpec((tm, D_MODEL), lambda i, j: (i, 0)),
            pl.BlockSpec((None, None, D_MODEL, COL_BLOCK), lambda i, j: (group, j, 0, 0)),
            pl.BlockSpec((tm, LANE), lambda i, j: (i % tiles, 0)),
            pl.BlockSpec((tm, LANE), lambda i, j: (i % tiles, 0)),
        ],
        out_specs=pl.BlockSpec((None, None, dil, tm // dil, COL_BLOCK), lambda i, j: (j, i // tiles, 0, i % tiles, 0)),
        out_shape=jax.ShapeDtypeStruct((3, batch, dil, seq // dil, COL_BLOCK), out_dtype),
        scratch_shapes=scratch,
        compiler_params=_cparams(("parallel", "arbitrary")),
        name=f"inproj_attn_g{group}",
    )(h, w, cos_t, sin_t)


def _inproj_rest_kernel(h_ref, w_ref, o_ref):
    tm = h_ref.shape[0]
    chunk = min(tm, ROW_CHUNK)
    for rc in range(tm // chunk):
        rows = slice(rc * chunk, (rc + 1) * chunk)
        o_ref[rows, :] = _dot(h_ref[rows, :], w_ref[...]).astype(o_ref.dtype)


def _inproj_rest(h, w, tm, out_dtype):
    n = h.shape[0]
    return pl.pallas_call(
        _inproj_rest_kernel,
        grid=(n // tm, N_REST),
        in_specs=[
            pl.BlockSpec((tm, D_MODEL), lambda i, j: (i, 0)),
            pl.BlockSpec((None, D_MODEL, COL_BLOCK), lambda i, j: (j, 0, 0)),
        ],
        out_specs=pl.BlockSpec((None, tm, COL_BLOCK), lambda i, j: (j, i, 0)),
        out_shape=jax.ShapeDtypeStruct((N_REST, n, COL_BLOCK), out_dtype),
        compiler_params=_cparams(("parallel", "arbitrary")),
        name="inproj_rest",
    )(h, w)


QUAD = 4
QUAD_WIDTH = QUAD * HEAD_DIM
MAX_Q_BLOCKS_PER_STEP = 8


def _attn_prompt_kernel(bias_ref, q_ref, kp_ref, kc_ref, vp_ref, vc_ref, o_ref, m_ref, l_ref):
    n = pl.program_id(2)
    rows = QUAD * Q_BLOCK
    row_head = _div_pow2(lax.broadcasted_iota(jnp.int32, (rows, QUAD_WIDTH), 0), Q_BLOCK)
    col = lax.broadcasted_iota(jnp.int32, (rows, QUAD_WIDTH), 1)
    q_lanes = _div_pow2(_mod_pow2(col, LANE), HEAD_DIM // 2) == row_head
    v_lanes = _div_pow2(col, HEAD_DIM) == row_head
    lane = lax.broadcasted_iota(jnp.int32, (Q_BLOCK, LANE), 1)
    for qb in range(q_ref.shape[0] // Q_BLOCK):
        cur = slice(qb * Q_BLOCK, (qb + 1) * Q_BLOCK)
        if qb == 0:
            bias = bias_ref[jnp.where(n == 0, 1, 0)]
        else:
            bias = bias_ref[0]
            prev = slice((qb - 1) * Q_BLOCK, qb * Q_BLOCK)
        m_all = jnp.zeros((Q_BLOCK, LANE), F32)
        l_all = jnp.ones((Q_BLOCK, LANE), F32)
        for c in range(A_OUT // QUAD_WIDTH):
            cs = slice(c * QUAD_WIDTH, (c + 1) * QUAD_WIDTH)
            q = q_ref[cur, cs].astype(F32)
            q4 = jnp.where(q_lanes, jnp.concatenate([q] * QUAD, axis=0), 0.0).astype(BF16)
            k_prev = kp_ref[:, cs] if qb == 0 else kc_ref[prev, cs]
            v_prev = vp_ref[:, cs] if qb == 0 else vc_ref[prev, cs]
            k = jnp.concatenate([k_prev, kc_ref[cur, cs]], axis=0)
            v = jnp.concatenate([v_prev, vc_ref[cur, cs]], axis=0)
            s = _dot_nt(q4, k) + bias
            m = jnp.max(s, axis=-1, keepdims=True)
            p = jnp.exp2(s - m)
            l = jnp.sum(p, axis=-1, keepdims=True)
            o4 = jnp.where(v_lanes, _dot(p.astype(BF16), v), 0.0)
            o = o4[0:Q_BLOCK]
            for j in range(QUAD):
                hs = slice(j * Q_BLOCK, (j + 1) * Q_BLOCK)
                if j > 0:
                    o = o + o4[hs]
                m_all = jnp.where(lane == c * QUAD + j, m[hs], m_all)
                l_all = jnp.where(lane == c * QUAD + j, l[hs], l_all)
            o_ref[cur, cs] = o.astype(o_ref.dtype)
        m_ref[cur, :] = m_all
        l_ref[cur, :] = l_all


def _band_bias(span):
    qi = np.arange(QUAD * Q_BLOCK)[:, None] % Q_BLOCK
    kj = np.arange(2 * Q_BLOCK)[None, :]
    dist = Q_BLOCK + qi - kj
    band = (dist >= 0) & (dist <= span)
    first = band & (kj >= Q_BLOCK)
    return jnp.asarray(np.where(np.stack([band, first]), 0.0, -np.inf), F32)


def _attn_prompt(qkv, group):
    window, dil = DSWA_GROUPS[group]
    _, batch, _, sub_len, _ = qkv.shape
    q_blocks = min(MAX_Q_BLOCKS_PER_STEP, sub_len // Q_BLOCK)
    step_rows = q_blocks * Q_BLOCK
    nb = sub_len // step_rows

    def spec(part, prev):
        if prev:
            return pl.BlockSpec((None, None, None, Q_BLOCK, COL_BLOCK),
                                lambda b, r, n: (part, b, r, jnp.maximum(n * q_blocks - 1, 0), 0))
        return pl.BlockSpec((None, None, None, step_rows, COL_BLOCK), lambda b, r, n: (part, b, r, n, 0))

    bias = _band_bias(window // dil)
    stat_spec = pl.BlockSpec((None, None, step_rows, LANE), lambda b, r, n: (b, r, n, 0))
    stat_shape = jax.ShapeDtypeStruct((batch, dil, sub_len, LANE), F32)
    return pl.pallas_call(
        _attn_prompt_kernel,
        grid=(batch, dil, nb),
        in_specs=[pl.BlockSpec(bias.shape, lambda b, r, n: (0, 0, 0)),
                  spec(0, False), spec(1, True), spec(1, False), spec(2, True), spec(2, False)],
        out_specs=[pl.BlockSpec((None, None, step_rows, A_OUT), lambda b, r, n: (b, r, n, 0)), stat_spec, stat_spec],
        out_shape=[jax.ShapeDtypeStruct((batch, dil, sub_len, A_OUT), BF16), stat_shape, stat_shape],
        compiler_params=_cparams(("parallel", "parallel", "arbitrary")),
        name=f"attn_prompt_g{group}",
    )(bias, qkv, qkv, qkv, qkv, qkv)


def _kv_tail_kernel(*refs, dil, aliased):
    if aliased:
        k_ref, v_ref, _, out_ref, scr = refs
        _kv_tail_body(k_ref, v_ref, out_ref, scr, dil)
        return
    k_ref, v_ref, out_ref, scr = refs

    @pl.when(pl.program_id(0) == 0)
    def _():
        _kv_tail_body(k_ref, v_ref, out_ref, scr, dil)

    @pl.when(pl.program_id(0) > 0)
    def _():
        out_ref[...] = jnp.zeros_like(out_ref)


def _kv_tail_body(k_ref, v_ref, out_ref, scr, dil):
    keep = out_ref.shape[2]
    sub = keep // dil
    half_w = HEAD_DIM // 2
    for kv, ref in enumerate((k_ref, v_ref)):
        for c in range(COL_BLOCK // LANE):
            cs = slice(c * LANE, (c + 1) * LANE)
            if dil == 1:
                tok = ref[0, :, cs].astype(F32)
            else:
                for r in range(dil):
                    scr[pl.ds(r, sub, stride=dil), :] = ref[r, :, cs].astype(F32)
                tok = scr[...]
            chan = tok.T
            if kv == 1:
                out_ref[kv, cs, :] = chan
            else:
                for j in range(QUAD):
                    dst = (QUAD * (c // 2) + j) * HEAD_DIM + (c % 2) * half_w
                    out_ref[kv, dst:dst + half_w, :] = chan[j * half_w:(j + 1) * half_w, :]


def _kv_tail(qkv, prev_out, layer, keep):
    _, batch, dil, sub_len, _ = qkv.shape
    last = sub_len // (keep // dil) - 1
    first = prev_out is None
    assert first == (layer == 0)
    phases = DEPTH if first else 1

    def spec(part):
        return pl.BlockSpec((None, None, dil, keep // dil, COL_BLOCK),
                            lambda p, b: (part, jnp.where(p == 0, b, batch - 1), 0, last, 0))

    in_specs = [spec(1), spec(2)]
    args = [qkv, qkv]
    aliases = {}
    if not first:
        in_specs.append(pl.BlockSpec(memory_space=pl.ANY))
        args.append(prev_out)
        aliases = {2: 0}
    return pl.pallas_call(
        functools.partial(_kv_tail_kernel, dil=dil, aliased=not first),
        grid=(phases, batch),
        in_specs=in_specs,
        out_specs=pl.BlockSpec((None, None, 2, A_OUT, keep), lambda p, b: (layer + p, b, 0, 0, 0)),
        out_shape=jax.ShapeDtypeStruct((DEPTH, batch, 2, A_OUT, keep), F32),
        scratch_shapes=[pltpu.VMEM((keep, LANE), F32)],
        input_output_aliases=aliases,
        compiler_params=_cparams(("arbitrary", "arbitrary")),
        name=f"kv_tail_{keep}",
    )(*args)


SHIFT_ROWS = 128
SAMPLE_BLOCK_BYTES = 8 * 1024 * 1024


def _attn_sample_kernel(*refs, dil, span, first):
    if first:
        q_ref, k_ref, v_ref, c_ref, o_ref, lse_ref, cout_ref = refs
    else:
        q_ref, k_ref, v_ref, c_ref, _, o_ref, lse_ref, cout_ref = refs
    n_rows = c_ref.shape[0]
    t_new = q_ref.shape[0] // n_rows

    def update(j):
        tok = pl.ds(j * t_new, t_new)
        q, k, v = q_ref.at[tok], k_ref.at[tok], v_ref.at[tok]
        _sample_attention(q, k, v, c_ref.at[j], o_ref.at[tok], lse_ref.at[tok], dil, span)
        _append_new(c_ref.at[j], k, v, cout_ref.at[j])

    def shift_only(j):
        for kv in range(2):
            for rb in range(A_OUT // SHIFT_ROWS):
                rs = slice(rb * SHIFT_ROWS, (rb + 1) * SHIFT_ROWS)
                cout_ref[j, kv, rs, :] = pltpu.roll(c_ref[j, kv, rs, :], c_ref.shape[3] - t_new, 1)

    if first:
        @pl.when(pl.program_id(0) == 0)
        def _():
            for j in range(n_rows):
                update(j)

        @pl.when(pl.program_id(0) > 0)
        def _():
            for j in range(n_rows):
                shift_only(j)
    else:
        for j in range(n_rows):
            update(j)


def _append_new(c_ref, k_ref, v_ref, cout_ref):
    t_new = k_ref.shape[0]
    lb = c_ref.shape[2]
    width = cout_ref.shape[2]
    lane = lax.broadcasted_iota(jnp.int32, (SHIFT_ROWS, LANE), 1)
    zeros = jnp.zeros((LANE - t_new, A_OUT), F32)
    for kv, ref in enumerate((k_ref, v_ref)):
        new = ref[...].astype(F32)
        new_t = pltpu.roll(jnp.concatenate([new, zeros], axis=0).T, LANE - t_new, 1)
        for rb in range(A_OUT // SHIFT_ROWS):
            rs = slice(rb * SHIFT_ROWS, (rb + 1) * SHIFT_ROWS)
            shifted = pltpu.roll(c_ref[kv, rs, lb - width:lb], width - t_new, 1)
            if width > LANE:
                cout_ref[kv, rs, 0:width - LANE] = shifted[:, 0:width - LANE]
            cout_ref[kv, rs, width - LANE:width] = jnp.where(lane >= LANE - t_new, new_t[rs, :], shifted[:, width - LANE:width])


def _sample_attention(q_ref, k_ref, v_ref, c_ref, o_ref, lse_ref, dil, span):
    t_new = q_ref.shape[0]
    lb = c_ref.shape[2]
    rows = HEADS_PER_GROUP * t_new
    q = q_ref[...].astype(F32)
    k_new = k_ref[...].astype(F32)
    v_new = v_ref[...].astype(F32)

    q_rep = jnp.concatenate([q] * HEADS_PER_GROUP, axis=0)
    row_head = _div_pow2(lax.broadcasted_iota(jnp.int32, (rows, A_OUT), 0), t_new)
    col_head = _div_pow2(lax.broadcasted_iota(jnp.int32, (rows, A_OUT), 1), HEAD_DIM)
    diag = row_head == col_head
    q_bd = jnp.where(diag, q_rep, 0.0).astype(BF16)

    s_buf = _dot(q_bd, c_ref[0].astype(BF16))
    s_new = _dot_nt(q_bd, k_new.astype(BF16))

    tok_b = _mod_pow2(lax.broadcasted_iota(jnp.int32, (rows, lb), 0), t_new)
    pos_b = lax.broadcasted_iota(jnp.int32, (rows, lb), 1)
    dist_b = lb + tok_b - pos_b
    ok_b = (_mod_pow2(dist_b, dil) == 0) & (dist_b <= span * dil)
    tok_n = _mod_pow2(lax.broadcasted_iota(jnp.int32, (rows, t_new), 0), t_new)
    pos_n = lax.broadcasted_iota(jnp.int32, (rows, t_new), 1)
    dist_n = tok_n - pos_n
    ok_n = (dist_n >= 0) & (_mod_pow2(dist_n, dil) == 0) & (dist_n <= span * dil)
    s_buf = jnp.where(ok_b, s_buf, -jnp.inf)
    s_new = jnp.where(ok_n, s_new, -jnp.inf)

    m = jnp.maximum(jnp.max(s_buf, axis=-1, keepdims=True), jnp.max(s_new, axis=-1, keepdims=True))
    p_buf = jnp.exp(s_buf - m)
    p_new = jnp.exp(s_new - m)
    l = jnp.sum(p_buf, axis=-1, keepdims=True) + jnp.sum(p_new, axis=-1, keepdims=True)
    o_full = (_dot_nt(p_buf.astype(BF16), c_ref[1].astype(BF16)) + _dot(p_new.astype(BF16), v_new.astype(BF16))) / l
    o_full = jnp.where(diag, o_full, 0.0)
    lse = (m + jnp.log(l)) * LOG2_E

    o = o_full[0:t_new, :]
    lane_t = lax.broadcasted_iota(jnp.int32, (t_new, LANE), 1)
    lse_all = jnp.zeros((t_new, LANE), F32)
    for h in range(HEADS_PER_GROUP):
        if h > 0:
            o = o + o_full[h * t_new:(h + 1) * t_new, :]
        lse_all = jnp.where(lane_t == h, lse[h * t_new:(h + 1) * t_new, :], lse_all)
    o_ref[...] = o.astype(o_ref.dtype)
    lse_ref[...] = lse_all


def _attn_sample(qkv, cache, prev_out, layer, group, t_new):
    window, dil = DSWA_GROUPS[group]
    n_rows = qkv.shape[1]
    batch = n_rows // t_new
    lb = cache.shape[-1]

    first = prev_out is None
    assert first == (layer == 0) and t_new <= LANE
    phases = DEPTH if first else 1
    rows = max(1, min(batch, SAMPLE_BLOCK_BYTES // (2 * A_OUT * lb * 4)))
    assert batch % rows == 0
    blocks = batch // rows

    def row(p, b):
        return jnp.where(p == 0, b, blocks - 1)

    def pspec(part):
        return pl.BlockSpec((None, rows * t_new, COL_BLOCK), lambda p, b: (part, row(p, b), 0))

    in_specs = [pspec(0), pspec(1), pspec(2),
                pl.BlockSpec((None, rows, 2, A_OUT, lb), lambda p, b: (layer + p, b, 0, 0, 0))]
    args = [qkv, qkv, qkv, cache]
    aliases = {}
    if first:
        cache_out = pl.BlockSpec((None, rows, 2, A_OUT, lb), lambda p, b: (layer + p, b, 0, 0, 0))
    else:
        in_specs.append(pl.BlockSpec(memory_space=pl.ANY))
        args.append(prev_out)
        aliases = {4: 2}
        cache_out = pl.BlockSpec((None, rows, 2, A_OUT, LANE), lambda p, b: (layer, b, 0, 0, lb // LANE - 1))
    return pl.pallas_call(
        functools.partial(_attn_sample_kernel, dil=dil, span=window // dil, first=first),
        grid=(phases, blocks),
        in_specs=in_specs,
        out_specs=[
            pl.BlockSpec((rows * t_new, A_OUT), lambda p, b: (row(p, b), 0)),
            pl.BlockSpec((rows * t_new, LANE), lambda p, b: (row(p, b), 0)),
            cache_out,
        ],
        out_shape=[
            jax.ShapeDtypeStruct((n_rows, A_OUT), F32),
            jax.ShapeDtypeStruct((n_rows, LANE), F32),
            jax.ShapeDtypeStruct(cache.shape, F32),
        ],
        input_output_aliases=aliases,
        compiler_params=_cparams(("arbitrary", "arbitrary")),
        name=f"attn_sample_g{group}",
    )(*args)


def _gla_kernel(q_ref, k_ref, v_ref, r_ref, glr_ref, wa_ref, ba_ref, gg_ref, o_ref, sout_ref, s_ref, *, chunk, n_chunks):
    step = pl.program_id(2)

    @pl.when(step == 0)
    def _():
        s_ref[...] = jnp.zeros_like(s_ref)

    ri = lax.broadcasted_iota(jnp.int32, (chunk, chunk), 0)
    ci = lax.broadcasted_iota(jnp.int32, (chunk, chunk), 1)
    causal = ri >= ci
    tri = causal.astype(BF16)
    gg = gg_ref[...]

    gate = _dot(glr_ref[...].astype(BF16), wa_ref[...]) + ba_ref[...]
    log_a = jax.nn.log_sigmoid(gate) * (1.0 / GATE_TAU)
    log_hi = log_a.astype(BF16)
    log_lo = (log_a - log_hi.astype(F32)).astype(BF16)
    chunks = [slice(c * chunk, (c + 1) * chunk) for c in range(n_chunks)]
    cums = [_dot(tri, log_hi[rs, :]) + _dot(tri, log_lo[rs, :]) for rs in chunks]
    lasts = [cum[chunk - 1:chunk, :] for cum in cums]
    cum = jnp.concatenate(cums, axis=0)
    last = jnp.concatenate([jnp.broadcast_to(l, (chunk, GLA_DK)) for l in lasts], axis=0)
    grow = jnp.exp(cum)
    q_in = (q_ref[...].astype(F32) * (GLA_DK ** -0.5) * grow).astype(BF16)
    k = k_ref[...].astype(F32)
    k_in = (k * jnp.exp(-cum)).astype(BF16)
    k_out = (k * jnp.exp(last - cum)).astype(BF16)
    v = v_ref[...].astype(BF16)
    o_intra, kv, decay = [], [], []
    for rs, l in zip(chunks, lasts):
        att = jnp.where(causal, _dot_nt(q_in[rs, :], k_in[rs, :]), 0.0).astype(BF16)
        o_intra.append(_dot(att, v[rs, :]))
        kv.append(_dot_tn(k_out[rs, :], v[rs, :]))
        d = jnp.broadcast_to(jnp.exp(l), (GLA_DK, GLA_DK)).T
        decay.append(jnp.concatenate([d] * (GLA_DV // GLA_DK), axis=1))

    state = s_ref[...]
    for c, rs in enumerate(chunks):
        o = o_intra[c] + _dot(q_in[rs, :], state.astype(BF16))
        state = decay[c] * state + kv[c]
        y = _rms(o, gg) * jax.nn.silu(r_ref[rs, :].astype(F32))
        o_ref[rs, :] = y.astype(o_ref.dtype)
    s_ref[...] = state

    @pl.when(step == pl.num_programs(2) - 1)
    def _():
        sout_ref[...] = state


def _gla_decode_kernel(q_ref, k_ref, v_ref, r_ref, glr_ref, wa_ref, ba_ref, gg_ref, s0_ref, o_ref, sout_ref):
    t_new = q_ref.shape[0]
    ri = lax.broadcasted_iota(jnp.int32, (t_new, t_new), 0)
    ci = lax.broadcasted_iota(jnp.int32, (t_new, t_new), 1)
    causal = ri >= ci
    tri = causal.astype(BF16)
    glr = glr_ref[...].astype(BF16)
    heads_per_block = COL_BLOCK // GLA_DV
    for h in range(GLA_HEADS):
        ks = slice(h * GLA_DK, (h + 1) * GLA_DK)
        vb, vs = h // heads_per_block, slice((h % heads_per_block) * GLA_DV, (h % heads_per_block + 1) * GLA_DV)
        gate = _dot(glr, wa_ref[:, ks]) + ba_ref[:, ks]
        log_a = jax.nn.log_sigmoid(gate) * (1.0 / GATE_TAU)
        log_hi = log_a.astype(BF16)
        log_lo = (log_a - log_hi.astype(F32)).astype(BF16)
        cum = _dot(tri, log_hi) + _dot(tri, log_lo)
        last = cum[t_new - 1:t_new, :]
        q_in = (q_ref[:, ks].astype(F32) * (GLA_DK ** -0.5) * jnp.exp(cum)).astype(BF16)
        k = k_ref[:, ks].astype(F32)
        k_in = (k * jnp.exp(-cum)).astype(BF16)
        k_out = (k * jnp.exp(last - cum)).astype(BF16)
        v = v_ref[vb, :, vs].astype(BF16)
        att = jnp.where(causal, _dot_nt(q_in, k_in), 0.0).astype(BF16)
        state = s0_ref[h].astype(F32)
        o = _dot(att, v) + _dot(q_in, state.astype(BF16))
        d = jnp.broadcast_to(jnp.exp(last), (GLA_DK, GLA_DK)).T
        sout_ref[h] = jnp.concatenate([d] * (GLA_DV // GLA_DK), axis=1) * state + _dot_tn(k_out, v)
        os_ = slice(h * GLA_DV, (h + 1) * GLA_DV)
        y = _rms(o, gg_ref[:, os_]) * jax.nn.silu(r_ref[vb, :, vs].astype(F32))
        o_ref[:, os_] = y.astype(o_ref.dtype)


def _gla_decode(rest, wa, ba, gg, s0, layer, batch, t_new, out_dtype):
    def rows(block, n_blocks=None):
        if n_blocks is None:
            return pl.BlockSpec((None, t_new, COL_BLOCK), lambda b: (block, b, 0))
        return pl.BlockSpec((n_blocks, t_new, COL_BLOCK), lambda b: (block // n_blocks, b, 0))

    def full(a):
        return pl.BlockSpec(a.shape, lambda b: (0, 0))

    v_blocks = GLA_VAL // COL_BLOCK
    return pl.pallas_call(
        _gla_decode_kernel,
        grid=(batch,),
        in_specs=[rows(R_QB), rows(R_KB), rows(R_VB, v_blocks), rows(R_RB, v_blocks),
                  pl.BlockSpec((None, t_new, LANE), lambda b: (R_GLR, b, 0)),
                  full(wa), full(ba), full(gg),
                  pl.BlockSpec((None, None, GLA_HEADS, GLA_DK, GLA_DV), lambda b: (layer, b, 0, 0, 0))],
        out_specs=[pl.BlockSpec((t_new, GLA_VAL), lambda b: (b, 0)),
                   pl.BlockSpec((None, GLA_HEADS, GLA_DK, GLA_DV), lambda b: (b, 0, 0, 0))],
        out_shape=[jax.ShapeDtypeStruct((batch * t_new, GLA_VAL), out_dtype),
                   jax.ShapeDtypeStruct((batch, GLA_HEADS, GLA_DK, GLA_DV), F32)],
        compiler_params=_cparams(("parallel",)),
        name="gla_decode",
    )(rest, rest, rest, rest, rest, wa, ba, gg, s0)


def _gla(rest, wa, ba, gg, batch, seq, chunk, n_chunks, out_dtype):
    n_rows = rest.shape[1]
    lblk = chunk * n_chunks
    steps = seq // lblk

    def pspec(width, block):
        per = COL_BLOCK // width
        return pl.BlockSpec((None, lblk, width), lambda b, h, n: (block + h // per, b * steps + n, h % per))

    in_specs = [
        pspec(GLA_DK, R_QB), pspec(GLA_DK, R_KB), pspec(GLA_DV, R_VB), pspec(GLA_DV, R_RB),
        pl.BlockSpec((None, lblk, LANE), lambda b, h, n: (R_GLR, b * steps + n, 0)),
        pl.BlockSpec((LANE, GLA_DK), lambda b, h, n: (0, h)),
        pl.BlockSpec((1, GLA_DK), lambda b, h, n: (0, h)),
        pl.BlockSpec((1, GLA_DV), lambda b, h, n: (0, h)),
    ]
    args = [rest, rest, rest, rest, rest, wa, ba, gg]
    return pl.pallas_call(
        functools.partial(_gla_kernel, chunk=chunk, n_chunks=n_chunks),
        grid=(batch, GLA_HEADS, steps),
        in_specs=in_specs,
        out_specs=[
            pl.BlockSpec((lblk, GLA_DV), lambda b, h, n: (b * steps + n, h)),
            pl.BlockSpec((None, None, GLA_DK, GLA_DV), lambda b, h, n: (b, h, 0, 0)),
        ],
        out_shape=[
            jax.ShapeDtypeStruct((n_rows, GLA_VAL), out_dtype),
            jax.ShapeDtypeStruct((batch, GLA_HEADS, GLA_DK, GLA_DV), F32),
        ],
        scratch_shapes=[pltpu.VMEM((GLA_DK, GLA_DV), F32)],
        compiler_params=_cparams(("parallel", "parallel", "arbitrary")),
        name="gla",
    )(*args)


def _merge_kernel(x_ref, o1_ref, o2_ref, o3_ref, m1_ref, m2_ref, m3_ref, l1_ref, l2_ref, l3_ref, ob_ref, ga_ref, gb_ref,
                  wpa_ref, wpb_ref, wo_ref, e_ref, out_ref, o_scr, l_scr, *, dils):
    tm = x_ref.shape[0]

    def token_order(ref, scr, dil):
        if dil == 1:
            return ref[0].astype(F32)
        chunks = ref.shape[2] // LANE
        for r in range(dil):
            for c in range(chunks):
                scr[c, pl.ds(r, tm // dil, stride=dil), :] = ref[r, :, c * LANE:(c + 1) * LANE].astype(F32)
        return jnp.concatenate([scr[c] for c in range(chunks)], axis=1)

    ms = [token_order(m_ref, l_scr, d) for m_ref, d in zip((m1_ref, m2_ref, m3_ref), dils)]
    ls = [token_order(l_ref, l_scr, d) for l_ref, d in zip((l1_ref, l2_ref, l3_ref), dils)]
    m = jnp.maximum(jnp.maximum(ms[0], ms[1]), ms[2])
    ws = [jnp.exp2(mg - m) for mg in ms]
    inv = 1.0 / (ws[0] * ls[0] + ws[1] * ls[1] + ws[2] * ls[2])
    expand = e_ref[...]
    o_a = jnp.zeros((tm, A_OUT), F32)
    for w, o_ref, d in zip(ws, (o1_ref, o2_ref, o3_ref), dils):
        o_a = o_a + _split_dot(w * inv, expand) * token_order(o_ref, o_scr, d)
    ya = _dot(o_a.astype(BF16), wpa_ref[...])
    yb = _dot(ob_ref[...].astype(BF16), wpb_ref[...])
    gate_a = jnp.concatenate([ga_ref[0], ga_ref[1]], axis=1).astype(F32)
    gate_b = jnp.concatenate([gb_ref[0], gb_ref[1]], axis=1).astype(F32)
    merged = jax.nn.sigmoid(gate_a) * ya + jax.nn.sigmoid(gate_b) * yb
    out_ref[...] = x_ref[...] + _dot(merged.astype(BF16), wo_ref[...])


def _merge(x, outs, maxes, dens, ob, rest, wpa, wpb, wo, expand, seq, tm):
    n = x.shape[0]
    tiles = seq // tm
    dils = tuple(o.shape[1] for o in outs)

    def rows(width):
        return pl.BlockSpec((tm, width), lambda i: (i, 0))

    def dilated(a):
        dil, width = a.shape[1], a.shape[3]
        return pl.BlockSpec((None, dil, tm // dil, width), lambda i: (i // tiles, 0, i % tiles, 0))

    def gate(block):
        return pl.BlockSpec((2, tm, COL_BLOCK), lambda i: (block // 2, i, 0))

    def full(a):
        return pl.BlockSpec(a.shape, lambda i: (0, 0))

    return pl.pallas_call(
        functools.partial(_merge_kernel, dils=dils),
        grid=(n // tm,),
        in_specs=[rows(D_MODEL)] + [dilated(a) for a in (*outs, *maxes, *dens)] + [rows(GLA_VAL)]
                 + [gate(R_GA), gate(R_GB)] + [full(wpa), full(wpb), full(wo), full(expand)],
        out_specs=rows(D_MODEL),
        out_shape=jax.ShapeDtypeStruct((n, D_MODEL), F32),
        scratch_shapes=[pltpu.VMEM((A_OUT // LANE, tm, LANE), F32), pltpu.VMEM((1, tm, LANE), F32)],
        compiler_params=_cparams(("parallel",)),
        name="merge",
    )(x, *outs, *maxes, *dens, ob, rest, rest, wpa, wpb, wo, expand)


def _ffn_kernel(x_ref, g_ref, gn_ref, wg_ref, wu_ref, wd_ref, o_ref, hn_ref):
    tm = x_ref.shape[0]
    chunk = min(tm, ROW_CHUNK)
    for rc in range(tm // chunk):
        rows = slice(rc * chunk, (rc + 1) * chunk)
        x = x_ref[rows, :]
        h = _rms(x, g_ref[...]).astype(BF16)
        mid = jax.nn.silu(_dot(h, wg_ref[...])) * _dot(h, wu_ref[...])
        y = x + _dot(mid.astype(BF16), wd_ref[...])
        o_ref[rows, :] = y
        hn_ref[rows, :] = _rms(y, gn_ref[...]).astype(hn_ref.dtype)


def _ffn(x, g, g_next, wg, wu, wd, tm):
    n = x.shape[0]

    def resident(a):
        return pl.BlockSpec(a.shape, lambda i: (0, 0), pipeline_mode=pl.Buffered(1))

    def rows():
        return pl.BlockSpec((tm, D_MODEL), lambda i: (i, 0))

    return pl.pallas_call(
        _ffn_kernel,
        grid=(n // tm,),
        in_specs=[rows(), resident(g), resident(g_next), resident(wg), resident(wu), resident(wd)],
        out_specs=[rows(), rows()],
        out_shape=[jax.ShapeDtypeStruct((n, D_MODEL), F32), jax.ShapeDtypeStruct((n, D_MODEL), BF16)],
        compiler_params=_cparams(("parallel",)),
        name="ffn",
    )(x, g, g_next, wg, wu, wd)


def _router_kernel(x_ref, g_ref, wr_ref, comb_ref, scol_ref, srow_ref, cnt_ref):
    h = _rms(x_ref[...], g_ref[...])
    w = wr_ref[...]
    h_hi, w_hi = h.astype(BF16), w.astype(BF16)
    h_lo, w_lo = (h - h_hi.astype(F32)).astype(BF16), (w - w_hi.astype(F32)).astype(BF16)
    logits = _dot(h_hi, w_hi) + (_dot(h_hi, w_lo) + _dot(h_lo, w_hi))
    lane = lax.broadcasted_iota(jnp.int32, logits.shape, 1)
    logits = jnp.where(lane < N_EXPERTS, logits, -jnp.inf)
    v1 = jnp.max(logits, axis=-1, keepdims=True)
    i1 = jnp.min(jnp.where(logits == v1, lane, LANE), axis=-1, keepdims=True)
    rest = jnp.where(lane == i1, -jnp.inf, logits)
    v2 = jnp.max(rest, axis=-1, keepdims=True)
    i2 = jnp.min(jnp.where(rest == v2, lane, LANE), axis=-1, keepdims=True)
    e2 = jnp.exp(v2 - v1)
    g1 = 1.0 / (1.0 + e2)
    g2 = e2 / (1.0 + e2)
    comb = jnp.where(lane == i1, g1, 0.0) + jnp.where(lane == i2, g2, 0.0)
    comb_ref[...] = comb

    tile = comb.shape[0]
    routed = comb > 0.0
    earlier = lax.broadcasted_iota(jnp.int32, (tile, tile), 1) < lax.broadcasted_iota(jnp.int32, (tile, tile), 0)
    rank = _dot(earlier.astype(BF16), routed.astype(BF16))
    slot = jnp.where(routed, rank, -1.0)
    scol_ref[...] = slot
    srow_ref[...] = slot.T[0:N_EXPERTS, :]
    count = jnp.sum(routed.astype(F32), axis=0, keepdims=True)
    cnt_ref[...] = jnp.broadcast_to(count, cnt_ref.shape).astype(jnp.int32)


def _router(x, g, wr, tile):
    n = x.shape[0]
    tiles = n // tile
    comb, slot_col, slot_row, counts = pl.pallas_call(
        _router_kernel,
        grid=(tiles,),
        in_specs=[
            pl.BlockSpec((tile, D_MODEL), lambda i: (i, 0)),
            pl.BlockSpec((1, D_MODEL), lambda i: (0, 0)),
            pl.BlockSpec((D_MODEL, LANE), lambda i: (0, 0)),
        ],
        out_specs=[
            pl.BlockSpec((tile, LANE), lambda i: (i, 0)),
            pl.BlockSpec((tile, LANE), lambda i: (i, 0)),
            pl.BlockSpec((None, N_EXPERTS, tile), lambda i: (i, 0, 0)),
            pl.BlockSpec((None, 8, LANE), lambda i: (i, 0, 0)),
        ],
        out_shape=[
            jax.ShapeDtypeStruct((n, LANE), F32),
            jax.ShapeDtypeStruct((n, LANE), F32),
            jax.ShapeDtypeStruct((tiles, N_EXPERTS, tile), F32),
            jax.ShapeDtypeStruct((tiles, 8, LANE), jnp.int32),
        ],
        compiler_params=_cparams(("parallel",)),
        name="router",
    )(x, g, wr)
    return (comb, counts[:, 0, :N_EXPERTS].reshape(tiles * N_EXPERTS), slot_col,
            slot_row.reshape(tiles * N_EXPERTS, 1, tile))


def _moe_kernel(cnt_ref, x_ref, g_ref, gn_ref, comb_ref, scol_ref, srow_ref, wg_ref, wu_ref, wd_ref, o_ref,
                h_ref, xc_ref, y_ref, *, sub, n_sub):
    i, e, f = pl.program_id(0), pl.program_id(1), pl.program_id(2)
    last_f = pl.num_programs(2) - 1
    tile = x_ref.shape[0]
    sub_pad = y_ref.shape[1]
    count = cnt_ref[i * N_EXPERTS + e]

    @pl.when((e == 0) & (f == 0))
    def _():
        h_ref[...] = _rms(x_ref[...], g_ref[...]).astype(BF16)
        o_ref[...] = x_ref[...]

    for s in range(n_sub):
        @pl.when((f == 0) & (s * sub < count))
        def _():
            slots = (lax.broadcasted_iota(jnp.int32, (sub, tile), 0) + s * sub).astype(F32)
            pick = (srow_ref[...] == slots).astype(BF16)
            xc_ref[s] = _dot(pick, h_ref[...]).astype(BF16)
            y_ref[s] = jnp.zeros((sub_pad, D_MODEL), F32)

        @pl.when(s * sub < count)
        def _():
            xs = xc_ref[s]
            mid = jax.nn.silu(_dot(xs, wg_ref[...])) * _dot(xs, wu_ref[...])
            y_ref[s, 0:sub, :] += _dot(mid.astype(BF16), wd_ref[...])

        @pl.when((f == last_f) & (s * sub < count))
        def _():
            mine = lax.broadcasted_iota(jnp.int32, (tile, LANE), 1) == e
            slot = jnp.sum(jnp.where(mine, scol_ref[...], 0.0), axis=-1, keepdims=True)
            gate = jnp.sum(jnp.where(mine, comb_ref[...], 0.0), axis=-1, keepdims=True)
            slots = (lax.broadcasted_iota(jnp.int32, (tile, sub_pad), 1) + s * sub).astype(F32)
            place = (slot == slots).astype(BF16)
            o_ref[...] += gate * _dot(place, y_ref[s].astype(BF16))

    @pl.when((e == pl.num_programs(1) - 1) & (f == last_f))
    def _():
        o_ref[...] = _rms(o_ref[...], gn_ref[...])


def _moe(x, g, g_out, comb, counts, slot_col, slot_row, wg, wu, wd, sub, tf):
    n = x.shape[0]
    tile = slot_row.shape[-1]
    n_sub = -(-tile // sub)
    sub_pad = -(-sub // LANE) * LANE
    grid_spec = pltpu.PrefetchScalarGridSpec(
        num_scalar_prefetch=1,
        grid=(n // tile, N_EXPERTS, D_FF_EXPERT // tf),
        in_specs=[
            pl.BlockSpec((tile, D_MODEL), lambda i, e, f, c: (i, 0)),
            pl.BlockSpec((1, D_MODEL), lambda i, e, f, c: (0, 0)),
            pl.BlockSpec((1, D_MODEL), lambda i, e, f, c: (0, 0)),
            pl.BlockSpec((tile, LANE), lambda i, e, f, c: (i, 0)),
            pl.BlockSpec((tile, LANE), lambda i, e, f, c: (i, 0)),
            pl.BlockSpec((None, 1, tile), lambda i, e, f, c: (i * N_EXPERTS + e, 0, 0)),
            pl.BlockSpec((None, D_MODEL, tf), lambda i, e, f, c: (e, 0, f)),
            pl.BlockSpec((None, D_MODEL, tf), lambda i, e, f, c: (e, 0, f)),
            pl.BlockSpec((None, tf, D_MODEL), lambda i, e, f, c: (e, f, 0)),
        ],
        out_specs=pl.BlockSpec((tile, D_MODEL), lambda i, e, f, c: (i, 0)),
        scratch_shapes=[
            pltpu.VMEM((tile, D_MODEL), BF16),
            pltpu.VMEM((n_sub, sub, D_MODEL), BF16),
            pltpu.VMEM((n_sub, sub_pad, D_MODEL), F32),
        ],
    )
    return pl.pallas_call(
        functools.partial(_moe_kernel, sub=sub, n_sub=n_sub),
        grid_spec=grid_spec,
        out_shape=jax.ShapeDtypeStruct((n, D_MODEL), F32),
        compiler_params=_cparams(("parallel", "arbitrary", "arbitrary")),
        name="moe",
    )(counts, x, g, g_out, comb, slot_col, slot_row, wg, wu, wd)


def _norm_kernel(x_ref, g_ref, o_ref):
    o_ref[...] = _rms(x_ref[...], g_ref[...]).astype(o_ref.dtype)


def _norm(x, g, tm, out_dtype):
    n = x.shape[0]
    return pl.pallas_call(
        _norm_kernel,
        grid=(n // tm,),
        in_specs=[pl.BlockSpec((tm, D_MODEL), lambda i: (i, 0)), pl.BlockSpec((1, D_MODEL), lambda i: (0, 0))],
        out_specs=pl.BlockSpec((tm, D_MODEL), lambda i: (i, 0)),
        out_shape=jax.ShapeDtypeStruct((n, D_MODEL), out_dtype),
        compiler_params=_cparams(("parallel",)),
        name="norm",
    )(x, g)


def _rope_tables(pos, split_half):
    half = HEAD_DIM // 2
    inv = ROPE_THETA ** (-jnp.arange(half, dtype=F32) / half)
    ang = pos.astype(F32)[:, None] * inv[None, :]
    cos = jnp.cos(ang)
    sin = jnp.sin(ang)
    if split_half:
        return jnp.tile(cos, (1, LANE // half)), jnp.tile(sin, (1, LANE // half))
    return (jnp.tile(jnp.concatenate([cos, cos], axis=-1), (1, LANE // HEAD_DIM)),
            jnp.tile(jnp.concatenate([-sin, sin], axis=-1), (1, LANE // HEAD_DIM)))


def _to_split_half(a):
    lead = a.shape[:-1]
    a = a.reshape(*lead, HEADS_PER_GROUP // QUAD, QUAD, 2, HEAD_DIM // 2)
    return jnp.swapaxes(a, -2, -3).reshape(*lead, A_OUT)


def _layer_weights(l, w_in, w_gate_a2, b_gate_a, g_gla, w_branch_a, w_branch_b, w_out):
    w = w_in[l]
    offs = np.cumsum((A_WIDTH, A_WIDTH, A_WIDTH, GLA_KEY, GLA_KEY, GLA_VAL, GLA_VAL, GATE_RANK, D_MODEL, D_MODEL))
    qa, ka, va, qb, kb, vb, rb, glr, ga, gb = jnp.split(w, [int(o) for o in offs[:-1]], axis=1)
    def attn_weights(split_half):
        relay = _to_split_half if split_half else (lambda a: a)
        return jnp.stack([jnp.stack([relay(qa[:, g * A_OUT:(g + 1) * A_OUT]), relay(ka[:, g * A_OUT:(g + 1) * A_OUT]),
                                     va[:, g * A_OUT:(g + 1) * A_OUT]]) for g in range(N_GROUPS)]).astype(BF16)

    w_attn = {False: attn_weights(False), True: attn_weights(True)}
    glr = jnp.pad(glr, ((0, 0), (0, COL_BLOCK - GATE_RANK)))
    rest = jnp.concatenate([qb, kb, vb, rb, ga, gb, glr], axis=1)
    w_rest = jnp.swapaxes(rest.reshape(D_MODEL, N_REST, COL_BLOCK), 0, 1).astype(BF16)
    wa = jnp.pad(w_gate_a2[l], ((0, LANE - GATE_RANK), (0, 0))).astype(BF16)
    return dict(w_attn=w_attn, w_rest=w_rest, wa=wa, ba=b_gate_a[l][None, :], gg=g_gla[l][None, :],
                wpa=w_branch_a[l].astype(BF16), wpb=w_branch_b[l].astype(BF16), wo=w_out[l].astype(BF16))


def _head_expand():
    e = np.zeros((LANE, A_OUT), np.float32)
    for h in range(HEADS_PER_GROUP):
        e[h, h * HEAD_DIM:(h + 1) * HEAD_DIM] = 1.0
    return jnp.asarray(e, BF16)


def _trunk(x, rope, caches, gla_state, lw, ffw, g_mix, g_ffn, g_final, batch, seq, tm, tm_proj, moe_tile, moe_sub,
           p_dtype):
    cos_t, sin_t = rope
    expand = _head_expand()
    prompt = caches is None
    cache_out = [None] * N_GROUPS
    states_out = []
    assert DEPTH % 2 == 0
    h = _norm(x, g_mix[0][None, :], tm, BF16)
    for l in range(DEPTH):
        w = lw[l]
        rest = _inproj_rest(h, w["w_rest"], tm_proj, p_dtype)
        outs, maxes, dens = [], [], []
        for g, (window, dil) in enumerate(DSWA_GROUPS):
            if prompt:
                qkv = _inproj_attn(h, w["w_attn"][True], cos_t, sin_t, g, batch, seq, dil, tm_proj, p_dtype, True)
                o, mx, den = _attn_prompt(qkv, g)
                cache_out[g] = _kv_tail(qkv, cache_out[g], l, min(window, seq))
            else:
                qkv = _inproj_attn(h, w["w_attn"][False], cos_t, sin_t, g, 1, batch * seq, 1, tm_proj, p_dtype, False)
                o, lse, cache_out[g] = _attn_sample(qkv.reshape(3, batch * seq, COL_BLOCK), caches[g], cache_out[g],
                                                    l, g, seq)
                o = o.reshape(1, 1, batch * seq, A_OUT)
                mx = lse.reshape(1, 1, batch * seq, LANE)
                den = jnp.ones_like(mx)
            outs.append(o)
            maxes.append(mx)
            dens.append(den)
        if prompt:
            ob, s_new = _gla(rest, w["wa"], w["ba"], w["gg"], batch, seq, GLA_CHUNK, GLA_CHUNKS_PER_STEP, BF16)
            x = _merge(x, outs, maxes, dens, ob, rest, w["wpa"], w["wpb"], w["wo"], expand, seq, min(tm, 512))
        else:
            ob, s_new = _gla_decode(rest, w["wa"], w["ba"], w["gg"], gla_state, l, batch, seq, F32)
            x = _merge(x, outs, maxes, dens, ob, rest, w["wpa"], w["wpb"], w["wo"], expand, batch * seq, tm)
        states_out.append(s_new)
        i = l // 2
        if l % 2 == 0:
            x, h = _ffn(x, g_ffn[l][None, :], g_mix[l + 1][None, :], ffw["wg"][i], ffw["wu"][i], ffw["wd"][i], tm)
        else:
            routing = _router(x, g_ffn[l][None, :], ffw["wr"][i], moe_tile)
            g_out = g_final if l == DEPTH - 1 else g_mix[l + 1]
            x = _moe(x, g_ffn[l][None, :], g_out[None, :], *routing, ffw["eg"][i], ffw["eu"][i], ffw["ed"][i],
                     moe_sub, D_FF_EXPERT // 2)
            assert l == DEPTH - 1
    y = x
    bufs = [jnp.transpose(c.reshape(DEPTH, batch, 2, HEADS_PER_GROUP, HEAD_DIM, c.shape[-1]), (0, 1, 5, 2, 3, 4))
            for c in cache_out]
    return (y, bufs[0], bufs[1], bufs[2], jnp.stack(states_out))


def kernel(x_prompt, x_sample, cache_kv_w128, cache_kv_w512, cache_kv_w2048, state_gla, g_mix, w_in, w_gate_a2, b_gate_a, g_gla, w_branch_a, w_branch_b, w_out, g_ffn, w_ff_gate, w_ff_up, w_ff_down, w_router, w_exp_gate, w_exp_up, w_exp_down, g_final):
    batch, seq, _ = x_prompt.shape
    dec_batch, dec_seq, _ = x_sample.shape
    assert seq % (DSWA_GROUPS[-1][1] * Q_BLOCK) == 0 and seq % (GLA_CHUNKS_PER_STEP * GLA_CHUNK) == 0
    assert dec_seq % GLA_CHUNK != 0 and dec_seq % 8 == 0
    assert all(c.shape[2] == w for c, (w, _) in zip((cache_kv_w128, cache_kv_w512, cache_kv_w2048), DSWA_GROUPS))

    lw = [_layer_weights(l, w_in, w_gate_a2, b_gate_a, g_gla, w_branch_a, w_branch_b, w_out) for l in range(DEPTH)]
    ffw = dict(wg=w_ff_gate.astype(BF16), wu=w_ff_up.astype(BF16), wd=w_ff_down.astype(BF16),
               wr=jnp.pad(w_router, ((0, 0), (0, 0), (0, LANE - N_EXPERTS))),
               eg=w_exp_gate.astype(BF16), eu=w_exp_up.astype(BF16), ed=w_exp_down.astype(BF16))

    y_p, kv128_p, kv512_p, kv2048_p, gla_p = _trunk(
        x_prompt.reshape(batch * seq, D_MODEL), _rope_tables(jnp.arange(seq, dtype=jnp.int32), True), None, None,
        lw, ffw, g_mix, g_ffn, g_final, batch, seq, 1024, 2048, 1024, 288, BF16)

    n_s = dec_batch * dec_seq
    pos_s = jnp.tile(PAST_LEN + jnp.arange(dec_seq, dtype=jnp.int32), dec_batch)
    caches = [jnp.transpose(c, (0, 1, 3, 4, 5, 2)).reshape(DEPTH, dec_batch, 2, A_OUT, c.shape[2])
              for c in (cache_kv_w128, cache_kv_w512, cache_kv_w2048)]
    y_s, kv128_s, kv512_s, kv2048_s, gla_s = _trunk(
        x_sample.reshape(n_s, D_MODEL), _rope_tables(pos_s, False), caches, state_gla,
        lw, ffw, g_mix, g_ffn, g_final, dec_batch, dec_seq, n_s, n_s, n_s, 96, F32)

    return (y_p.reshape(batch, seq, D_MODEL), y_s.reshape(dec_batch, dec_seq, D_MODEL),
            kv128_p, kv512_p, kv2048_p, gla_p, kv128_s, kv512_s, kv2048_s, gla_s)
```

```python
import functools

import jax
import jax.numpy as jnp
import numpy as np
from jax import lax
from jax.experimental import pallas as pl
from jax.experimental.pallas import tpu as pltpu

F32 = jnp.float32
BF16 = jnp.bfloat16

D_MODEL = 1024
DEPTH = 2
PAST_LEN = 16384
HEAD_DIM = 64
DSWA_GROUPS = ((128, 1), (512, 4), (2048, 16))
N_GROUPS = 3
HEADS_PER_GROUP = 8
A_WIDTH = N_GROUPS * HEADS_PER_GROUP * HEAD_DIM
A_OUT = HEADS_PER_GROUP * HEAD_DIM
Q_BLOCK = 128
GLA_HEADS = 4
GLA_KEY = 512
GLA_VAL = 1024
GLA_DK = 128
GLA_DV = 256
GATE_RANK = 16
GATE_TAU = 16.0
GLA_CHUNK = 64
D_FF = 2816
N_EXPERTS = 8
D_FF_EXPERT = 3584
ROPE_THETA = 10000.0
EPS = 1e-6
LOG2_E = 1.4426950408889634

LANE = 128
COL_BLOCK = 512
R_QB, R_KB, R_VB, R_RB, R_GA, R_GB, R_GLR, N_REST = 0, 1, 2, 4, 6, 8, 10, 11
ROW_CHUNK = 256
GLA_CHUNKS_PER_STEP = 16
VMEM_LIMIT = 56 * 1024 * 1024


def _cparams(sem):
    return pltpu.CompilerParams(dimension_semantics=sem, vmem_limit_bytes=VMEM_LIMIT)


def _rms(xf, g):
    return xf * lax.rsqrt(jnp.mean(xf * xf, axis=-1, keepdims=True) + EPS) * g


def _dot(a, b):
    return jnp.dot(a, b, preferred_element_type=F32)


def _dot_nt(a, b):
    return lax.dot_general(a, b, (((1,), (1,)), ((), ())), preferred_element_type=F32)


def _dot_tn(a, b):
    return lax.dot_general(a, b, (((0,), (0,)), ((), ())), preferred_element_type=F32)


def _div_pow2(x, d):
    assert d & (d - 1) == 0
    return lax.shift_right_logical(x, int(d).bit_length() - 1)


def _mod_pow2(x, d):
    assert d & (d - 1) == 0
    return x & (d - 1)


def _split_dot(a_f32, b_twice):
    hi = a_f32.astype(BF16)
    lo = (a_f32 - hi.astype(F32)).astype(BF16)
    return _dot(jnp.concatenate([hi, lo], axis=1), b_twice)


def _inproj_attn_kernel(h_ref, w_ref, cos_ref, sin_ref, o_ref, *scratch, dil, split_half, q_scale):
    part = pl.program_id(1)
    scale = jnp.where(part == 0, q_scale, 1.0).astype(F32)
    tm = h_ref.shape[0]
    chunk = min(tm, ROW_CHUNK)
    n_lane_groups = COL_BLOCK // LANE
    for rc in range(tm // chunk):
        rows = slice(rc * chunk, (rc + 1) * chunk)
        acc = _dot(h_ref[rows, :], w_ref[...])
        cos = jnp.where(part == 2, 1.0, cos_ref[rows, :] * scale)
        sin = jnp.where(part == 2, 0.0, sin_ref[rows, :] * scale)
        ys = []
        if split_half:
            for c in range(0, n_lane_groups, 2):
                a = acc[:, c * LANE:(c + 1) * LANE]
                b = acc[:, (c + 1) * LANE:(c + 2) * LANE]
                ys += [a * cos - b * sin, a * sin + b * cos]
        else:
            first_half = _mod_pow2(lax.broadcasted_iota(jnp.int32, cos.shape, 1), HEAD_DIM) < HEAD_DIM // 2
            for c in range(n_lane_groups):
                xs = acc[:, c * LANE:(c + 1) * LANE]
                swapped = jnp.where(first_half, pltpu.roll(xs, LANE - HEAD_DIM // 2, 1), pltpu.roll(xs, HEAD_DIM // 2, 1))
                ys.append(xs * cos + swapped * sin)
        if dil == 1:
            for c, y in enumerate(ys):
                o_ref[0, rows, c * LANE:(c + 1) * LANE] = y.astype(o_ref.dtype)
        else:
            sub = chunk // dil
            pitch = _residue_pitch(dil)
            base = rc * sub * pitch
            for c, y in enumerate(ys):
                if pitch == dil:
                    scratch[0][c, rows, :] = y
                else:
                    for u in range(sub):
                        scratch[0][c, base + u * pitch:base + u * pitch + dil, :] = y[u * dil:(u + 1) * dil, :]
            for r in range(dil):
                for c in range(n_lane_groups):
                    o_ref[r, rc * sub:(rc + 1) * sub, c * LANE:(c + 1) * LANE] = (
                        scratch[0][c, pl.ds(base + r, sub, stride=pitch), :].astype(o_ref.dtype))


def _residue_pitch(dil):
    return dil + 8 if dil % 16 == 0 else dil


def _inproj_attn(h, w, cos_t, sin_t, group, batch, seq, dil, tm, out_dtype, split_half):
    tiles = seq // tm
    scratch = [pltpu.VMEM((COL_BLOCK // LANE, tm // dil * _residue_pitch(dil), LANE), F32)] if dil > 1 else []
    q_scale = HEAD_DIM ** -0.5 * (LOG2_E if split_half else 1.0)
    return pl.pallas_call(
        functools.partial(_inproj_attn_kernel, dil=dil, split_half=split_half, q_scale=q_scale),
        grid=(batch * tiles, 3),
        in_specs=[
            pl.BlockSpec((tm, D_MODEL), lambda i, j: (i, 0)),
            pl.BlockSpec((None, None, D_MODEL, COL_BLOCK), lambda i, j: (group, j, 0, 0)),
            pl.BlockSpec((tm, LANE), lambda i, j: (i % tiles, 0)),
            pl.BlockSpec((tm, LANE), lambda i, j: (i % tiles, 0)),
        ],
        out_specs=pl.BlockSpec((None, None, dil, tm // dil, COL_BLOCK), lambda i, j: (j, i // tiles, 0, i % tiles, 0)),
        out_shape=jax.ShapeDtypeStruct((3, batch, dil, seq // dil, COL_BLOCK), out_dtype),
        scratch_shapes=scratch,
        compiler_params=_cparams(("parallel", "arbitrary")),
        name=f"inproj_attn_g{group}",
    )(h, w, cos_t, sin_t)


def _inproj_rest_kernel(h_ref, w_ref, o_ref):
    tm = h_ref.shape[0]
    chunk = min(tm, ROW_CHUNK)
    for rc in range(tm // chunk):
        rows = slice(rc * chunk, (rc + 1) * chunk)
        o_ref[rows, :] = _dot(h_ref[rows, :], w_ref[...]).astype(o_ref.dtype)


def _inproj_rest(h, w, tm, out_dtype):
    n = h.shape[0]
    return pl.pallas_call(
        _inproj_rest_kernel,
        grid=(n // tm, N_REST),
        in_specs=[
            pl.BlockSpec((tm, D_MODEL), lambda i, j: (i, 0)),
            pl.BlockSpec((None, D_MODEL, COL_BLOCK), lambda i, j: (j, 0, 0)),
        ],
        out_specs=pl.BlockSpec((None, tm, COL_BLOCK), lambda i, j: (j, i, 0)),
        out_shape=jax.ShapeDtypeStruct((N_REST, n, COL_BLOCK), out_dtype),
        compiler_params=_cparams(("parallel", "arbitrary")),
        name="inproj_rest",
    )(h, w)


QUAD = 4
QUAD_WIDTH = QUAD * HEAD_DIM
MAX_Q_BLOCKS_PER_STEP = 8


def _attn_prompt_kernel(bias_ref, *refs):
    n = pl.program_id(2)
    for r in range(refs[0].shape[0]):
        _attn_prompt_subsequence(n, bias_ref, *(ref.at[r] for ref in refs))


def _attn_prompt_subsequence(n, bias_ref, q_ref, kp_ref, kc_ref, vp_ref, vc_ref, o_ref, m_ref, l_ref):
    rows = QUAD * Q_BLOCK
    row_head = _div_pow2(lax.broadcasted_iota(jnp.int32, (rows, QUAD_WIDTH), 0), Q_BLOCK)
    col = lax.broadcasted_iota(jnp.int32, (rows, QUAD_WIDTH), 1)
    q_lanes = _div_pow2(_mod_pow2(col, LANE), HEAD_DIM // 2) == row_head
    v_lanes = _div_pow2(col, HEAD_DIM) == row_head
    lane = lax.broadcasted_iota(jnp.int32, (Q_BLOCK, LANE), 1)
    for qb in range(q_ref.shape[0] // Q_BLOCK):
        cur = slice(qb * Q_BLOCK, (qb + 1) * Q_BLOCK)
        if qb == 0:
            bias = bias_ref[jnp.where(n == 0, 1, 0)]
        else:
            bias = bias_ref[0]
            prev = slice((qb - 1) * Q_BLOCK, qb * Q_BLOCK)
        m_all = jnp.zeros((Q_BLOCK, LANE), F32)
        l_all = jnp.ones((Q_BLOCK, LANE), F32)
        for c in range(A_OUT // QUAD_WIDTH):
            cs = slice(c * QUAD_WIDTH, (c + 1) * QUAD_WIDTH)
            q = q_ref[cur, cs].astype(F32)
            q4 = jnp.where(q_lanes, jnp.concatenate([q] * QUAD, axis=0), 0.0).astype(BF16)
            k_prev = kp_ref[:, cs] if qb == 0 else kc_ref[prev, cs]
            v_prev = vp_ref[:, cs] if qb == 0 else vc_ref[prev, cs]
            k = jnp.concatenate([k_prev, kc_ref[cur, cs]], axis=0)
            v = jnp.concatenate([v_prev, vc_ref[cur, cs]], axis=0)
            s = _dot_nt(q4, k) + bias
            m = jnp.max(s, axis=-1, keepdims=True)
            p = jnp.exp2(s - m)
            l = jnp.sum(p, axis=-1, keepdims=True)
            o4 = jnp.where(v_lanes, _dot(p.astype(BF16), v), 0.0)
            o = o4[0:Q_BLOCK]
            for j in range(QUAD):
                hs = slice(j * Q_BLOCK, (j + 1) * Q_BLOCK)
                if j > 0:
                    o = o + o4[hs]
                m_all = jnp.where(lane == c * QUAD + j, m[hs], m_all)
                l_all = jnp.where(lane == c * QUAD + j, l[hs], l_all)
            o_ref[cur, cs] = o.astype(o_ref.dtype)
        m_ref[cur, :] = m_all
        l_ref[cur, :] = l_all


def _band_bias(span):
    qi = np.arange(QUAD * Q_BLOCK)[:, None] % Q_BLOCK
    kj = np.arange(2 * Q_BLOCK)[None, :]
    dist = Q_BLOCK + qi - kj
    band = (dist >= 0) & (dist <= span)
    first = band & (kj >= Q_BLOCK)
    return jnp.asarray(np.where(np.stack([band, first]), 0.0, -np.inf), F32)


def _attn_prompt(qkv, group):
    window, dil = DSWA_GROUPS[group]
    _, batch, _, sub_len, _ = qkv.shape
    q_blocks = min(MAX_Q_BLOCKS_PER_STEP, sub_len // Q_BLOCK)
    step_rows = q_blocks * Q_BLOCK
    nb = sub_len // step_rows
    res = min(dil, MAX_Q_BLOCKS_PER_STEP // q_blocks)

    def spec(part, prev):
        if prev:
            return pl.BlockSpec((None, None, res, Q_BLOCK, COL_BLOCK),
                                lambda b, r, n: (part, b, r, jnp.maximum(n * q_blocks - 1, 0), 0))
        return pl.BlockSpec((None, None, res, step_rows, COL_BLOCK), lambda b, r, n: (part, b, r, n, 0))

    bias = _band_bias(window // dil)
    stat_spec = pl.BlockSpec((None, res, step_rows, LANE), lambda b, r, n: (b, r, n, 0))
    stat_shape = jax.ShapeDtypeStruct((batch, dil, sub_len, LANE), F32)
    return pl.pallas_call(
        _attn_prompt_kernel,
        grid=(batch, dil // res, nb),
        in_specs=[pl.BlockSpec(bias.shape, lambda b, r, n: (0, 0, 0)),
                  spec(0, False), spec(1, True), spec(1, False), spec(2, True), spec(2, False)],
        out_specs=[pl.BlockSpec((None, res, step_rows, A_OUT), lambda b, r, n: (b, r, n, 0)), stat_spec, stat_spec],
        out_shape=[jax.ShapeDtypeStruct((batch, dil, sub_len, A_OUT), BF16), stat_shape, stat_shape],
        compiler_params=_cparams(("parallel", "parallel", "arbitrary")),
        name=f"attn_prompt_g{group}",
    )(bias, qkv, qkv, qkv, qkv, qkv)


def _kv_tail_kernel(*refs, dil, aliased):
    if aliased:
        k_ref, v_ref, _, out_ref, scr = refs
        _kv_tail_body(k_ref, v_ref, out_ref, scr, dil)
        return
    k_ref, v_ref, out_ref, scr = refs

    @pl.when(pl.program_id(0) == 0)
    def _():
        _kv_tail_body(k_ref, v_ref, out_ref, scr, dil)

    @pl.when(pl.program_id(0) > 0)
    def _():
        out_ref[...] = jnp.zeros_like(out_ref)


def _kv_tail_body(k_ref, v_ref, out_ref, scr, dil):
    keep = out_ref.shape[2]
    sub = keep // dil
    half_w = HEAD_DIM // 2
    for kv, ref in enumerate((k_ref, v_ref)):
        for c in range(COL_BLOCK // LANE):
            cs = slice(c * LANE, (c + 1) * LANE)
            if dil == 1:
                tok = ref[0, :, cs].astype(F32)
            else:
                for r in range(dil):
                    scr[pl.ds(r, sub, stride=dil), :] = ref[r, :, cs].astype(F32)
                tok = scr[...]
            chan = tok.T
            if kv == 1:
                out_ref[kv, cs, :] = chan
            else:
                for j in range(QUAD):
                    dst = (QUAD * (c // 2) + j) * HEAD_DIM + (c % 2) * half_w
                    out_ref[kv, dst:dst + half_w, :] = chan[j * half_w:(j + 1) * half_w, :]


def _kv_tail(qkv, prev_out, layer, keep):
    _, batch, dil, sub_len, _ = qkv.shape
    last = sub_len // (keep // dil) - 1
    first = prev_out is None
    assert first == (layer == 0)
    phases = DEPTH if first else 1

    def spec(part):
        return pl.BlockSpec((None, None, dil, keep // dil, COL_BLOCK),
                            lambda p, b: (part, jnp.where(p == 0, b, batch - 1), 0, last, 0))

    in_specs = [spec(1), spec(2)]
    args = [qkv, qkv]
    aliases = {}
    if not first:
        in_specs.append(pl.BlockSpec(memory_space=pl.ANY))
        args.append(prev_out)
        aliases = {2: 0}
    return pl.pallas_call(
        functools.partial(_kv_tail_kernel, dil=dil, aliased=not first),
        grid=(phases, batch),
        in_specs=in_specs,
        out_specs=pl.BlockSpec((None, None, 2, A_OUT, keep), lambda p, b: (layer + p, b, 0, 0, 0)),
        out_shape=jax.ShapeDtypeStruct((DEPTH, batch, 2, A_OUT, keep), F32),
        scratch_shapes=[pltpu.VMEM((keep, LANE), F32)],
        input_output_aliases=aliases,
        compiler_params=_cparams(("arbitrary", "arbitrary")),
        name=f"kv_tail_{keep}",
    )(*args)


SHIFT_ROWS = 128
SAMPLE_BLOCK_BYTES = 8 * 1024 * 1024


def _attn_sample_kernel(*refs, dil, span, first):
    if first:
        q_ref, k_ref, v_ref, c_ref, o_ref, lse_ref, cout_ref = refs
    else:
        q_ref, k_ref, v_ref, c_ref, _, o_ref, lse_ref, cout_ref = refs
    n_rows = c_ref.shape[0]
    t_new = q_ref.shape[0] // n_rows

    def update(j):
        tok = pl.ds(j * t_new, t_new)
        q, k, v = q_ref.at[tok], k_ref.at[tok], v_ref.at[tok]
        _sample_attention(q, k, v, c_ref.at[j], o_ref.at[tok], lse_ref.at[tok], dil, span)
        _append_new(c_ref.at[j], k, v, cout_ref.at[j])

    def shift_only(j):
        for kv in range(2):
            for rb in range(A_OUT // SHIFT_ROWS):
                rs = slice(rb * SHIFT_ROWS, (rb + 1) * SHIFT_ROWS)
                cout_ref[j, kv, rs, :] = pltpu.roll(c_ref[j, kv, rs, :], c_ref.shape[3] - t_new, 1)

    if first:
        @pl.when(pl.program_id(0) == 0)
        def _():
            for j in range(n_rows):
                update(j)

        @pl.when(pl.program_id(0) > 0)
        def _():
            for j in range(n_rows):
                shift_only(j)
    else:
        for j in range(n_rows):
            update(j)


def _append_new(c_ref, k_ref, v_ref, cout_ref):
    t_new = k_ref.shape[0]
    lb = c_ref.shape[2]
    width = cout_ref.shape[2]
    lane = lax.broadcasted_iota(jnp.int32, (SHIFT_ROWS, LANE), 1)
    zeros = jnp.zeros((LANE - t_new, A_OUT), F32)
    for kv, ref in enumerate((k_ref, v_ref)):
        new = ref[...].astype(F32)
        new_t = pltpu.roll(jnp.concatenate([new, zeros], axis=0).T, LANE - t_new, 1)
        for rb in range(A_OUT // SHIFT_ROWS):
            rs = slice(rb * SHIFT_ROWS, (rb + 1) * SHIFT_ROWS)
            shifted = pltpu.roll(c_ref[kv, rs, lb - width:lb], width - t_new, 1)
            if width > LANE:
                cout_ref[kv, rs, 0:width - LANE] = shifted[:, 0:width - LANE]
            cout_ref[kv, rs, width - LANE:width] = jnp.where(lane >= LANE - t_new, new_t[rs, :], shifted[:, width - LANE:width])


def _sample_attention(q_ref, k_ref, v_ref, c_ref, o_ref, lse_ref, dil, span):
    t_new = q_ref.shape[0]
    lb = c_ref.shape[2]
    rows = HEADS_PER_GROUP * t_new
    q = q_ref[...].astype(F32)
    k_new = k_ref[...].astype(F32)
    v_new = v_ref[...].astype(F32)

    q_rep = jnp.concatenate([q] * HEADS_PER_GROUP, axis=0)
    row_head = _div_pow2(lax.broadcasted_iota(jnp.int32, (rows, A_OUT), 0), t_new)
    col_head = _div_pow2(lax.broadcasted_iota(jnp.int32, (rows, A_OUT), 1), HEAD_DIM)
    diag = row_head == col_head
    q_bd = jnp.where(diag, q_rep, 0.0).astype(BF16)

    s_buf = _dot(q_bd, c_ref[0].astype(BF16))
    s_new = _dot_nt(q_bd, k_new.astype(BF16))

    tok_b = _mod_pow2(lax.broadcasted_iota(jnp.int32, (rows, lb), 0), t_new)
    pos_b = lax.broadcasted_iota(jnp.int32, (rows, lb), 1)
    dist_b = lb + tok_b - pos_b
    ok_b = (_mod_pow2(dist_b, dil) == 0) & (dist_b <= span * dil)
    tok_n = _mod_pow2(lax.broadcasted_iota(jnp.int32, (rows, t_new), 0), t_new)
    pos_n = lax.broadcasted_iota(jnp.int32, (rows, t_new), 1)
    dist_n = tok_n - pos_n
    ok_n = (dist_n >= 0) & (_mod_pow2(dist_n, dil) == 0) & (dist_n <= span * dil)
    s_buf = jnp.where(ok_b, s_buf, -jnp.inf)
    s_new = jnp.where(ok_n, s_new, -jnp.inf)

    m = jnp.maximum(jnp.max(s_buf, axis=-1, keepdims=True), jnp.max(s_new, axis=-1, keepdims=True))
    p_buf = jnp.exp(s_buf - m)
    p_new = jnp.exp(s_new - m)
    l = jnp.sum(p_buf, axis=-1, keepdims=True) + jnp.sum(p_new, axis=-1, keepdims=True)
    o_full = (_dot_nt(p_buf.astype(BF16), c_ref[1].astype(BF16)) + _dot(p_new.astype(BF16), v_new.astype(BF16))) / l
    o_full = jnp.where(diag, o_full, 0.0)
    lse = (m + jnp.log(l)) * LOG2_E

    o = o_full[0:t_new, :]
    lane_t = lax.broadcasted_iota(jnp.int32, (t_new, LANE), 1)
    lse_all = jnp.zeros((t_new, LANE), F32)
    for h in range(HEADS_PER_GROUP):
        if h > 0:
            o = o + o_full[h * t_new:(h + 1) * t_new, :]
        lse_all = jnp.where(lane_t == h, lse[h * t_new:(h + 1) * t_new, :], lse_all)
    o_ref[...] = o.astype(o_ref.dtype)
    lse_ref[...] = lse_all


def _attn_sample(qkv, cache, prev_out, layer, group, t_new):
    window, dil = DSWA_GROUPS[group]
    n_rows = qkv.shape[1]
    batch = n_rows // t_new
    lb = cache.shape[-1]

    first = prev_out is None
    assert first == (layer == 0) and t_new <= LANE
    phases = DEPTH if first else 1
    rows = max(1, min(batch, SAMPLE_BLOCK_BYTES // (2 * A_OUT * lb * 4)))
    assert batch % rows == 0
    blocks = batch // rows

    def row(p, b):
        return jnp.where(p == 0, b, blocks - 1)

    def pspec(part):
        return pl.BlockSpec((None, rows * t_new, COL_BLOCK), lambda p, b: (part, row(p, b), 0))

    in_specs = [pspec(0), pspec(1), pspec(2),
                pl.BlockSpec((None, rows, 2, A_OUT, lb), lambda p, b: (layer + p, b, 0, 0, 0))]
    args = [qkv, qkv, qkv, cache]
    aliases = {}
    if first:
        cache_out = pl.BlockSpec((None, rows, 2, A_OUT, lb), lambda p, b: (layer + p, b, 0, 0, 0))
    else:
        in_specs.append(pl.BlockSpec(memory_space=pl.ANY))
        args.append(prev_out)
        aliases = {4: 2}
        cache_out = pl.BlockSpec((None, rows, 2, A_OUT, LANE), lambda p, b: (layer, b, 0, 0, lb // LANE - 1))
    return pl.pallas_call(
        functools.partial(_attn_sample_kernel, dil=dil, span=window // dil, first=first),
        grid=(phases, blocks),
        in_specs=in_specs,
        out_specs=[
            pl.BlockSpec((rows * t_new, A_OUT), lambda p, b: (row(p, b), 0)),
            pl.BlockSpec((rows * t_new, LANE), lambda p, b: (row(p, b), 0)),
            cache_out,
        ],
        out_shape=[
            jax.ShapeDtypeStruct((n_rows, A_OUT), F32),
            jax.ShapeDtypeStruct((n_rows, LANE), F32),
            jax.ShapeDtypeStruct(cache.shape, F32),
        ],
        input_output_aliases=aliases,
        compiler_params=_cparams(("arbitrary", "arbitrary")),
        name=f"attn_sample_g{group}",
    )(*args)


def _gla_kernel(q_ref, k_ref, v_ref, r_ref, glr_ref, wa_ref, ba_ref, gg_ref, o_ref, sout_ref, s_ref, *, chunk, n_chunks):
    step = pl.program_id(2)

    @pl.when(step == 0)
    def _():
        s_ref[...] = jnp.zeros_like(s_ref)

    ri = lax.broadcasted_iota(jnp.int32, (chunk, chunk), 0)
    ci = lax.broadcasted_iota(jnp.int32, (chunk, chunk), 1)
    causal = ri >= ci
    tri = causal.astype(BF16)
    gg = gg_ref[...]

    gate = _dot(glr_ref[...].astype(BF16), wa_ref[...]) + ba_ref[...]
    log_a = jax.nn.log_sigmoid(gate) * (1.0 / GATE_TAU)
    log_hi = log_a.astype(BF16)
    log_lo = (log_a - log_hi.astype(F32)).astype(BF16)
    chunks = [slice(c * chunk, (c + 1) * chunk) for c in range(n_chunks)]
    cums = [_dot(tri, log_hi[rs, :]) + _dot(tri, log_lo[rs, :]) for rs in chunks]
    lasts = [cum[chunk - 1:chunk, :] for cum in cums]
    cum = jnp.concatenate(cums, axis=0)
    last = jnp.concatenate([jnp.broadcast_to(l, (chunk, GLA_DK)) for l in lasts], axis=0)
    grow = jnp.exp(cum)
    q_in = (q_ref[...].astype(F32) * (GLA_DK ** -0.5) * grow).astype(BF16)
    k = k_ref[...].astype(F32)
    k_in = (k * jnp.exp(-cum)).astype(BF16)
    k_out = (k * jnp.exp(last - cum)).astype(BF16)
    v = v_ref[...].astype(BF16)
    o_intra, kv, decay = [], [], []
    for rs, l in zip(chunks, lasts):
        att = jnp.where(causal, _dot_nt(q_in[rs, :], k_in[rs, :]), 0.0).astype(BF16)
        o_intra.append(_dot(att, v[rs, :]))
        kv.append(_dot_tn(k_out[rs, :], v[rs, :]))
        d = jnp.broadcast_to(jnp.exp(l), (GLA_DK, GLA_DK)).T
        decay.append(jnp.concatenate([d] * (GLA_DV // GLA_DK), axis=1))

    state = s_ref[...]
    for c, rs in enumerate(chunks):
        o = o_intra[c] + _dot(q_in[rs, :], state.astype(BF16))
        state = decay[c] * state + kv[c]
        y = _rms(o, gg) * jax.nn.silu(r_ref[rs, :].astype(F32))
        o_ref[rs, :] = y.astype(o_ref.dtype)
    s_ref[...] = state

    @pl.when(step == pl.num_programs(2) - 1)
    def _():
        sout_ref[...] = state


def _gla_decode_kernel(q_ref, k_ref, v_ref, r_ref, glr_ref, wa_ref, ba_ref, gg_ref, s0_ref, o_ref, sout_ref):
    t_new = q_ref.shape[0]
    ri = lax.broadcasted_iota(jnp.int32, (t_new, t_new), 0)
    ci = lax.broadcasted_iota(jnp.int32, (t_new, t_new), 1)
    causal = ri >= ci
    tri = causal.astype(BF16)
    glr = glr_ref[...].astype(BF16)
    heads_per_block = COL_BLOCK // GLA_DV
    for h in range(GLA_HEADS):
        ks = slice(h * GLA_DK, (h + 1) * GLA_DK)
        vb, vs = h // heads_per_block, slice((h % heads_per_block) * GLA_DV, (h % heads_per_block + 1) * GLA_DV)
        gate = _dot(glr, wa_ref[:, ks]) + ba_ref[:, ks]
        log_a = jax.nn.log_sigmoid(gate) * (1.0 / GATE_TAU)
        log_hi = log_a.astype(BF16)
        log_lo = (log_a - log_hi.astype(F32)).astype(BF16)
        cum = _dot(tri, log_hi) + _dot(tri, log_lo)
        last = cum[t_new - 1:t_new, :]
        q_in = (q_ref[:, ks].astype(F32) * (GLA_DK ** -0.5) * jnp.exp(cum)).astype(BF16)
        k = k_ref[:, ks].astype(F32)
        k_in = (k * jnp.exp(-cum)).astype(BF16)
        k_out = (k * jnp.exp(last - cum)).astype(BF16)
        v = v_ref[vb, :, vs].astype(BF16)
        att = jnp.where(causal, _dot_nt(q_in, k_in), 0.0).astype(BF16)
        state = s0_ref[h].astype(F32)
        o = _dot(att, v) + _dot(q_in, state.astype(BF16))
        d = jnp.broadcast_to(jnp.exp(last), (GLA_DK, GLA_DK)).T
        sout_ref[h] = jnp.concatenate([d] * (GLA_DV // GLA_DK), axis=1) * state + _dot_tn(k_out, v)
        os_ = slice(h * GLA_DV, (h + 1) * GLA_DV)
        y = _rms(o, gg_ref[:, os_]) * jax.nn.silu(r_ref[vb, :, vs].astype(F32))
        o_ref[:, os_] = y.astype(o_ref.dtype)


def _gla_decode(rest, wa, ba, gg, s0, layer, batch, t_new, out_dtype):
    def rows(block, n_blocks=None):
        if n_blocks is None:
            return pl.BlockSpec((None, t_new, COL_BLOCK), lambda b: (block, b, 0))
        return pl.BlockSpec((n_blocks, t_new, COL_BLOCK), lambda b: (block // n_blocks, b, 0))

    def full(a):
        return pl.BlockSpec(a.shape, lambda b: (0, 0))

    v_blocks = GLA_VAL // COL_BLOCK
    return pl.pallas_call(
        _gla_decode_kernel,
        grid=(batch,),
        in_specs=[rows(R_QB), rows(R_KB), rows(R_VB, v_blocks), rows(R_RB, v_blocks),
                  pl.BlockSpec((None, t_new, LANE), lambda b: (R_GLR, b, 0)),
                  full(wa), full(ba), full(gg),
                  pl.BlockSpec((None, None, GLA_HEADS, GLA_DK, GLA_DV), lambda b: (layer, b, 0, 0, 0))],
        out_specs=[pl.BlockSpec((t_new, GLA_VAL), lambda b: (b, 0)),
                   pl.BlockSpec((None, GLA_HEADS, GLA_DK, GLA_DV), lambda b: (b, 0, 0, 0))],
        out_shape=[jax.ShapeDtypeStruct((batch * t_new, GLA_VAL), out_dtype),
                   jax.ShapeDtypeStruct((batch, GLA_HEADS, GLA_DK, GLA_DV), F32)],
        compiler_params=_cparams(("parallel",)),
        name="gla_decode",
    )(rest, rest, rest, rest, rest, wa, ba, gg, s0)


def _gla(rest, wa, ba, gg, batch, seq, chunk, n_chunks, out_dtype):
    n_rows = rest.shape[1]
    lblk = chunk * n_chunks
    steps = seq // lblk

    def pspec(width, block):
        per = COL_BLOCK // width
        return pl.BlockSpec((None, lblk, width), lambda b, h, n: (block + h // per, b * steps + n, h % per))

    in_specs = [
        pspec(GLA_DK, R_QB), pspec(GLA_DK, R_KB), pspec(GLA_DV, R_VB), pspec(GLA_DV, R_RB),
        pl.BlockSpec((None, lblk, LANE), lambda b, h, n: (R_GLR, b * steps + n, 0)),
        pl.BlockSpec((LANE, GLA_DK), lambda b, h, n: (0, h)),
        pl.BlockSpec((1, GLA_DK), lambda b, h, n: (0, h)),
        pl.BlockSpec((1, GLA_DV), lambda b, h, n: (0, h)),
    ]
    args = [rest, rest, rest, rest, rest, wa, ba, gg]
    return pl.pallas_call(
        functools.partial(_gla_kernel, chunk=chunk, n_chunks=n_chunks),
        grid=(batch, GLA_HEADS, steps),
        in_specs=in_specs,
        out_specs=[
            pl.BlockSpec((lblk, GLA_DV), lambda b, h, n: (b * steps + n, h)),
            pl.BlockSpec((None, None, GLA_DK, GLA_DV), lambda b, h, n: (b, h, 0, 0)),
        ],
        out_shape=[
            jax.ShapeDtypeStruct((n_rows, GLA_VAL), out_dtype),
            jax.ShapeDtypeStruct((batch, GLA_HEADS, GLA_DK, GLA_DV), F32),
        ],
        scratch_shapes=[pltpu.VMEM((GLA_DK, GLA_DV), F32)],
        compiler_params=_cparams(("parallel", "parallel", "arbitrary")),
        name="gla",
    )(*args)


def _merge_kernel(x_ref, o1_ref, o2_ref, o3_ref, m1_ref, m2_ref, m3_ref, l1_ref, l2_ref, l3_ref, ob_ref, ga_ref, gb_ref,
                  wpa_ref, wpb_ref, wo_ref, e_ref, out_ref, o_scr, l_scr, *, dils):
    tm = x_ref.shape[0]

    def token_order(ref, scr, dil):
        if dil == 1:
            return ref[0].astype(F32)
        chunks = ref.shape[2] // LANE
        for r in range(dil):
            for c in range(chunks):
                scr[c, pl.ds(r, tm // dil, stride=dil), :] = ref[r, :, c * LANE:(c + 1) * LANE].astype(F32)
        return jnp.concatenate([scr[c] for c in range(chunks)], axis=1)

    ms = [token_order(m_ref, l_scr, d) for m_ref, d in zip((m1_ref, m2_ref, m3_ref), dils)]
    ls = [token_order(l_ref, l_scr, d) for l_ref, d in zip((l1_ref, l2_ref, l3_ref), dils)]
    m = jnp.maximum(jnp.maximum(ms[0], ms[1]), ms[2])
    ws = [jnp.exp2(mg - m) for mg in ms]
    inv = 1.0 / (ws[0] * ls[0] + ws[1] * ls[1] + ws[2] * ls[2])
    expand = e_ref[...]
    o_a = jnp.zeros((tm, A_OUT), F32)
    for w, o_ref, d in zip(ws, (o1_ref, o2_ref, o3_ref), dils):
        o_a = o_a + _split_dot(w * inv, expand) * token_order(o_ref, o_scr, d)
    ya = _dot(o_a.astype(BF16), wpa_ref[...])
    yb = _dot(ob_ref[...].astype(BF16), wpb_ref[...])
    gate_a = jnp.concatenate([ga_ref[0], ga_ref[1]], axis=1).astype(F32)
    gate_b = jnp.concatenate([gb_ref[0], gb_ref[1]], axis=1).astype(F32)
    merged = jax.nn.sigmoid(gate_a) * ya + jax.nn.sigmoid(gate_b) * yb
    out_ref[...] = x_ref[...] + _dot(merged.astype(BF16), wo_ref[...])


def _merge(x, outs, maxes, dens, ob, rest, wpa, wpb, wo, expand, seq, tm):
    n = x.shape[0]
    tiles = seq // tm
    dils = tuple(o.shape[1] for o in outs)

    def rows(width):
        return pl.BlockSpec((tm, width), lambda i: (i, 0))

    def dilated(a):
        dil, width = a.shape[1], a.shape[3]
        return pl.BlockSpec((None, dil, tm // dil, width), lambda i: (i // tiles, 0, i % tiles, 0))

    def gate(block):
        return pl.BlockSpec((2, tm, COL_BLOCK), lambda i: (block // 2, i, 0))

    def full(a):
        return pl.BlockSpec(a.shape, lambda i: (0, 0))

    return pl.pallas_call(
        functools.partial(_merge_kernel, dils=dils),
        grid=(n // tm,),
        in_specs=[rows(D_MODEL)] + [dilated(a) for a in (*outs, *maxes, *dens)] + [rows(GLA_VAL)]
                 + [gate(R_GA), gate(R_GB)] + [full(wpa), full(wpb), full(wo), full(expand)],
        out_specs=rows(D_MODEL),
        out_shape=jax.ShapeDtypeStruct((n, D_MODEL), F32),
        scratch_shapes=[pltpu.VMEM((A_OUT // LANE, tm, LANE), F32), pltpu.VMEM((1, tm, LANE), F32)],
        compiler_params=_cparams(("parallel",)),
        name="merge",
    )(x, *outs, *maxes, *dens, ob, rest, rest, wpa, wpb, wo, expand)


def _ffn_kernel(x_ref, g_ref, gn_ref, wg_ref, wu_ref, wd_ref, o_ref, hn_ref):
    tm = x_ref.shape[0]
    chunk = min(tm, ROW_CHUNK)
    for rc in range(tm // chunk):
        rows = slice(rc * chunk, (rc + 1) * chunk)
        x = x_ref[rows, :]
        h = _rms(x, g_ref[...]).astype(BF16)
        mid = jax.nn.silu(_dot(h, wg_ref[...])) * _dot(h, wu_ref[...])
        y = x + _dot(mid.astype(BF16), wd_ref[...])
        o_ref[rows, :] = y
        hn_ref[rows, :] = _rms(y, gn_ref[...]).astype(hn_ref.dtype)


def _ffn(x, g, g_next, wg, wu, wd, tm):
    n = x.shape[0]

    def resident(a):
        return pl.BlockSpec(a.shape, lambda i: (0, 0), pipeline_mode=pl.Buffered(1))

    def rows():
        return pl.BlockSpec((tm, D_MODEL), lambda i: (i, 0))

    return pl.pallas_call(
        _ffn_kernel,
        grid=(n // tm,),
        in_specs=[rows(), resident(g), resident(g_next), resident(wg), resident(wu), resident(wd)],
        out_specs=[rows(), rows()],
        out_shape=[jax.ShapeDtypeStruct((n, D_MODEL), F32), jax.ShapeDtypeStruct((n, D_MODEL), BF16)],
        compiler_params=_cparams(("parallel",)),
        name="ffn",
    )(x, g, g_next, wg, wu, wd)


def _router_kernel(x_ref, g_ref, wr_ref, comb_ref, scol_ref, srow_ref, cnt_ref):
    h = _rms(x_ref[...], g_ref[...])
    w = wr_ref[...]
    h_hi, w_hi = h.astype(BF16), w.astype(BF16)
    h_lo, w_lo = (h - h_hi.astype(F32)).astype(BF16), (w - w_hi.astype(F32)).astype(BF16)
    logits = _dot(h_hi, w_hi) + (_dot(h_hi, w_lo) + _dot(h_lo, w_hi))
    lane = lax.broadcasted_iota(jnp.int32, logits.shape, 1)
    logits = jnp.where(lane < N_EXPERTS, logits, -jnp.inf)
    v1 = jnp.max(logits, axis=-1, keepdims=True)
    i1 = jnp.min(jnp.where(logits == v1, lane, LANE), axis=-1, keepdims=True)
    rest = jnp.where(lane == i1, -jnp.inf, logits)
    v2 = jnp.max(rest, axis=-1, keepdims=True)
    i2 = jnp.min(jnp.where(rest == v2, lane, LANE), axis=-1, keepdims=True)
    e2 = jnp.exp(v2 - v1)
    g1 = 1.0 / (1.0 + e2)
    g2 = e2 / (1.0 + e2)
    comb = jnp.where(lane == i1, g1, 0.0) + jnp.where(lane == i2, g2, 0.0)
    comb_ref[...] = comb

    tile = comb.shape[0]
    routed = comb > 0.0
    earlier = lax.broadcasted_iota(jnp.int32, (tile, tile), 1) < lax.broadcasted_iota(jnp.int32, (tile, tile), 0)
    rank = _dot(earlier.astype(BF16), routed.astype(BF16))
    slot = jnp.where(routed, rank, -1.0)
    scol_ref[...] = slot
    srow_ref[...] = slot.T[0:N_EXPERTS, :]
    count = jnp.sum(routed.astype(F32), axis=0, keepdims=True)
    cnt_ref[...] = jnp.broadcast_to(count, cnt_ref.shape).astype(jnp.int32)


def _router(x, g, wr, tile):
    n = x.shape[0]
    tiles = n // tile
    comb, slot_col, slot_row, counts = pl.pallas_call(
        _router_kernel,
        grid=(tiles,),
        in_specs=[
            pl.BlockSpec((tile, D_MODEL), lambda i: (i, 0)),
            pl.BlockSpec((1, D_MODEL), lambda i: (0, 0)),
            pl.BlockSpec((D_MODEL, LANE), lambda i: (0, 0)),
        ],
        out_specs=[
            pl.BlockSpec((tile, LANE), lambda i: (i, 0)),
            pl.BlockSpec((tile, LANE), lambda i: (i, 0)),
            pl.BlockSpec((None, N_EXPERTS, tile), lambda i: (i, 0, 0)),
            pl.BlockSpec((None, 8, LANE), lambda i: (i, 0, 0)),
        ],
        out_shape=[
            jax.ShapeDtypeStruct((n, LANE), F32),
            jax.ShapeDtypeStruct((n, LANE), F32),
            jax.ShapeDtypeStruct((tiles, N_EXPERTS, tile), F32),
            jax.ShapeDtypeStruct((tiles, 8, LANE), jnp.int32),
        ],
        compiler_params=_cparams(("parallel",)),
        name="router",
    )(x, g, wr)
    return (comb, counts[:, 0, :N_EXPERTS].reshape(tiles * N_EXPERTS), slot_col,
            slot_row.reshape(tiles * N_EXPERTS, 1, tile))


def _moe_kernel(cnt_ref, x_ref, g_ref, gn_ref, comb_ref, scol_ref, srow_ref, wg_ref, wu_ref, wd_ref, o_ref,
                h_ref, xc_ref, y_ref, *, sub, n_sub):
    i, e, f = pl.program_id(0), pl.program_id(1), pl.program_id(2)
    last_f = pl.num_programs(2) - 1
    tile = x_ref.shape[0]
    sub_pad = y_ref.shape[1]
    count = cnt_ref[i * N_EXPERTS + e]

    @pl.when((e == 0) & (f == 0))
    def _():
        h_ref[...] = _rms(x_ref[...], g_ref[...]).astype(BF16)
        o_ref[...] = x_ref[...]

    for s in range(n_sub):
        @pl.when((f == 0) & (s * sub < count))
        def _():
            slots = (lax.broadcasted_iota(jnp.int32, (sub, tile), 0) + s * sub).astype(F32)
            pick = (srow_ref[...] == slots).astype(BF16)
            xc_ref[s] = _dot(pick, h_ref[...]).astype(BF16)
            y_ref[s] = jnp.zeros((sub_pad, D_MODEL), F32)

        @pl.when(s * sub < count)
        def _():
            xs = xc_ref[s]
            mid = jax.nn.silu(_dot(xs, wg_ref[...])) * _dot(xs, wu_ref[...])
            y_ref[s, 0:sub, :] += _dot(mid.astype(BF16), wd_ref[...])

        @pl.when((f == last_f) & (s * sub < count))
        def _():
            mine = lax.broadcasted_iota(jnp.int32, (tile, LANE), 1) == e
            slot = jnp.sum(jnp.where(mine, scol_ref[...], 0.0), axis=-1, keepdims=True)
            gate = jnp.sum(jnp.where(mine, comb_ref[...], 0.0), axis=-1, keepdims=True)
            slots = (lax.broadcasted_iota(jnp.int32, (tile, sub_pad), 1) + s * sub).astype(F32)
            place = (slot == slots).astype(BF16)
            o_ref[...] += gate * _dot(place, y_ref[s].astype(BF16))

    @pl.when((e == pl.num_programs(1) - 1) & (f == last_f))
    def _():
        o_ref[...] = _rms(o_ref[...], gn_ref[...])


def _moe(x, g, g_out, comb, counts, slot_col, slot_row, wg, wu, wd, sub, tf):
    n = x.shape[0]
    tile = slot_row.shape[-1]
    n_sub = -(-tile // sub)
    sub_pad = -(-sub // LANE) * LANE
    grid_spec = pltpu.PrefetchScalarGridSpec(
        num_scalar_prefetch=1,
        grid=(n // tile, N_EXPERTS, D_FF_EXPERT // tf),
        in_specs=[
            pl.BlockSpec((tile, D_MODEL), lambda i, e, f, c: (i, 0)),
            pl.BlockSpec((1, D_MODEL), lambda i, e, f, c: (0, 0)),
            pl.BlockSpec((1, D_MODEL), lambda i, e, f, c: (0, 0)),
            pl.BlockSpec((tile, LANE), lambda i, e, f, c: (i, 0)),
            pl.BlockSpec((tile, LANE), lambda i, e, f, c: (i, 0)),
            pl.BlockSpec((None, 1, tile), lambda i, e, f, c: (i * N_EXPERTS + e, 0, 0)),
            pl.BlockSpec((None, D_MODEL, tf), lambda i, e, f, c: (e, 0, f)),
            pl.BlockSpec((None, D_MODEL, tf), lambda i, e, f, c: (e, 0, f)),
            pl.BlockSpec((None, tf, D_MODEL), lambda i, e, f, c: (e, f, 0)),
        ],
        out_specs=pl.BlockSpec((tile, D_MODEL), lambda i, e, f, c: (i, 0)),
        scratch_shapes=[
            pltpu.VMEM((tile, D_MODEL), BF16),
            pltpu.VMEM((n_sub, sub, D_MODEL), BF16),
            pltpu.VMEM((n_sub, sub_pad, D_MODEL), F32),
        ],
    )
    return pl.pallas_call(
        functools.partial(_moe_kernel, sub=sub, n_sub=n_sub),
        grid_spec=grid_spec,
        out_shape=jax.ShapeDtypeStruct((n, D_MODEL), F32),
        compiler_params=_cparams(("parallel", "arbitrary", "arbitrary")),
        name="moe",
    )(counts, x, g, g_out, comb, slot_col, slot_row, wg, wu, wd)


def _norm_kernel(x_ref, g_ref, o_ref):
    o_ref[...] = _rms(x_ref[...], g_ref[...]).astype(o_ref.dtype)


def _norm(x, g, tm, out_dtype):
    n = x.shape[0]
    return pl.pallas_call(
        _norm_kernel,
        grid=(n // tm,),
        in_specs=[pl.BlockSpec((tm, D_MODEL), lambda i: (i, 0)), pl.BlockSpec((1, D_MODEL), lambda i: (0, 0))],
        out_specs=pl.BlockSpec((tm, D_MODEL), lambda i: (i, 0)),
        out_shape=jax.ShapeDtypeStruct((n, D_MODEL), out_dtype),
        compiler_params=_cparams(("parallel",)),
        name="norm",
    )(x, g)


def _rope_tables(pos, split_half):
    half = HEAD_DIM // 2
    inv = ROPE_THETA ** (-jnp.arange(half, dtype=F32) / half)
    ang = pos.astype(F32)[:, None] * inv[None, :]
    cos = jnp.cos(ang)
    sin = jnp.sin(ang)
    if split_half:
        return jnp.tile(cos, (1, LANE // half)), jnp.tile(sin, (1, LANE // half))
    return (jnp.tile(jnp.concatenate([cos, cos], axis=-1), (1, LANE // HEAD_DIM)),
            jnp.tile(jnp.concatenate([-sin, sin], axis=-1), (1, LANE // HEAD_DIM)))


def _to_split_half(a):
    lead = a.shape[:-1]
    a = a.reshape(*lead, HEADS_PER_GROUP // QUAD, QUAD, 2, HEAD_DIM // 2)
    return jnp.swapaxes(a, -2, -3).reshape(*lead, A_OUT)


def _layer_weights(l, w_in, w_gate_a2, b_gate_a, g_gla, w_branch_a, w_branch_b, w_out):
    w = w_in[l]
    offs = np.cumsum((A_WIDTH, A_WIDTH, A_WIDTH, GLA_KEY, GLA_KEY, GLA_VAL, GLA_VAL, GATE_RANK, D_MODEL, D_MODEL))
    qa, ka, va, qb, kb, vb, rb, glr, ga, gb = jnp.split(w, [int(o) for o in offs[:-1]], axis=1)
    def attn_weights(split_half):
        relay = _to_split_half if split_half else (lambda a: a)
        return jnp.stack([jnp.stack([relay(qa[:, g * A_OUT:(g + 1) * A_OUT]), relay(ka[:, g * A_OUT:(g + 1) * A_OUT]),
                                     va[:, g * A_OUT:(g + 1) * A_OUT]]) for g in range(N_GROUPS)]).astype(BF16)

    w_attn = {False: attn_weights(False), True: attn_weights(True)}
    glr = jnp.pad(glr, ((0, 0), (0, COL_BLOCK - GATE_RANK)))
    rest = jnp.concatenate([qb, kb, vb, rb, ga, gb, glr], axis=1)
    w_rest = jnp.swapaxes(rest.reshape(D_MODEL, N_REST, COL_BLOCK), 0, 1).astype(BF16)
    wa = jnp.pad(w_gate_a2[l], ((0, LANE - GATE_RANK), (0, 0))).astype(BF16)
    return dict(w_attn=w_attn, w_rest=w_rest, wa=wa, ba=b_gate_a[l][None, :], gg=g_gla[l][None, :],
                wpa=w_branch_a[l].astype(BF16), wpb=w_branch_b[l].astype(BF16), wo=w_out[l].astype(BF16))


def _head_expand():
    e = np.zeros((LANE, A_OUT), np.float32)
    for h in range(HEADS_PER_GROUP):
        e[h, h * HEAD_DIM:(h + 1) * HEAD_DIM] = 1.0
    return jnp.asarray(np.concatenate([e, e], axis=0), BF16)


def _trunk(x, rope, caches, gla_state, lw, ffw, g_mix, g_ffn, g_final, batch, seq, tm, tm_proj, moe_tile, moe_sub,
           p_dtype):
    cos_t, sin_t = rope
    expand = _head_expand()
    prompt = caches is None
    cache_out = [None] * N_GROUPS
    states_out = []
    assert DEPTH % 2 == 0
    h = _norm(x, g_mix[0][None, :], tm, BF16)
    for l in range(DEPTH):
        w = lw[l]
        rest = _inproj_rest(h, w["w_rest"], tm_proj, p_dtype)
        outs, maxes, dens = [], [], []
        for g, (window, dil) in enumerate(DSWA_GROUPS):
            if prompt:
                qkv = _inproj_attn(h, w["w_attn"][True], cos_t, sin_t, g, batch, seq, dil, tm_proj, p_dtype, True)
                o, mx, den = _attn_prompt(qkv, g)
                cache_out[g] = _kv_tail(qkv, cache_out[g], l, min(window, seq))
            else:
                qkv = _inproj_attn(h, w["w_attn"][False], cos_t, sin_t, g, 1, batch * seq, 1, tm_proj, p_dtype, False)
                o, lse, cache_out[g] = _attn_sample(qkv.reshape(3, batch * seq, COL_BLOCK), caches[g], cache_out[g],
                                                    l, g, seq)
                o = o.reshape(1, 1, batch * seq, A_OUT)
                mx = lse.reshape(1, 1, batch * seq, LANE)
                den = jnp.ones_like(mx)
            outs.append(o)
            maxes.append(mx)
            dens.append(den)
        if prompt:
            ob, s_new = _gla(rest, w["wa"], w["ba"], w["gg"], batch, seq, GLA_CHUNK, GLA_CHUNKS_PER_STEP, BF16)
            x = _merge(x, outs, maxes, dens, ob, rest, w["wpa"], w["wpb"], w["wo"], expand, seq, min(tm, 512))
        else:
            ob, s_new = _gla_decode(rest, w["wa"], w["ba"], w["gg"], gla_state, l, batch, seq, F32)
            x = _merge(x, outs, maxes, dens, ob, rest, w["wpa"], w["wpb"], w["wo"], expand, batch * seq, tm)
        states_out.append(s_new)
        i = l // 2
        if l % 2 == 0:
            x, h = _ffn(x, g_ffn[l][None, :], g_mix[l + 1][None, :], ffw["wg"][i], ffw["wu"][i], ffw["wd"][i], tm)
        else:
            routing = _router(x, g_ffn[l][None, :], ffw["wr"][i], moe_tile)
            g_out = g_final if l == DEPTH - 1 else g_mix[l + 1]
            x = _moe(x, g_ffn[l][None, :], g_out[None, :], *routing, ffw["eg"][i], ffw["eu"][i], ffw["ed"][i],
                     moe_sub, D_FF_EXPERT // 2)
            assert l == DEPTH - 1
    y = x
    bufs = [jnp.transpose(c.reshape(DEPTH, batch, 2, HEADS_PER_GROUP, HEAD_DIM, c.shape[-1]), (0, 1, 5, 2, 3, 4))
            for c in cache_out]
    return (y, bufs[0], bufs[1], bufs[2], jnp.stack(states_out))


def kernel(x_prompt, x_sample, cache_kv_w128, cache_kv_w512, cache_kv_w2048, state_gla, g_mix, w_in, w_gate_a2, b_gate_a, g_gla, w_branch_a, w_branch_b, w_out, g_ffn, w_ff_gate, w_ff_up, w_ff_down, w_router, w_exp_gate, w_exp_up, w_exp_down, g_final):
    batch, seq, _ = x_prompt.shape
    dec_batch, dec_seq, _ = x_sample.shape
    assert seq % (DSWA_GROUPS[-1][1] * Q_BLOCK) == 0 and seq % (GLA_CHUNKS_PER_STEP * GLA_CHUNK) == 0
    assert dec_seq % GLA_CHUNK != 0 and dec_seq % 8 == 0
    assert all(c.shape[2] == w for c, (w, _) in zip((cache_kv_w128, cache_kv_w512, cache_kv_w2048), DSWA_GROUPS))

    lw = [_layer_weights(l, w_in, w_gate_a2, b_gate_a, g_gla, w_branch_a, w_branch_b, w_out) for l in range(DEPTH)]
    ffw = dict(wg=w_ff_gate.astype(BF16), wu=w_ff_up.astype(BF16), wd=w_ff_down.astype(BF16),
               wr=jnp.pad(w_router, ((0, 0), (0, 0), (0, LANE - N_EXPERTS))),
               eg=w_exp_gate.astype(BF16), eu=w_exp_up.astype(BF16), ed=w_exp_down.astype(BF16))

    y_p, kv128_p, kv512_p, kv2048_p, gla_p = _trunk(
        x_prompt.reshape(batch * seq, D_MODEL), _rope_tables(jnp.arange(seq, dtype=jnp.int32), True), None, None,
        lw, ffw, g_mix, g_ffn, g_final, batch, seq, 1024, 2048, 1024, 288, BF16)

    n_s = dec_batch * dec_seq
    pos_s = jnp.tile(PAST_LEN + jnp.arange(dec_seq, dtype=jnp.int32), dec_batch)
    caches = [jnp.transpose(c, (0, 1, 3, 4, 5, 2)).reshape(DEPTH, dec_batch, 2, A_OUT, c.shape[2])
              for c in (cache_kv_w128, cache_kv_w512, cache_kv_w2048)]
    y_s, kv128_s, kv512_s, kv2048_s, gla_s = _trunk(
        x_sample.reshape(n_s, D_MODEL), _rope_tables(pos_s, False), caches, state_gla,
        lw, ffw, g_mix, g_ffn, g_final, dec_batch, dec_seq, n_s, n_s, n_s, 96, F32)

    return (y_p.reshape(batch, seq, D_MODEL), y_s.reshape(dec_batch, dec_seq, D_MODEL),
            kv128_p, kv512_p, kv2048_p, gla_p, kv128_s, kv512_s, kv2048_s, gla_s)
```

```python
import functools

import jax
import jax.numpy as jnp
import numpy as np
from jax import lax
from jax.experimental import pallas as pl
from jax.experimental.pallas import tpu as pltpu

F32 = jnp.float32
BF16 = jnp.bfloat16

D_MODEL = 1024
DEPTH = 2
PAST_LEN = 16384
HEAD_DIM = 64
DSWA_GROUPS = ((128, 1), (512, 4), (2048, 16))
N_GROUPS = 3
HEADS_PER_GROUP = 8
A_WIDTH = N_GROUPS * HEADS_PER_GROUP * HEAD_DIM
A_OUT = HEADS_PER_GROUP * HEAD_DIM
Q_BLOCK = 128
GLA_HEADS = 4
GLA_KEY = 512
GLA_VAL = 1024
GLA_DK = 128
GLA_DV = 256
GATE_RANK = 16
GATE_TAU = 16.0
GLA_CHUNK = 64
D_FF = 2816
N_EXPERTS = 8
D_FF_EXPERT = 3584
ROPE_THETA = 10000.0
EPS = 1e-6
LOG2_E = 1.4426950408889634

LANE = 128
COL_BLOCK = 512
R_QB, R_KB, R_VB, R_RB, R_GA, R_GB, R_GLR, N_REST = 0, 1, 2, 4, 6, 8, 10, 11
ROW_CHUNK = 256
GLA_CHUNKS_PER_STEP = 16
VMEM_LIMIT = 56 * 1024 * 1024


def _cparams(sem):
    return pltpu.CompilerParams(dimension_semantics=sem, vmem_limit_bytes=VMEM_LIMIT)


def _rms(xf, g):
    return xf * lax.rsqrt(jnp.mean(xf * xf, axis=-1, keepdims=True) + EPS) * g


def _dot(a, b):
    return jnp.dot(a, b, preferred_element_type=F32)


def _dot_nt(a, b):
    return lax.dot_general(a, b, (((1,), (1,)), ((), ())), preferred_element_type=F32)


def _dot_tn(a, b):
    return lax.dot_general(a, b, (((0,), (0,)), ((), ())), preferred_element_type=F32)


def _div_pow2(x, d):
    assert d & (d - 1) == 0
    return lax.shift_right_logical(x, int(d).bit_length() - 1)


def _mod_pow2(x, d):
    assert d & (d - 1) == 0
    return x & (d - 1)


def _split_dot(a_f32, b_twice):
    hi = a_f32.astype(BF16)
    lo = (a_f32 - hi.astype(F32)).astype(BF16)
    return _dot(jnp.concatenate([hi, lo], axis=1), b_twice)


def _inproj_attn_kernel(h_ref, w_ref, cos_ref, sin_ref, o_ref, *scratch, dil, split_half, q_scale):
    part = pl.program_id(1)
    scale = jnp.where(part == 0, q_scale, 1.0).astype(F32)
    tm = h_ref.shape[0]
    chunk = min(tm, ROW_CHUNK)
    n_lane_groups = COL_BLOCK // LANE
    for rc in range(tm // chunk):
        rows = slice(rc * chunk, (rc + 1) * chunk)
        acc = _dot(h_ref[rows, :], w_ref[...])
        cos = jnp.where(part == 2, 1.0, cos_ref[rows, :] * scale)
        sin = jnp.where(part == 2, 0.0, sin_ref[rows, :] * scale)
        ys = []
        if split_half:
            for c in range(0, n_lane_groups, 2):
                a = acc[:, c * LANE:(c + 1) * LANE]
                b = acc[:, (c + 1) * LANE:(c + 2) * LANE]
                ys += [a * cos - b * sin, a * sin + b * cos]
        else:
            first_half = _mod_pow2(lax.broadcasted_iota(jnp.int32, cos.shape, 1), HEAD_DIM) < HEAD_DIM // 2
            for c in range(n_lane_groups):
                xs = acc[:, c * LANE:(c + 1) * LANE]
                swapped = jnp.where(first_half, pltpu.roll(xs, LANE - HEAD_DIM // 2, 1), pltpu.roll(xs, HEAD_DIM // 2, 1))
                ys.append(xs * cos + swapped * sin)
        if dil == 1:
            for c, y in enumerate(ys):
                o_ref[0, rows, c * LANE:(c + 1) * LANE] = y.astype(o_ref.dtype)
        else:
            sub = chunk // dil
            pitch = _residue_pitch(dil)
            base = rc * sub * pitch
            for c, y in enumerate(ys):
                if pitch == dil:
                    scratch[0][c, rows, :] = y
                else:
                    for u in range(sub):
                        scratch[0][c, base + u * pitch:base + u * pitch + dil, :] = y[u * dil:(u + 1) * dil, :]
            for r in range(dil):
                for c in range(n_lane_groups):
                    o_ref[r, rc * sub:(rc + 1) * sub, c * LANE:(c + 1) * LANE] = (
                        scratch[0][c, pl.ds(base + r, sub, stride=pitch), :].astype(o_ref.dtype))


def _residue_pitch(dil):
    return dil + 8 if dil % 16 == 0 else dil


def _inproj_attn(h, w, cos_t, sin_t, group, batch, seq, dil, tm, out_dtype, split_half):
    tiles = seq // tm
    scratch = [pltpu.VMEM((COL_BLOCK // LANE, tm // dil * _residue_pitch(dil), LANE), F32)] if dil > 1 else []
    q_scale = HEAD_DIM ** -0.5 * (LOG2_E if split_half else 1.0)
    return pl.pallas_call(
        functools.partial(_inproj_attn_kernel, dil=dil, split_half=split_half, q_scale=q_scale),
        grid=(batch * tiles, 3),
        in_specs=[
            pl.BlockSpec((tm, D_MODEL), lambda i, j: (i, 0)),
            pl.BlockSpec((None, None, D_MODEL, COL_BLOCK), lambda i, j: (group, j, 0, 0)),
            pl.BlockSpec((tm, LANE), lambda i, j: (i % tiles, 0)),
            pl.BlockSpec((tm, LANE), lambda i, j: (i % tiles, 0)),
        ],
        out_specs=pl.BlockSpec((None, None, dil, tm // dil, COL_BLOCK), lambda i, j: (j, i // tiles, 0, i % tiles, 0)),
        out_shape=jax.ShapeDtypeStruct((3, batch, dil, seq // dil, COL_BLOCK), out_dtype),
        scratch_shapes=scratch,
        compiler_params=_cparams(("parallel", "arbitrary")),
        name=f"inproj_attn_g{group}",
    )(h, w, cos_t, sin_t)


def _inproj_rest_kernel(h_ref, w_ref, o_ref):
    tm = h_ref.shape[0]
    chunk = min(tm, ROW_CHUNK)
    for rc in range(tm // chunk):
        rows = slice(rc * chunk, (rc + 1) * chunk)
        o_ref[rows, :] = _dot(h_ref[rows, :], w_ref[...]).astype(o_ref.dtype)


def _inproj_rest(h, w, tm, out_dtype):
    n = h.shape[0]
    return pl.pallas_call(
        _inproj_rest_kernel,
        grid=(n // tm, N_REST),
        in_specs=[
            pl.BlockSpec((tm, D_MODEL), lambda i, j: (i, 0)),
            pl.BlockSpec((None, D_MODEL, COL_BLOCK), lambda i, j: (j, 0, 0)),
        ],
        out_specs=pl.BlockSpec((None, tm, COL_BLOCK), lambda i, j: (j, i, 0)),
        out_shape=jax.ShapeDtypeStruct((N_REST, n, COL_BLOCK), out_dtype),
        compiler_params=_cparams(("parallel", "arbitrary")),
        name="inproj_rest",
    )(h, w)


QUAD = 4
QUAD_WIDTH = QUAD * HEAD_DIM
MAX_Q_BLOCKS_PER_STEP = 8


def _attn_prompt_kernel(bias_ref, *refs):
    n = pl.program_id(2)
    for r in range(refs[0].shape[0]):
        _attn_prompt_subsequence(n, bias_ref, *(ref.at[r] for ref in refs))


def _attn_prompt_subsequence(n, bias_ref, q_ref, kp_ref, kc_ref, vp_ref, vc_ref, o_ref, m_ref, l_ref):
    rows = QUAD * Q_BLOCK
    row_head = _div_pow2(lax.broadcasted_iota(jnp.int32, (rows, QUAD_WIDTH), 0), Q_BLOCK)
    col = lax.broadcasted_iota(jnp.int32, (rows, QUAD_WIDTH), 1)
    q_lanes = _div_pow2(_mod_pow2(col, LANE), HEAD_DIM // 2) == row_head
    v_lanes = _div_pow2(col, HEAD_DIM) == row_head
    lane = lax.broadcasted_iota(jnp.int32, (Q_BLOCK, LANE), 1)
    for qb in range(q_ref.shape[0] // Q_BLOCK):
        cur = slice(qb * Q_BLOCK, (qb + 1) * Q_BLOCK)
        if qb == 0:
            bias = bias_ref[jnp.where(n == 0, 1, 0)]
        else:
            bias = bias_ref[0]
            prev = slice((qb - 1) * Q_BLOCK, qb * Q_BLOCK)
        m_all = jnp.zeros((Q_BLOCK, LANE), F32)
        l_all = jnp.ones((Q_BLOCK, LANE), F32)
        for c in range(A_OUT // QUAD_WIDTH):
            cs = slice(c * QUAD_WIDTH, (c + 1) * QUAD_WIDTH)
            q = q_ref[cur, cs].astype(F32)
            q4 = jnp.where(q_lanes, jnp.concatenate([q] * QUAD, axis=0), 0.0).astype(BF16)
            k_prev = kp_ref[:, cs] if qb == 0 else kc_ref[prev, cs]
            v_prev = vp_ref[:, cs] if qb == 0 else vc_ref[prev, cs]
            k = jnp.concatenate([k_prev, kc_ref[cur, cs]], axis=0)
            v = jnp.concatenate([v_prev, vc_ref[cur, cs]], axis=0)
            s = _dot_nt(q4, k) + bias
            m = jnp.max(s, axis=-1, keepdims=True)
            p = jnp.exp2(s - m)
            l = jnp.sum(p, axis=-1, keepdims=True)
            o4 = jnp.where(v_lanes, _dot(p.astype(BF16), v), 0.0)
            o = o4[0:Q_BLOCK]
            for j in range(QUAD):
                hs = slice(j * Q_BLOCK, (j + 1) * Q_BLOCK)
                if j > 0:
                    o = o + o4[hs]
                m_all = jnp.where(lane == c * QUAD + j, m[hs], m_all)
                l_all = jnp.where(lane == c * QUAD + j, l[hs], l_all)
            o_ref[cur, cs] = o.astype(o_ref.dtype)
        m_ref[cur, :] = m_all
        l_ref[cur, :] = l_all


def _band_bias(span):
    qi = np.arange(QUAD * Q_BLOCK)[:, None] % Q_BLOCK
    kj = np.arange(2 * Q_BLOCK)[None, :]
    dist = Q_BLOCK + qi - kj
    band = (dist >= 0) & (dist <= span)
    first = band & (kj >= Q_BLOCK)
    return jnp.asarray(np.where(np.stack([band, first]), 0.0, -np.inf), F32)


def _attn_prompt(qkv, group):
    window, dil = DSWA_GROUPS[group]
    _, batch, _, sub_len, _ = qkv.shape
    q_blocks = min(MAX_Q_BLOCKS_PER_STEP, sub_len // Q_BLOCK)
    step_rows = q_blocks * Q_BLOCK
    nb = sub_len // step_rows
    res = min(dil, MAX_Q_BLOCKS_PER_STEP // q_blocks)

    def spec(part, prev):
        if prev:
            return pl.BlockSpec((None, None, res, Q_BLOCK, COL_BLOCK),
                                lambda b, r, n: (part, b, r, jnp.maximum(n * q_blocks - 1, 0), 0))
        return pl.BlockSpec((None, None, res, step_rows, COL_BLOCK), lambda b, r, n: (part, b, r, n, 0))

    bias = _band_bias(window // dil)
    stat_spec = pl.BlockSpec((None, res, step_rows, LANE), lambda b, r, n: (b, r, n, 0))
    stat_shape = jax.ShapeDtypeStruct((batch, dil, sub_len, LANE), F32)
    return pl.pallas_call(
        _attn_prompt_kernel,
        grid=(batch, dil // res, nb),
        in_specs=[pl.BlockSpec(bias.shape, lambda b, r, n: (0, 0, 0)),
                  spec(0, False), spec(1, True), spec(1, False), spec(2, True), spec(2, False)],
        out_specs=[pl.BlockSpec((None, res, step_rows, A_OUT), lambda b, r, n: (b, r, n, 0)), stat_spec, stat_spec],
        out_shape=[jax.ShapeDtypeStruct((batch, dil, sub_len, A_OUT), BF16), stat_shape, stat_shape],
        compiler_params=_cparams(("parallel", "parallel", "arbitrary")),
        name=f"attn_prompt_g{group}",
    )(bias, qkv, qkv, qkv, qkv, qkv)


def _kv_tail_kernel(*refs, dil, aliased):
    if aliased:
        k_ref, v_ref, _, out_ref, scr = refs
        _kv_tail_body(k_ref, v_ref, out_ref, scr, dil)
        return
    k_ref, v_ref, out_ref, scr = refs

    @pl.when(pl.program_id(0) == 0)
    def _():
        _kv_tail_body(k_ref, v_ref, out_ref, scr, dil)

    @pl.when(pl.program_id(0) > 0)
    def _():
        out_ref[...] = jnp.zeros_like(out_ref)


def _kv_tail_body(k_ref, v_ref, out_ref, scr, dil):
    keep = out_ref.shape[2]
    sub = keep // dil
    half_w = HEAD_DIM // 2
    for kv, ref in enumerate((k_ref, v_ref)):
        for c in range(COL_BLOCK // LANE):
            cs = slice(c * LANE, (c + 1) * LANE)
            if dil == 1:
                tok = ref[0, :, cs].astype(F32)
            else:
                for r in range(dil):
                    scr[pl.ds(r, sub, stride=dil), :] = ref[r, :, cs].astype(F32)
                tok = scr[...]
            chan = tok.T
            if kv == 1:
                out_ref[kv, cs, :] = chan
            else:
                for j in range(QUAD):
                    dst = (QUAD * (c // 2) + j) * HEAD_DIM + (c % 2) * half_w
                    out_ref[kv, dst:dst + half_w, :] = chan[j * half_w:(j + 1) * half_w, :]


def _kv_tail(qkv, prev_out, layer, keep):
    _, batch, dil, sub_len, _ = qkv.shape
    last = sub_len // (keep // dil) - 1
    first = prev_out is None
    assert first == (layer == 0)
    phases = DEPTH if first else 1

    def spec(part):
        return pl.BlockSpec((None, None, dil, keep // dil, COL_BLOCK),
                            lambda p, b: (part, jnp.where(p == 0, b, batch - 1), 0, last, 0))

    in_specs = [spec(1), spec(2)]
    args = [qkv, qkv]
    aliases = {}
    if not first:
        in_specs.append(pl.BlockSpec(memory_space=pl.ANY))
        args.append(prev_out)
        aliases = {2: 0}
    return pl.pallas_call(
        functools.partial(_kv_tail_kernel, dil=dil, aliased=not first),
        grid=(phases, batch),
        in_specs=in_specs,
        out_specs=pl.BlockSpec((None, None, 2, A_OUT, keep), lambda p, b: (layer + p, b, 0, 0, 0)),
        out_shape=jax.ShapeDtypeStruct((DEPTH, batch, 2, A_OUT, keep), F32),
        scratch_shapes=[pltpu.VMEM((keep, LANE), F32)],
        input_output_aliases=aliases,
        compiler_params=_cparams(("arbitrary", "arbitrary")),
        name=f"kv_tail_{keep}",
    )(*args)


SHIFT_ROWS = 128
SAMPLE_BLOCK_BYTES = 8 * 1024 * 1024


def _attn_sample_kernel(*refs, dil, span, first):
    if first:
        q_ref, k_ref, v_ref, c_ref, o_ref, lse_ref, cout_ref = refs
    else:
        q_ref, k_ref, v_ref, c_ref, _, o_ref, lse_ref, cout_ref = refs
    n_rows = c_ref.shape[0]
    t_new = q_ref.shape[0] // n_rows

    def update(j):
        tok = pl.ds(j * t_new, t_new)
        q, k, v = q_ref.at[tok], k_ref.at[tok], v_ref.at[tok]
        _sample_attention(q, k, v, c_ref.at[j], o_ref.at[tok], lse_ref.at[tok], dil, span)
        _append_new(c_ref.at[j], k, v, cout_ref.at[j])

    def shift_only(j):
        for kv in range(2):
            for rb in range(A_OUT // SHIFT_ROWS):
                rs = slice(rb * SHIFT_ROWS, (rb + 1) * SHIFT_ROWS)
                cout_ref[j, kv, rs, :] = pltpu.roll(c_ref[j, kv, rs, :], c_ref.shape[3] - t_new, 1)

    if first:
        @pl.when(pl.program_id(0) == 0)
        def _():
            for j in range(n_rows):
                update(j)

        @pl.when(pl.program_id(0) > 0)
        def _():
            for j in range(n_rows):
                shift_only(j)
    else:
        for j in range(n_rows):
            update(j)


def _append_new(c_ref, k_ref, v_ref, cout_ref):
    t_new = k_ref.shape[0]
    lb = c_ref.shape[2]
    width = cout_ref.shape[2]
    lane = lax.broadcasted_iota(jnp.int32, (SHIFT_ROWS, LANE), 1)
    zeros = jnp.zeros((LANE - t_new, A_OUT), F32)
    for kv, ref in enumerate((k_ref, v_ref)):
        new = ref[...].astype(F32)
        new_t = pltpu.roll(jnp.concatenate([new, zeros], axis=0).T, LANE - t_new, 1)
        for rb in range(A_OUT // SHIFT_ROWS):
            rs = slice(rb * SHIFT_ROWS, (rb + 1) * SHIFT_ROWS)
            shifted = pltpu.roll(c_ref[kv, rs, lb - width:lb], width - t_new, 1)
            if width > LANE:
                cout_ref[kv, rs, 0:width - LANE] = shifted[:, 0:width - LANE]
            cout_ref[kv, rs, width - LANE:width] = jnp.where(lane >= LANE - t_new, new_t[rs, :], shifted[:, width - LANE:width])


def _sample_attention(q_ref, k_ref, v_ref, c_ref, o_ref, lse_ref, dil, span):
    t_new = q_ref.shape[0]
    lb = c_ref.shape[2]
    rows = HEADS_PER_GROUP * t_new
    q = q_ref[...].astype(F32)
    k_new = k_ref[...].astype(F32)
    v_new = v_ref[...].astype(F32)

    q_rep = jnp.concatenate([q] * HEADS_PER_GROUP, axis=0)
    row_head = _div_pow2(lax.broadcasted_iota(jnp.int32, (rows, A_OUT), 0), t_new)
    col_head = _div_pow2(lax.broadcasted_iota(jnp.int32, (rows, A_OUT), 1), HEAD_DIM)
    diag = row_head == col_head
    q_bd = jnp.where(diag, q_rep, 0.0).astype(BF16)

    s_buf = _dot(q_bd, c_ref[0].astype(BF16))
    s_new = _dot_nt(q_bd, k_new.astype(BF16))

    tok_b = _mod_pow2(lax.broadcasted_iota(jnp.int32, (rows, lb), 0), t_new)
    pos_b = lax.broadcasted_iota(jnp.int32, (rows, lb), 1)
    dist_b = lb + tok_b - pos_b
    ok_b = (_mod_pow2(dist_b, dil) == 0) & (dist_b <= span * dil)
    tok_n = _mod_pow2(lax.broadcasted_iota(jnp.int32, (rows, t_new), 0), t_new)
    pos_n = lax.broadcasted_iota(jnp.int32, (rows, t_new), 1)
    dist_n = tok_n - pos_n
    ok_n = (dist_n >= 0) & (_mod_pow2(dist_n, dil) == 0) & (dist_n <= span * dil)
    s_buf = jnp.where(ok_b, s_buf, -jnp.inf)
    s_new = jnp.where(ok_n, s_new, -jnp.inf)

    m = jnp.maximum(jnp.max(s_buf, axis=-1, keepdims=True), jnp.max(s_new, axis=-1, keepdims=True))
    p_buf = jnp.exp(s_buf - m)
    p_new = jnp.exp(s_new - m)
    l = jnp.sum(p_buf, axis=-1, keepdims=True) + jnp.sum(p_new, axis=-1, keepdims=True)
    o_full = (_dot_nt(p_buf.astype(BF16), c_ref[1].astype(BF16)) + _dot(p_new.astype(BF16), v_new.astype(BF16))) / l
    o_full = jnp.where(diag, o_full, 0.0)
    lse = (m + jnp.log(l)) * LOG2_E

    o = o_full[0:t_new, :]
    lane_t = lax.broadcasted_iota(jnp.int32, (t_new, LANE), 1)
    lse_all = jnp.zeros((t_new, LANE), F32)
    for h in range(HEADS_PER_GROUP):
        if h > 0:
            o = o + o_full[h * t_new:(h + 1) * t_new, :]
        lse_all = jnp.where(lane_t == h, lse[h * t_new:(h + 1) * t_new, :], lse_all)
    o_ref[...] = o.astype(o_ref.dtype)
    lse_ref[...] = lse_all


def _attn_sample(qkv, cache, prev_out, layer, group, t_new):
    window, dil = DSWA_GROUPS[group]
    n_rows = qkv.shape[1]
    batch = n_rows // t_new
    lb = cache.shape[-1]

    first = prev_out is None
    assert first == (layer == 0) and t_new <= LANE
    phases = DEPTH if first else 1
    rows = max(1, min(batch, SAMPLE_BLOCK_BYTES // (2 * A_OUT * lb * 4)))
    assert batch % rows == 0
    blocks = batch // rows

    def row(p, b):
        return jnp.where(p == 0, b, blocks - 1)

    def pspec(part):
        return pl.BlockSpec((None, rows * t_new, COL_BLOCK), lambda p, b: (part, row(p, b), 0))

    in_specs = [pspec(0), pspec(1), pspec(2),
                pl.BlockSpec((None, rows, 2, A_OUT, lb), lambda p, b: (layer + p, b, 0, 0, 0))]
    args = [qkv, qkv, qkv, cache]
    aliases = {}
    if first:
        cache_out = pl.BlockSpec((None, rows, 2, A_OUT, lb), lambda p, b: (layer + p, b, 0, 0, 0))
    else:
        in_specs.append(pl.BlockSpec(memory_space=pl.ANY))
        args.append(prev_out)
        aliases = {4: 2}
        cache_out = pl.BlockSpec((None, rows, 2, A_OUT, LANE), lambda p, b: (layer, b, 0, 0, lb // LANE - 1))
    return pl.pallas_call(
        functools.partial(_attn_sample_kernel, dil=dil, span=window // dil, first=first),
        grid=(phases, blocks),
        in_specs=in_specs,
        out_specs=[
            pl.BlockSpec((rows * t_new, A_OUT), lambda p, b: (row(p, b), 0)),
            pl.BlockSpec((rows * t_new, LANE), lambda p, b: (row(p, b), 0)),
            cache_out,
        ],
        out_shape=[
            jax.ShapeDtypeStruct((n_rows, A_OUT), F32),
            jax.ShapeDtypeStruct((n_rows, LANE), F32),
            jax.ShapeDtypeStruct(cache.shape, F32),
        ],
        input_output_aliases=aliases,
        compiler_params=_cparams(("arbitrary", "arbitrary")),
        name=f"attn_sample_g{group}",
    )(*args)


def _gla_kernel(q_ref, k_ref, v_ref, r_ref, glr_ref, wa_ref, ba_ref, gg_ref, o_ref, sout_ref, s_ref, *, chunk, n_chunks):
    step = pl.program_id(2)

    @pl.when(step == 0)
    def _():
        s_ref[...] = jnp.zeros_like(s_ref)

    ri = lax.broadcasted_iota(jnp.int32, (chunk, chunk), 0)
    ci = lax.broadcasted_iota(jnp.int32, (chunk, chunk), 1)
    causal = ri >= ci
    tri = causal.astype(BF16)
    gg = gg_ref[...]

    gate = _dot(glr_ref[...].astype(BF16), wa_ref[...]) + ba_ref[...]
    log_a = jax.nn.log_sigmoid(gate) * (1.0 / GATE_TAU)
    log_hi = log_a.astype(BF16)
    log_lo = (log_a - log_hi.astype(F32)).astype(BF16)
    chunks = [slice(c * chunk, (c + 1) * chunk) for c in range(n_chunks)]
    cums = [_dot(tri, log_hi[rs, :]) + _dot(tri, log_lo[rs, :]) for rs in chunks]
    lasts = [cum[chunk - 1:chunk, :] for cum in cums]
    cum = jnp.concatenate(cums, axis=0)
    last = jnp.concatenate([jnp.broadcast_to(l, (chunk, GLA_DK)) for l in lasts], axis=0)
    grow = jnp.exp(cum)
    q_in = (q_ref[...].astype(F32) * (GLA_DK ** -0.5) * grow).astype(BF16)
    k = k_ref[...].astype(F32)
    k_in = (k * jnp.exp(-cum)).astype(BF16)
    k_out = (k * jnp.exp(last - cum)).astype(BF16)
    v = v_ref[...].astype(BF16)
    o_intra, kv, decay = [], [], []
    for rs, l in zip(chunks, lasts):
        att = jnp.where(causal, _dot_nt(q_in[rs, :], k_in[rs, :]), 0.0).astype(BF16)
        o_intra.append(_dot(att, v[rs, :]))
        kv.append(_dot_tn(k_out[rs, :], v[rs, :]))
        d = jnp.broadcast_to(jnp.exp(l), (GLA_DK, GLA_DK)).T
        decay.append(jnp.concatenate([d] * (GLA_DV // GLA_DK), axis=1))

    state = s_ref[...]
    for c, rs in enumerate(chunks):
        o = o_intra[c] + _dot(q_in[rs, :], state.astype(BF16))
        state = decay[c] * state + kv[c]
        y = _rms(o, gg) * jax.nn.silu(r_ref[rs, :].astype(F32))
        o_ref[rs, :] = y.astype(o_ref.dtype)
    s_ref[...] = state

    @pl.when(step == pl.num_programs(2) - 1)
    def _():
        sout_ref[...] = state


def _gla_decode_kernel(q_ref, k_ref, v_ref, r_ref, glr_ref, wa_ref, ba_ref, gg_ref, s0_ref, o_ref, sout_ref):
    n_rows = s0_ref.shape[0]
    rows = q_ref.shape[0]
    t_new = rows // n_rows
    ri = lax.broadcasted_iota(jnp.int32, (rows, rows), 0)
    ci = lax.broadcasted_iota(jnp.int32, (rows, rows), 1)
    same_row = _div_pow2(ri, t_new) == _div_pow2(ci, t_new)
    causal = same_row & (ri >= ci)
    gate = _dot(glr_ref[...].astype(BF16), wa_ref[...]) + ba_ref[...]
    log_a = jax.nn.log_sigmoid(gate) * (1.0 / GATE_TAU)
    log_hi = log_a.astype(BF16)
    log_parts = jnp.concatenate([log_hi, (log_a - log_hi.astype(F32)).astype(BF16)], axis=0)
    tri, blk = causal.astype(BF16), same_row.astype(BF16)
    cum = _dot(jnp.concatenate([tri, tri], axis=1), log_parts)
    total = _dot(jnp.concatenate([blk, blk], axis=1), log_parts)
    q_in = (q_ref[...].astype(F32) * (GLA_DK ** -0.5) * jnp.exp(cum)).astype(BF16)
    k = k_ref[...].astype(F32)
    k_in = (k * jnp.exp(-cum)).astype(BF16)
    k_out = (k * jnp.exp(total - cum)).astype(BF16)
    decay = jnp.exp(total)
    heads_per_block = COL_BLOCK // GLA_DV
    for h in range(GLA_HEADS):
        ks = slice(h * GLA_DK, (h + 1) * GLA_DK)
        os_ = slice(h * GLA_DV, (h + 1) * GLA_DV)
        vb, vs = h // heads_per_block, slice((h % heads_per_block) * GLA_DV, (h % heads_per_block + 1) * GLA_DV)
        v = v_ref[vb, :, vs].astype(BF16)
        att = jnp.where(causal, _dot_nt(q_in[:, ks], k_in[:, ks]), 0.0).astype(BF16)
        o_intra = _dot(att, v)
        for j in range(n_rows):
            tok = slice(j * t_new, (j + 1) * t_new)
            state = s0_ref[j, h].astype(F32)
            o = o_intra[tok, :] + _dot(q_in[tok, ks], state.astype(BF16))
            d = jnp.broadcast_to(decay[j * t_new:j * t_new + 1, ks], (GLA_DK, GLA_DK)).T
            sout_ref[j, h] = jnp.concatenate([d] * (GLA_DV // GLA_DK), axis=1) * state + _dot_tn(k_out[tok, ks], v[tok, :])
            y = _rms(o, gg_ref[:, os_]) * jax.nn.silu(r_ref[vb, tok, vs].astype(F32))
            o_ref[tok, os_] = y.astype(o_ref.dtype)


DECODE_ROWS_PER_STEP = 4


def _gla_decode(rest, wa, ba, gg, s0, layer, batch, t_new, out_dtype):
    n_rows = min(batch, DECODE_ROWS_PER_STEP)
    assert batch % n_rows == 0
    tm = n_rows * t_new

    def rows(block, n_blocks=None):
        if n_blocks is None:
            return pl.BlockSpec((None, tm, COL_BLOCK), lambda b: (block, b, 0))
        return pl.BlockSpec((n_blocks, tm, COL_BLOCK), lambda b: (block // n_blocks, b, 0))

    def full(a):
        return pl.BlockSpec(a.shape, lambda b: (0, 0))

    v_blocks = GLA_VAL // COL_BLOCK
    return pl.pallas_call(
        _gla_decode_kernel,
        grid=(batch // n_rows,),
        in_specs=[rows(R_QB), rows(R_KB), rows(R_VB, v_blocks), rows(R_RB, v_blocks),
                  pl.BlockSpec((None, tm, LANE), lambda b: (R_GLR, b, 0)),
                  full(wa), full(ba), full(gg),
                  pl.BlockSpec((None, n_rows, GLA_HEADS, GLA_DK, GLA_DV), lambda b: (layer, b, 0, 0, 0))],
        out_specs=[pl.BlockSpec((tm, GLA_VAL), lambda b: (b, 0)),
                   pl.BlockSpec((n_rows, GLA_HEADS, GLA_DK, GLA_DV), lambda b: (b, 0, 0, 0))],
        out_shape=[jax.ShapeDtypeStruct((batch * t_new, GLA_VAL), out_dtype),
                   jax.ShapeDtypeStruct((batch, GLA_HEADS, GLA_DK, GLA_DV), F32)],
        compiler_params=_cparams(("parallel",)),
        name="gla_decode",
    )(rest, rest, rest, rest, rest, wa, ba, gg, s0)


def _gla(rest, wa, ba, gg, batch, seq, chunk, n_chunks, out_dtype):
    n_rows = rest.shape[1]
    lblk = chunk * n_chunks
    steps = seq // lblk

    def pspec(width, block):
        per = COL_BLOCK // width
        return pl.BlockSpec((None, lblk, width), lambda b, h, n: (block + h // per, b * steps + n, h % per))

    in_specs = [
        pspec(GLA_DK, R_QB), pspec(GLA_DK, R_KB), pspec(GLA_DV, R_VB), pspec(GLA_DV, R_RB),
        pl.BlockSpec((None, lblk, LANE), lambda b, h, n: (R_GLR, b * steps + n, 0)),
        pl.BlockSpec((LANE, GLA_DK), lambda b, h, n: (0, h)),
        pl.BlockSpec((1, GLA_DK), lambda b, h, n: (0, h)),
        pl.BlockSpec((1, GLA_DV), lambda b, h, n: (0, h)),
    ]
    args = [rest, rest, rest, rest, rest, wa, ba, gg]
    return pl.pallas_call(
        functools.partial(_gla_kernel, chunk=chunk, n_chunks=n_chunks),
        grid=(batch, GLA_HEADS, steps),
        in_specs=in_specs,
        out_specs=[
            pl.BlockSpec((lblk, GLA_DV), lambda b, h, n: (b * steps + n, h)),
            pl.BlockSpec((None, None, GLA_DK, GLA_DV), lambda b, h, n: (b, h, 0, 0)),
        ],
        out_shape=[
            jax.ShapeDtypeStruct((n_rows, GLA_VAL), out_dtype),
            jax.ShapeDtypeStruct((batch, GLA_HEADS, GLA_DK, GLA_DV), F32),
        ],
        scratch_shapes=[pltpu.VMEM((GLA_DK, GLA_DV), F32)],
        compiler_params=_cparams(("parallel", "parallel", "arbitrary")),
        name="gla",
    )(*args)


def _merge_kernel(x_ref, o1_ref, o2_ref, o3_ref, m1_ref, m2_ref, m3_ref, l1_ref, l2_ref, l3_ref, ob_ref, ga_ref, gb_ref,
                  wpa_ref, wpb_ref, wo_ref, e_ref, out_ref, o_scr, l_scr, *, dils):
    tm = x_ref.shape[0]

    def token_order(ref, scr, dil):
        if dil == 1:
            return ref[0].astype(F32)
        chunks = ref.shape[2] // LANE
        for r in range(dil):
            for c in range(chunks):
                scr[c, pl.ds(r, tm // dil, stride=dil), :] = ref[r, :, c * LANE:(c + 1) * LANE].astype(F32)
        return jnp.concatenate([scr[c] for c in range(chunks)], axis=1)

    ms = [token_order(m_ref, l_scr, d) for m_ref, d in zip((m1_ref, m2_ref, m3_ref), dils)]
    ls = [token_order(l_ref, l_scr, d) for l_ref, d in zip((l1_ref, l2_ref, l3_ref), dils)]
    m = jnp.maximum(jnp.maximum(ms[0], ms[1]), ms[2])
    ws = [jnp.exp2(mg - m) for mg in ms]
    inv = 1.0 / (ws[0] * ls[0] + ws[1] * ls[1] + ws[2] * ls[2])
    expand = e_ref[...]
    o_a = jnp.zeros((tm, A_OUT), F32)
    for w, o_ref, d in zip(ws, (o1_ref, o2_ref, o3_ref), dils):
        o_a = o_a + _split_dot(w * inv, expand) * token_order(o_ref, o_scr, d)
    ya = _dot(o_a.astype(BF16), wpa_ref[...])
    yb = _dot(ob_ref[...].astype(BF16), wpb_ref[...])
    gate_a = jnp.concatenate([ga_ref[0], ga_ref[1]], axis=1).astype(F32)
    gate_b = jnp.concatenate([gb_ref[0], gb_ref[1]], axis=1).astype(F32)
    merged = jax.nn.sigmoid(gate_a) * ya + jax.nn.sigmoid(gate_b) * yb
    out_ref[...] = x_ref[...] + _dot(merged.astype(BF16), wo_ref[...])


def _merge(x, outs, maxes, dens, ob, rest, wpa, wpb, wo, expand, seq, tm):
    n = x.shape[0]
    tiles = seq // tm
    dils = tuple(o.shape[1] for o in outs)

    def rows(width):
        return pl.BlockSpec((tm, width), lambda i: (i, 0))

    def dilated(a):
        dil, width = a.shape[1], a.shape[3]
        return pl.BlockSpec((None, dil, tm // dil, width), lambda i: (i // tiles, 0, i % tiles, 0))

    def gate(block):
        return pl.BlockSpec((2, tm, COL_BLOCK), lambda i: (block // 2, i, 0))

    def full(a):
        return pl.BlockSpec(a.shape, lambda i: (0, 0))

    return pl.pallas_call(
        functools.partial(_merge_kernel, dils=dils),
        grid=(n // tm,),
        in_specs=[rows(D_MODEL)] + [dilated(a) for a in (*outs, *maxes, *dens)] + [rows(GLA_VAL)]
                 + [gate(R_GA), gate(R_GB)] + [full(wpa), full(wpb), full(wo), full(expand)],
        out_specs=rows(D_MODEL),
        out_shape=jax.ShapeDtypeStruct((n, D_MODEL), F32),
        scratch_shapes=[pltpu.VMEM((A_OUT // LANE, tm, LANE), F32), pltpu.VMEM((1, tm, LANE), F32)],
        compiler_params=_cparams(("parallel",)),
        name="merge",
    )(x, *outs, *maxes, *dens, ob, rest, rest, wpa, wpb, wo, expand)


def _ffn_kernel(x_ref, g_ref, gn_ref, wg_ref, wu_ref, wd_ref, o_ref, hn_ref):
    tm = x_ref.shape[0]
    chunk = min(tm, ROW_CHUNK)
    for rc in range(tm // chunk):
        rows = slice(rc * chunk, (rc + 1) * chunk)
        x = x_ref[rows, :]
        h = _rms(x, g_ref[...]).astype(BF16)
        mid = jax.nn.silu(_dot(h, wg_ref[...])) * _dot(h, wu_ref[...])
        y = x + _dot(mid.astype(BF16), wd_ref[...])
        o_ref[rows, :] = y
        hn_ref[rows, :] = _rms(y, gn_ref[...]).astype(hn_ref.dtype)


def _ffn(x, g, g_next, wg, wu, wd, tm):
    n = x.shape[0]

    def resident(a):
        return pl.BlockSpec(a.shape, lambda i: (0, 0), pipeline_mode=pl.Buffered(1))

    def rows():
        return pl.BlockSpec((tm, D_MODEL), lambda i: (i, 0))

    return pl.pallas_call(
        _ffn_kernel,
        grid=(n // tm,),
        in_specs=[rows(), resident(g), resident(g_next), resident(wg), resident(wu), resident(wd)],
        out_specs=[rows(), rows()],
        out_shape=[jax.ShapeDtypeStruct((n, D_MODEL), F32), jax.ShapeDtypeStruct((n, D_MODEL), BF16)],
        compiler_params=_cparams(("parallel",)),
        name="ffn",
    )(x, g, g_next, wg, wu, wd)


def _router_kernel(x_ref, g_ref, wr_ref, comb_ref, scol_ref, srow_ref, cnt_ref):
    h = _rms(x_ref[...], g_ref[...])
    w = wr_ref[...]
    h_hi, w_hi = h.astype(BF16), w.astype(BF16)
    h_lo, w_lo = (h - h_hi.astype(F32)).astype(BF16), (w - w_hi.astype(F32)).astype(BF16)
    logits = _dot(h_hi, w_hi) + (_dot(h_hi, w_lo) + _dot(h_lo, w_hi))
    lane = lax.broadcasted_iota(jnp.int32, logits.shape, 1)
    logits = jnp.where(lane < N_EXPERTS, logits, -jnp.inf)
    v1 = jnp.max(logits, axis=-1, keepdims=True)
    i1 = jnp.min(jnp.where(logits == v1, lane, LANE), axis=-1, keepdims=True)
    rest = jnp.where(lane == i1, -jnp.inf, logits)
    v2 = jnp.max(rest, axis=-1, keepdims=True)
    i2 = jnp.min(jnp.where(rest == v2, lane, LANE), axis=-1, keepdims=True)
    e2 = jnp.exp(v2 - v1)
    g1 = 1.0 / (1.0 + e2)
    g2 = e2 / (1.0 + e2)
    comb = jnp.where(lane == i1, g1, 0.0) + jnp.where(lane == i2, g2, 0.0)
    comb_ref[...] = comb

    tile = comb.shape[0]
    routed = comb > 0.0
    earlier = lax.broadcasted_iota(jnp.int32, (tile, tile), 1) < lax.broadcasted_iota(jnp.int32, (tile, tile), 0)
    rank = _dot(earlier.astype(BF16), routed.astype(BF16))
    slot = jnp.where(routed, rank, -1.0)
    scol_ref[...] = slot
    srow_ref[...] = slot.T[0:N_EXPERTS, :]
    count = jnp.sum(routed.astype(F32), axis=0, keepdims=True)
    cnt_ref[...] = jnp.broadcast_to(count, cnt_ref.shape).astype(jnp.int32)


def _router(x, g, wr, tile):
    n = x.shape[0]
    tiles = n // tile
    comb, slot_col, slot_row, counts = pl.pallas_call(
        _router_kernel,
        grid=(tiles,),
        in_specs=[
            pl.BlockSpec((tile, D_MODEL), lambda i: (i, 0)),
            pl.BlockSpec((1, D_MODEL), lambda i: (0, 0)),
            pl.BlockSpec((D_MODEL, LANE), lambda i: (0, 0)),
        ],
        out_specs=[
            pl.BlockSpec((tile, LANE), lambda i: (i, 0)),
            pl.BlockSpec((tile, LANE), lambda i: (i, 0)),
            pl.BlockSpec((None, N_EXPERTS, tile), lambda i: (i, 0, 0)),
            pl.BlockSpec((None, 8, LANE), lambda i: (i, 0, 0)),
        ],
        out_shape=[
            jax.ShapeDtypeStruct((n, LANE), F32),
            jax.ShapeDtypeStruct((n, LANE), F32),
            jax.ShapeDtypeStruct((tiles, N_EXPERTS, tile), F32),
            jax.ShapeDtypeStruct((tiles, 8, LANE), jnp.int32),
        ],
        compiler_params=_cparams(("parallel",)),
        name="router",
    )(x, g, wr)
    return (comb, counts[:, 0, :N_EXPERTS].reshape(tiles * N_EXPERTS), slot_col,
            slot_row.reshape(tiles * N_EXPERTS, 1, tile))


def _moe_kernel(cnt_ref, x_ref, g_ref, gn_ref, comb_ref, scol_ref, srow_ref, wg_ref, wu_ref, wd_ref, o_ref,
                h_ref, xc_ref, y_ref, *, sub, n_sub):
    i, e, f = pl.program_id(0), pl.program_id(1), pl.program_id(2)
    last_f = pl.num_programs(2) - 1
    tile = x_ref.shape[0]
    sub_pad = y_ref.shape[1]
    count = cnt_ref[i * N_EXPERTS + e]

    @pl.when((e == 0) & (f == 0))
    def _():
        h_ref[...] = _rms(x_ref[...], g_ref[...]).astype(BF16)
        o_ref[...] = x_ref[...]

    for s in range(n_sub):
        @pl.when((f == 0) & (s * sub < count))
        def _():
            slots = (lax.broadcasted_iota(jnp.int32, (sub, tile), 0) + s * sub).astype(F32)
            pick = (srow_ref[...] == slots).astype(BF16)
            xc_ref[s] = _dot(pick, h_ref[...]).astype(BF16)
            y_ref[s] = jnp.zeros((sub_pad, D_MODEL), F32)

        @pl.when(s * sub < count)
        def _():
            xs = xc_ref[s]
            mid = jax.nn.silu(_dot(xs, wg_ref[...])) * _dot(xs, wu_ref[...])
            y_ref[s, 0:sub, :] += _dot(mid.astype(BF16), wd_ref[...])

        @pl.when((f == last_f) & (s * sub < count))
        def _():
            mine = lax.broadcasted_iota(jnp.int32, (tile, LANE), 1) == e
            slot = jnp.sum(jnp.where(mine, scol_ref[...], 0.0), axis=-1, keepdims=True)
            gate = jnp.sum(jnp.where(mine, comb_ref[...], 0.0), axis=-1, keepdims=True)
            slots = (lax.broadcasted_iota(jnp.int32, (tile, sub_pad), 1) + s * sub).astype(F32)
            place = (slot == slots).astype(BF16)
            o_ref[...] += gate * _dot(place, y_ref[s].astype(BF16))

    @pl.when((e == pl.num_programs(1) - 1) & (f == last_f))
    def _():
        o_ref[...] = _rms(o_ref[...], gn_ref[...])


def _moe(x, g, g_out, comb, counts, slot_col, slot_row, wg, wu, wd, sub, tf):
    n = x.shape[0]
    tile = slot_row.shape[-1]
    n_sub = -(-tile // sub)
    sub_pad = -(-sub // LANE) * LANE
    grid_spec = pltpu.PrefetchScalarGridSpec(
        num_scalar_prefetch=1,
        grid=(n // tile, N_EXPERTS, D_FF_EXPERT // tf),
        in_specs=[
            pl.BlockSpec((tile, D_MODEL), lambda i, e, f, c: (i, 0)),
            pl.BlockSpec((1, D_MODEL), lambda i, e, f, c: (0, 0)),
            pl.BlockSpec((1, D_MODEL), lambda i, e, f, c: (0, 0)),
            pl.BlockSpec((tile, LANE), lambda i, e, f, c: (i, 0)),
            pl.BlockSpec((tile, LANE), lambda i, e, f, c: (i, 0)),
            pl.BlockSpec((None, 1, tile), lambda i, e, f, c: (i * N_EXPERTS + e, 0, 0)),
            pl.BlockSpec((None, D_MODEL, tf), lambda i, e, f, c: (e, 0, f)),
            pl.BlockSpec((None, D_MODEL, tf), lambda i, e, f, c: (e, 0, f)),
            pl.BlockSpec((None, tf, D_MODEL), lambda i, e, f, c: (e, f, 0)),
        ],
        out_specs=pl.BlockSpec((tile, D_MODEL), lambda i, e, f, c: (i, 0)),
        scratch_shapes=[
            pltpu.VMEM((tile, D_MODEL), BF16),
            pltpu.VMEM((n_sub, sub, D_MODEL), BF16),
            pltpu.VMEM((n_sub, sub_pad, D_MODEL), F32),
        ],
    )
    return pl.pallas_call(
        functools.partial(_moe_kernel, sub=sub, n_sub=n_sub),
        grid_spec=grid_spec,
        out_shape=jax.ShapeDtypeStruct((n, D_MODEL), F32),
        compiler_params=_cparams(("parallel", "arbitrary", "arbitrary")),
        name="moe",
    )(counts, x, g, g_out, comb, slot_col, slot_row, wg, wu, wd)


def _norm_kernel(x_ref, g_ref, o_ref):
    o_ref[...] = _rms(x_ref[...], g_ref[...]).astype(o_ref.dtype)


def _norm(x, g, tm, out_dtype):
    n = x.shape[0]
    return pl.pallas_call(
        _norm_kernel,
        grid=(n // tm,),
        in_specs=[pl.BlockSpec((tm, D_MODEL), lambda i: (i, 0)), pl.BlockSpec((1, D_MODEL), lambda i: (0, 0))],
        out_specs=pl.BlockSpec((tm, D_MODEL), lambda i: (i, 0)),
        out_shape=jax.ShapeDtypeStruct((n, D_MODEL), out_dtype),
        compiler_params=_cparams(("parallel",)),
        name="norm",
    )(x, g)


def _rope_tables(pos, split_half):
    half = HEAD_DIM // 2
    inv = ROPE_THETA ** (-jnp.arange(half, dtype=F32) / half)
    ang = pos.astype(F32)[:, None] * inv[None, :]
    cos = jnp.cos(ang)
    sin = jnp.sin(ang)
    if split_half:
        return jnp.tile(cos, (1, LANE // half)), jnp.tile(sin, (1, LANE // half))
    return (jnp.tile(jnp.concatenate([cos, cos], axis=-1), (1, LANE // HEAD_DIM)),
            jnp.tile(jnp.concatenate([-sin, sin], axis=-1), (1, LANE // HEAD_DIM)))


def _to_split_half(a):
    lead = a.shape[:-1]
    a = a.reshape(*lead, HEADS_PER_GROUP // QUAD, QUAD, 2, HEAD_DIM // 2)
    return jnp.swapaxes(a, -2, -3).reshape(*lead, A_OUT)


def _layer_weights(l, w_in, w_gate_a2, b_gate_a, g_gla, w_branch_a, w_branch_b, w_out):
    w = w_in[l]
    offs = np.cumsum((A_WIDTH, A_WIDTH, A_WIDTH, GLA_KEY, GLA_KEY, GLA_VAL, GLA_VAL, GATE_RANK, D_MODEL, D_MODEL))
    qa, ka, va, qb, kb, vb, rb, glr, ga, gb = jnp.split(w, [int(o) for o in offs[:-1]], axis=1)
    def attn_weights(split_half):
        relay = _to_split_half if split_half else (lambda a: a)
        return jnp.stack([jnp.stack([relay(qa[:, g * A_OUT:(g + 1) * A_OUT]), relay(ka[:, g * A_OUT:(g + 1) * A_OUT]),
                                     va[:, g * A_OUT:(g + 1) * A_OUT]]) for g in range(N_GROUPS)]).astype(BF16)

    w_attn = {False: attn_weights(False), True: attn_weights(True)}
    glr = jnp.pad(glr, ((0, 0), (0, COL_BLOCK - GATE_RANK)))
    rest = jnp.concatenate([qb, kb, vb, rb, ga, gb, glr], axis=1)
    w_rest = jnp.swapaxes(rest.reshape(D_MODEL, N_REST, COL_BLOCK), 0, 1).astype(BF16)
    wa = jnp.pad(w_gate_a2[l], ((0, LANE - GATE_RANK), (0, 0))).astype(BF16)
    return dict(w_attn=w_attn, w_rest=w_rest, wa=wa, ba=b_gate_a[l][None, :], gg=g_gla[l][None, :],
                wpa=w_branch_a[l].astype(BF16), wpb=w_branch_b[l].astype(BF16), wo=w_out[l].astype(BF16))


def _head_expand():
    e = np.zeros((LANE, A_OUT), np.float32)
    for h in range(HEADS_PER_GROUP):
        e[h, h * HEAD_DIM:(h + 1) * HEAD_DIM] = 1.0
    return jnp.asarray(np.concatenate([e, e], axis=0), BF16)


def _trunk(x, rope, caches, gla_state, lw, ffw, g_mix, g_ffn, g_final, batch, seq, tm, tm_proj, moe_tile, moe_sub,
           p_dtype):
    cos_t, sin_t = rope
    expand = _head_expand()
    prompt = caches is None
    cache_out = [None] * N_GROUPS
    states_out = []
    assert DEPTH % 2 == 0
    h = _norm(x, g_mix[0][None, :], tm, BF16)
    for l in range(DEPTH):
        w = lw[l]
        rest = _inproj_rest(h, w["w_rest"], tm_proj, p_dtype)
        outs, maxes, dens = [], [], []
        for g, (window, dil) in enumerate(DSWA_GROUPS):
            if prompt:
                qkv = _inproj_attn(h, w["w_attn"][True], cos_t, sin_t, g, batch, seq, dil, tm_proj, p_dtype, True)
                o, mx, den = _attn_prompt(qkv, g)
                cache_out[g] = _kv_tail(qkv, cache_out[g], l, min(window, seq))
            else:
                qkv = _inproj_attn(h, w["w_attn"][False], cos_t, sin_t, g, 1, batch * seq, 1, tm_proj, p_dtype, False)
                o, lse, cache_out[g] = _attn_sample(qkv.reshape(3, batch * seq, COL_BLOCK), caches[g], cache_out[g],
                                                    l, g, seq)
                o = o.reshape(1, 1, batch * seq, A_OUT)
                mx = lse.reshape(1, 1, batch * seq, LANE)
                den = jnp.ones_like(mx)
            outs.append(o)
            maxes.append(mx)
            dens.append(den)
        if prompt:
            ob, s_new = _gla(rest, w["wa"], w["ba"], w["gg"], batch, seq, GLA_CHUNK, GLA_CHUNKS_PER_STEP, BF16)
            x = _merge(x, outs, maxes, dens, ob, rest, w["wpa"], w["wpb"], w["wo"], expand, seq, min(tm, 512))
        else:
            ob, s_new = _gla_decode(rest, w["wa"], w["ba"], w["gg"], gla_state, l, batch, seq, F32)
            x = _merge(x, outs, maxes, dens, ob, rest, w["wpa"], w["wpb"], w["wo"], expand, batch * seq, tm)
        states_out.append(s_new)
        i = l // 2
        if l % 2 == 0:
            x, h = _ffn(x, g_ffn[l][None, :], g_mix[l + 1][None, :], ffw["wg"][i], ffw["wu"][i], ffw["wd"][i], tm)
        else:
            routing = _router(x, g_ffn[l][None, :], ffw["wr"][i], moe_tile)
            g_out = g_final if l == DEPTH - 1 else g_mix[l + 1]
            x = _moe(x, g_ffn[l][None, :], g_out[None, :], *routing, ffw["eg"][i], ffw["eu"][i], ffw["ed"][i],
                     moe_sub, D_FF_EXPERT // 2)
            assert l == DEPTH - 1
    y = x
    bufs = [jnp.transpose(c.reshape(DEPTH, batch, 2, HEADS_PER_GROUP, HEAD_DIM, c.shape[-1]), (0, 1, 5, 2, 3, 4))
            for c in cache_out]
    return (y, bufs[0], bufs[1], bufs[2], jnp.stack(states_out))


def kernel(x_prompt, x_sample, cache_kv_w128, cache_kv_w512, cache_kv_w2048, state_gla, g_mix, w_in, w_gate_a2, b_gate_a, g_gla, w_branch_a, w_branch_b, w_out, g_ffn, w_ff_gate, w_ff_up, w_ff_down, w_router, w_exp_gate, w_exp_up, w_exp_down, g_final):
    batch, seq, _ = x_prompt.shape
    dec_batch, dec_seq, _ = x_sample.shape
    assert seq % (DSWA_GROUPS[-1][1] * Q_BLOCK) == 0 and seq % (GLA_CHUNKS_PER_STEP * GLA_CHUNK) == 0
    assert dec_seq % GLA_CHUNK != 0 and dec_seq % 8 == 0
    assert all(c.shape[2] == w for c, (w, _) in zip((cache_kv_w128, cache_kv_w512, cache_kv_w2048), DSWA_GROUPS))

    lw = [_layer_weights(l, w_in, w_gate_a2, b_gate_a, g_gla, w_branch_a, w_branch_b, w_out) for l in range(DEPTH)]
    ffw = dict(wg=w_ff_gate.astype(BF16), wu=w_ff_up.astype(BF16), wd=w_ff_down.astype(BF16),
               wr=jnp.pad(w_router, ((0, 0), (0, 0), (0, LANE - N_EXPERTS))),
               eg=w_exp_gate.astype(BF16), eu=w_exp_up.astype(BF16), ed=w_exp_down.astype(BF16))

    y_p, kv128_p, kv512_p, kv2048_p, gla_p = _trunk(
        x_prompt.reshape(batch * seq, D_MODEL), _rope_tables(jnp.arange(seq, dtype=jnp.int32), True), None, None,
        lw, ffw, g_mix, g_ffn, g_final, batch, seq, 1024, 2048, 1024, 288, BF16)

    n_s = dec_batch * dec_seq
    pos_s = jnp.tile(PAST_LEN + jnp.arange(dec_seq, dtype=jnp.int32), dec_batch)
    caches = [jnp.transpose(c, (0, 1, 3, 4, 5, 2)).reshape(DEPTH, dec_batch, 2, A_OUT, c.shape[2])
              for c in (cache_kv_w128, cache_kv_w512, cache_kv_w2048)]
    y_s, kv128_s, kv512_s, kv2048_s, gla_s = _trunk(
        x_sample.reshape(n_s, D_MODEL), _rope_tables(pos_s, False), caches, state_gla,
        lw, ffw, g_mix, g_ffn, g_final, dec_batch, dec_seq, n_s, n_s, n_s, 96, F32)

    return (y_p.reshape(batch, seq, D_MODEL), y_s.reshape(dec_batch, dec_seq, D_MODEL),
            kv128_p, kv512_p, kv2048_p, gla_p, kv128_s, kv512_s, kv2048_s, gla_s)
```

```python
import functools

import jax
import jax.numpy as jnp
import numpy as np
from jax import lax
from jax.experimental import pallas as pl
from jax.experimental.pallas import tpu as pltpu

F32 = jnp.float32
BF16 = jnp.bfloat16

D_MODEL = 1024
DEPTH = 2
PAST_LEN = 16384
HEAD_DIM = 64
DSWA_GROUPS = ((128, 1), (512, 4), (2048, 16))
N_GROUPS = 3
HEADS_PER_GROUP = 8
A_WIDTH = N_GROUPS * HEADS_PER_GROUP * HEAD_DIM
A_OUT = HEADS_PER_GROUP * HEAD_DIM
Q_BLOCK = 128
GLA_HEADS = 4
GLA_KEY = 512
GLA_VAL = 1024
GLA_DK = 128
GLA_DV = 256
GATE_RANK = 16
GATE_TAU = 16.0
GLA_CHUNK = 64
D_FF = 2816
N_EXPERTS = 8
D_FF_EXPERT = 3584
ROPE_THETA = 10000.0
EPS = 1e-6
LOG2_E = 1.4426950408889634

LANE = 128
COL_BLOCK = 512
R_QB, R_KB, R_VB, R_RB, R_GA, R_GB, N_REST = 0, 1, 2, 4, 6, 8, 10
ROW_CHUNK = 256
GLA_CHUNKS_PER_STEP = 16
VMEM_LIMIT = 56 * 1024 * 1024


def _cparams(sem):
    return pltpu.CompilerParams(dimension_semantics=sem, vmem_limit_bytes=VMEM_LIMIT)


def _rms(xf, g):
    return xf * lax.rsqrt(jnp.mean(xf * xf, axis=-1, keepdims=True) + EPS) * g


def _dot(a, b):
    return jnp.dot(a, b, preferred_element_type=F32)


def _dot_nt(a, b):
    return lax.dot_general(a, b, (((1,), (1,)), ((), ())), preferred_element_type=F32)


def _dot_tn(a, b):
    return lax.dot_general(a, b, (((0,), (0,)), ((), ())), preferred_element_type=F32)


def _div_pow2(x, d):
    assert d & (d - 1) == 0
    return lax.shift_right_logical(x, int(d).bit_length() - 1)


def _mod_pow2(x, d):
    assert d & (d - 1) == 0
    return x & (d - 1)


def _split_dot(a_f32, b_twice):
    hi = a_f32.astype(BF16)
    lo = (a_f32 - hi.astype(F32)).astype(BF16)
    return _dot(jnp.concatenate([hi, lo], axis=1), b_twice)


def _inproj_attn_kernel(h_ref, w_ref, cos_ref, sin_ref, *refs, dils, split_half, q_scale):
    step = pl.program_id(1)
    outs, stages = refs[:len(dils)], list(refs[len(dils):])
    for g, dil in enumerate(dils):
        stage = stages.pop(0) if dil > 1 else None

        @pl.when((step >= 3 * g) & (step < 3 * g + 3))
        def _(g=g, dil=dil, stage=stage):
            _project_rotate(h_ref, w_ref, cos_ref, sin_ref, outs[g], stage, step - 3 * g, dil, split_half, q_scale)


def _project_rotate(h_ref, w_ref, cos_ref, sin_ref, o_ref, stage, part, dil, split_half, q_scale):
    scratch = [stage]
    scale = jnp.where(part == 0, q_scale, 1.0).astype(F32)
    tm = h_ref.shape[0]
    chunk = min(tm, ROW_CHUNK)
    n_lane_groups = COL_BLOCK // LANE
    for rc in range(tm // chunk):
        rows = slice(rc * chunk, (rc + 1) * chunk)
        acc = _dot(h_ref[rows, :], w_ref[...])
        cos = jnp.where(part == 2, 1.0, cos_ref[rows, :] * scale)
        sin = jnp.where(part == 2, 0.0, sin_ref[rows, :] * scale)
        ys = []
        if split_half:
            for c in range(0, n_lane_groups, 2):
                a = acc[:, c * LANE:(c + 1) * LANE]
                b = acc[:, (c + 1) * LANE:(c + 2) * LANE]
                ys += [a * cos - b * sin, a * sin + b * cos]
        else:
            first_half = _mod_pow2(lax.broadcasted_iota(jnp.int32, cos.shape, 1), HEAD_DIM) < HEAD_DIM // 2
            for c in range(n_lane_groups):
                xs = acc[:, c * LANE:(c + 1) * LANE]
                swapped = jnp.where(first_half, pltpu.roll(xs, LANE - HEAD_DIM // 2, 1), pltpu.roll(xs, HEAD_DIM // 2, 1))
                ys.append(xs * cos + swapped * sin)
        if dil == 1:
            for c, y in enumerate(ys):
                o_ref[0, rows, c * LANE:(c + 1) * LANE] = y.astype(o_ref.dtype)
        else:
            sub = chunk // dil
            pitch = _residue_pitch(dil)
            base = rc * sub * pitch
            for c, y in enumerate(ys):
                if pitch == dil:
                    scratch[0][c, rows, :] = y
                else:
                    for u in range(sub):
                        scratch[0][c, base + u * pitch:base + u * pitch + dil, :] = y[u * dil:(u + 1) * dil, :]
            for r in range(dil):
                for c in range(n_lane_groups):
                    o_ref[r, rc * sub:(rc + 1) * sub, c * LANE:(c + 1) * LANE] = (
                        scratch[0][c, pl.ds(base + r, sub, stride=pitch), :].astype(o_ref.dtype))


def _residue_pitch(dil):
    return dil + 8 if dil % 16 == 0 else dil


def _inproj_attn(h, w, cos_t, sin_t, batch, seq, dils, tm, out_dtype, split_half):
    tiles = seq // tm
    scratch = [pltpu.VMEM((COL_BLOCK // LANE, tm // dil * _residue_pitch(dil), LANE), F32) for dil in dils if dil > 1]
    q_scale = HEAD_DIM ** -0.5 * (LOG2_E if split_half else 1.0)

    def out_spec(g, dil):
        return pl.BlockSpec((None, None, dil, tm // dil, COL_BLOCK),
                            lambda i, j: (jnp.clip(j - 3 * g, 0, 2), i // tiles, 0, i % tiles, 0))

    return pl.pallas_call(
        functools.partial(_inproj_attn_kernel, dils=tuple(dils), split_half=split_half, q_scale=q_scale),
        grid=(batch * tiles, 3 * len(dils)),
        in_specs=[
            pl.BlockSpec((tm, D_MODEL), lambda i, j: (i, 0)),
            pl.BlockSpec((None, None, D_MODEL, COL_BLOCK), lambda i, j: (j // 3, j % 3, 0, 0)),
            pl.BlockSpec((tm, LANE), lambda i, j: (i % tiles, 0)),
            pl.BlockSpec((tm, LANE), lambda i, j: (i % tiles, 0)),
        ],
        out_specs=[out_spec(g, dil) for g, dil in enumerate(dils)],
        out_shape=[jax.ShapeDtypeStruct((3, batch, dil, seq // dil, COL_BLOCK), out_dtype) for dil in dils],
        scratch_shapes=scratch,
        compiler_params=_cparams(("parallel", "arbitrary")),
        name="inproj_attn",
    )(h, w, cos_t, sin_t)


def _inproj_rest_kernel(h_ref, w_ref, wlr_ref, o_ref, lr_ref):
    tm = h_ref.shape[0]
    chunk = min(tm, ROW_CHUNK)

    @pl.when(pl.program_id(1) == 0)
    def _():
        for rc in range(tm // chunk):
            rows = slice(rc * chunk, (rc + 1) * chunk)
            lr_ref[rows, :] = _dot(h_ref[rows, :], wlr_ref[...]).astype(lr_ref.dtype)

    for rc in range(tm // chunk):
        rows = slice(rc * chunk, (rc + 1) * chunk)
        o_ref[rows, :] = _dot(h_ref[rows, :], w_ref[...]).astype(o_ref.dtype)


def _inproj_rest(h, w, w_lowrank, tm, out_dtype):
    n = h.shape[0]
    return pl.pallas_call(
        _inproj_rest_kernel,
        grid=(n // tm, N_REST),
        in_specs=[
            pl.BlockSpec((tm, D_MODEL), lambda i, j: (i, 0)),
            pl.BlockSpec((None, D_MODEL, COL_BLOCK), lambda i, j: (j, 0, 0)),
            pl.BlockSpec((D_MODEL, LANE), lambda i, j: (0, 0)),
        ],
        out_specs=[pl.BlockSpec((None, tm, COL_BLOCK), lambda i, j: (j, i, 0)),
                   pl.BlockSpec((tm, LANE), lambda i, j: (i, 0))],
        out_shape=[jax.ShapeDtypeStruct((N_REST, n, COL_BLOCK), out_dtype),
                   jax.ShapeDtypeStruct((n, LANE), out_dtype)],
        compiler_params=_cparams(("parallel", "arbitrary")),
        name="inproj_rest",
    )(h, w, w_lowrank)


QUAD = 4
QUAD_WIDTH = QUAD * HEAD_DIM
MAX_Q_BLOCKS_PER_STEP = 8


def _attn_prompt_kernel(bias_ref, *refs):
    n = pl.program_id(2)
    for r in range(refs[0].shape[0]):
        _attn_prompt_subsequence(n, bias_ref, *(ref.at[r] for ref in refs))


def _attn_prompt_subsequence(n, bias_ref, q_ref, kp_ref, kc_ref, vp_ref, vc_ref, o_ref, m_ref, l_ref):
    rows = QUAD * Q_BLOCK
    row_head = _div_pow2(lax.broadcasted_iota(jnp.int32, (rows, QUAD_WIDTH), 0), Q_BLOCK)
    col = lax.broadcasted_iota(jnp.int32, (rows, QUAD_WIDTH), 1)
    q_lanes = _div_pow2(_mod_pow2(col, LANE), HEAD_DIM // 2) == row_head
    v_lanes = _div_pow2(col, HEAD_DIM) == row_head
    lane = lax.broadcasted_iota(jnp.int32, (Q_BLOCK, LANE), 1)
    for qb in range(q_ref.shape[0] // Q_BLOCK):
        cur = slice(qb * Q_BLOCK, (qb + 1) * Q_BLOCK)
        if qb == 0:
            bias = bias_ref[jnp.where(n == 0, 1, 0)]
        else:
            bias = bias_ref[0]
            prev = slice((qb - 1) * Q_BLOCK, qb * Q_BLOCK)
        m_all = jnp.zeros((Q_BLOCK, LANE), F32)
        l_all = jnp.ones((Q_BLOCK, LANE), F32)
        for c in range(A_OUT // QUAD_WIDTH):
            cs = slice(c * QUAD_WIDTH, (c + 1) * QUAD_WIDTH)
            q = q_ref[cur, cs].astype(F32)
            q4 = jnp.where(q_lanes, jnp.concatenate([q] * QUAD, axis=0), 0.0).astype(BF16)
            k_prev = kp_ref[:, cs] if qb == 0 else kc_ref[prev, cs]
            v_prev = vp_ref[:, cs] if qb == 0 else vc_ref[prev, cs]
            k = jnp.concatenate([k_prev, kc_ref[cur, cs]], axis=0)
            v = jnp.concatenate([v_prev, vc_ref[cur, cs]], axis=0)
            s = _dot_nt(q4, k) + bias
            m = jnp.max(s, axis=-1, keepdims=True)
            p = jnp.exp2(s - m)
            l = jnp.sum(p, axis=-1, keepdims=True)
            o4 = jnp.where(v_lanes, _dot(p.astype(BF16), v), 0.0)
            o = o4[0:Q_BLOCK]
            for j in range(QUAD):
                hs = slice(j * Q_BLOCK, (j + 1) * Q_BLOCK)
                if j > 0:
                    o = o + o4[hs]
                m_all = jnp.where(lane == c * QUAD + j, m[hs], m_all)
                l_all = jnp.where(lane == c * QUAD + j, l[hs], l_all)
            o_ref[cur, cs] = o.astype(o_ref.dtype)
        m_ref[cur, :] = m_all
        l_ref[cur, :] = l_all


def _band_bias(span):
    qi = np.arange(QUAD * Q_BLOCK)[:, None] % Q_BLOCK
    kj = np.arange(2 * Q_BLOCK)[None, :]
    dist = Q_BLOCK + qi - kj
    band = (dist >= 0) & (dist <= span)
    first = band & (kj >= Q_BLOCK)
    return jnp.asarray(np.where(np.stack([band, first]), 0.0, -np.inf), F32)


def _attn_prompt(qkv, group):
    window, dil = DSWA_GROUPS[group]
    _, batch, _, sub_len, _ = qkv.shape
    q_blocks = min(MAX_Q_BLOCKS_PER_STEP, sub_len // Q_BLOCK)
    step_rows = q_blocks * Q_BLOCK
    nb = sub_len // step_rows
    res = min(dil, MAX_Q_BLOCKS_PER_STEP // q_blocks)

    def spec(part, prev):
        if prev:
            return pl.BlockSpec((None, None, res, Q_BLOCK, COL_BLOCK),
                                lambda b, r, n: (part, b, r, jnp.maximum(n * q_blocks - 1, 0), 0))
        return pl.BlockSpec((None, None, res, step_rows, COL_BLOCK), lambda b, r, n: (part, b, r, n, 0))

    bias = _band_bias(window // dil)
    stat_spec = pl.BlockSpec((None, res, step_rows, LANE), lambda b, r, n: (b, r, n, 0))
    stat_shape = jax.ShapeDtypeStruct((batch, dil, sub_len, LANE), F32)
    return pl.pallas_call(
        _attn_prompt_kernel,
        grid=(batch, dil // res, nb),
        in_specs=[pl.BlockSpec(bias.shape, lambda b, r, n: (0, 0, 0)),
                  spec(0, False), spec(1, True), spec(1, False), spec(2, True), spec(2, False)],
        out_specs=[pl.BlockSpec((None, res, step_rows, A_OUT), lambda b, r, n: (b, r, n, 0)), stat_spec, stat_spec],
        out_shape=[jax.ShapeDtypeStruct((batch, dil, sub_len, A_OUT), BF16), stat_shape, stat_shape],
        compiler_params=_cparams(("parallel", "parallel", "arbitrary")),
        name=f"attn_prompt_g{group}",
    )(bias, qkv, qkv, qkv, qkv, qkv)


def _kv_tail_kernel(*refs, dil, aliased):
    if aliased:
        k_ref, v_ref, _, out_ref, scr = refs
        _kv_tail_body(k_ref, v_ref, out_ref, scr, dil)
        return
    k_ref, v_ref, out_ref, scr = refs

    @pl.when(pl.program_id(0) == 0)
    def _():
        _kv_tail_body(k_ref, v_ref, out_ref, scr, dil)

    @pl.when(pl.program_id(0) > 0)
    def _():
        out_ref[...] = jnp.zeros_like(out_ref)


def _kv_tail_body(k_ref, v_ref, out_ref, scr, dil):
    keep = out_ref.shape[2]
    sub = keep // dil
    half_w = HEAD_DIM // 2
    for kv, ref in enumerate((k_ref, v_ref)):
        for c in range(COL_BLOCK // LANE):
            cs = slice(c * LANE, (c + 1) * LANE)
            if dil == 1:
                tok = ref[0, :, cs].astype(F32)
            else:
                for r in range(dil):
                    scr[pl.ds(r, sub, stride=dil), :] = ref[r, :, cs].astype(F32)
                tok = scr[...]
            chan = tok.T
            if kv == 1:
                out_ref[kv, cs, :] = chan
            else:
                for j in range(QUAD):
                    dst = (QUAD * (c // 2) + j) * HEAD_DIM + (c % 2) * half_w
                    out_ref[kv, dst:dst + half_w, :] = chan[j * half_w:(j + 1) * half_w, :]


def _kv_tail(qkv, prev_out, layer, keep):
    _, batch, dil, sub_len, _ = qkv.shape
    last = sub_len // (keep // dil) - 1
    first = prev_out is None
    assert first == (layer == 0)
    phases = DEPTH if first else 1

    def spec(part):
        return pl.BlockSpec((None, None, dil, keep // dil, COL_BLOCK),
                            lambda p, b: (part, jnp.where(p == 0, b, batch - 1), 0, last, 0))

    in_specs = [spec(1), spec(2)]
    args = [qkv, qkv]
    aliases = {}
    if not first:
        in_specs.append(pl.BlockSpec(memory_space=pl.ANY))
        args.append(prev_out)
        aliases = {2: 0}
    return pl.pallas_call(
        functools.partial(_kv_tail_kernel, dil=dil, aliased=not first),
        grid=(phases, batch),
        in_specs=in_specs,
        out_specs=pl.BlockSpec((None, None, 2, A_OUT, keep), lambda p, b: (layer + p, b, 0, 0, 0)),
        out_shape=jax.ShapeDtypeStruct((DEPTH, batch, 2, A_OUT, keep), F32),
        scratch_shapes=[pltpu.VMEM((keep, LANE), F32)],
        input_output_aliases=aliases,
        compiler_params=_cparams(("arbitrary", "arbitrary")),
        name=f"kv_tail_{keep}",
    )(*args)


SHIFT_ROWS = 128
SAMPLE_BLOCK_BYTES = 8 * 1024 * 1024


def _attn_sample_kernel(*refs, dil, span, first):
    if first:
        q_ref, k_ref, v_ref, c_ref, o_ref, lse_ref, cout_ref = refs
    else:
        q_ref, k_ref, v_ref, c_ref, _, o_ref, lse_ref, cout_ref = refs
    n_rows = c_ref.shape[0]
    t_new = q_ref.shape[0] // n_rows

    def update(j):
        tok = pl.ds(j * t_new, t_new)
        q, k, v = q_ref.at[tok], k_ref.at[tok], v_ref.at[tok]
        _sample_attention(q, k, v, c_ref.at[j], o_ref.at[tok], lse_ref.at[tok], dil, span)
        _append_new(c_ref.at[j], k, v, cout_ref.at[j])

    def shift_only(j):
        for kv in range(2):
            for rb in range(A_OUT // SHIFT_ROWS):
                rs = slice(rb * SHIFT_ROWS, (rb + 1) * SHIFT_ROWS)
                cout_ref[j, kv, rs, :] = pltpu.roll(c_ref[j, kv, rs, :], c_ref.shape[3] - t_new, 1)

    if first:
        @pl.when(pl.program_id(0) == 0)
        def _():
            for j in range(n_rows):
                update(j)

        @pl.when(pl.program_id(0) > 0)
        def _():
            for j in range(n_rows):
                shift_only(j)
    else:
        for j in range(n_rows):
            update(j)


def _append_new(c_ref, k_ref, v_ref, cout_ref):
    t_new = k_ref.shape[0]
    lb = c_ref.shape[2]
    width = cout_ref.shape[2]
    lane = lax.broadcasted_iota(jnp.int32, (SHIFT_ROWS, LANE), 1)
    zeros = jnp.zeros((LANE - t_new, A_OUT), F32)
    for kv, ref in enumerate((k_ref, v_ref)):
        new = ref[...].astype(F32)
        new_t = pltpu.roll(jnp.concatenate([new, zeros], axis=0).T, LANE - t_new, 1)
        for rb in range(A_OUT // SHIFT_ROWS):
            rs = slice(rb * SHIFT_ROWS, (rb + 1) * SHIFT_ROWS)
            shifted = pltpu.roll(c_ref[kv, rs, lb - width:lb], width - t_new, 1)
            if width > LANE:
                cout_ref[kv, rs, 0:width - LANE] = shifted[:, 0:width - LANE]
            cout_ref[kv, rs, width - LANE:width] = jnp.where(lane >= LANE - t_new, new_t[rs, :], shifted[:, width - LANE:width])


def _sample_attention(q_ref, k_ref, v_ref, c_ref, o_ref, lse_ref, dil, span):
    t_new = q_ref.shape[0]
    lb = c_ref.shape[2]
    rows = HEADS_PER_GROUP * t_new
    q = q_ref[...].astype(F32)
    k_new = k_ref[...].astype(F32)
    v_new = v_ref[...].astype(F32)

    q_rep = jnp.concatenate([q] * HEADS_PER_GROUP, axis=0)
    row_head = _div_pow2(lax.broadcasted_iota(jnp.int32, (rows, A_OUT), 0), t_new)
    col_head = _div_pow2(lax.broadcasted_iota(jnp.int32, (rows, A_OUT), 1), HEAD_DIM)
    diag = row_head == col_head
    q_bd = jnp.where(diag, q_rep, 0.0).astype(BF16)

    s_buf = _dot(q_bd, c_ref[0].astype(BF16))
    s_new = _dot_nt(q_bd, k_new.astype(BF16))

    tok_b = _mod_pow2(lax.broadcasted_iota(jnp.int32, (rows, lb), 0), t_new)
    pos_b = lax.broadcasted_iota(jnp.int32, (rows, lb), 1)
    dist_b = lb + tok_b - pos_b
    ok_b = (_mod_pow2(dist_b, dil) == 0) & (dist_b <= span * dil)
    tok_n = _mod_pow2(lax.broadcasted_iota(jnp.int32, (rows, t_new), 0), t_new)
    pos_n = lax.broadcasted_iota(jnp.int32, (rows, t_new), 1)
    dist_n = tok_n - pos_n
    ok_n = (dist_n >= 0) & (_mod_pow2(dist_n, dil) == 0) & (dist_n <= span * dil)
    s_buf = jnp.where(ok_b, s_buf, -jnp.inf)
    s_new = jnp.where(ok_n, s_new, -jnp.inf)

    m = jnp.maximum(jnp.max(s_buf, axis=-1, keepdims=True), jnp.max(s_new, axis=-1, keepdims=True))
    p_buf = jnp.exp(s_buf - m)
    p_new = jnp.exp(s_new - m)
    l = jnp.sum(p_buf, axis=-1, keepdims=True) + jnp.sum(p_new, axis=-1, keepdims=True)
    o_full = (_dot_nt(p_buf.astype(BF16), c_ref[1].astype(BF16)) + _dot(p_new.astype(BF16), v_new.astype(BF16))) / l
    o_full = jnp.where(diag, o_full, 0.0)
    lse = (m + jnp.log(l)) * LOG2_E

    o = o_full[0:t_new, :]
    lane_t = lax.broadcasted_iota(jnp.int32, (t_new, LANE), 1)
    lse_all = jnp.zeros((t_new, LANE), F32)
    for h in range(HEADS_PER_GROUP):
        if h > 0:
            o = o + o_full[h * t_new:(h + 1) * t_new, :]
        lse_all = jnp.where(lane_t == h, lse[h * t_new:(h + 1) * t_new, :], lse_all)
    o_ref[...] = o.astype(o_ref.dtype)
    lse_ref[...] = lse_all


def _attn_sample(qkv, cache, prev_out, layer, group, t_new):
    window, dil = DSWA_GROUPS[group]
    n_rows = qkv.shape[1]
    batch = n_rows // t_new
    lb = cache.shape[-1]

    first = prev_out is None
    assert first == (layer == 0) and t_new <= LANE
    phases = DEPTH if first else 1
    rows = max(1, min(batch, SAMPLE_BLOCK_BYTES // (2 * A_OUT * lb * 4)))
    assert batch % rows == 0
    blocks = batch // rows

    def row(p, b):
        return jnp.where(p == 0, b, blocks - 1)

    def pspec(part):
        return pl.BlockSpec((None, rows * t_new, COL_BLOCK), lambda p, b: (part, row(p, b), 0))

    in_specs = [pspec(0), pspec(1), pspec(2),
                pl.BlockSpec((None, rows, 2, A_OUT, lb), lambda p, b: (layer + p, b, 0, 0, 0))]
    args = [qkv, qkv, qkv, cache]
    aliases = {}
    if first:
        cache_out = pl.BlockSpec((None, rows, 2, A_OUT, lb), lambda p, b: (layer + p, b, 0, 0, 0))
    else:
        in_specs.append(pl.BlockSpec(memory_space=pl.ANY))
        args.append(prev_out)
        aliases = {4: 2}
        cache_out = pl.BlockSpec((None, rows, 2, A_OUT, LANE), lambda p, b: (layer, b, 0, 0, lb // LANE - 1))
    return pl.pallas_call(
        functools.partial(_attn_sample_kernel, dil=dil, span=window // dil, first=first),
        grid=(phases, blocks),
        in_specs=in_specs,
        out_specs=[
            pl.BlockSpec((rows * t_new, A_OUT), lambda p, b: (row(p, b), 0)),
            pl.BlockSpec((rows * t_new, LANE), lambda p, b: (row(p, b), 0)),
            cache_out,
        ],
        out_shape=[
            jax.ShapeDtypeStruct((n_rows, A_OUT), F32),
            jax.ShapeDtypeStruct((n_rows, LANE), F32),
            jax.ShapeDtypeStruct(cache.shape, F32),
        ],
        input_output_aliases=aliases,
        compiler_params=_cparams(("arbitrary", "arbitrary")),
        name=f"attn_sample_g{group}",
    )(*args)


def _gla_kernel(q_ref, k_ref, v_ref, r_ref, glr_ref, wa_ref, ba_ref, gg_ref, o_ref, sout_ref, s_ref, *, chunk, n_chunks):
    step = pl.program_id(2)

    @pl.when(step == 0)
    def _():
        s_ref[...] = jnp.zeros_like(s_ref)

    ri = lax.broadcasted_iota(jnp.int32, (chunk, chunk), 0)
    ci = lax.broadcasted_iota(jnp.int32, (chunk, chunk), 1)
    causal = ri >= ci
    tri = causal.astype(BF16)
    gg = gg_ref[...]

    gate = _dot(glr_ref[...].astype(BF16), wa_ref[...]) + ba_ref[...]
    log_a = jax.nn.log_sigmoid(gate) * (1.0 / GATE_TAU)
    log_hi = log_a.astype(BF16)
    log_lo = (log_a - log_hi.astype(F32)).astype(BF16)
    chunks = [slice(c * chunk, (c + 1) * chunk) for c in range(n_chunks)]
    cums = [_dot(tri, log_hi[rs, :]) + _dot(tri, log_lo[rs, :]) for rs in chunks]
    lasts = [cum[chunk - 1:chunk, :] for cum in cums]
    cum = jnp.concatenate(cums, axis=0)
    last = jnp.concatenate([jnp.broadcast_to(l, (chunk, GLA_DK)) for l in lasts], axis=0)
    grow = jnp.exp(cum)
    q_in = (q_ref[...].astype(F32) * (GLA_DK ** -0.5) * grow).astype(BF16)
    k = k_ref[...].astype(F32)
    k_in = (k * jnp.exp(-cum)).astype(BF16)
    k_out = (k * jnp.exp(last - cum)).astype(BF16)
    v = v_ref[...].astype(BF16)
    o_intra, kv, decay = [], [], []
    for rs, l in zip(chunks, lasts):
        att = jnp.where(causal, _dot_nt(q_in[rs, :], k_in[rs, :]), 0.0).astype(BF16)
        o_intra.append(_dot(att, v[rs, :]))
        kv.append(_dot_tn(k_out[rs, :], v[rs, :]))
        d = jnp.broadcast_to(jnp.exp(l), (GLA_DK, GLA_DK)).T
        decay.append(jnp.concatenate([d] * (GLA_DV // GLA_DK), axis=1))

    state = s_ref[...]
    for c, rs in enumerate(chunks):
        o = o_intra[c] + _dot(q_in[rs, :], state.astype(BF16))
        state = decay[c] * state + kv[c]
        y = _rms(o, gg) * jax.nn.silu(r_ref[rs, :].astype(F32))
        o_ref[rs, :] = y.astype(o_ref.dtype)
    s_ref[...] = state

    @pl.when(step == pl.num_programs(2) - 1)
    def _():
        sout_ref[...] = state


def _gla_decode_kernel(q_ref, k_ref, v_ref, r_ref, glr_ref, wa_ref, ba_ref, gg_ref, s0_ref, o_ref, sout_ref):
    n_rows = s0_ref.shape[0]
    rows = q_ref.shape[0]
    t_new = rows // n_rows
    ri = lax.broadcasted_iota(jnp.int32, (rows, rows), 0)
    ci = lax.broadcasted_iota(jnp.int32, (rows, rows), 1)
    same_row = _div_pow2(ri, t_new) == _div_pow2(ci, t_new)
    causal = same_row & (ri >= ci)
    gate = _dot(glr_ref[...].astype(BF16), wa_ref[...]) + ba_ref[...]
    log_a = jax.nn.log_sigmoid(gate) * (1.0 / GATE_TAU)
    log_hi = log_a.astype(BF16)
    log_parts = jnp.concatenate([log_hi, (log_a - log_hi.astype(F32)).astype(BF16)], axis=0)
    tri, blk = causal.astype(BF16), same_row.astype(BF16)
    cum = _dot(jnp.concatenate([tri, tri], axis=1), log_parts)
    total = _dot(jnp.concatenate([blk, blk], axis=1), log_parts)
    q_in = (q_ref[...].astype(F32) * (GLA_DK ** -0.5) * jnp.exp(cum)).astype(BF16)
    k = k_ref[...].astype(F32)
    k_in = (k * jnp.exp(-cum)).astype(BF16)
    k_out = (k * jnp.exp(total - cum)).astype(BF16)
    decay = jnp.exp(total)
    heads_per_block = COL_BLOCK // GLA_DV
    for h in range(GLA_HEADS):
        ks = slice(h * GLA_DK, (h + 1) * GLA_DK)
        os_ = slice(h * GLA_DV, (h + 1) * GLA_DV)
        vb, vs = h // heads_per_block, slice((h % heads_per_block) * GLA_DV, (h % heads_per_block + 1) * GLA_DV)
        v = v_ref[vb, :, vs].astype(BF16)
        att = jnp.where(causal, _dot_nt(q_in[:, ks], k_in[:, ks]), 0.0).astype(BF16)
        o_intra = _dot(att, v)
        for j in range(n_rows):
            tok = slice(j * t_new, (j + 1) * t_new)
            state = s0_ref[j, h].astype(F32)
            o = o_intra[tok, :] + _dot(q_in[tok, ks], state.astype(BF16))
            d = jnp.broadcast_to(decay[j * t_new:j * t_new + 1, ks], (GLA_DK, GLA_DK)).T
            sout_ref[j, h] = jnp.concatenate([d] * (GLA_DV // GLA_DK), axis=1) * state + _dot_tn(k_out[tok, ks], v[tok, :])
            y = _rms(o, gg_ref[:, os_]) * jax.nn.silu(r_ref[vb, tok, vs].astype(F32))
            o_ref[tok, os_] = y.astype(o_ref.dtype)


DECODE_ROWS_PER_STEP = 4


def _gla_decode(rest, lowrank, wa, ba, gg, s0, layer, batch, t_new, out_dtype):
    n_rows = min(batch, DECODE_ROWS_PER_STEP)
    assert batch % n_rows == 0
    tm = n_rows * t_new

    def rows(block, n_blocks=None):
        if n_blocks is None:
            return pl.BlockSpec((None, tm, COL_BLOCK), lambda b: (block, b, 0))
        return pl.BlockSpec((n_blocks, tm, COL_BLOCK), lambda b: (block // n_blocks, b, 0))

    def full(a):
        return pl.BlockSpec(a.shape, lambda b: (0, 0))

    v_blocks = GLA_VAL // COL_BLOCK
    return pl.pallas_call(
        _gla_decode_kernel,
        grid=(batch // n_rows,),
        in_specs=[rows(R_QB), rows(R_KB), rows(R_VB, v_blocks), rows(R_RB, v_blocks),
                  pl.BlockSpec((tm, LANE), lambda b: (b, 0)),
                  full(wa), full(ba), full(gg),
                  pl.BlockSpec((None, n_rows, GLA_HEADS, GLA_DK, GLA_DV), lambda b: (layer, b, 0, 0, 0))],
        out_specs=[pl.BlockSpec((tm, GLA_VAL), lambda b: (b, 0)),
                   pl.BlockSpec((n_rows, GLA_HEADS, GLA_DK, GLA_DV), lambda b: (b, 0, 0, 0))],
        out_shape=[jax.ShapeDtypeStruct((batch * t_new, GLA_VAL), out_dtype),
                   jax.ShapeDtypeStruct((batch, GLA_HEADS, GLA_DK, GLA_DV), F32)],
        compiler_params=_cparams(("parallel",)),
        name="gla_decode",
    )(rest, rest, rest, rest, lowrank, wa, ba, gg, s0)


def _gla(rest, lowrank, wa, ba, gg, batch, seq, chunk, n_chunks, out_dtype):
    n_rows = rest.shape[1]
    lblk = chunk * n_chunks
    steps = seq // lblk

    def pspec(width, block):
        per = COL_BLOCK // width
        return pl.BlockSpec((None, lblk, width), lambda b, h, n: (block + h // per, b * steps + n, h % per))

    in_specs = [
        pspec(GLA_DK, R_QB), pspec(GLA_DK, R_KB), pspec(GLA_DV, R_VB), pspec(GLA_DV, R_RB),
        pl.BlockSpec((lblk, LANE), lambda b, h, n: (b * steps + n, 0)),
        pl.BlockSpec((LANE, GLA_DK), lambda b, h, n: (0, h)),
        pl.BlockSpec((1, GLA_DK), lambda b, h, n: (0, h)),
        pl.BlockSpec((1, GLA_DV), lambda b, h, n: (0, h)),
    ]
    args = [rest, rest, rest, rest, lowrank, wa, ba, gg]
    return pl.pallas_call(
        functools.partial(_gla_kernel, chunk=chunk, n_chunks=n_chunks),
        grid=(batch, GLA_HEADS, steps),
        in_specs=in_specs,
        out_specs=[
            pl.BlockSpec((lblk, GLA_DV), lambda b, h, n: (b * steps + n, h)),
            pl.BlockSpec((None, None, GLA_DK, GLA_DV), lambda b, h, n: (b, h, 0, 0)),
        ],
        out_shape=[
            jax.ShapeDtypeStruct((n_rows, GLA_VAL), out_dtype),
            jax.ShapeDtypeStruct((batch, GLA_HEADS, GLA_DK, GLA_DV), F32),
        ],
        scratch_shapes=[pltpu.VMEM((GLA_DK, GLA_DV), F32)],
        compiler_params=_cparams(("parallel", "parallel", "arbitrary")),
        name="gla",
    )(*args)


def _merge_kernel(x_ref, o1_ref, o2_ref, o3_ref, m1_ref, m2_ref, m3_ref, l1_ref, l2_ref, l3_ref, ob_ref, ga_ref, gb_ref,
                  wpa_ref, wpb_ref, wo_ref, e_ref, out_ref, o_scr, l_scr, *, dils):
    tm = x_ref.shape[0]

    def token_order(ref, scr, dil):
        if dil == 1:
            return ref[0].astype(F32)
        chunks = ref.shape[2] // LANE
        for r in range(dil):
            for c in range(chunks):
                scr[c, pl.ds(r, tm // dil, stride=dil), :] = ref[r, :, c * LANE:(c + 1) * LANE].astype(F32)
        return jnp.concatenate([scr[c] for c in range(chunks)], axis=1)

    ms = [token_order(m_ref, l_scr, d) for m_ref, d in zip((m1_ref, m2_ref, m3_ref), dils)]
    ls = [token_order(l_ref, l_scr, d) for l_ref, d in zip((l1_ref, l2_ref, l3_ref), dils)]
    m = jnp.maximum(jnp.maximum(ms[0], ms[1]), ms[2])
    ws = [jnp.exp2(mg - m) for mg in ms]
    inv = 1.0 / (ws[0] * ls[0] + ws[1] * ls[1] + ws[2] * ls[2])
    expand = e_ref[...]
    o_a = jnp.zeros((tm, A_OUT), F32)
    for w, o_ref, d in zip(ws, (o1_ref, o2_ref, o3_ref), dils):
        o_a = o_a + _split_dot(w * inv, expand) * token_order(o_ref, o_scr, d)
    ya = _dot(o_a.astype(BF16), wpa_ref[...])
    yb = _dot(ob_ref[...].astype(BF16), wpb_ref[...])
    gate_a = jnp.concatenate([ga_ref[0], ga_ref[1]], axis=1).astype(F32)
    gate_b = jnp.concatenate([gb_ref[0], gb_ref[1]], axis=1).astype(F32)
    merged = jax.nn.sigmoid(gate_a) * ya + jax.nn.sigmoid(gate_b) * yb
    out_ref[...] = x_ref[...] + _dot(merged.astype(BF16), wo_ref[...])


def _merge(x, outs, maxes, dens, ob, rest, wpa, wpb, wo, expand, seq, tm):
    n = x.shape[0]
    tiles = seq // tm
    dils = tuple(o.shape[1] for o in outs)

    def rows(width):
        return pl.BlockSpec((tm, width), lambda i: (i, 0))

    def dilated(a):
        dil, width = a.shape[1], a.shape[3]
        return pl.BlockSpec((None, dil, tm // dil, width), lambda i: (i // tiles, 0, i % tiles, 0))

    def gate(block):
        return pl.BlockSpec((2, tm, COL_BLOCK), lambda i: (block // 2, i, 0))

    def full(a):
        return pl.BlockSpec(a.shape, lambda i: (0, 0))

    return pl.pallas_call(
        functools.partial(_merge_kernel, dils=dils),
        grid=(n // tm,),
        in_specs=[rows(D_MODEL)] + [dilated(a) for a in (*outs, *maxes, *dens)] + [rows(GLA_VAL)]
                 + [gate(R_GA), gate(R_GB)] + [full(wpa), full(wpb), full(wo), full(expand)],
        out_specs=rows(D_MODEL),
        out_shape=jax.ShapeDtypeStruct((n, D_MODEL), F32),
        scratch_shapes=[pltpu.VMEM((A_OUT // LANE, tm, LANE), F32), pltpu.VMEM((1, tm, LANE), F32)],
        compiler_params=_cparams(("parallel",)),
        name="merge",
    )(x, *outs, *maxes, *dens, ob, rest, rest, wpa, wpb, wo, expand)


def _ffn_kernel(x_ref, g_ref, gn_ref, wg_ref, wu_ref, wd_ref, o_ref, hn_ref):
    tm = x_ref.shape[0]
    chunk = min(tm, ROW_CHUNK)
    for rc in range(tm // chunk):
        rows = slice(rc * chunk, (rc + 1) * chunk)
        x = x_ref[rows, :]
        h = _rms(x, g_ref[...]).astype(BF16)
        mid = jax.nn.silu(_dot(h, wg_ref[...])) * _dot(h, wu_ref[...])
        y = x + _dot(mid.astype(BF16), wd_ref[...])
        o_ref[rows, :] = y
        hn_ref[rows, :] = _rms(y, gn_ref[...]).astype(hn_ref.dtype)


def _ffn(x, g, g_next, wg, wu, wd, tm):
    n = x.shape[0]

    def resident(a):
        return pl.BlockSpec(a.shape, lambda i: (0, 0), pipeline_mode=pl.Buffered(1))

    def rows():
        return pl.BlockSpec((tm, D_MODEL), lambda i: (i, 0))

    return pl.pallas_call(
        _ffn_kernel,
        grid=(n // tm,),
        in_specs=[rows(), resident(g), resident(g_next), resident(wg), resident(wu), resident(wd)],
        out_specs=[rows(), rows()],
        out_shape=[jax.ShapeDtypeStruct((n, D_MODEL), F32), jax.ShapeDtypeStruct((n, D_MODEL), BF16)],
        compiler_params=_cparams(("parallel",)),
        name="ffn",
    )(x, g, g_next, wg, wu, wd)


def _router_kernel(x_ref, g_ref, wr_ref, comb_ref, scol_ref, srow_ref, cnt_ref):
    h = _rms(x_ref[...], g_ref[...])
    w = wr_ref[...]
    h_hi, w_hi = h.astype(BF16), w.astype(BF16)
    h_lo, w_lo = (h - h_hi.astype(F32)).astype(BF16), (w - w_hi.astype(F32)).astype(BF16)
    logits = _dot(h_hi, w_hi) + (_dot(h_hi, w_lo) + _dot(h_lo, w_hi))
    lane = lax.broadcasted_iota(jnp.int32, logits.shape, 1)
    logits = jnp.where(lane < N_EXPERTS, logits, -jnp.inf)
    v1 = jnp.max(logits, axis=-1, keepdims=True)
    i1 = jnp.min(jnp.where(logits == v1, lane, LANE), axis=-1, keepdims=True)
    rest = jnp.where(lane == i1, -jnp.inf, logits)
    v2 = jnp.max(rest, axis=-1, keepdims=True)
    i2 = jnp.min(jnp.where(rest == v2, lane, LANE), axis=-1, keepdims=True)
    e2 = jnp.exp(v2 - v1)
    g1 = 1.0 / (1.0 + e2)
    g2 = e2 / (1.0 + e2)
    comb = jnp.where(lane == i1, g1, 0.0) + jnp.where(lane == i2, g2, 0.0)
    comb_ref[...] = comb

    tile = comb.shape[0]
    routed = comb > 0.0
    earlier = lax.broadcasted_iota(jnp.int32, (tile, tile), 1) < lax.broadcasted_iota(jnp.int32, (tile, tile), 0)
    rank = _dot(earlier.astype(BF16), routed.astype(BF16))
    slot = jnp.where(routed, rank, -1.0)
    scol_ref[...] = slot
    srow_ref[...] = slot.T[0:N_EXPERTS, :]
    count = jnp.sum(routed.astype(F32), axis=0, keepdims=True)
    cnt_ref[...] = jnp.broadcast_to(count, cnt_ref.shape).astype(jnp.int32)


def _router(x, g, wr, tile):
    n = x.shape[0]
    tiles = n // tile
    comb, slot_col, slot_row, counts = pl.pallas_call(
        _router_kernel,
        grid=(tiles,),
        in_specs=[
            pl.BlockSpec((tile, D_MODEL), lambda i: (i, 0)),
            pl.BlockSpec((1, D_MODEL), lambda i: (0, 0)),
            pl.BlockSpec((D_MODEL, LANE), lambda i: (0, 0)),
        ],
        out_specs=[
            pl.BlockSpec((tile, LANE), lambda i: (i, 0)),
            pl.BlockSpec((tile, LANE), lambda i: (i, 0)),
            pl.BlockSpec((None, N_EXPERTS, tile), lambda i: (i, 0, 0)),
            pl.BlockSpec((None, 8, LANE), lambda i: (i, 0, 0)),
        ],
        out_shape=[
            jax.ShapeDtypeStruct((n, LANE), F32),
            jax.ShapeDtypeStruct((n, LANE), F32),
            jax.ShapeDtypeStruct((tiles, N_EXPERTS, tile), F32),
            jax.ShapeDtypeStruct((tiles, 8, LANE), jnp.int32),
        ],
        compiler_params=_cparams(("parallel",)),
        name="router",
    )(x, g, wr)
    return (comb, counts[:, 0, :N_EXPERTS].reshape(tiles * N_EXPERTS), slot_col,
            slot_row.reshape(tiles * N_EXPERTS, 1, tile))


def _moe_kernel(cnt_ref, x_ref, g_ref, gn_ref, comb_ref, scol_ref, srow_ref, wg_ref, wu_ref, wd_ref, o_ref,
                h_ref, xc_ref, y_ref, *, sub, n_sub):
    i, e, f = pl.program_id(0), pl.program_id(1), pl.program_id(2)
    last_f = pl.num_programs(2) - 1
    tile = x_ref.shape[0]
    sub_pad = y_ref.shape[1]
    count = cnt_ref[i * N_EXPERTS + e]

    @pl.when((e == 0) & (f == 0))
    def _():
        h_ref[...] = _rms(x_ref[...], g_ref[...]).astype(BF16)
        o_ref[...] = x_ref[...]

    for s in range(n_sub):
        @pl.when((f == 0) & (s * sub < count))
        def _():
            slots = (lax.broadcasted_iota(jnp.int32, (sub, tile), 0) + s * sub).astype(F32)
            pick = (srow_ref[...] == slots).astype(BF16)
            xc_ref[s] = _dot(pick, h_ref[...]).astype(BF16)
            y_ref[s] = jnp.zeros((sub_pad, D_MODEL), F32)

        @pl.when(s * sub < count)
        def _():
            xs = xc_ref[s]
            mid = jax.nn.silu(_dot(xs, wg_ref[...])) * _dot(xs, wu_ref[...])
            y_ref[s, 0:sub, :] += _dot(mid.astype(BF16), wd_ref[...])

        @pl.when((f == last_f) & (s * sub < count))
        def _():
            mine = lax.broadcasted_iota(jnp.int32, (tile, LANE), 1) == e
            slot = jnp.sum(jnp.where(mine, scol_ref[...], 0.0), axis=-1, keepdims=True)
            gate = jnp.sum(jnp.where(mine, comb_ref[...], 0.0), axis=-1, keepdims=True)
            slots = (lax.broadcasted_iota(jnp.int32, (tile, sub_pad), 1) + s * sub).astype(F32)
            place = (slot == slots).astype(BF16)
            o_ref[...] += gate * _dot(place, y_ref[s].astype(BF16))

    @pl.when((e == pl.num_programs(1) - 1) & (f == last_f))
    def _():
        o_ref[...] = _rms(o_ref[...], gn_ref[...])


def _moe(x, g, g_out, comb, counts, slot_col, slot_row, wg, wu, wd, sub, tf):
    n = x.shape[0]
    tile = slot_row.shape[-1]
    n_sub = -(-tile // sub)
    sub_pad = -(-sub // LANE) * LANE
    grid_spec = pltpu.PrefetchScalarGridSpec(
        num_scalar_prefetch=1,
        grid=(n // tile, N_EXPERTS, D_FF_EXPERT // tf),
        in_specs=[
            pl.BlockSpec((tile, D_MODEL), lambda i, e, f, c: (i, 0)),
            pl.BlockSpec((1, D_MODEL), lambda i, e, f, c: (0, 0)),
            pl.BlockSpec((1, D_MODEL), lambda i, e, f, c: (0, 0)),
            pl.BlockSpec((tile, LANE), lambda i, e, f, c: (i, 0)),
            pl.BlockSpec((tile, LANE), lambda i, e, f, c: (i, 0)),
            pl.BlockSpec((None, 1, tile), lambda i, e, f, c: (i * N_EXPERTS + e, 0, 0)),
            pl.BlockSpec((None, D_MODEL, tf), lambda i, e, f, c: (e, 0, f)),
            pl.BlockSpec((None, D_MODEL, tf), lambda i, e, f, c: (e, 0, f)),
            pl.BlockSpec((None, tf, D_MODEL), lambda i, e, f, c: (e, f, 0)),
        ],
        out_specs=pl.BlockSpec((tile, D_MODEL), lambda i, e, f, c: (i, 0)),
        scratch_shapes=[
            pltpu.VMEM((tile, D_MODEL), BF16),
            pltpu.VMEM((n_sub, sub, D_MODEL), BF16),
            pltpu.VMEM((n_sub, sub_pad, D_MODEL), F32),
        ],
    )
    return pl.pallas_call(
        functools.partial(_moe_kernel, sub=sub, n_sub=n_sub),
        grid_spec=grid_spec,
        out_shape=jax.ShapeDtypeStruct((n, D_MODEL), F32),
        compiler_params=_cparams(("parallel", "arbitrary", "arbitrary")),
        name="moe",
    )(counts, x, g, g_out, comb, slot_col, slot_row, wg, wu, wd)


def _norm_kernel(x_ref, g_ref, o_ref):
    o_ref[...] = _rms(x_ref[...], g_ref[...]).astype(o_ref.dtype)


def _norm(x, g, tm, out_dtype):
    n = x.shape[0]
    return pl.pallas_call(
        _norm_kernel,
        grid=(n // tm,),
        in_specs=[pl.BlockSpec((tm, D_MODEL), lambda i: (i, 0)), pl.BlockSpec((1, D_MODEL), lambda i: (0, 0))],
        out_specs=pl.BlockSpec((tm, D_MODEL), lambda i: (i, 0)),
        out_shape=jax.ShapeDtypeStruct((n, D_MODEL), out_dtype),
        compiler_params=_cparams(("parallel",)),
        name="norm",
    )(x, g)


def _rope_tables(pos, split_half):
    half = HEAD_DIM // 2
    inv = ROPE_THETA ** (-jnp.arange(half, dtype=F32) / half)
    ang = pos.astype(F32)[:, None] * inv[None, :]
    cos = jnp.cos(ang)
    sin = jnp.sin(ang)
    if split_half:
        return jnp.tile(cos, (1, LANE // half)), jnp.tile(sin, (1, LANE // half))
    return (jnp.tile(jnp.concatenate([cos, cos], axis=-1), (1, LANE // HEAD_DIM)),
            jnp.tile(jnp.concatenate([-sin, sin], axis=-1), (1, LANE // HEAD_DIM)))


def _to_split_half(a):
    lead = a.shape[:-1]
    a = a.reshape(*lead, HEADS_PER_GROUP // QUAD, QUAD, 2, HEAD_DIM // 2)
    return jnp.swapaxes(a, -2, -3).reshape(*lead, A_OUT)


def _layer_weights(l, w_in, w_gate_a2, b_gate_a, g_gla, w_branch_a, w_branch_b, w_out):
    w = w_in[l]
    offs = np.cumsum((A_WIDTH, A_WIDTH, A_WIDTH, GLA_KEY, GLA_KEY, GLA_VAL, GLA_VAL, GATE_RANK, D_MODEL, D_MODEL))
    qa, ka, va, qb, kb, vb, rb, glr, ga, gb = jnp.split(w, [int(o) for o in offs[:-1]], axis=1)
    def attn_weights(split_half):
        relay = _to_split_half if split_half else (lambda a: a)
        return jnp.stack([jnp.stack([relay(qa[:, g * A_OUT:(g + 1) * A_OUT]), relay(ka[:, g * A_OUT:(g + 1) * A_OUT]),
                                     va[:, g * A_OUT:(g + 1) * A_OUT]]) for g in range(N_GROUPS)]).astype(BF16)

    w_attn = {False: attn_weights(False), True: attn_weights(True)}
    w_lowrank = jnp.pad(glr, ((0, 0), (0, LANE - GATE_RANK))).astype(BF16)
    rest = jnp.concatenate([qb, kb, vb, rb, ga, gb], axis=1)
    w_rest = jnp.swapaxes(rest.reshape(D_MODEL, N_REST, COL_BLOCK), 0, 1).astype(BF16)
    wa = jnp.pad(w_gate_a2[l], ((0, LANE - GATE_RANK), (0, 0))).astype(BF16)
    return dict(w_attn=w_attn, w_rest=w_rest, w_lowrank=w_lowrank, wa=wa, ba=b_gate_a[l][None, :], gg=g_gla[l][None, :],
                wpa=w_branch_a[l].astype(BF16), wpb=w_branch_b[l].astype(BF16), wo=w_out[l].astype(BF16))


def _head_expand():
    e = np.zeros((LANE, A_OUT), np.float32)
    for h in range(HEADS_PER_GROUP):
        e[h, h * HEAD_DIM:(h + 1) * HEAD_DIM] = 1.0
    return jnp.asarray(np.concatenate([e, e], axis=0), BF16)


def _trunk(x, rope, caches, gla_state, lw, ffw, g_mix, g_ffn, g_final, batch, seq, tm, tm_proj, moe_tile, moe_sub,
           p_dtype):
    cos_t, sin_t = rope
    expand = _head_expand()
    prompt = caches is None
    cache_out = [None] * N_GROUPS
    states_out = []
    assert DEPTH % 2 == 0
    h = _norm(x, g_mix[0][None, :], tm, BF16)
    for l in range(DEPTH):
        w = lw[l]
        rest, lowrank = _inproj_rest(h, w["w_rest"], w["w_lowrank"], tm_proj, p_dtype)
        outs, maxes, dens = [], [], []
        if prompt:
            qkvs = _inproj_attn(h, w["w_attn"][True], cos_t, sin_t, batch, seq, [d for _, d in DSWA_GROUPS], tm_proj,
                                p_dtype, True)
        else:
            qkvs = _inproj_attn(h, w["w_attn"][False], cos_t, sin_t, 1, batch * seq, [1] * N_GROUPS, tm_proj, p_dtype, False)
        for g, (window, dil) in enumerate(DSWA_GROUPS):
            qkv = qkvs[g]
            if prompt:
                o, mx, den = _attn_prompt(qkv, g)
                cache_out[g] = _kv_tail(qkv, cache_out[g], l, min(window, seq))
            else:
                o, lse, cache_out[g] = _attn_sample(qkv.reshape(3, batch * seq, COL_BLOCK), caches[g], cache_out[g],
                                                    l, g, seq)
                o = o.reshape(1, 1, batch * seq, A_OUT)
                mx = lse.reshape(1, 1, batch * seq, LANE)
                den = jnp.ones_like(mx)
            outs.append(o)
            maxes.append(mx)
            dens.append(den)
        if prompt:
            ob, s_new = _gla(rest, lowrank, w["wa"], w["ba"], w["gg"], batch, seq, GLA_CHUNK, GLA_CHUNKS_PER_STEP, BF16)
            x = _merge(x, outs, maxes, dens, ob, rest, w["wpa"], w["wpb"], w["wo"], expand, seq, min(tm, 512))
        else:
            ob, s_new = _gla_decode(rest, lowrank, w["wa"], w["ba"], w["gg"], gla_state, l, batch, seq, F32)
            x = _merge(x, outs, maxes, dens, ob, rest, w["wpa"], w["wpb"], w["wo"], expand, batch * seq, tm)
        states_out.append(s_new)
        i = l // 2
        if l % 2 == 0:
            x, h = _ffn(x, g_ffn[l][None, :], g_mix[l + 1][None, :], ffw["wg"][i], ffw["wu"][i], ffw["wd"][i], tm)
        else:
            routing = _router(x, g_ffn[l][None, :], ffw["wr"][i], moe_tile)
            g_out = g_final if l == DEPTH - 1 else g_mix[l + 1]
            x = _moe(x, g_ffn[l][None, :], g_out[None, :], *routing, ffw["eg"][i], ffw["eu"][i], ffw["ed"][i],
                     moe_sub, D_FF_EXPERT // 2)
            assert l == DEPTH - 1
    y = x
    bufs = [jnp.transpose(c.reshape(DEPTH, batch, 2, HEADS_PER_GROUP, HEAD_DIM, c.shape[-1]), (0, 1, 5, 2, 3, 4))
            for c in cache_out]
    return (y, bufs[0], bufs[1], bufs[2], jnp.stack(states_out))


def kernel(x_prompt, x_sample, cache_kv_w128, cache_kv_w512, cache_kv_w2048, state_gla, g_mix, w_in, w_gate_a2, b_gate_a, g_gla, w_branch_a, w_branch_b, w_out, g_ffn, w_ff_gate, w_ff_up, w_ff_down, w_router, w_exp_gate, w_exp_up, w_exp_down, g_final):
    batch, seq, _ = x_prompt.shape
    dec_batch, dec_seq, _ = x_sample.shape
    assert seq % (DSWA_GROUPS[-1][1] * Q_BLOCK) == 0 and seq % (GLA_CHUNKS_PER_STEP * GLA_CHUNK) == 0
    assert dec_seq % GLA_CHUNK != 0 and dec_seq % 8 == 0
    assert all(c.shape[2] == w for c, (w, _) in zip((cache_kv_w128, cache_kv_w512, cache_kv_w2048), DSWA_GROUPS))

    lw = [_layer_weights(l, w_in, w_gate_a2, b_gate_a, g_gla, w_branch_a, w_branch_b, w_out) for l in range(DEPTH)]
    ffw = dict(wg=w_ff_gate.astype(BF16), wu=w_ff_up.astype(BF16), wd=w_ff_down.astype(BF16),
               wr=jnp.pad(w_router, ((0, 0), (0, 0), (0, LANE - N_EXPERTS))),
               eg=w_exp_gate.astype(BF16), eu=w_exp_up.astype(BF16), ed=w_exp_down.astype(BF16))

    y_p, kv128_p, kv512_p, kv2048_p, gla_p = _trunk(
        x_prompt.reshape(batch * seq, D_MODEL), _rope_tables(jnp.arange(seq, dtype=jnp.int32), True), None, None,
        lw, ffw, g_mix, g_ffn, g_final, batch, seq, 1024, 2048, 1024, 288, BF16)

    n_s = dec_batch * dec_seq
    pos_s = jnp.tile(PAST_LEN + jnp.arange(dec_seq, dtype=jnp.int32), dec_batch)
    caches = [jnp.transpose(c, (0, 1, 3, 4, 5, 2)).reshape(DEPTH, dec_batch, 2, A_OUT, c.shape[2])
              for c in (cache_kv_w128, cache_kv_w512, cache_kv_w2048)]
    y_s, kv128_s, kv512_s, kv2048_s, gla_s = _trunk(
        x_sample.reshape(n_s, D_MODEL), _rope_tables(pos_s, False), caches, state_gla,
        lw, ffw, g_mix, g_ffn, g_final, dec_batch, dec_seq, n_s, n_s, n_s, 96, F32)

    return (y_p.reshape(batch, seq, D_MODEL), y_s.reshape(dec_batch, dec_seq, D_MODEL),
            kv128_p, kv512_p, kv2048_p, gla_p, kv128_s, kv512_s, kv2048_s, gla_s)
```

```python
import functools

import jax
import jax.numpy as jnp
import numpy as np
from jax import lax
from jax.experimental import pallas as pl
from jax.experimental.pallas import tpu as pltpu

F32 = jnp.float32
BF16 = jnp.bfloat16

D_MODEL = 1024
DEPTH = 2
PAST_LEN = 16384
HEAD_DIM = 64
DSWA_GROUPS = ((128, 1), (512, 4), (2048, 16))
N_GROUPS = 3
HEADS_PER_GROUP = 8
A_WIDTH = N_GROUPS * HEADS_PER_GROUP * HEAD_DIM
A_OUT = HEADS_PER_GROUP * HEAD_DIM
Q_BLOCK = 128
GLA_HEADS = 4
GLA_KEY = 512
GLA_VAL = 1024
GLA_DK = 128
GLA_DV = 256
GATE_RANK = 16
GATE_TAU = 16.0
GLA_CHUNK = 64
D_FF = 2816
N_EXPERTS = 8
D_FF_EXPERT = 3584
ROPE_THETA = 10000.0
EPS = 1e-6
LOG2_E = 1.4426950408889634

LANE = 128
COL_BLOCK = 512
R_QB, R_KB, R_VB, R_RB, R_GA, R_GB, N_REST = 0, 1, 2, 4, 6, 8, 10
ROW_CHUNK = 256
GLA_CHUNKS_PER_STEP = 32
VMEM_LIMIT = 56 * 1024 * 1024


def _cparams(sem):
    return pltpu.CompilerParams(dimension_semantics=sem, vmem_limit_bytes=VMEM_LIMIT)


def _rms(xf, g):
    return xf * lax.rsqrt(jnp.mean(xf * xf, axis=-1, keepdims=True) + EPS) * g


def _dot(a, b):
    return jnp.dot(a, b, preferred_element_type=F32)


def _dot_nt(a, b):
    return lax.dot_general(a, b, (((1,), (1,)), ((), ())), preferred_element_type=F32)


def _dot_tn(a, b):
    return lax.dot_general(a, b, (((0,), (0,)), ((), ())), preferred_element_type=F32)


def _div_pow2(x, d):
    assert d & (d - 1) == 0
    return lax.shift_right_logical(x, int(d).bit_length() - 1)


def _mod_pow2(x, d):
    assert d & (d - 1) == 0
    return x & (d - 1)


def _split_dot(a_f32, b_twice):
    hi = a_f32.astype(BF16)
    lo = (a_f32 - hi.astype(F32)).astype(BF16)
    return _dot(jnp.concatenate([hi, lo], axis=1), b_twice)


def _inproj_attn_kernel(h_ref, w_ref, cos_ref, sin_ref, *refs, dils, split_half, q_scale):
    step = pl.program_id(1)
    outs, stages = refs[:len(dils)], list(refs[len(dils):])
    for g, dil in enumerate(dils):
        stage = stages.pop(0) if dil > 1 else None

        @pl.when((step >= 3 * g) & (step < 3 * g + 3))
        def _(g=g, dil=dil, stage=stage):
            _project_rotate(h_ref, w_ref, cos_ref, sin_ref, outs[g], stage, step - 3 * g, dil, split_half, q_scale)


def _project_rotate(h_ref, w_ref, cos_ref, sin_ref, o_ref, stage, part, dil, split_half, q_scale):
    scratch = [stage]
    scale = jnp.where(part == 0, q_scale, 1.0).astype(F32)
    tm = h_ref.shape[0]
    chunk = min(tm, ROW_CHUNK)
    n_lane_groups = COL_BLOCK // LANE
    for rc in range(tm // chunk):
        rows = slice(rc * chunk, (rc + 1) * chunk)
        acc = _dot(h_ref[rows, :], w_ref[...])
        cos = jnp.where(part == 2, 1.0, cos_ref[rows, :] * scale)
        sin = jnp.where(part == 2, 0.0, sin_ref[rows, :] * scale)
        ys = []
        if split_half:
            for c in range(0, n_lane_groups, 2):
                a = acc[:, c * LANE:(c + 1) * LANE]
                b = acc[:, (c + 1) * LANE:(c + 2) * LANE]
                ys += [a * cos - b * sin, a * sin + b * cos]
        else:
            first_half = _mod_pow2(lax.broadcasted_iota(jnp.int32, cos.shape, 1), HEAD_DIM) < HEAD_DIM // 2
            for c in range(n_lane_groups):
                xs = acc[:, c * LANE:(c + 1) * LANE]
                swapped = jnp.where(first_half, pltpu.roll(xs, LANE - HEAD_DIM // 2, 1), pltpu.roll(xs, HEAD_DIM // 2, 1))
                ys.append(xs * cos + swapped * sin)
        if dil == 1:
            for c, y in enumerate(ys):
                o_ref[0, rows, c * LANE:(c + 1) * LANE] = y.astype(o_ref.dtype)
        else:
            sub = chunk // dil
            pitch = _residue_pitch(dil)
            base = rc * sub * pitch
            for c, y in enumerate(ys):
                if pitch == dil:
                    scratch[0][c, rows, :] = y
                else:
                    for u in range(sub):
                        scratch[0][c, base + u * pitch:base + u * pitch + dil, :] = y[u * dil:(u + 1) * dil, :]
            for r in range(dil):
                for c in range(n_lane_groups):
                    o_ref[r, rc * sub:(rc + 1) * sub, c * LANE:(c + 1) * LANE] = (
                        scratch[0][c, pl.ds(base + r, sub, stride=pitch), :].astype(o_ref.dtype))


def _residue_pitch(dil):
    return dil + 8 if dil % 16 == 0 else dil


def _inproj_attn(h, w, cos_t, sin_t, batch, seq, dils, tm, out_dtype, split_half):
    tiles = seq // tm
    scratch = [pltpu.VMEM((COL_BLOCK // LANE, tm // dil * _residue_pitch(dil), LANE), F32) for dil in dils if dil > 1]
    q_scale = HEAD_DIM ** -0.5 * (LOG2_E if split_half else 1.0)

    def out_spec(g, dil):
        return pl.BlockSpec((None, None, dil, tm // dil, COL_BLOCK),
                            lambda i, j: (jnp.clip(j - 3 * g, 0, 2), i // tiles, 0, i % tiles, 0))

    return pl.pallas_call(
        functools.partial(_inproj_attn_kernel, dils=tuple(dils), split_half=split_half, q_scale=q_scale),
        grid=(batch * tiles, 3 * len(dils)),
        in_specs=[
            pl.BlockSpec((tm, D_MODEL), lambda i, j: (i, 0)),
            pl.BlockSpec((None, None, D_MODEL, COL_BLOCK), lambda i, j: (j // 3, j % 3, 0, 0)),
            pl.BlockSpec((tm, LANE), lambda i, j: (i % tiles, 0)),
            pl.BlockSpec((tm, LANE), lambda i, j: (i % tiles, 0)),
        ],
        out_specs=[out_spec(g, dil) for g, dil in enumerate(dils)],
        out_shape=[jax.ShapeDtypeStruct((3, batch, dil, seq // dil, COL_BLOCK), out_dtype) for dil in dils],
        scratch_shapes=scratch,
        compiler_params=_cparams(("parallel", "arbitrary")),
        name="inproj_attn",
    )(h, w, cos_t, sin_t)


def _inproj_rest_kernel(h_ref, w_ref, wlr_ref, o_ref, lr_ref):
    tm = h_ref.shape[0]
    chunk = min(tm, ROW_CHUNK)

    @pl.when(pl.program_id(1) == 0)
    def _():
        for rc in range(tm // chunk):
            rows = slice(rc * chunk, (rc + 1) * chunk)
            lr_ref[rows, :] = _dot(h_ref[rows, :], wlr_ref[...]).astype(lr_ref.dtype)

    for rc in range(tm // chunk):
        rows = slice(rc * chunk, (rc + 1) * chunk)
        o_ref[rows, :] = _dot(h_ref[rows, :], w_ref[...]).astype(o_ref.dtype)


def _inproj_rest(h, w, w_lowrank, tm, out_dtype):
    n = h.shape[0]
    return pl.pallas_call(
        _inproj_rest_kernel,
        grid=(n // tm, N_REST),
        in_specs=[
            pl.BlockSpec((tm, D_MODEL), lambda i, j: (i, 0)),
            pl.BlockSpec((None, D_MODEL, COL_BLOCK), lambda i, j: (j, 0, 0)),
            pl.BlockSpec((D_MODEL, LANE), lambda i, j: (0, 0)),
        ],
        out_specs=[pl.BlockSpec((None, tm, COL_BLOCK), lambda i, j: (j, i, 0)),
                   pl.BlockSpec((tm, LANE), lambda i, j: (i, 0))],
        out_shape=[jax.ShapeDtypeStruct((N_REST, n, COL_BLOCK), out_dtype),
                   jax.ShapeDtypeStruct((n, LANE), out_dtype)],
        compiler_params=_cparams(("parallel", "arbitrary")),
        name="inproj_rest",
    )(h, w, w_lowrank)


QUAD = 4
QUAD_WIDTH = QUAD * HEAD_DIM
MAX_Q_BLOCKS_PER_STEP = 8


def _attn_prompt_kernel(bias_ref, *refs):
    n = pl.program_id(2)
    for r in range(refs[0].shape[0]):
        _attn_prompt_subsequence(n, bias_ref, *(ref.at[r] for ref in refs))


def _attn_prompt_subsequence(n, bias_ref, q_ref, kp_ref, kc_ref, vp_ref, vc_ref, o_ref, m_ref, l_ref):
    rows = QUAD * Q_BLOCK
    row_head = _div_pow2(lax.broadcasted_iota(jnp.int32, (rows, QUAD_WIDTH), 0), Q_BLOCK)
    col = lax.broadcasted_iota(jnp.int32, (rows, QUAD_WIDTH), 1)
    q_lanes = _div_pow2(_mod_pow2(col, LANE), HEAD_DIM // 2) == row_head
    v_lanes = _div_pow2(col, HEAD_DIM) == row_head
    lane = lax.broadcasted_iota(jnp.int32, (Q_BLOCK, LANE), 1)
    for qb in range(q_ref.shape[0] // Q_BLOCK):
        cur = slice(qb * Q_BLOCK, (qb + 1) * Q_BLOCK)
        if qb == 0:
            bias = bias_ref[jnp.where(n == 0, 1, 0)]
        else:
            bias = bias_ref[0]
            prev = slice((qb - 1) * Q_BLOCK, qb * Q_BLOCK)
        m_all = jnp.zeros((Q_BLOCK, LANE), F32)
        l_all = jnp.ones((Q_BLOCK, LANE), F32)
        for c in range(A_OUT // QUAD_WIDTH):
            cs = slice(c * QUAD_WIDTH, (c + 1) * QUAD_WIDTH)
            q = q_ref[cur, cs].astype(F32)
            q4 = jnp.where(q_lanes, jnp.concatenate([q] * QUAD, axis=0), 0.0).astype(BF16)
            k_prev = kp_ref[:, cs] if qb == 0 else kc_ref[prev, cs]
            v_prev = vp_ref[:, cs] if qb == 0 else vc_ref[prev, cs]
            k = jnp.concatenate([k_prev, kc_ref[cur, cs]], axis=0)
            v = jnp.concatenate([v_prev, vc_ref[cur, cs]], axis=0)
            s = _dot_nt(q4, k) + bias
            m = jnp.max(s, axis=-1, keepdims=True)
            p = jnp.exp2(s - m)
            l = jnp.sum(p, axis=-1, keepdims=True)
            o4 = jnp.where(v_lanes, _dot(p.astype(BF16), v), 0.0)
            o = o4[0:Q_BLOCK]
            for j in range(QUAD):
                hs = slice(j * Q_BLOCK, (j + 1) * Q_BLOCK)
                if j > 0:
                    o = o + o4[hs]
                m_all = jnp.where(lane == c * QUAD + j, m[hs], m_all)
                l_all = jnp.where(lane == c * QUAD + j, l[hs], l_all)
            o_ref[cur, cs] = o.astype(o_ref.dtype)
        m_ref[cur, :] = m_all
        l_ref[cur, :] = l_all


def _band_bias(span):
    qi = np.arange(QUAD * Q_BLOCK)[:, None] % Q_BLOCK
    kj = np.arange(2 * Q_BLOCK)[None, :]
    dist = Q_BLOCK + qi - kj
    band = (dist >= 0) & (dist <= span)
    first = band & (kj >= Q_BLOCK)
    return jnp.asarray(np.where(np.stack([band, first]), 0.0, -np.inf), F32)


def _attn_prompt(qkv, group):
    window, dil = DSWA_GROUPS[group]
    _, batch, _, sub_len, _ = qkv.shape
    q_blocks = min(MAX_Q_BLOCKS_PER_STEP, sub_len // Q_BLOCK)
    step_rows = q_blocks * Q_BLOCK
    nb = sub_len // step_rows
    res = min(dil, MAX_Q_BLOCKS_PER_STEP // q_blocks)

    def spec(part, prev):
        if prev:
            return pl.BlockSpec((None, None, res, Q_BLOCK, COL_BLOCK),
                                lambda b, r, n: (part, b, r, jnp.maximum(n * q_blocks - 1, 0), 0))
        return pl.BlockSpec((None, None, res, step_rows, COL_BLOCK), lambda b, r, n: (part, b, r, n, 0))

    bias = _band_bias(window // dil)
    stat_spec = pl.BlockSpec((None, res, step_rows, LANE), lambda b, r, n: (b, r, n, 0))
    stat_shape = jax.ShapeDtypeStruct((batch, dil, sub_len, LANE), F32)
    return pl.pallas_call(
        _attn_prompt_kernel,
        grid=(batch, dil // res, nb),
        in_specs=[pl.BlockSpec(bias.shape, lambda b, r, n: (0, 0, 0)),
                  spec(0, False), spec(1, True), spec(1, False), spec(2, True), spec(2, False)],
        out_specs=[pl.BlockSpec((None, res, step_rows, A_OUT), lambda b, r, n: (b, r, n, 0)), stat_spec, stat_spec],
        out_shape=[jax.ShapeDtypeStruct((batch, dil, sub_len, A_OUT), BF16), stat_shape, stat_shape],
        compiler_params=_cparams(("parallel", "parallel", "arbitrary")),
        name=f"attn_prompt_g{group}",
    )(bias, qkv, qkv, qkv, qkv, qkv)


def _kv_tail_kernel(*refs, dil, aliased):
    if aliased:
        k_ref, v_ref, _, out_ref, scr = refs
        _kv_tail_body(k_ref, v_ref, out_ref, scr, dil)
        return
    k_ref, v_ref, out_ref, scr = refs

    @pl.when(pl.program_id(0) == 0)
    def _():
        _kv_tail_body(k_ref, v_ref, out_ref, scr, dil)

    @pl.when(pl.program_id(0) > 0)
    def _():
        out_ref[...] = jnp.zeros_like(out_ref)


def _kv_tail_body(k_ref, v_ref, out_ref, scr, dil):
    keep = out_ref.shape[2]
    sub = keep // dil
    half_w = HEAD_DIM // 2
    for kv, ref in enumerate((k_ref, v_ref)):
        for c in range(COL_BLOCK // LANE):
            cs = slice(c * LANE, (c + 1) * LANE)
            if dil == 1:
                tok = ref[0, :, cs].astype(F32)
            else:
                for r in range(dil):
                    scr[pl.ds(r, sub, stride=dil), :] = ref[r, :, cs].astype(F32)
                tok = scr[...]
            chan = tok.T
            if kv == 1:
                out_ref[kv, cs, :] = chan
            else:
                for j in range(QUAD):
                    dst = (QUAD * (c // 2) + j) * HEAD_DIM + (c % 2) * half_w
                    out_ref[kv, dst:dst + half_w, :] = chan[j * half_w:(j + 1) * half_w, :]


def _kv_tail(qkv, prev_out, layer, keep):
    _, batch, dil, sub_len, _ = qkv.shape
    last = sub_len // (keep // dil) - 1
    first = prev_out is None
    assert first == (layer == 0)
    phases = DEPTH if first else 1

    def spec(part):
        return pl.BlockSpec((None, None, dil, keep // dil, COL_BLOCK),
                            lambda p, b: (part, jnp.where(p == 0, b, batch - 1), 0, last, 0))

    in_specs = [spec(1), spec(2)]
    args = [qkv, qkv]
    aliases = {}
    if not first:
        in_specs.append(pl.BlockSpec(memory_space=pl.ANY))
        args.append(prev_out)
        aliases = {2: 0}
    return pl.pallas_call(
        functools.partial(_kv_tail_kernel, dil=dil, aliased=not first),
        grid=(phases, batch),
        in_specs=in_specs,
        out_specs=pl.BlockSpec((None, None, 2, A_OUT, keep), lambda p, b: (layer + p, b, 0, 0, 0)),
        out_shape=jax.ShapeDtypeStruct((DEPTH, batch, 2, A_OUT, keep), F32),
        scratch_shapes=[pltpu.VMEM((keep, LANE), F32)],
        input_output_aliases=aliases,
        compiler_params=_cparams(("arbitrary", "arbitrary")),
        name=f"kv_tail_{keep}",
    )(*args)


SHIFT_ROWS = 128
SAMPLE_BLOCK_BYTES = 8 * 1024 * 1024


def _attn_sample_kernel(*refs, dil, span, first):
    if first:
        q_ref, k_ref, v_ref, c_ref, o_ref, lse_ref, cout_ref = refs
    else:
        q_ref, k_ref, v_ref, c_ref, _, o_ref, lse_ref, cout_ref = refs
    n_rows = c_ref.shape[0]
    t_new = q_ref.shape[0] // n_rows

    def update(j):
        tok = pl.ds(j * t_new, t_new)
        q, k, v = q_ref.at[tok], k_ref.at[tok], v_ref.at[tok]
        _sample_attention(q, k, v, c_ref.at[j], o_ref.at[tok], lse_ref.at[tok], dil, span)
        _append_new(c_ref.at[j], k, v, cout_ref.at[j])

    def shift_only(j):
        for kv in range(2):
            for rb in range(A_OUT // SHIFT_ROWS):
                rs = slice(rb * SHIFT_ROWS, (rb + 1) * SHIFT_ROWS)
                cout_ref[j, kv, rs, :] = pltpu.roll(c_ref[j, kv, rs, :], c_ref.shape[3] - t_new, 1)

    if first:
        @pl.when(pl.program_id(0) == 0)
        def _():
            for j in range(n_rows):
                update(j)

        @pl.when(pl.program_id(0) > 0)
        def _():
            for j in range(n_rows):
                shift_only(j)
    else:
        for j in range(n_rows):
            update(j)


def _append_new(c_ref, k_ref, v_ref, cout_ref):
    t_new = k_ref.shape[0]
    lb = c_ref.shape[2]
    width = cout_ref.shape[2]
    lane = lax.broadcasted_iota(jnp.int32, (SHIFT_ROWS, LANE), 1)
    zeros = jnp.zeros((LANE - t_new, A_OUT), F32)
    for kv, ref in enumerate((k_ref, v_ref)):
        new = ref[...].astype(F32)
        new_t = pltpu.roll(jnp.concatenate([new, zeros], axis=0).T, LANE - t_new, 1)
        for rb in range(A_OUT // SHIFT_ROWS):
            rs = slice(rb * SHIFT_ROWS, (rb + 1) * SHIFT_ROWS)
            shifted = pltpu.roll(c_ref[kv, rs, lb - width:lb], width - t_new, 1)
            if width > LANE:
                cout_ref[kv, rs, 0:width - LANE] = shifted[:, 0:width - LANE]
            cout_ref[kv, rs, width - LANE:width] = jnp.where(lane >= LANE - t_new, new_t[rs, :], shifted[:, width - LANE:width])


def _sample_attention(q_ref, k_ref, v_ref, c_ref, o_ref, lse_ref, dil, span):
    t_new = q_ref.shape[0]
    lb = c_ref.shape[2]
    rows = HEADS_PER_GROUP * t_new
    q = q_ref[...].astype(F32)
    k_new = k_ref[...].astype(F32)
    v_new = v_ref[...].astype(F32)

    q_rep = jnp.concatenate([q] * HEADS_PER_GROUP, axis=0)
    row_head = _div_pow2(lax.broadcasted_iota(jnp.int32, (rows, A_OUT), 0), t_new)
    col_head = _div_pow2(lax.broadcasted_iota(jnp.int32, (rows, A_OUT), 1), HEAD_DIM)
    diag = row_head == col_head
    q_bd = jnp.where(diag, q_rep, 0.0).astype(BF16)

    s_buf = _dot(q_bd, c_ref[0].astype(BF16))
    s_new = _dot_nt(q_bd, k_new.astype(BF16))

    tok_b = _mod_pow2(lax.broadcasted_iota(jnp.int32, (rows, lb), 0), t_new)
    pos_b = lax.broadcasted_iota(jnp.int32, (rows, lb), 1)
    dist_b = lb + tok_b - pos_b
    ok_b = (_mod_pow2(dist_b, dil) == 0) & (dist_b <= span * dil)
    tok_n = _mod_pow2(lax.broadcasted_iota(jnp.int32, (rows, t_new), 0), t_new)
    pos_n = lax.broadcasted_iota(jnp.int32, (rows, t_new), 1)
    dist_n = tok_n - pos_n
    ok_n = (dist_n >= 0) & (_mod_pow2(dist_n, dil) == 0) & (dist_n <= span * dil)
    s_buf = jnp.where(ok_b, s_buf, -jnp.inf)
    s_new = jnp.where(ok_n, s_new, -jnp.inf)

    m = jnp.maximum(jnp.max(s_buf, axis=-1, keepdims=True), jnp.max(s_new, axis=-1, keepdims=True))
    p_buf = jnp.exp(s_buf - m)
    p_new = jnp.exp(s_new - m)
    l = jnp.sum(p_buf, axis=-1, keepdims=True) + jnp.sum(p_new, axis=-1, keepdims=True)
    o_full = (_dot_nt(p_buf.astype(BF16), c_ref[1].astype(BF16)) + _dot(p_new.astype(BF16), v_new.astype(BF16))) / l
    o_full = jnp.where(diag, o_full, 0.0)
    lse = (m + jnp.log(l)) * LOG2_E

    o = o_full[0:t_new, :]
    lane_t = lax.broadcasted_iota(jnp.int32, (t_new, LANE), 1)
    lse_all = jnp.zeros((t_new, LANE), F32)
    for h in range(HEADS_PER_GROUP):
        if h > 0:
            o = o + o_full[h * t_new:(h + 1) * t_new, :]
        lse_all = jnp.where(lane_t == h, lse[h * t_new:(h + 1) * t_new, :], lse_all)
    o_ref[...] = o.astype(o_ref.dtype)
    lse_ref[...] = lse_all


def _attn_sample(qkv, cache, prev_out, layer, group, t_new):
    window, dil = DSWA_GROUPS[group]
    n_rows = qkv.shape[1]
    batch = n_rows // t_new
    lb = cache.shape[-1]

    first = prev_out is None
    assert first == (layer == 0) and t_new <= LANE
    phases = DEPTH if first else 1
    rows = max(1, min(batch, SAMPLE_BLOCK_BYTES // (2 * A_OUT * lb * 4)))
    assert batch % rows == 0
    blocks = batch // rows

    def row(p, b):
        return jnp.where(p == 0, b, blocks - 1)

    def pspec(part):
        return pl.BlockSpec((None, rows * t_new, COL_BLOCK), lambda p, b: (part, row(p, b), 0))

    in_specs = [pspec(0), pspec(1), pspec(2),
                pl.BlockSpec((None, rows, 2, A_OUT, lb), lambda p, b: (layer + p, b, 0, 0, 0))]
    args = [qkv, qkv, qkv, cache]
    aliases = {}
    if first:
        cache_out = pl.BlockSpec((None, rows, 2, A_OUT, lb), lambda p, b: (layer + p, b, 0, 0, 0))
    else:
        in_specs.append(pl.BlockSpec(memory_space=pl.ANY))
        args.append(prev_out)
        aliases = {4: 2}
        cache_out = pl.BlockSpec((None, rows, 2, A_OUT, LANE), lambda p, b: (layer, b, 0, 0, lb // LANE - 1))
    return pl.pallas_call(
        functools.partial(_attn_sample_kernel, dil=dil, span=window // dil, first=first),
        grid=(phases, blocks),
        in_specs=in_specs,
        out_specs=[
            pl.BlockSpec((rows * t_new, A_OUT), lambda p, b: (row(p, b), 0)),
            pl.BlockSpec((rows * t_new, LANE), lambda p, b: (row(p, b), 0)),
            cache_out,
        ],
        out_shape=[
            jax.ShapeDtypeStruct((n_rows, A_OUT), F32),
            jax.ShapeDtypeStruct((n_rows, LANE), F32),
            jax.ShapeDtypeStruct(cache.shape, F32),
        ],
        input_output_aliases=aliases,
        compiler_params=_cparams(("arbitrary", "arbitrary")),
        name=f"attn_sample_g{group}",
    )(*args)


def _gla_kernel(q_ref, k_ref, v_ref, r_ref, glr_ref, wa_ref, ba_ref, gg_ref, o_ref, sout_ref, s_ref, *, chunk, n_chunks):
    step = pl.program_id(2)

    @pl.when(step == 0)
    def _():
        s_ref[...] = jnp.zeros_like(s_ref)

    ri = lax.broadcasted_iota(jnp.int32, (chunk, chunk), 0)
    ci = lax.broadcasted_iota(jnp.int32, (chunk, chunk), 1)
    causal = ri >= ci
    tri = causal.astype(BF16)
    gg = gg_ref[...]

    gate = _dot(glr_ref[...].astype(BF16), wa_ref[...]) + ba_ref[...]
    log_a = jax.nn.log_sigmoid(gate) * (1.0 / GATE_TAU)
    log_hi = log_a.astype(BF16)
    log_lo = (log_a - log_hi.astype(F32)).astype(BF16)
    chunks = [slice(c * chunk, (c + 1) * chunk) for c in range(n_chunks)]
    cums = [_dot(tri, log_hi[rs, :]) + _dot(tri, log_lo[rs, :]) for rs in chunks]
    lasts = [cum[chunk - 1:chunk, :] for cum in cums]
    cum = jnp.concatenate(cums, axis=0)
    last = jnp.concatenate([jnp.broadcast_to(l, (chunk, GLA_DK)) for l in lasts], axis=0)
    grow = jnp.exp(cum)
    q_in = (q_ref[...].astype(F32) * (GLA_DK ** -0.5) * grow).astype(BF16)
    k = k_ref[...].astype(F32)
    k_in = (k * jnp.exp(-cum)).astype(BF16)
    k_out = (k * jnp.exp(last - cum)).astype(BF16)
    v = v_ref[...].astype(BF16)
    o_intra, kv, decay = [], [], []
    for rs, l in zip(chunks, lasts):
        att = jnp.where(causal, _dot_nt(q_in[rs, :], k_in[rs, :]), 0.0).astype(BF16)
        o_intra.append(_dot(att, v[rs, :]))
        kv.append(_dot_tn(k_out[rs, :], v[rs, :]))
        d = jnp.broadcast_to(jnp.exp(l), (GLA_DK, GLA_DK)).T
        decay.append(jnp.concatenate([d] * (GLA_DV // GLA_DK), axis=1))

    state = s_ref[...]
    for c, rs in enumerate(chunks):
        o = o_intra[c] + _dot(q_in[rs, :], state.astype(BF16))
        state = decay[c] * state + kv[c]
        y = _rms(o, gg) * jax.nn.silu(r_ref[rs, :].astype(F32))
        o_ref[rs, :] = y.astype(o_ref.dtype)
    s_ref[...] = state

    @pl.when(step == pl.num_programs(2) - 1)
    def _():
        sout_ref[...] = state


def _gla_decode_kernel(q_ref, k_ref, v_ref, r_ref, glr_ref, wa_ref, ba_ref, gg_ref, s0_ref, o_ref, sout_ref):
    n_rows = s0_ref.shape[0]
    rows = q_ref.shape[0]
    t_new = rows // n_rows
    ri = lax.broadcasted_iota(jnp.int32, (rows, rows), 0)
    ci = lax.broadcasted_iota(jnp.int32, (rows, rows), 1)
    same_row = _div_pow2(ri, t_new) == _div_pow2(ci, t_new)
    causal = same_row & (ri >= ci)
    gate = _dot(glr_ref[...].astype(BF16), wa_ref[...]) + ba_ref[...]
    log_a = jax.nn.log_sigmoid(gate) * (1.0 / GATE_TAU)
    log_hi = log_a.astype(BF16)
    log_parts = jnp.concatenate([log_hi, (log_a - log_hi.astype(F32)).astype(BF16)], axis=0)
    tri, blk = causal.astype(BF16), same_row.astype(BF16)
    cum = _dot(jnp.concatenate([tri, tri], axis=1), log_parts)
    total = _dot(jnp.concatenate([blk, blk], axis=1), log_parts)
    q_in = (q_ref[...].astype(F32) * (GLA_DK ** -0.5) * jnp.exp(cum)).astype(BF16)
    k = k_ref[...].astype(F32)
    k_in = (k * jnp.exp(-cum)).astype(BF16)
    k_out = (k * jnp.exp(total - cum)).astype(BF16)
    decay = jnp.exp(total)
    heads_per_block = COL_BLOCK // GLA_DV
    for h in range(GLA_HEADS):
        ks = slice(h * GLA_DK, (h + 1) * GLA_DK)
        os_ = slice(h * GLA_DV, (h + 1) * GLA_DV)
        vb, vs = h // heads_per_block, slice((h % heads_per_block) * GLA_DV, (h % heads_per_block + 1) * GLA_DV)
        v = v_ref[vb, :, vs].astype(BF16)
        att = jnp.where(causal, _dot_nt(q_in[:, ks], k_in[:, ks]), 0.0).astype(BF16)
        o_intra = _dot(att, v)
        for j in range(n_rows):
            tok = slice(j * t_new, (j + 1) * t_new)
            state = s0_ref[j, h].astype(F32)
            o = o_intra[tok, :] + _dot(q_in[tok, ks], state.astype(BF16))
            d = jnp.broadcast_to(decay[j * t_new:j * t_new + 1, ks], (GLA_DK, GLA_DK)).T
            sout_ref[j, h] = jnp.concatenate([d] * (GLA_DV // GLA_DK), axis=1) * state + _dot_tn(k_out[tok, ks], v[tok, :])
            y = _rms(o, gg_ref[:, os_]) * jax.nn.silu(r_ref[vb, tok, vs].astype(F32))
            o_ref[tok, os_] = y.astype(o_ref.dtype)


DECODE_ROWS_PER_STEP = 4


def _gla_decode(rest, lowrank, wa, ba, gg, s0, layer, batch, t_new, out_dtype):
    n_rows = min(batch, DECODE_ROWS_PER_STEP)
    assert batch % n_rows == 0
    tm = n_rows * t_new

    def rows(block, n_blocks=None):
        if n_blocks is None:
            return pl.BlockSpec((None, tm, COL_BLOCK), lambda b: (block, b, 0))
        return pl.BlockSpec((n_blocks, tm, COL_BLOCK), lambda b: (block // n_blocks, b, 0))

    def full(a):
        return pl.BlockSpec(a.shape, lambda b: (0, 0))

    v_blocks = GLA_VAL // COL_BLOCK
    return pl.pallas_call(
        _gla_decode_kernel,
        grid=(batch // n_rows,),
        in_specs=[rows(R_QB), rows(R_KB), rows(R_VB, v_blocks), rows(R_RB, v_blocks),
                  pl.BlockSpec((tm, LANE), lambda b: (b, 0)),
                  full(wa), full(ba), full(gg),
                  pl.BlockSpec((None, n_rows, GLA_HEADS, GLA_DK, GLA_DV), lambda b: (layer, b, 0, 0, 0))],
        out_specs=[pl.BlockSpec((tm, GLA_VAL), lambda b: (b, 0)),
                   pl.BlockSpec((n_rows, GLA_HEADS, GLA_DK, GLA_DV), lambda b: (b, 0, 0, 0))],
        out_shape=[jax.ShapeDtypeStruct((batch * t_new, GLA_VAL), out_dtype),
                   jax.ShapeDtypeStruct((batch, GLA_HEADS, GLA_DK, GLA_DV), F32)],
        compiler_params=_cparams(("parallel",)),
        name="gla_decode",
    )(rest, rest, rest, rest, lowrank, wa, ba, gg, s0)


def _gla(rest, lowrank, wa, ba, gg, batch, seq, chunk, n_chunks, out_dtype):
    n_rows = rest.shape[1]
    lblk = chunk * n_chunks
    steps = seq // lblk

    def pspec(width, block):
        per = COL_BLOCK // width
        return pl.BlockSpec((None, lblk, width), lambda b, h, n: (block + h // per, b * steps + n, h % per))

    in_specs = [
        pspec(GLA_DK, R_QB), pspec(GLA_DK, R_KB), pspec(GLA_DV, R_VB), pspec(GLA_DV, R_RB),
        pl.BlockSpec((lblk, LANE), lambda b, h, n: (b * steps + n, 0)),
        pl.BlockSpec((LANE, GLA_DK), lambda b, h, n: (0, h)),
        pl.BlockSpec((1, GLA_DK), lambda b, h, n: (0, h)),
        pl.BlockSpec((1, GLA_DV), lambda b, h, n: (0, h)),
    ]
    args = [rest, rest, rest, rest, lowrank, wa, ba, gg]
    return pl.pallas_call(
        functools.partial(_gla_kernel, chunk=chunk, n_chunks=n_chunks),
        grid=(batch, GLA_HEADS, steps),
        in_specs=in_specs,
        out_specs=[
            pl.BlockSpec((lblk, GLA_DV), lambda b, h, n: (b * steps + n, h)),
            pl.BlockSpec((None, None, GLA_DK, GLA_DV), lambda b, h, n: (b, h, 0, 0)),
        ],
        out_shape=[
            jax.ShapeDtypeStruct((n_rows, GLA_VAL), out_dtype),
            jax.ShapeDtypeStruct((batch, GLA_HEADS, GLA_DK, GLA_DV), F32),
        ],
        scratch_shapes=[pltpu.VMEM((GLA_DK, GLA_DV), F32)],
        compiler_params=_cparams(("parallel", "parallel", "arbitrary")),
        name="gla",
    )(*args)


def _merge_kernel(x_ref, o1_ref, o2_ref, o3_ref, m1_ref, m2_ref, m3_ref, l1_ref, l2_ref, l3_ref, ob_ref, ga_ref, gb_ref,
                  wpa_ref, wpb_ref, wo_ref, e_ref, out_ref, o_scr, l_scr, *, dils):
    tm = x_ref.shape[0]

    def token_order(ref, scr, dil):
        if dil == 1:
            return ref[0].astype(F32)
        chunks = ref.shape[2] // LANE
        for r in range(dil):
            for c in range(chunks):
                scr[c, pl.ds(r, tm // dil, stride=dil), :] = ref[r, :, c * LANE:(c + 1) * LANE].astype(F32)
        return jnp.concatenate([scr[c] for c in range(chunks)], axis=1)

    ms = [token_order(m_ref, l_scr, d) for m_ref, d in zip((m1_ref, m2_ref, m3_ref), dils)]
    ls = [token_order(l_ref, l_scr, d) for l_ref, d in zip((l1_ref, l2_ref, l3_ref), dils)]
    m = jnp.maximum(jnp.maximum(ms[0], ms[1]), ms[2])
    ws = [jnp.exp2(mg - m) for mg in ms]
    inv = 1.0 / (ws[0] * ls[0] + ws[1] * ls[1] + ws[2] * ls[2])
    expand = e_ref[...]
    o_a = jnp.zeros((tm, A_OUT), F32)
    for w, o_ref, d in zip(ws, (o1_ref, o2_ref, o3_ref), dils):
        o_a = o_a + _split_dot(w * inv, expand) * token_order(o_ref, o_scr, d)
    ya = _dot(o_a.astype(BF16), wpa_ref[...])
    yb = _dot(ob_ref[...].astype(BF16), wpb_ref[...])
    gate_a = jnp.concatenate([ga_ref[0], ga_ref[1]], axis=1).astype(F32)
    gate_b = jnp.concatenate([gb_ref[0], gb_ref[1]], axis=1).astype(F32)
    merged = jax.nn.sigmoid(gate_a) * ya + jax.nn.sigmoid(gate_b) * yb
    out_ref[...] = x_ref[...] + _dot(merged.astype(BF16), wo_ref[...])


def _merge(x, outs, maxes, dens, ob, rest, wpa, wpb, wo, expand, seq, tm):
    n = x.shape[0]
    tiles = seq // tm
    dils = tuple(o.shape[1] for o in outs)

    def rows(width):
        return pl.BlockSpec((tm, width), lambda i: (i, 0))

    def dilated(a):
        dil, width = a.shape[1], a.shape[3]
        return pl.BlockSpec((None, dil, tm // dil, width), lambda i: (i // tiles, 0, i % tiles, 0))

    def gate(block):
        return pl.BlockSpec((2, tm, COL_BLOCK), lambda i: (block // 2, i, 0))

    def full(a):
        return pl.BlockSpec(a.shape, lambda i: (0, 0))

    return pl.pallas_call(
        functools.partial(_merge_kernel, dils=dils),
        grid=(n // tm,),
        in_specs=[rows(D_MODEL)] + [dilated(a) for a in (*outs, *maxes, *dens)] + [rows(GLA_VAL)]
                 + [gate(R_GA), gate(R_GB)] + [full(wpa), full(wpb), full(wo), full(expand)],
        out_specs=rows(D_MODEL),
        out_shape=jax.ShapeDtypeStruct((n, D_MODEL), F32),
        scratch_shapes=[pltpu.VMEM((A_OUT // LANE, tm, LANE), F32), pltpu.VMEM((1, tm, LANE), F32)],
        compiler_params=_cparams(("parallel",)),
        name="merge",
    )(x, *outs, *maxes, *dens, ob, rest, rest, wpa, wpb, wo, expand)


def _ffn_kernel(x_ref, g_ref, gn_ref, wg_ref, wu_ref, wd_ref, o_ref, hn_ref):
    tm = x_ref.shape[0]
    chunk = min(tm, ROW_CHUNK)
    for rc in range(tm // chunk):
        rows = slice(rc * chunk, (rc + 1) * chunk)
        x = x_ref[rows, :]
        h = _rms(x, g_ref[...]).astype(BF16)
        mid = jax.nn.silu(_dot(h, wg_ref[...])) * _dot(h, wu_ref[...])
        y = x + _dot(mid.astype(BF16), wd_ref[...])
        o_ref[rows, :] = y
        hn_ref[rows, :] = _rms(y, gn_ref[...]).astype(hn_ref.dtype)


def _ffn(x, g, g_next, wg, wu, wd, tm):
    n = x.shape[0]

    def resident(a):
        return pl.BlockSpec(a.shape, lambda i: (0, 0), pipeline_mode=pl.Buffered(1))

    def rows():
        return pl.BlockSpec((tm, D_MODEL), lambda i: (i, 0))

    return pl.pallas_call(
        _ffn_kernel,
        grid=(n // tm,),
        in_specs=[rows(), resident(g), resident(g_next), resident(wg), resident(wu), resident(wd)],
        out_specs=[rows(), rows()],
        out_shape=[jax.ShapeDtypeStruct((n, D_MODEL), F32), jax.ShapeDtypeStruct((n, D_MODEL), BF16)],
        compiler_params=_cparams(("parallel",)),
        name="ffn",
    )(x, g, g_next, wg, wu, wd)


def _router_kernel(x_ref, g_ref, wr_ref, comb_ref, scol_ref, srow_ref, cnt_ref):
    h = _rms(x_ref[...], g_ref[...])
    w = wr_ref[...]
    h_hi, w_hi = h.astype(BF16), w.astype(BF16)
    h_lo, w_lo = (h - h_hi.astype(F32)).astype(BF16), (w - w_hi.astype(F32)).astype(BF16)
    logits = _dot(h_hi, w_hi) + (_dot(h_hi, w_lo) + _dot(h_lo, w_hi))
    lane = lax.broadcasted_iota(jnp.int32, logits.shape, 1)
    logits = jnp.where(lane < N_EXPERTS, logits, -jnp.inf)
    v1 = jnp.max(logits, axis=-1, keepdims=True)
    i1 = jnp.min(jnp.where(logits == v1, lane, LANE), axis=-1, keepdims=True)
    rest = jnp.where(lane == i1, -jnp.inf, logits)
    v2 = jnp.max(rest, axis=-1, keepdims=True)
    i2 = jnp.min(jnp.where(rest == v2, lane, LANE), axis=-1, keepdims=True)
    e2 = jnp.exp(v2 - v1)
    g1 = 1.0 / (1.0 + e2)
    g2 = e2 / (1.0 + e2)
    comb = jnp.where(lane == i1, g1, 0.0) + jnp.where(lane == i2, g2, 0.0)
    comb_ref[...] = comb

    tile = comb.shape[0]
    routed = comb > 0.0
    earlier = lax.broadcasted_iota(jnp.int32, (tile, tile), 1) < lax.broadcasted_iota(jnp.int32, (tile, tile), 0)
    rank = _dot(earlier.astype(BF16), routed.astype(BF16))
    slot = jnp.where(routed, rank, -1.0)
    scol_ref[...] = slot
    srow_ref[...] = slot.T[0:N_EXPERTS, :]
    count = jnp.sum(routed.astype(F32), axis=0, keepdims=True)
    cnt_ref[...] = jnp.broadcast_to(count, cnt_ref.shape).astype(jnp.int32)


def _router(x, g, wr, tile):
    n = x.shape[0]
    tiles = n // tile
    comb, slot_col, slot_row, counts = pl.pallas_call(
        _router_kernel,
        grid=(tiles,),
        in_specs=[
            pl.BlockSpec((tile, D_MODEL), lambda i: (i, 0)),
            pl.BlockSpec((1, D_MODEL), lambda i: (0, 0)),
            pl.BlockSpec((D_MODEL, LANE), lambda i: (0, 0)),
        ],
        out_specs=[
            pl.BlockSpec((tile, LANE), lambda i: (i, 0)),
            pl.BlockSpec((tile, LANE), lambda i: (i, 0)),
            pl.BlockSpec((None, N_EXPERTS, tile), lambda i: (i, 0, 0)),
            pl.BlockSpec((None, 8, LANE), lambda i: (i, 0, 0)),
        ],
        out_shape=[
            jax.ShapeDtypeStruct((n, LANE), F32),
            jax.ShapeDtypeStruct((n, LANE), F32),
            jax.ShapeDtypeStruct((tiles, N_EXPERTS, tile), F32),
            jax.ShapeDtypeStruct((tiles, 8, LANE), jnp.int32),
        ],
        compiler_params=_cparams(("parallel",)),
        name="router",
    )(x, g, wr)
    return (comb, counts[:, 0, :N_EXPERTS].reshape(tiles * N_EXPERTS), slot_col,
            slot_row.reshape(tiles * N_EXPERTS, 1, tile))


def _moe_kernel(cnt_ref, x_ref, g_ref, gn_ref, comb_ref, scol_ref, srow_ref, wg_ref, wu_ref, wd_ref, o_ref,
                h_ref, xc_ref, y_ref, *, sub, n_sub):
    i, e, f = pl.program_id(0), pl.program_id(1), pl.program_id(2)
    last_f = pl.num_programs(2) - 1
    tile = x_ref.shape[0]
    sub_pad = y_ref.shape[1]
    count = cnt_ref[i * N_EXPERTS + e]

    @pl.when((e == 0) & (f == 0))
    def _():
        h_ref[...] = _rms(x_ref[...], g_ref[...]).astype(BF16)
        o_ref[...] = x_ref[...]

    for s in range(n_sub):
        @pl.when((f == 0) & (s * sub < count))
        def _():
            slots = (lax.broadcasted_iota(jnp.int32, (sub, tile), 0) + s * sub).astype(F32)
            pick = (srow_ref[...] == slots).astype(BF16)
            xc_ref[s] = _dot(pick, h_ref[...]).astype(BF16)
            y_ref[s] = jnp.zeros((sub_pad, D_MODEL), F32)

        @pl.when(s * sub < count)
        def _():
            xs = xc_ref[s]
            mid = jax.nn.silu(_dot(xs, wg_ref[...])) * _dot(xs, wu_ref[...])
            y_ref[s, 0:sub, :] += _dot(mid.astype(BF16), wd_ref[...])

        @pl.when((f == last_f) & (s * sub < count))
        def _():
            mine = lax.broadcasted_iota(jnp.int32, (tile, LANE), 1) == e
            slot = jnp.sum(jnp.where(mine, scol_ref[...], 0.0), axis=-1, keepdims=True)
            gate = jnp.sum(jnp.where(mine, comb_ref[...], 0.0), axis=-1, keepdims=True)
            slots = (lax.broadcasted_iota(jnp.int32, (tile, sub_pad), 1) + s * sub).astype(F32)
            place = (slot == slots).astype(BF16)
            o_ref[...] += gate * _dot(place, y_ref[s].astype(BF16))

    @pl.when((e == pl.num_programs(1) - 1) & (f == last_f))
    def _():
        o_ref[...] = _rms(o_ref[...], gn_ref[...])


def _moe(x, g, g_out, comb, counts, slot_col, slot_row, wg, wu, wd, sub, tf):
    n = x.shape[0]
    tile = slot_row.shape[-1]
    n_sub = -(-tile // sub)
    sub_pad = -(-sub // LANE) * LANE
    grid_spec = pltpu.PrefetchScalarGridSpec(
        num_scalar_prefetch=1,
        grid=(n // tile, N_EXPERTS, D_FF_EXPERT // tf),
        in_specs=[
            pl.BlockSpec((tile, D_MODEL), lambda i, e, f, c: (i, 0)),
            pl.BlockSpec((1, D_MODEL), lambda i, e, f, c: (0, 0)),
            pl.BlockSpec((1, D_MODEL), lambda i, e, f, c: (0, 0)),
            pl.BlockSpec((tile, LANE), lambda i, e, f, c: (i, 0)),
            pl.BlockSpec((tile, LANE), lambda i, e, f, c: (i, 0)),
            pl.BlockSpec((None, 1, tile), lambda i, e, f, c: (i * N_EXPERTS + e, 0, 0)),
            pl.BlockSpec((None, D_MODEL, tf), lambda i, e, f, c: (e, 0, f)),
            pl.BlockSpec((None, D_MODEL, tf), lambda i, e, f, c: (e, 0, f)),
            pl.BlockSpec((None, tf, D_MODEL), lambda i, e, f, c: (e, f, 0)),
        ],
        out_specs=pl.BlockSpec((tile, D_MODEL), lambda i, e, f, c: (i, 0)),
        scratch_shapes=[
            pltpu.VMEM((tile, D_MODEL), BF16),
            pltpu.VMEM((n_sub, sub, D_MODEL), BF16),
            pltpu.VMEM((n_sub, sub_pad, D_MODEL), F32),
        ],
    )
    return pl.pallas_call(
        functools.partial(_moe_kernel, sub=sub, n_sub=n_sub),
        grid_spec=grid_spec,
        out_shape=jax.ShapeDtypeStruct((n, D_MODEL), F32),
        compiler_params=_cparams(("parallel", "arbitrary", "arbitrary")),
        name="moe",
    )(counts, x, g, g_out, comb, slot_col, slot_row, wg, wu, wd)


def _norm_kernel(x_ref, g_ref, o_ref):
    o_ref[...] = _rms(x_ref[...], g_ref[...]).astype(o_ref.dtype)


def _norm(x, g, tm, out_dtype):
    n = x.shape[0]
    return pl.pallas_call(
        _norm_kernel,
        grid=(n // tm,),
        in_specs=[pl.BlockSpec((tm, D_MODEL), lambda i: (i, 0)), pl.BlockSpec((1, D_MODEL), lambda i: (0, 0))],
        out_specs=pl.BlockSpec((tm, D_MODEL), lambda i: (i, 0)),
        out_shape=jax.ShapeDtypeStruct((n, D_MODEL), out_dtype),
        compiler_params=_cparams(("parallel",)),
        name="norm",
    )(x, g)


def _rope_tables(pos, split_half):
    half = HEAD_DIM // 2
    inv = ROPE_THETA ** (-jnp.arange(half, dtype=F32) / half)
    ang = pos.astype(F32)[:, None] * inv[None, :]
    cos = jnp.cos(ang)
    sin = jnp.sin(ang)
    if split_half:
        return jnp.tile(cos, (1, LANE // half)), jnp.tile(sin, (1, LANE // half))
    return (jnp.tile(jnp.concatenate([cos, cos], axis=-1), (1, LANE // HEAD_DIM)),
            jnp.tile(jnp.concatenate([-sin, sin], axis=-1), (1, LANE // HEAD_DIM)))


def _to_split_half(a):
    lead = a.shape[:-1]
    a = a.reshape(*lead, HEADS_PER_GROUP // QUAD, QUAD, 2, HEAD_DIM // 2)
    return jnp.swapaxes(a, -2, -3).reshape(*lead, A_OUT)


def _layer_weights(l, w_in, w_gate_a2, b_gate_a, g_gla, w_branch_a, w_branch_b, w_out):
    w = w_in[l]
    offs = np.cumsum((A_WIDTH, A_WIDTH, A_WIDTH, GLA_KEY, GLA_KEY, GLA_VAL, GLA_VAL, GATE_RANK, D_MODEL, D_MODEL))
    qa, ka, va, qb, kb, vb, rb, glr, ga, gb = jnp.split(w, [int(o) for o in offs[:-1]], axis=1)
    def attn_weights(split_half):
        relay = _to_split_half if split_half else (lambda a: a)
        return jnp.stack([jnp.stack([relay(qa[:, g * A_OUT:(g + 1) * A_OUT]), relay(ka[:, g * A_OUT:(g + 1) * A_OUT]),
                                     va[:, g * A_OUT:(g + 1) * A_OUT]]) for g in range(N_GROUPS)]).astype(BF16)

    w_attn = {False: attn_weights(False), True: attn_weights(True)}
    w_lowrank = jnp.pad(glr, ((0, 0), (0, LANE - GATE_RANK))).astype(BF16)
    rest = jnp.concatenate([qb, kb, vb, rb, ga, gb], axis=1)
    w_rest = jnp.swapaxes(rest.reshape(D_MODEL, N_REST, COL_BLOCK), 0, 1).astype(BF16)
    wa = jnp.pad(w_gate_a2[l], ((0, LANE - GATE_RANK), (0, 0))).astype(BF16)
    return dict(w_attn=w_attn, w_rest=w_rest, w_lowrank=w_lowrank, wa=wa, ba=b_gate_a[l][None, :], gg=g_gla[l][None, :],
                wpa=w_branch_a[l].astype(BF16), wpb=w_branch_b[l].astype(BF16), wo=w_out[l].astype(BF16))


def _head_expand():
    e = np.zeros((LANE, A_OUT), np.float32)
    for h in range(HEADS_PER_GROUP):
        e[h, h * HEAD_DIM:(h + 1) * HEAD_DIM] = 1.0
    return jnp.asarray(np.concatenate([e, e], axis=0), BF16)


def _trunk(x, rope, caches, gla_state, lw, ffw, g_mix, g_ffn, g_final, batch, seq, tm, tm_proj, moe_tile, moe_sub,
           p_dtype):
    cos_t, sin_t = rope
    expand = _head_expand()
    prompt = caches is None
    cache_out = [None] * N_GROUPS
    states_out = []
    assert DEPTH % 2 == 0
    h = _norm(x, g_mix[0][None, :], tm, BF16)
    for l in range(DEPTH):
        w = lw[l]
        rest, lowrank = _inproj_rest(h, w["w_rest"], w["w_lowrank"], tm_proj, p_dtype)
        outs, maxes, dens = [], [], []
        if prompt:
            qkvs = _inproj_attn(h, w["w_attn"][True], cos_t, sin_t, batch, seq, [d for _, d in DSWA_GROUPS], tm_proj,
                                p_dtype, True)
        else:
            qkvs = _inproj_attn(h, w["w_attn"][False], cos_t, sin_t, 1, batch * seq, [1] * N_GROUPS, tm_proj, p_dtype, False)
        for g, (window, dil) in enumerate(DSWA_GROUPS):
            qkv = qkvs[g]
            if prompt:
                o, mx, den = _attn_prompt(qkv, g)
                cache_out[g] = _kv_tail(qkv, cache_out[g], l, min(window, seq))
            else:
                o, lse, cache_out[g] = _attn_sample(qkv.reshape(3, batch * seq, COL_BLOCK), caches[g], cache_out[g],
                                                    l, g, seq)
                o = o.reshape(1, 1, batch * seq, A_OUT)
                mx = lse.reshape(1, 1, batch * seq, LANE)
                den = jnp.ones_like(mx)
            outs.append(o)
            maxes.append(mx)
            dens.append(den)
        if prompt:
            ob, s_new = _gla(rest, lowrank, w["wa"], w["ba"], w["gg"], batch, seq, GLA_CHUNK, GLA_CHUNKS_PER_STEP, BF16)
            x = _merge(x, outs, maxes, dens, ob, rest, w["wpa"], w["wpb"], w["wo"], expand, seq, min(tm, 512))
        else:
            ob, s_new = _gla_decode(rest, lowrank, w["wa"], w["ba"], w["gg"], gla_state, l, batch, seq, F32)
            x = _merge(x, outs, maxes, dens, ob, rest, w["wpa"], w["wpb"], w["wo"], expand, batch * seq, tm)
        states_out.append(s_new)
        i = l // 2
        if l % 2 == 0:
            x, h = _ffn(x, g_ffn[l][None, :], g_mix[l + 1][None, :], ffw["wg"][i], ffw["wu"][i], ffw["wd"][i], tm)
        else:
            routing = _router(x, g_ffn[l][None, :], ffw["wr"][i], moe_tile)
            g_out = g_final if l == DEPTH - 1 else g_mix[l + 1]
            x = _moe(x, g_ffn[l][None, :], g_out[None, :], *routing, ffw["eg"][i], ffw["eu"][i], ffw["ed"][i],
                     moe_sub, D_FF_EXPERT // 2)
            assert l == DEPTH - 1
    y = x
    bufs = [jnp.transpose(c.reshape(DEPTH, batch, 2, HEADS_PER_GROUP, HEAD_DIM, c.shape[-1]), (0, 1, 5, 2, 3, 4))
            for c in cache_out]
    return (y, bufs[0], bufs[1], bufs[2], jnp.stack(states_out))


def kernel(x_prompt, x_sample, cache_kv_w128, cache_kv_w512, cache_kv_w2048, state_gla, g_mix, w_in, w_gate_a2, b_gate_a, g_gla, w_branch_a, w_branch_b, w_out, g_ffn, w_ff_gate, w_ff_up, w_ff_down, w_router, w_exp_gate, w_exp_up, w_exp_down, g_final):
    batch, seq, _ = x_prompt.shape
    dec_batch, dec_seq, _ = x_sample.shape
    assert seq % (DSWA_GROUPS[-1][1] * Q_BLOCK) == 0 and seq % (GLA_CHUNKS_PER_STEP * GLA_CHUNK) == 0
    assert dec_seq % GLA_CHUNK != 0 and dec_seq % 8 == 0
    assert all(c.shape[2] == w for c, (w, _) in zip((cache_kv_w128, cache_kv_w512, cache_kv_w2048), DSWA_GROUPS))

    lw = [_layer_weights(l, w_in, w_gate_a2, b_gate_a, g_gla, w_branch_a, w_branch_b, w_out) for l in range(DEPTH)]
    ffw = dict(wg=w_ff_gate.astype(BF16), wu=w_ff_up.astype(BF16), wd=w_ff_down.astype(BF16),
               wr=jnp.pad(w_router, ((0, 0), (0, 0), (0, LANE - N_EXPERTS))),
               eg=w_exp_gate.astype(BF16), eu=w_exp_up.astype(BF16), ed=w_exp_down.astype(BF16))

    y_p, kv128_p, kv512_p, kv2048_p, gla_p = _trunk(
        x_prompt.reshape(batch * seq, D_MODEL), _rope_tables(jnp.arange(seq, dtype=jnp.int32), True), None, None,
        lw, ffw, g_mix, g_ffn, g_final, batch, seq, 1024, 2048, 1024, 288, BF16)

    n_s = dec_batch * dec_seq
    pos_s = jnp.tile(PAST_LEN + jnp.arange(dec_seq, dtype=jnp.int32), dec_batch)
    caches = [jnp.transpose(c, (0, 1, 3, 4, 5, 2)).reshape(DEPTH, dec_batch, 2, A_OUT, c.shape[2])
              for c in (cache_kv_w128, cache_kv_w512, cache_kv_w2048)]
    y_s, kv128_s, kv512_s, kv2048_s, gla_s = _trunk(
        x_sample.reshape(n_s, D_MODEL), _rope_tables(pos_s, False), caches, state_gla,
        lw, ffw, g_mix, g_ffn, g_final, dec_batch, dec_seq, n_s, n_s, n_s, 96, F32)

    return (y_p.reshape(batch, seq, D_MODEL), y_s.reshape(dec_batch, dec_seq, D_MODEL),
            kv128_p, kv512_p, kv2048_p, gla_p, kv128_s, kv512_s, kv2048_s, gla_s)
```

```python
import functools

import jax
import jax.numpy as jnp
import numpy as np
from jax import lax
from jax.experimental import pallas as pl
from jax.experimental.pallas import tpu as pltpu

F32 = jnp.float32
BF16 = jnp.bfloat16

D_MODEL = 1024
DEPTH = 2
PAST_LEN = 16384
HEAD_DIM = 64
DSWA_GROUPS = ((128, 1), (512, 4), (2048, 16))
N_GROUPS = 3
HEADS_PER_GROUP = 8
A_WIDTH = N_GROUPS * HEADS_PER_GROUP * HEAD_DIM
A_OUT = HEADS_PER_GROUP * HEAD_DIM
Q_BLOCK = 128
GLA_HEADS = 4
GLA_KEY = 512
GLA_VAL = 1024
GLA_DK = 128
GLA_DV = 256
GATE_RANK = 16
GATE_TAU = 16.0
GLA_CHUNK = 64
D_FF = 2816
N_EXPERTS = 8
D_FF_EXPERT = 3584
ROPE_THETA = 10000.0
EPS = 1e-6
LOG2_E = 1.4426950408889634

LANE = 128
COL_BLOCK = 512
R_QB, R_KB, R_VB, R_RB, R_GA, R_GB, N_REST = 0, 1, 2, 4, 6, 8, 10
ROW_CHUNK = 256
GLA_CHUNKS_PER_STEP = 32
VMEM_LIMIT = 56 * 1024 * 1024


def _cparams(sem):
    return pltpu.CompilerParams(dimension_semantics=sem, vmem_limit_bytes=VMEM_LIMIT)


def _rms(xf, g):
    return xf * lax.rsqrt(jnp.mean(xf * xf, axis=-1, keepdims=True) + EPS) * g


def _dot(a, b):
    return jnp.dot(a, b, preferred_element_type=F32)


def _dot_nt(a, b):
    return lax.dot_general(a, b, (((1,), (1,)), ((), ())), preferred_element_type=F32)


def _dot_tn(a, b):
    return lax.dot_general(a, b, (((0,), (0,)), ((), ())), preferred_element_type=F32)


def _div_pow2(x, d):
    assert d & (d - 1) == 0
    return lax.shift_right_logical(x, int(d).bit_length() - 1)


def _mod_pow2(x, d):
    assert d & (d - 1) == 0
    return x & (d - 1)


def _split_dot(a_f32, b_twice):
    hi = a_f32.astype(BF16)
    lo = (a_f32 - hi.astype(F32)).astype(BF16)
    return _dot(jnp.concatenate([hi, lo], axis=1), b_twice)


def _inproj_attn_kernel(h_ref, w_ref, cos_ref, sin_ref, *refs, dils, split_half, q_scale):
    step = pl.program_id(1)
    outs, stages = refs[:len(dils)], list(refs[len(dils):])
    for g, dil in enumerate(dils):
        stage = stages.pop(0) if dil > 1 else None

        @pl.when((step >= 3 * g) & (step < 3 * g + 3))
        def _(g=g, dil=dil, stage=stage):
            _project_rotate(h_ref, w_ref, cos_ref, sin_ref, outs[g], stage, step - 3 * g, dil, split_half, q_scale)


def _project_rotate(h_ref, w_ref, cos_ref, sin_ref, o_ref, stage, part, dil, split_half, q_scale):
    scratch = [stage]
    scale = jnp.where(part == 0, q_scale, 1.0).astype(F32)
    tm = h_ref.shape[0]
    chunk = min(tm, ROW_CHUNK)
    n_lane_groups = COL_BLOCK // LANE
    for rc in range(tm // chunk):
        rows = slice(rc * chunk, (rc + 1) * chunk)
        acc = _dot(h_ref[rows, :], w_ref[...])
        cos = jnp.where(part == 2, 1.0, cos_ref[rows, :] * scale)
        sin = jnp.where(part == 2, 0.0, sin_ref[rows, :] * scale)
        ys = []
        if split_half:
            for c in range(0, n_lane_groups, 2):
                a = acc[:, c * LANE:(c + 1) * LANE]
                b = acc[:, (c + 1) * LANE:(c + 2) * LANE]
                ys += [a * cos - b * sin, a * sin + b * cos]
        else:
            first_half = _mod_pow2(lax.broadcasted_iota(jnp.int32, cos.shape, 1), HEAD_DIM) < HEAD_DIM // 2
            for c in range(n_lane_groups):
                xs = acc[:, c * LANE:(c + 1) * LANE]
                swapped = jnp.where(first_half, pltpu.roll(xs, LANE - HEAD_DIM // 2, 1), pltpu.roll(xs, HEAD_DIM // 2, 1))
                ys.append(xs * cos + swapped * sin)
        if dil == 1:
            for c, y in enumerate(ys):
                o_ref[0, rows, c * LANE:(c + 1) * LANE] = y.astype(o_ref.dtype)
        else:
            sub = chunk // dil
            pitch = _residue_pitch(dil)
            base = rc * sub * pitch
            for c, y in enumerate(ys):
                if pitch == dil:
                    scratch[0][c, rows, :] = y
                else:
                    for u in range(sub):
                        scratch[0][c, base + u * pitch:base + u * pitch + dil, :] = y[u * dil:(u + 1) * dil, :]
            for r in range(dil):
                for c in range(n_lane_groups):
                    o_ref[r, rc * sub:(rc + 1) * sub, c * LANE:(c + 1) * LANE] = (
                        scratch[0][c, pl.ds(base + r, sub, stride=pitch), :].astype(o_ref.dtype))


def _residue_pitch(dil):
    return dil + 8 if dil % 16 == 0 else dil


def _inproj_attn(h, w, cos_t, sin_t, batch, seq, dils, tm, out_dtype, split_half):
    tiles = seq // tm
    scratch = [pltpu.VMEM((COL_BLOCK // LANE, tm // dil * _residue_pitch(dil), LANE), F32) for dil in dils if dil > 1]
    q_scale = HEAD_DIM ** -0.5 * (LOG2_E if split_half else 1.0)

    def out_spec(g, dil):
        return pl.BlockSpec((None, None, dil, tm // dil, COL_BLOCK),
                            lambda i, j: (jnp.clip(j - 3 * g, 0, 2), i // tiles, 0, i % tiles, 0))

    return pl.pallas_call(
        functools.partial(_inproj_attn_kernel, dils=tuple(dils), split_half=split_half, q_scale=q_scale),
        grid=(batch * tiles, 3 * len(dils)),
        in_specs=[
            pl.BlockSpec((tm, D_MODEL), lambda i, j: (i, 0)),
            pl.BlockSpec((None, None, D_MODEL, COL_BLOCK), lambda i, j: (j // 3, j % 3, 0, 0)),
            pl.BlockSpec((tm, LANE), lambda i, j: (i % tiles, 0)),
            pl.BlockSpec((tm, LANE), lambda i, j: (i % tiles, 0)),
        ],
        out_specs=[out_spec(g, dil) for g, dil in enumerate(dils)],
        out_shape=[jax.ShapeDtypeStruct((3, batch, dil, seq // dil, COL_BLOCK), out_dtype) for dil in dils],
        scratch_shapes=scratch,
        compiler_params=_cparams(("parallel", "arbitrary")),
        name="inproj_attn",
    )(h, w, cos_t, sin_t)


def _inproj_rest_kernel(h_ref, w_ref, wlr_ref, o_ref, lr_ref):
    tm = h_ref.shape[0]
    chunk = min(tm, ROW_CHUNK)

    @pl.when(pl.program_id(1) == 0)
    def _():
        for rc in range(tm // chunk):
            rows = slice(rc * chunk, (rc + 1) * chunk)
            lr_ref[rows, :] = _dot(h_ref[rows, :], wlr_ref[...]).astype(lr_ref.dtype)

    for rc in range(tm // chunk):
        rows = slice(rc * chunk, (rc + 1) * chunk)
        o_ref[rows, :] = _dot(h_ref[rows, :], w_ref[...]).astype(o_ref.dtype)


def _inproj_rest(h, w, w_lowrank, tm, out_dtype):
    n = h.shape[0]
    return pl.pallas_call(
        _inproj_rest_kernel,
        grid=(n // tm, N_REST),
        in_specs=[
            pl.BlockSpec((tm, D_MODEL), lambda i, j: (i, 0)),
            pl.BlockSpec((None, D_MODEL, COL_BLOCK), lambda i, j: (j, 0, 0)),
            pl.BlockSpec((D_MODEL, LANE), lambda i, j: (0, 0)),
        ],
        out_specs=[pl.BlockSpec((None, tm, COL_BLOCK), lambda i, j: (j, i, 0)),
                   pl.BlockSpec((tm, LANE), lambda i, j: (i, 0))],
        out_shape=[jax.ShapeDtypeStruct((N_REST, n, COL_BLOCK), out_dtype),
                   jax.ShapeDtypeStruct((n, LANE), out_dtype)],
        compiler_params=_cparams(("parallel", "arbitrary")),
        name="inproj_rest",
    )(h, w, w_lowrank)


QUAD = 4
QUAD_WIDTH = QUAD * HEAD_DIM
MAX_Q_BLOCKS_PER_STEP = 16


def _attn_prompt_kernel(bias_ref, *refs):
    n = pl.program_id(2)
    for r in range(refs[0].shape[0]):
        _attn_prompt_subsequence(n, bias_ref, *(ref.at[r] for ref in refs))


def _attn_prompt_subsequence(n, bias_ref, q_ref, kp_ref, kc_ref, vp_ref, vc_ref, o_ref, m_ref, l_ref):
    rows = QUAD * Q_BLOCK
    row_head = _div_pow2(lax.broadcasted_iota(jnp.int32, (rows, QUAD_WIDTH), 0), Q_BLOCK)
    col = lax.broadcasted_iota(jnp.int32, (rows, QUAD_WIDTH), 1)
    q_lanes = _div_pow2(_mod_pow2(col, LANE), HEAD_DIM // 2) == row_head
    v_lanes = _div_pow2(col, HEAD_DIM) == row_head
    lane = lax.broadcasted_iota(jnp.int32, (Q_BLOCK, LANE), 1)
    for qb in range(q_ref.shape[0] // Q_BLOCK):
        cur = slice(qb * Q_BLOCK, (qb + 1) * Q_BLOCK)
        if qb == 0:
            bias = bias_ref[jnp.where(n == 0, 1, 0)]
        else:
            bias = bias_ref[0]
            prev = slice((qb - 1) * Q_BLOCK, qb * Q_BLOCK)
        m_all = jnp.zeros((Q_BLOCK, LANE), F32)
        l_all = jnp.ones((Q_BLOCK, LANE), F32)
        for c in range(A_OUT // QUAD_WIDTH):
            cs = slice(c * QUAD_WIDTH, (c + 1) * QUAD_WIDTH)
            q = q_ref[cur, cs].astype(F32)
            q4 = jnp.where(q_lanes, jnp.concatenate([q] * QUAD, axis=0), 0.0).astype(BF16)
            k_prev = kp_ref[:, cs] if qb == 0 else kc_ref[prev, cs]
            v_prev = vp_ref[:, cs] if qb == 0 else vc_ref[prev, cs]
            k = jnp.concatenate([k_prev, kc_ref[cur, cs]], axis=0)
            v = jnp.concatenate([v_prev, vc_ref[cur, cs]], axis=0)
            s = _dot_nt(q4, k) + bias
            m = jnp.max(s, axis=-1, keepdims=True)
            p = jnp.exp2(s - m)
            l = jnp.sum(p, axis=-1, keepdims=True)
            o4 = jnp.where(v_lanes, _dot(p.astype(BF16), v), 0.0)
            o = o4[0:Q_BLOCK]
            for j in range(QUAD):
                hs = slice(j * Q_BLOCK, (j + 1) * Q_BLOCK)
                if j > 0:
                    o = o + o4[hs]
                m_all = jnp.where(lane == c * QUAD + j, m[hs], m_all)
                l_all = jnp.where(lane == c * QUAD + j, l[hs], l_all)
            o_ref[cur, cs] = o.astype(o_ref.dtype)
        m_ref[cur, :] = m_all
        l_ref[cur, :] = l_all


def _band_bias(span):
    qi = np.arange(QUAD * Q_BLOCK)[:, None] % Q_BLOCK
    kj = np.arange(2 * Q_BLOCK)[None, :]
    dist = Q_BLOCK + qi - kj
    band = (dist >= 0) & (dist <= span)
    first = band & (kj >= Q_BLOCK)
    return jnp.asarray(np.where(np.stack([band, first]), 0.0, -np.inf), F32)


def _attn_prompt(qkv, group):
    window, dil = DSWA_GROUPS[group]
    _, batch, _, sub_len, _ = qkv.shape
    q_blocks = min(MAX_Q_BLOCKS_PER_STEP, sub_len // Q_BLOCK)
    step_rows = q_blocks * Q_BLOCK
    nb = sub_len // step_rows
    res = min(dil, MAX_Q_BLOCKS_PER_STEP // q_blocks)

    def spec(part, prev):
        if prev:
            return pl.BlockSpec((None, None, res, Q_BLOCK, COL_BLOCK),
                                lambda b, r, n: (part, b, r, jnp.maximum(n * q_blocks - 1, 0), 0))
        return pl.BlockSpec((None, None, res, step_rows, COL_BLOCK), lambda b, r, n: (part, b, r, n, 0))

    bias = _band_bias(window // dil)
    stat_spec = pl.BlockSpec((None, res, step_rows, LANE), lambda b, r, n: (b, r, n, 0))
    stat_shape = jax.ShapeDtypeStruct((batch, dil, sub_len, LANE), F32)
    return pl.pallas_call(
        _attn_prompt_kernel,
        grid=(batch, dil // res, nb),
        in_specs=[pl.BlockSpec(bias.shape, lambda b, r, n: (0, 0, 0)),
                  spec(0, False), spec(1, True), spec(1, False), spec(2, True), spec(2, False)],
        out_specs=[pl.BlockSpec((None, res, step_rows, A_OUT), lambda b, r, n: (b, r, n, 0)), stat_spec, stat_spec],
        out_shape=[jax.ShapeDtypeStruct((batch, dil, sub_len, A_OUT), BF16), stat_shape, stat_shape],
        compiler_params=_cparams(("parallel", "parallel", "arbitrary")),
        name=f"attn_prompt_g{group}",
    )(bias, qkv, qkv, qkv, qkv, qkv)


def _kv_tail_kernel(*refs, dil, aliased):
    if aliased:
        k_ref, v_ref, _, out_ref, scr = refs
        _kv_tail_body(k_ref, v_ref, out_ref, scr, dil)
        return
    k_ref, v_ref, out_ref, scr = refs

    @pl.when(pl.program_id(0) == 0)
    def _():
        _kv_tail_body(k_ref, v_ref, out_ref, scr, dil)

    @pl.when(pl.program_id(0) > 0)
    def _():
        out_ref[...] = jnp.zeros_like(out_ref)


def _kv_tail_body(k_ref, v_ref, out_ref, scr, dil):
    keep = out_ref.shape[2]
    sub = keep // dil
    half_w = HEAD_DIM // 2
    for kv, ref in enumerate((k_ref, v_ref)):
        for c in range(COL_BLOCK // LANE):
            cs = slice(c * LANE, (c + 1) * LANE)
            if dil == 1:
                tok = ref[0, :, cs].astype(F32)
            else:
                for r in range(dil):
                    scr[pl.ds(r, sub, stride=dil), :] = ref[r, :, cs].astype(F32)
                tok = scr[...]
            chan = tok.T
            if kv == 1:
                out_ref[kv, cs, :] = chan
            else:
                for j in range(QUAD):
                    dst = (QUAD * (c // 2) + j) * HEAD_DIM + (c % 2) * half_w
                    out_ref[kv, dst:dst + half_w, :] = chan[j * half_w:(j + 1) * half_w, :]


def _kv_tail(qkv, prev_out, layer, keep):
    _, batch, dil, sub_len, _ = qkv.shape
    last = sub_len // (keep // dil) - 1
    first = prev_out is None
    assert first == (layer == 0)
    phases = DEPTH if first else 1

    def spec(part):
        return pl.BlockSpec((None, None, dil, keep // dil, COL_BLOCK),
                            lambda p, b: (part, jnp.where(p == 0, b, batch - 1), 0, last, 0))

    in_specs = [spec(1), spec(2)]
    args = [qkv, qkv]
    aliases = {}
    if not first:
        in_specs.append(pl.BlockSpec(memory_space=pl.ANY))
        args.append(prev_out)
        aliases = {2: 0}
    return pl.pallas_call(
        functools.partial(_kv_tail_kernel, dil=dil, aliased=not first),
        grid=(phases, batch),
        in_specs=in_specs,
        out_specs=pl.BlockSpec((None, None, 2, A_OUT, keep), lambda p, b: (layer + p, b, 0, 0, 0)),
        out_shape=jax.ShapeDtypeStruct((DEPTH, batch, 2, A_OUT, keep), F32),
        scratch_shapes=[pltpu.VMEM((keep, LANE), F32)],
        input_output_aliases=aliases,
        compiler_params=_cparams(("arbitrary", "arbitrary")),
        name=f"kv_tail_{keep}",
    )(*args)


SHIFT_ROWS = 128
SAMPLE_BLOCK_BYTES = 8 * 1024 * 1024


def _attn_sample_kernel(*refs, dil, span, first):
    if first:
        q_ref, k_ref, v_ref, c_ref, o_ref, lse_ref, cout_ref = refs
    else:
        q_ref, k_ref, v_ref, c_ref, _, o_ref, lse_ref, cout_ref = refs
    n_rows = c_ref.shape[0]
    t_new = q_ref.shape[0] // n_rows

    def update(j):
        tok = pl.ds(j * t_new, t_new)
        q, k, v = q_ref.at[tok], k_ref.at[tok], v_ref.at[tok]
        _sample_attention(q, k, v, c_ref.at[j], o_ref.at[tok], lse_ref.at[tok], dil, span)
        _append_new(c_ref.at[j], k, v, cout_ref.at[j])

    def shift_only(j):
        for kv in range(2):
            for rb in range(A_OUT // SHIFT_ROWS):
                rs = slice(rb * SHIFT_ROWS, (rb + 1) * SHIFT_ROWS)
                cout_ref[j, kv, rs, :] = pltpu.roll(c_ref[j, kv, rs, :], c_ref.shape[3] - t_new, 1)

    if first:
        @pl.when(pl.program_id(0) == 0)
        def _():
            for j in range(n_rows):
                update(j)

        @pl.when(pl.program_id(0) > 0)
        def _():
            for j in range(n_rows):
                shift_only(j)
    else:
        for j in range(n_rows):
            update(j)


def _append_new(c_ref, k_ref, v_ref, cout_ref):
    t_new = k_ref.shape[0]
    lb = c_ref.shape[2]
    width = cout_ref.shape[2]
    lane = lax.broadcasted_iota(jnp.int32, (SHIFT_ROWS, LANE), 1)
    zeros = jnp.zeros((LANE - t_new, A_OUT), F32)
    for kv, ref in enumerate((k_ref, v_ref)):
        new = ref[...].astype(F32)
        new_t = pltpu.roll(jnp.concatenate([new, zeros], axis=0).T, LANE - t_new, 1)
        for rb in range(A_OUT // SHIFT_ROWS):
            rs = slice(rb * SHIFT_ROWS, (rb + 1) * SHIFT_ROWS)
            shifted = pltpu.roll(c_ref[kv, rs, lb - width:lb], width - t_new, 1)
            if width > LANE:
                cout_ref[kv, rs, 0:width - LANE] = shifted[:, 0:width - LANE]
            cout_ref[kv, rs, width - LANE:width] = jnp.where(lane >= LANE - t_new, new_t[rs, :], shifted[:, width - LANE:width])


def _sample_attention(q_ref, k_ref, v_ref, c_ref, o_ref, lse_ref, dil, span):
    t_new = q_ref.shape[0]
    lb = c_ref.shape[2]
    rows = HEADS_PER_GROUP * t_new
    q = q_ref[...].astype(F32)
    k_new = k_ref[...].astype(F32)
    v_new = v_ref[...].astype(F32)

    q_rep = jnp.concatenate([q] * HEADS_PER_GROUP, axis=0)
    row_head = _div_pow2(lax.broadcasted_iota(jnp.int32, (rows, A_OUT), 0), t_new)
    col_head = _div_pow2(lax.broadcasted_iota(jnp.int32, (rows, A_OUT), 1), HEAD_DIM)
    diag = row_head == col_head
    q_bd = jnp.where(diag, q_rep, 0.0).astype(BF16)

    s_buf = _dot(q_bd, c_ref[0].astype(BF16))
    s_new = _dot_nt(q_bd, k_new.astype(BF16))

    tok_b = _mod_pow2(lax.broadcasted_iota(jnp.int32, (rows, lb), 0), t_new)
    pos_b = lax.broadcasted_iota(jnp.int32, (rows, lb), 1)
    dist_b = lb + tok_b - pos_b
    ok_b = (_mod_pow2(dist_b, dil) == 0) & (dist_b <= span * dil)
    tok_n = _mod_pow2(lax.broadcasted_iota(jnp.int32, (rows, t_new), 0), t_new)
    pos_n = lax.broadcasted_iota(jnp.int32, (rows, t_new), 1)
    dist_n = tok_n - pos_n
    ok_n = (dist_n >= 0) & (_mod_pow2(dist_n, dil) == 0) & (dist_n <= span * dil)
    s_buf = jnp.where(ok_b, s_buf, -jnp.inf)
    s_new = jnp.where(ok_n, s_new, -jnp.inf)

    m = jnp.maximum(jnp.max(s_buf, axis=-1, keepdims=True), jnp.max(s_new, axis=-1, keepdims=True))
    p_buf = jnp.exp(s_buf - m)
    p_new = jnp.exp(s_new - m)
    l = jnp.sum(p_buf, axis=-1, keepdims=True) + jnp.sum(p_new, axis=-1, keepdims=True)
    o_full = (_dot_nt(p_buf.astype(BF16), c_ref[1].astype(BF16)) + _dot(p_new.astype(BF16), v_new.astype(BF16))) / l
    o_full = jnp.where(diag, o_full, 0.0)
    lse = (m + jnp.log(l)) * LOG2_E

    o = o_full[0:t_new, :]
    lane_t = lax.broadcasted_iota(jnp.int32, (t_new, LANE), 1)
    lse_all = jnp.zeros((t_new, LANE), F32)
    for h in range(HEADS_PER_GROUP):
        if h > 0:
            o = o + o_full[h * t_new:(h + 1) * t_new, :]
        lse_all = jnp.where(lane_t == h, lse[h * t_new:(h + 1) * t_new, :], lse_all)
    o_ref[...] = o.astype(o_ref.dtype)
    lse_ref[...] = lse_all


def _attn_sample(qkv, cache, prev_out, layer, group, t_new):
    window, dil = DSWA_GROUPS[group]
    n_rows = qkv.shape[1]
    batch = n_rows // t_new
    lb = cache.shape[-1]

    first = prev_out is None
    assert first == (layer == 0) and t_new <= LANE
    phases = DEPTH if first else 1
    rows = max(1, min(batch, SAMPLE_BLOCK_BYTES // (2 * A_OUT * lb * 4)))
    assert batch % rows == 0
    blocks = batch // rows

    def row(p, b):
        return jnp.where(p == 0, b, blocks - 1)

    def pspec(part):
        return pl.BlockSpec((None, rows * t_new, COL_BLOCK), lambda p, b: (part, row(p, b), 0))

    in_specs = [pspec(0), pspec(1), pspec(2),
                pl.BlockSpec((None, rows, 2, A_OUT, lb), lambda p, b: (layer + p, b, 0, 0, 0))]
    args = [qkv, qkv, qkv, cache]
    aliases = {}
    if first:
        cache_out = pl.BlockSpec((None, rows, 2, A_OUT, lb), lambda p, b: (layer + p, b, 0, 0, 0))
    else:
        in_specs.append(pl.BlockSpec(memory_space=pl.ANY))
        args.append(prev_out)
        aliases = {4: 2}
        cache_out = pl.BlockSpec((None, rows, 2, A_OUT, LANE), lambda p, b: (layer, b, 0, 0, lb // LANE - 1))
    return pl.pallas_call(
        functools.partial(_attn_sample_kernel, dil=dil, span=window // dil, first=first),
        grid=(phases, blocks),
        in_specs=in_specs,
        out_specs=[
            pl.BlockSpec((rows * t_new, A_OUT), lambda p, b: (row(p, b), 0)),
            pl.BlockSpec((rows * t_new, LANE), lambda p, b: (row(p, b), 0)),
            cache_out,
        ],
        out_shape=[
            jax.ShapeDtypeStruct((n_rows, A_OUT), F32),
            jax.ShapeDtypeStruct((n_rows, LANE), F32),
            jax.ShapeDtypeStruct(cache.shape, F32),
        ],
        input_output_aliases=aliases,
        compiler_params=_cparams(("arbitrary", "arbitrary")),
        name=f"attn_sample_g{group}",
    )(*args)


def _gla_kernel(q_ref, k_ref, v_ref, r_ref, glr_ref, wa_ref, ba_ref, gg_ref, o_ref, sout_ref, s_ref, *, chunk, n_chunks):
    step = pl.program_id(2)

    @pl.when(step == 0)
    def _():
        s_ref[...] = jnp.zeros_like(s_ref)

    ri = lax.broadcasted_iota(jnp.int32, (chunk, chunk), 0)
    ci = lax.broadcasted_iota(jnp.int32, (chunk, chunk), 1)
    causal = ri >= ci
    tri = causal.astype(BF16)
    gg = gg_ref[...]

    gate = _dot(glr_ref[...].astype(BF16), wa_ref[...]) + ba_ref[...]
    log_a = jax.nn.log_sigmoid(gate) * (1.0 / GATE_TAU)
    log_hi = log_a.astype(BF16)
    log_lo = (log_a - log_hi.astype(F32)).astype(BF16)
    chunks = [slice(c * chunk, (c + 1) * chunk) for c in range(n_chunks)]
    cums = [_dot(tri, log_hi[rs, :]) + _dot(tri, log_lo[rs, :]) for rs in chunks]
    lasts = [cum[chunk - 1:chunk, :] for cum in cums]
    cum = jnp.concatenate(cums, axis=0)
    last = jnp.concatenate([jnp.broadcast_to(l, (chunk, GLA_DK)) for l in lasts], axis=0)
    grow = jnp.exp(cum)
    q_in = (q_ref[...].astype(F32) * (GLA_DK ** -0.5) * grow).astype(BF16)
    k = k_ref[...].astype(F32)
    k_in = (k * jnp.exp(-cum)).astype(BF16)
    k_out = (k * jnp.exp(last - cum)).astype(BF16)
    v = v_ref[...].astype(BF16)
    o_intra, kv, decay = [], [], []
    for rs, l in zip(chunks, lasts):
        att = jnp.where(causal, _dot_nt(q_in[rs, :], k_in[rs, :]), 0.0).astype(BF16)
        o_intra.append(_dot(att, v[rs, :]))
        kv.append(_dot_tn(k_out[rs, :], v[rs, :]))
        d = jnp.broadcast_to(jnp.exp(l), (GLA_DK, GLA_DK)).T
        decay.append(jnp.concatenate([d] * (GLA_DV // GLA_DK), axis=1))

    state = s_ref[...]
    for c, rs in enumerate(chunks):
        o = o_intra[c] + _dot(q_in[rs, :], state.astype(BF16))
        state = decay[c] * state + kv[c]
        y = _rms(o, gg) * jax.nn.silu(r_ref[rs, :].astype(F32))
        o_ref[rs, :] = y.astype(o_ref.dtype)
    s_ref[...] = state

    @pl.when(step == pl.num_programs(2) - 1)
    def _():
        sout_ref[...] = state


def _gla_decode_kernel(q_ref, k_ref, v_ref, r_ref, glr_ref, wa_ref, ba_ref, gg_ref, s0_ref, o_ref, sout_ref):
    n_rows = s0_ref.shape[0]
    rows = q_ref.shape[0]
    t_new = rows // n_rows
    ri = lax.broadcasted_iota(jnp.int32, (rows, rows), 0)
    ci = lax.broadcasted_iota(jnp.int32, (rows, rows), 1)
    same_row = _div_pow2(ri, t_new) == _div_pow2(ci, t_new)
    causal = same_row & (ri >= ci)
    gate = _dot(glr_ref[...].astype(BF16), wa_ref[...]) + ba_ref[...]
    log_a = jax.nn.log_sigmoid(gate) * (1.0 / GATE_TAU)
    log_hi = log_a.astype(BF16)
    log_parts = jnp.concatenate([log_hi, (log_a - log_hi.astype(F32)).astype(BF16)], axis=0)
    tri, blk = causal.astype(BF16), same_row.astype(BF16)
    cum = _dot(jnp.concatenate([tri, tri], axis=1), log_parts)
    total = _dot(jnp.concatenate([blk, blk], axis=1), log_parts)
    q_in = (q_ref[...].astype(F32) * (GLA_DK ** -0.5) * jnp.exp(cum)).astype(BF16)
    k = k_ref[...].astype(F32)
    k_in = (k * jnp.exp(-cum)).astype(BF16)
    k_out = (k * jnp.exp(total - cum)).astype(BF16)
    decay = jnp.exp(total)
    heads_per_block = COL_BLOCK // GLA_DV
    for h in range(GLA_HEADS):
        ks = slice(h * GLA_DK, (h + 1) * GLA_DK)
        os_ = slice(h * GLA_DV, (h + 1) * GLA_DV)
        vb, vs = h // heads_per_block, slice((h % heads_per_block) * GLA_DV, (h % heads_per_block + 1) * GLA_DV)
        v = v_ref[vb, :, vs].astype(BF16)
        att = jnp.where(causal, _dot_nt(q_in[:, ks], k_in[:, ks]), 0.0).astype(BF16)
        o_intra = _dot(att, v)
        for j in range(n_rows):
            tok = slice(j * t_new, (j + 1) * t_new)
            state = s0_ref[j, h].astype(F32)
            o = o_intra[tok, :] + _dot(q_in[tok, ks], state.astype(BF16))
            d = jnp.broadcast_to(decay[j * t_new:j * t_new + 1, ks], (GLA_DK, GLA_DK)).T
            sout_ref[j, h] = jnp.concatenate([d] * (GLA_DV // GLA_DK), axis=1) * state + _dot_tn(k_out[tok, ks], v[tok, :])
            y = _rms(o, gg_ref[:, os_]) * jax.nn.silu(r_ref[vb, tok, vs].astype(F32))
            o_ref[tok, os_] = y.astype(o_ref.dtype)


DECODE_ROWS_PER_STEP = 8


def _gla_decode(rest, lowrank, wa, ba, gg, s0, layer, batch, t_new, out_dtype):
    n_rows = min(batch, DECODE_ROWS_PER_STEP)
    assert batch % n_rows == 0
    tm = n_rows * t_new

    def rows(block, n_blocks=None):
        if n_blocks is None:
            return pl.BlockSpec((None, tm, COL_BLOCK), lambda b: (block, b, 0))
        return pl.BlockSpec((n_blocks, tm, COL_BLOCK), lambda b: (block // n_blocks, b, 0))

    def full(a):
        return pl.BlockSpec(a.shape, lambda b: (0, 0))

    v_blocks = GLA_VAL // COL_BLOCK
    return pl.pallas_call(
        _gla_decode_kernel,
        grid=(batch // n_rows,),
        in_specs=[rows(R_QB), rows(R_KB), rows(R_VB, v_blocks), rows(R_RB, v_blocks),
                  pl.BlockSpec((tm, LANE), lambda b: (b, 0)),
                  full(wa), full(ba), full(gg),
                  pl.BlockSpec((None, n_rows, GLA_HEADS, GLA_DK, GLA_DV), lambda b: (layer, b, 0, 0, 0))],
        out_specs=[pl.BlockSpec((tm, GLA_VAL), lambda b: (b, 0)),
                   pl.BlockSpec((n_rows, GLA_HEADS, GLA_DK, GLA_DV), lambda b: (b, 0, 0, 0))],
        out_shape=[jax.ShapeDtypeStruct((batch * t_new, GLA_VAL), out_dtype),
                   jax.ShapeDtypeStruct((batch, GLA_HEADS, GLA_DK, GLA_DV), F32)],
        compiler_params=_cparams(("parallel",)),
        name="gla_decode",
    )(rest, rest, rest, rest, lowrank, wa, ba, gg, s0)


def _gla(rest, lowrank, wa, ba, gg, batch, seq, chunk, n_chunks, out_dtype):
    n_rows = rest.shape[1]
    lblk = chunk * n_chunks
    steps = seq // lblk

    def pspec(width, block):
        per = COL_BLOCK // width
        return pl.BlockSpec((None, lblk, width), lambda b, h, n: (block + h // per, b * steps + n, h % per))

    in_specs = [
        pspec(GLA_DK, R_QB), pspec(GLA_DK, R_KB), pspec(GLA_DV, R_VB), pspec(GLA_DV, R_RB),
        pl.BlockSpec((lblk, LANE), lambda b, h, n: (b * steps + n, 0)),
        pl.BlockSpec((LANE, GLA_DK), lambda b, h, n: (0, h)),
        pl.BlockSpec((1, GLA_DK), lambda b, h, n: (0, h)),
        pl.BlockSpec((1, GLA_DV), lambda b, h, n: (0, h)),
    ]
    args = [rest, rest, rest, rest, lowrank, wa, ba, gg]
    return pl.pallas_call(
        functools.partial(_gla_kernel, chunk=chunk, n_chunks=n_chunks),
        grid=(batch, GLA_HEADS, steps),
        in_specs=in_specs,
        out_specs=[
            pl.BlockSpec((lblk, GLA_DV), lambda b, h, n: (b * steps + n, h)),
            pl.BlockSpec((None, None, GLA_DK, GLA_DV), lambda b, h, n: (b, h, 0, 0)),
        ],
        out_shape=[
            jax.ShapeDtypeStruct((n_rows, GLA_VAL), out_dtype),
            jax.ShapeDtypeStruct((batch, GLA_HEADS, GLA_DK, GLA_DV), F32),
        ],
        scratch_shapes=[pltpu.VMEM((GLA_DK, GLA_DV), F32)],
        compiler_params=_cparams(("parallel", "parallel", "arbitrary")),
        name="gla",
    )(*args)


def _merge_kernel(x_ref, o1_ref, o2_ref, o3_ref, m1_ref, m2_ref, m3_ref, l1_ref, l2_ref, l3_ref, ob_ref, ga_ref, gb_ref,
                  wpa_ref, wpb_ref, wo_ref, e_ref, out_ref, o_scr, l_scr, *, dils):
    tm = x_ref.shape[0]

    def token_order(ref, scr, dil):
        if dil == 1:
            return ref[0].astype(F32)
        chunks = ref.shape[2] // LANE
        for r in range(dil):
            for c in range(chunks):
                scr[c, pl.ds(r, tm // dil, stride=dil), :] = ref[r, :, c * LANE:(c + 1) * LANE].astype(F32)
        return jnp.concatenate([scr[c] for c in range(chunks)], axis=1)

    ms = [token_order(m_ref, l_scr, d) for m_ref, d in zip((m1_ref, m2_ref, m3_ref), dils)]
    ls = [token_order(l_ref, l_scr, d) for l_ref, d in zip((l1_ref, l2_ref, l3_ref), dils)]
    m = jnp.maximum(jnp.maximum(ms[0], ms[1]), ms[2])
    ws = [jnp.exp2(mg - m) for mg in ms]
    inv = 1.0 / (ws[0] * ls[0] + ws[1] * ls[1] + ws[2] * ls[2])
    expand = e_ref[...]
    o_a = jnp.zeros((tm, A_OUT), F32)
    for w, o_ref, d in zip(ws, (o1_ref, o2_ref, o3_ref), dils):
        o_a = o_a + _split_dot(w * inv, expand) * token_order(o_ref, o_scr, d)
    ya = _dot(o_a.astype(BF16), wpa_ref[...])
    yb = _dot(ob_ref[...].astype(BF16), wpb_ref[...])
    gate_a = jnp.concatenate([ga_ref[0], ga_ref[1]], axis=1).astype(F32)
    gate_b = jnp.concatenate([gb_ref[0], gb_ref[1]], axis=1).astype(F32)
    merged = jax.nn.sigmoid(gate_a) * ya + jax.nn.sigmoid(gate_b) * yb
    out_ref[...] = x_ref[...] + _dot(merged.astype(BF16), wo_ref[...])


def _merge(x, outs, maxes, dens, ob, rest, wpa, wpb, wo, expand, seq, tm):
    n = x.shape[0]
    tiles = seq // tm
    dils = tuple(o.shape[1] for o in outs)

    def rows(width):
        return pl.BlockSpec((tm, width), lambda i: (i, 0))

    def dilated(a):
        dil, width = a.shape[1], a.shape[3]
        return pl.BlockSpec((None, dil, tm // dil, width), lambda i: (i // tiles, 0, i % tiles, 0))

    def gate(block):
        return pl.BlockSpec((2, tm, COL_BLOCK), lambda i: (block // 2, i, 0))

    def full(a):
        return pl.BlockSpec(a.shape, lambda i: (0, 0))

    return pl.pallas_call(
        functools.partial(_merge_kernel, dils=dils),
        grid=(n // tm,),
        in_specs=[rows(D_MODEL)] + [dilated(a) for a in (*outs, *maxes, *dens)] + [rows(GLA_VAL)]
                 + [gate(R_GA), gate(R_GB)] + [full(wpa), full(wpb), full(wo), full(expand)],
        out_specs=rows(D_MODEL),
        out_shape=jax.ShapeDtypeStruct((n, D_MODEL), F32),
        scratch_shapes=[pltpu.VMEM((A_OUT // LANE, tm, LANE), F32), pltpu.VMEM((1, tm, LANE), F32)],
        compiler_params=_cparams(("parallel",)),
        name="merge",
    )(x, *outs, *maxes, *dens, ob, rest, rest, wpa, wpb, wo, expand)


def _ffn_kernel(x_ref, g_ref, gn_ref, wg_ref, wu_ref, wd_ref, o_ref, hn_ref):
    tm = x_ref.shape[0]
    chunk = min(tm, ROW_CHUNK)
    for rc in range(tm // chunk):
        rows = slice(rc * chunk, (rc + 1) * chunk)
        x = x_ref[rows, :]
        h = _rms(x, g_ref[...]).astype(BF16)
        mid = jax.nn.silu(_dot(h, wg_ref[...])) * _dot(h, wu_ref[...])
        y = x + _dot(mid.astype(BF16), wd_ref[...])
        o_ref[rows, :] = y
        hn_ref[rows, :] = _rms(y, gn_ref[...]).astype(hn_ref.dtype)


def _ffn(x, g, g_next, wg, wu, wd, tm):
    n = x.shape[0]

    def resident(a):
        return pl.BlockSpec(a.shape, lambda i: (0, 0), pipeline_mode=pl.Buffered(1))

    def rows():
        return pl.BlockSpec((tm, D_MODEL), lambda i: (i, 0))

    return pl.pallas_call(
        _ffn_kernel,
        grid=(n // tm,),
        in_specs=[rows(), resident(g), resident(g_next), resident(wg), resident(wu), resident(wd)],
        out_specs=[rows(), rows()],
        out_shape=[jax.ShapeDtypeStruct((n, D_MODEL), F32), jax.ShapeDtypeStruct((n, D_MODEL), BF16)],
        compiler_params=_cparams(("parallel",)),
        name="ffn",
    )(x, g, g_next, wg, wu, wd)


def _router_kernel(x_ref, g_ref, wr_ref, comb_ref, scol_ref, srow_ref, cnt_ref):
    h = _rms(x_ref[...], g_ref[...])
    w = wr_ref[...]
    h_hi, w_hi = h.astype(BF16), w.astype(BF16)
    h_lo, w_lo = (h - h_hi.astype(F32)).astype(BF16), (w - w_hi.astype(F32)).astype(BF16)
    logits = _dot(h_hi, w_hi) + (_dot(h_hi, w_lo) + _dot(h_lo, w_hi))
    lane = lax.broadcasted_iota(jnp.int32, logits.shape, 1)
    logits = jnp.where(lane < N_EXPERTS, logits, -jnp.inf)
    v1 = jnp.max(logits, axis=-1, keepdims=True)
    i1 = jnp.min(jnp.where(logits == v1, lane, LANE), axis=-1, keepdims=True)
    rest = jnp.where(lane == i1, -jnp.inf, logits)
    v2 = jnp.max(rest, axis=-1, keepdims=True)
    i2 = jnp.min(jnp.where(rest == v2, lane, LANE), axis=-1, keepdims=True)
    e2 = jnp.exp(v2 - v1)
    g1 = 1.0 / (1.0 + e2)
    g2 = e2 / (1.0 + e2)
    comb = jnp.where(lane == i1, g1, 0.0) + jnp.where(lane == i2, g2, 0.0)
    comb_ref[...] = comb

    tile = comb.shape[0]
    routed = comb > 0.0
    earlier = lax.broadcasted_iota(jnp.int32, (tile, tile), 1) < lax.broadcasted_iota(jnp.int32, (tile, tile), 0)
    rank = _dot(earlier.astype(BF16), routed.astype(BF16))
    slot = jnp.where(routed, rank, -1.0)
    scol_ref[...] = slot
    srow_ref[...] = slot.T[0:N_EXPERTS, :]
    count = jnp.sum(routed.astype(F32), axis=0, keepdims=True)
    cnt_ref[...] = jnp.broadcast_to(count, cnt_ref.shape).astype(jnp.int32)


def _router(x, g, wr, tile):
    n = x.shape[0]
    tiles = n // tile
    comb, slot_col, slot_row, counts = pl.pallas_call(
        _router_kernel,
        grid=(tiles,),
        in_specs=[
            pl.BlockSpec((tile, D_MODEL), lambda i: (i, 0)),
            pl.BlockSpec((1, D_MODEL), lambda i: (0, 0)),
            pl.BlockSpec((D_MODEL, LANE), lambda i: (0, 0)),
        ],
        out_specs=[
            pl.BlockSpec((tile, LANE), lambda i: (i, 0)),
            pl.BlockSpec((tile, LANE), lambda i: (i, 0)),
            pl.BlockSpec((None, N_EXPERTS, tile), lambda i: (i, 0, 0)),
            pl.BlockSpec((None, 8, LANE), lambda i: (i, 0, 0)),
        ],
        out_shape=[
            jax.ShapeDtypeStruct((n, LANE), F32),
            jax.ShapeDtypeStruct((n, LANE), F32),
            jax.ShapeDtypeStruct((tiles, N_EXPERTS, tile), F32),
            jax.ShapeDtypeStruct((tiles, 8, LANE), jnp.int32),
        ],
        compiler_params=_cparams(("parallel",)),
        name="router",
    )(x, g, wr)
    return (comb, counts[:, 0, :N_EXPERTS].reshape(tiles * N_EXPERTS), slot_col,
            slot_row.reshape(tiles * N_EXPERTS, 1, tile))


def _moe_kernel(cnt_ref, x_ref, g_ref, gn_ref, comb_ref, scol_ref, srow_ref, wg_ref, wu_ref, wd_ref, o_ref,
                h_ref, xc_ref, y_ref, *, sub, n_sub):
    i, e, f = pl.program_id(0), pl.program_id(1), pl.program_id(2)
    last_f = pl.num_programs(2) - 1
    tile = x_ref.shape[0]
    sub_pad = y_ref.shape[1]
    count = cnt_ref[i * N_EXPERTS + e]

    @pl.when((e == 0) & (f == 0))
    def _():
        h_ref[...] = _rms(x_ref[...], g_ref[...]).astype(BF16)
        o_ref[...] = x_ref[...]

    for s in range(n_sub):
        @pl.when((f == 0) & (s * sub < count))
        def _():
            slots = (lax.broadcasted_iota(jnp.int32, (sub, tile), 0) + s * sub).astype(F32)
            pick = (srow_ref[...] == slots).astype(BF16)
            xc_ref[s] = _dot(pick, h_ref[...]).astype(BF16)
            y_ref[s] = jnp.zeros((sub_pad, D_MODEL), F32)

        @pl.when(s * sub < count)
        def _():
            xs = xc_ref[s]
            mid = jax.nn.silu(_dot(xs, wg_ref[...])) * _dot(xs, wu_ref[...])
            y_ref[s, 0:sub, :] += _dot(mid.astype(BF16), wd_ref[...])

        @pl.when((f == last_f) & (s * sub < count))
        def _():
            mine = lax.broadcasted_iota(jnp.int32, (tile, LANE), 1) == e
            slot = jnp.sum(jnp.where(mine, scol_ref[...], 0.0), axis=-1, keepdims=True)
            gate = jnp.sum(jnp.where(mine, comb_ref[...], 0.0), axis=-1, keepdims=True)
            slots = (lax.broadcasted_iota(jnp.int32, (tile, sub_pad), 1) + s * sub).astype(F32)
            place = (slot == slots).astype(BF16)
            o_ref[...] += gate * _dot(place, y_ref[s].astype(BF16))

    @pl.when((e == pl.num_programs(1) - 1) & (f == last_f))
    def _():
        o_ref[...] = _rms(o_ref[...], gn_ref[...])


def _moe(x, g, g_out, comb, counts, slot_col, slot_row, wg, wu, wd, sub, tf):
    n = x.shape[0]
    tile = slot_row.shape[-1]
    n_sub = -(-tile // sub)
    sub_pad = -(-sub // LANE) * LANE
    grid_spec = pltpu.PrefetchScalarGridSpec(
        num_scalar_prefetch=1,
        grid=(n // tile, N_EXPERTS, D_FF_EXPERT // tf),
        in_specs=[
            pl.BlockSpec((tile, D_MODEL), lambda i, e, f, c: (i, 0)),
            pl.BlockSpec((1, D_MODEL), lambda i, e, f, c: (0, 0)),
            pl.BlockSpec((1, D_MODEL), lambda i, e, f, c: (0, 0)),
            pl.BlockSpec((tile, LANE), lambda i, e, f, c: (i, 0)),
            pl.BlockSpec((tile, LANE), lambda i, e, f, c: (i, 0)),
            pl.BlockSpec((None, 1, tile), lambda i, e, f, c: (i * N_EXPERTS + e, 0, 0)),
            pl.BlockSpec((None, D_MODEL, tf), lambda i, e, f, c: (e, 0, f)),
            pl.BlockSpec((None, D_MODEL, tf), lambda i, e, f, c: (e, 0, f)),
            pl.BlockSpec((None, tf, D_MODEL), lambda i, e, f, c: (e, f, 0)),
        ],
        out_specs=pl.BlockSpec((tile, D_MODEL), lambda i, e, f, c: (i, 0)),
        scratch_shapes=[
            pltpu.VMEM((tile, D_MODEL), BF16),
            pltpu.VMEM((n_sub, sub, D_MODEL), BF16),
            pltpu.VMEM((n_sub, sub_pad, D_MODEL), F32),
        ],
    )
    return pl.pallas_call(
        functools.partial(_moe_kernel, sub=sub, n_sub=n_sub),
        grid_spec=grid_spec,
        out_shape=jax.ShapeDtypeStruct((n, D_MODEL), F32),
        compiler_params=_cparams(("parallel", "arbitrary", "arbitrary")),
        name="moe",
    )(counts, x, g, g_out, comb, slot_col, slot_row, wg, wu, wd)


def _norm_kernel(x_ref, g_ref, o_ref):
    o_ref[...] = _rms(x_ref[...], g_ref[...]).astype(o_ref.dtype)


def _norm(x, g, tm, out_dtype):
    n = x.shape[0]
    return pl.pallas_call(
        _norm_kernel,
        grid=(n // tm,),
        in_specs=[pl.BlockSpec((tm, D_MODEL), lambda i: (i, 0)), pl.BlockSpec((1, D_MODEL), lambda i: (0, 0))],
        out_specs=pl.BlockSpec((tm, D_MODEL), lambda i: (i, 0)),
        out_shape=jax.ShapeDtypeStruct((n, D_MODEL), out_dtype),
        compiler_params=_cparams(("parallel",)),
        name="norm",
    )(x, g)


def _rope_tables(pos, split_half):
    half = HEAD_DIM // 2
    inv = ROPE_THETA ** (-jnp.arange(half, dtype=F32) / half)
    ang = pos.astype(F32)[:, None] * inv[None, :]
    cos = jnp.cos(ang)
    sin = jnp.sin(ang)
    if split_half:
        return jnp.tile(cos, (1, LANE // half)), jnp.tile(sin, (1, LANE // half))
    return (jnp.tile(jnp.concatenate([cos, cos], axis=-1), (1, LANE // HEAD_DIM)),
            jnp.tile(jnp.concatenate([-sin, sin], axis=-1), (1, LANE // HEAD_DIM)))


def _to_split_half(a):
    lead = a.shape[:-1]
    a = a.reshape(*lead, HEADS_PER_GROUP // QUAD, QUAD, 2, HEAD_DIM // 2)
    return jnp.swapaxes(a, -2, -3).reshape(*lead, A_OUT)


def _layer_weights(l, w_in, w_gate_a2, b_gate_a, g_gla, w_branch_a, w_branch_b, w_out):
    w = w_in[l]
    offs = np.cumsum((A_WIDTH, A_WIDTH, A_WIDTH, GLA_KEY, GLA_KEY, GLA_VAL, GLA_VAL, GATE_RANK, D_MODEL, D_MODEL))
    qa, ka, va, qb, kb, vb, rb, glr, ga, gb = jnp.split(w, [int(o) for o in offs[:-1]], axis=1)
    def attn_weights(split_half):
        relay = _to_split_half if split_half else (lambda a: a)
        return jnp.stack([jnp.stack([relay(qa[:, g * A_OUT:(g + 1) * A_OUT]), relay(ka[:, g * A_OUT:(g + 1) * A_OUT]),
                                     va[:, g * A_OUT:(g + 1) * A_OUT]]) for g in range(N_GROUPS)]).astype(BF16)

    w_attn = {False: attn_weights(False), True: attn_weights(True)}
    w_lowrank = jnp.pad(glr, ((0, 0), (0, LANE - GATE_RANK))).astype(BF16)
    rest = jnp.concatenate([qb, kb, vb, rb, ga, gb], axis=1)
    w_rest = jnp.swapaxes(rest.reshape(D_MODEL, N_REST, COL_BLOCK), 0, 1).astype(BF16)
    wa = jnp.pad(w_gate_a2[l], ((0, LANE - GATE_RANK), (0, 0))).astype(BF16)
    return dict(w_attn=w_attn, w_rest=w_rest, w_lowrank=w_lowrank, wa=wa, ba=b_gate_a[l][None, :], gg=g_gla[l][None, :],
                wpa=w_branch_a[l].astype(BF16), wpb=w_branch_b[l].astype(BF16), wo=w_out[l].astype(BF16))


def _head_expand():
    e = np.zeros((LANE, A_OUT), np.float32)
    for h in range(HEADS_PER_GROUP):
        e[h, h * HEAD_DIM:(h + 1) * HEAD_DIM] = 1.0
    return jnp.asarray(np.concatenate([e, e], axis=0), BF16)


def _trunk(x, rope, caches, gla_state, lw, ffw, g_mix, g_ffn, g_final, batch, seq, tm, tm_proj, moe_tile, moe_sub,
           p_dtype):
    cos_t, sin_t = rope
    expand = _head_expand()
    prompt = caches is None
    cache_out = [None] * N_GROUPS
    states_out = []
    assert DEPTH % 2 == 0
    h = _norm(x, g_mix[0][None, :], tm, BF16)
    for l in range(DEPTH):
        w = lw[l]
        rest, lowrank = _inproj_rest(h, w["w_rest"], w["w_lowrank"], tm_proj, p_dtype)
        outs, maxes, dens = [], [], []
        if prompt:
            qkvs = _inproj_attn(h, w["w_attn"][True], cos_t, sin_t, batch, seq, [d for _, d in DSWA_GROUPS], tm_proj,
                                p_dtype, True)
        else:
            qkvs = _inproj_attn(h, w["w_attn"][False], cos_t, sin_t, 1, batch * seq, [1] * N_GROUPS, tm_proj, p_dtype, False)
        for g, (window, dil) in enumerate(DSWA_GROUPS):
            qkv = qkvs[g]
            if prompt:
                o, mx, den = _attn_prompt(qkv, g)
                cache_out[g] = _kv_tail(qkv, cache_out[g], l, min(window, seq))
            else:
                o, lse, cache_out[g] = _attn_sample(qkv.reshape(3, batch * seq, COL_BLOCK), caches[g], cache_out[g],
                                                    l, g, seq)
                o = o.reshape(1, 1, batch * seq, A_OUT)
                mx = lse.reshape(1, 1, batch * seq, LANE)
                den = jnp.ones_like(mx)
            outs.append(o)
            maxes.append(mx)
            dens.append(den)
        if prompt:
            ob, s_new = _gla(rest, lowrank, w["wa"], w["ba"], w["gg"], batch, seq, GLA_CHUNK, GLA_CHUNKS_PER_STEP, BF16)
            x = _merge(x, outs, maxes, dens, ob, rest, w["wpa"], w["wpb"], w["wo"], expand, seq, min(tm, 512))
        else:
            ob, s_new = _gla_decode(rest, lowrank, w["wa"], w["ba"], w["gg"], gla_state, l, batch, seq, F32)
            x = _merge(x, outs, maxes, dens, ob, rest, w["wpa"], w["wpb"], w["wo"], expand, batch * seq, tm)
        states_out.append(s_new)
        i = l // 2
        if l % 2 == 0:
            x, h = _ffn(x, g_ffn[l][None, :], g_mix[l + 1][None, :], ffw["wg"][i], ffw["wu"][i], ffw["wd"][i], tm)
        else:
            routing = _router(x, g_ffn[l][None, :], ffw["wr"][i], moe_tile)
            g_out = g_final if l == DEPTH - 1 else g_mix[l + 1]
            x = _moe(x, g_ffn[l][None, :], g_out[None, :], *routing, ffw["eg"][i], ffw["eu"][i], ffw["ed"][i],
                     moe_sub, D_FF_EXPERT // 2)
            assert l == DEPTH - 1
    y = x
    bufs = [jnp.transpose(c.reshape(DEPTH, batch, 2, HEADS_PER_GROUP, HEAD_DIM, c.shape[-1]), (0, 1, 5, 2, 3, 4))
            for c in cache_out]
    return (y, bufs[0], bufs[1], bufs[2], jnp.stack(states_out))


def kernel(x_prompt, x_sample, cache_kv_w128, cache_kv_w512, cache_kv_w2048, state_gla, g_mix, w_in, w_gate_a2, b_gate_a, g_gla, w_branch_a, w_branch_b, w_out, g_ffn, w_ff_gate, w_ff_up, w_ff_down, w_router, w_exp_gate, w_exp_up, w_exp_down, g_final):
    batch, seq, _ = x_prompt.shape
    dec_batch, dec_seq, _ = x_sample.shape
    assert seq % (DSWA_GROUPS[-1][1] * Q_BLOCK) == 0 and seq % (GLA_CHUNKS_PER_STEP * GLA_CHUNK) == 0
    assert dec_seq % GLA_CHUNK != 0 and dec_seq % 8 == 0
    assert all(c.shape[2] == w for c, (w, _) in zip((cache_kv_w128, cache_kv_w512, cache_kv_w2048), DSWA_GROUPS))

    lw = [_layer_weights(l, w_in, w_gate_a2, b_gate_a, g_gla, w_branch_a, w_branch_b, w_out) for l in range(DEPTH)]
    ffw = dict(wg=w_ff_gate.astype(BF16), wu=w_ff_up.astype(BF16), wd=w_ff_down.astype(BF16),
               wr=jnp.pad(w_router, ((0, 0), (0, 0), (0, LANE - N_EXPERTS))),
               eg=w_exp_gate.astype(BF16), eu=w_exp_up.astype(BF16), ed=w_exp_down.astype(BF16))

    y_p, kv128_p, kv512_p, kv2048_p, gla_p = _trunk(
        x_prompt.reshape(batch * seq, D_MODEL), _rope_tables(jnp.arange(seq, dtype=jnp.int32), True), None, None,
        lw, ffw, g_mix, g_ffn, g_final, batch, seq, 1024, 2048, 1024, 288, BF16)

    n_s = dec_batch * dec_seq
    pos_s = jnp.tile(PAST_LEN + jnp.arange(dec_seq, dtype=jnp.int32), dec_batch)
    caches = [jnp.transpose(c, (0, 1, 3, 4, 5, 2)).reshape(DEPTH, dec_batch, 2, A_OUT, c.shape[2])
              for c in (cache_kv_w128, cache_kv_w512, cache_kv_w2048)]
    y_s, kv128_s, kv512_s, kv2048_s, gla_s = _trunk(
        x_sample.reshape(n_s, D_MODEL), _rope_tables(pos_s, False), caches, state_gla,
        lw, ffw, g_mix, g_ffn, g_final, dec_batch, dec_seq, n_s, n_s, n_s, 96, F32)

    return (y_p.reshape(batch, seq, D_MODEL), y_s.reshape(dec_batch, dec_seq, D_MODEL),
            kv128_p, kv512_p, kv2048_p, gla_p, kv128_s, kv512_s, kv2048_s, gla_s)
```
